```python
import jax
import jax.numpy as jnp
from jax import lax
import numpy as np

D_MODEL = 1024
BATCH = 8
SEQ = 4096
DEPTH = 2

N_BRANCH = 3
W_BRANCH = 1024
SC_KERNEL = 3
CV_KERNEL = 31
CHUNK = 128
SG_HEADS = 8
SG_HEAD_DIM = W_BRANCH // SG_HEADS
IN_COLS = 7 * W_BRANCH
N_EXPERTS = 16
N_GROUPS = 4
EXPERTS_PER_GROUP = N_EXPERTS // N_GROUPS
TOP_K = 2
D_EXPERT = 512
MOE_BLOCK = 128
ALPHA = (2.0 * DEPTH) ** 0.25
BETA = (8.0 * DEPTH) ** -0.25
LN_EPS = 1e-5

kernel_name = 'hybrid_conv_sgu_moe_deepnorm'


def layer_norm(x, g, b):
    xf = x.astype(jnp.float32)
    mu = jnp.mean(xf, axis=-1, keepdims=True)
    xc = xf - mu
    var = jnp.mean(xc * xc, axis=-1, keepdims=True)
    y = xc * lax.rsqrt(var + LN_EPS) * g.astype(jnp.float32) + b.astype(jnp.float32)
    return y.astype(x.dtype)


def causal_depthwise_conv(x, w):
    k = w.shape[0]
    return lax.conv_general_dilated(
        x, w[:, None, :], window_strides=(1,), padding=[(k - 1, 0)],
        dimension_numbers=('NWC', 'WIO', 'NWC'), feature_group_count=x.shape[-1])


def token_mixer(u, w_in, sc_conv, cv_conv, cv_conv_b, cv_ln_g, cv_ln_b, sg_ln_g, sg_ln_b,
                sg_w, sg_b, w_branch, w_gate, b_gate, w_o):
    bsz, seq, _ = u.shape
    proj = jnp.einsum('bsd,dn->bsn', u, w_in)
    wb = W_BRANCH
    sc_b, sc_c, sc_h, cv_a, cv_g, sg_u, sg_v = jnp.split(
        proj, [wb, 2 * wb, 3 * wb, 4 * wb, 5 * wb, 6 * wb], axis=-1)
    y_a = sc_b * causal_depthwise_conv(sc_c * sc_h, sc_conv)
    cv = cv_a * jax.nn.sigmoid(cv_g)
    cv = causal_depthwise_conv(cv, cv_conv) + cv_conv_b
    y_b = jax.nn.silu(layer_norm(cv, cv_ln_g, cv_ln_b))
    sg_u = jax.nn.gelu(sg_u)
    sg_v = layer_norm(jax.nn.gelu(sg_v), sg_ln_g, sg_ln_b)
    v = sg_v.reshape(bsz, seq // CHUNK, CHUNK, SG_HEADS, SG_HEAD_DIM)
    mask = jnp.tril(jnp.ones((CHUNK, CHUNK), dtype=sg_w.dtype))
    v = jnp.einsum('hts,bnshc->bnthc', sg_w * mask, v) + sg_b.T[:, :, None]
    y_c = sg_u * v.reshape(bsz, seq, W_BRANCH)
    branches = jnp.stack([y_a, y_b, y_c], axis=0)
    y = jnp.einsum('nbsw,nwd->nbsd', branches, w_branch)
    gates = jax.nn.sigmoid(jnp.einsum('bsd,dne->nbse', u, w_gate) + b_gate[:, None, None, :])
    merged = jnp.sum(gates * y, axis=0)
    return jnp.einsum('bsd,de->bse', merged, w_o)


def grouped_moe(u, w_router, b_router, w1, w3, w2):
    bsz, seq, d = u.shape
    n_tok = bsz * seq
    xt = u.reshape(n_tok, d)
    logits = jnp.einsum('td,de->te', xt, w_router).astype(jnp.float32)
    scores = jax.nn.softmax(logits, axis=-1)
    sel = (scores + b_router.astype(jnp.float32)).reshape(n_tok, N_GROUPS, EXPERTS_PER_GROUP)
    group_score = jnp.sum(lax.top_k(sel, TOP_K)[0], axis=-1)
    g_idx = jnp.argmax(group_score, axis=-1).astype(jnp.int32)
    sel_in = jnp.take_along_axis(sel, g_idx[:, None, None], axis=1)[:, 0]
    _, loc = lax.top_k(sel_in, TOP_K)
    eid = g_idx[:, None] * EXPERTS_PER_GROUP + loc.astype(jnp.int32)
    gate = jnp.take_along_axis(scores, eid, axis=1)
    gate = gate / jnp.sum(gate, axis=-1, keepdims=True)
    n_assign = n_tok * TOP_K
    e_flat = eid.reshape(n_assign)
    tok_flat = jnp.arange(n_assign, dtype=jnp.int32) // TOP_K
    order = jnp.argsort(e_flat)
    e_sorted = e_flat[order]
    tok_sorted = tok_flat[order]
    gate_sorted = gate.reshape(n_assign)[order]
    counts = jnp.zeros((N_EXPERTS,), jnp.int32).at[e_flat].add(1)
    padded = (counts + MOE_BLOCK - 1) // MOE_BLOCK * MOE_BLOCK
    pad_end = jnp.cumsum(padded)
    pad_start = pad_end - padded
    seg_start = jnp.cumsum(counts) - counts
    dest = pad_start[e_sorted] + jnp.arange(n_assign, dtype=jnp.int32) - seg_start[e_sorted]
    n_blocks = (n_assign + N_EXPERTS * (MOE_BLOCK - 1) + MOE_BLOCK - 1) // MOE_BLOCK
    buf = jnp.zeros((n_blocks * MOE_BLOCK, d), u.dtype).at[dest].set(xt[tok_sorted])
    block_start = jnp.arange(n_blocks, dtype=jnp.int32) * MOE_BLOCK
    block_e = jnp.minimum(jnp.searchsorted(pad_end, block_start, side='right'),
                          N_EXPERTS - 1).astype(jnp.int32)

    def expert_block(args):
        xb, e = args
        h = jax.nn.silu(xb @ w1[e]) * (xb @ w3[e])
        return h @ w2[e]

    out_buf = lax.map(expert_block, (buf.reshape(n_blocks, MOE_BLOCK, d), block_e))
    out_buf = out_buf.reshape(n_blocks * MOE_BLOCK, d)
    contrib = out_buf[dest] * gate_sorted.astype(u.dtype)[:, None]
    out = jnp.zeros((n_tok, d), u.dtype).at[tok_sorted].add(contrib)
    return out.reshape(bsz, seq, d)


def setup_inputs(seed: int = 0) -> dict:
    key = jax.random.key(seed)
    ks = jax.random.split(key, 32)
    f32 = jnp.float32

    def nrm(k, shape, s):
        return jax.random.normal(k, shape, f32) * s

    d, wb, e, fe = D_MODEL, W_BRANCH, N_EXPERTS, D_EXPERT
    return {
        'x': nrm(ks[0], (BATCH, SEQ, d), 1.0),
        'c': nrm(ks[1], (BATCH, d), 1.0),
        'w_ada': nrm(ks[2], (DEPTH, d, 6 * d), d ** -0.5),
        'b_ada': nrm(ks[3], (DEPTH, 6 * d), 0.02),
        'w_in': nrm(ks[4], (DEPTH, d, IN_COLS), d ** -0.5),
        'sc_conv': nrm(ks[5], (DEPTH, SC_KERNEL, wb), SC_KERNEL ** -0.5),
        'cv_conv': nrm(ks[6], (DEPTH, CV_KERNEL, wb), CV_KERNEL ** -0.5),
        'cv_conv_b': nrm(ks[7], (DEPTH, wb), 0.02),
        'cv_ln_g': 1.0 + nrm(ks[8], (DEPTH, wb), 0.05),
        'cv_ln_b': nrm(ks[9], (DEPTH, wb), 0.02),
        'sg_ln_g': 1.0 + nrm(ks[10], (DEPTH, wb), 0.05),
        'sg_ln_b': nrm(ks[11], (DEPTH, wb), 0.02),
        'sg_w': nrm(ks[12], (DEPTH, SG_HEADS, CHUNK, CHUNK), CHUNK ** -0.5),
        'sg_b': 1.0 + nrm(ks[13], (DEPTH, SG_HEADS, CHUNK), 0.1),
        'w_branch': nrm(ks[14], (DEPTH, N_BRANCH, wb, d), BETA * wb ** -0.5),
        'w_gate': nrm(ks[15], (DEPTH, d, N_BRANCH, d), d ** -0.5),
        'b_gate': nrm(ks[16], (DEPTH, N_BRANCH, d), 0.02),
        'w_o': nrm(ks[17], (DEPTH, d, d), BETA * d ** -0.5),
        'ln1_g': 1.0 + nrm(ks[18], (DEPTH, d), 0.05),
        'ln1_b': nrm(ks[19], (DEPTH, d), 0.02),
        'w_router': nrm(ks[20], (d, e), d ** -0.5),
        'b_router': nrm(ks[21], (e,), 0.01),
        'w1': nrm(ks[22], (DEPTH, e, d, fe), d ** -0.5),
        'w3': nrm(ks[23], (DEPTH, e, d, fe), d ** -0.5),
        'w2': nrm(ks[24], (DEPTH, e, fe, d), BETA * fe ** -0.5),
        'ln2_g': 1.0 + nrm(ks[25], (DEPTH, d), 0.05),
        'ln2_b': nrm(ks[26], (DEPTH, d), 0.02),
    }


def reference(x, c, w_ada, b_ada, w_in, sc_conv, cv_conv, cv_conv_b, cv_ln_g, cv_ln_b,
              sg_ln_g, sg_ln_b, sg_w, sg_b, w_branch, w_gate, b_gate, w_o, ln1_g, ln1_b,
              w_router, b_router, w1, w3, w2, ln2_g, ln2_b):
    c_act = jax.nn.silu(c)
    for l in range(DEPTH):
        mod = jnp.einsum('bd,dn->bn', c_act, w_ada[l]) + b_ada[l]
        shift1, scale1, gate1, shift2, scale2, gate2 = jnp.split(mod[:, None, :], 6, axis=-1)
        u = x * (1.0 + scale1) + shift1
        h = token_mixer(u, w_in[l], sc_conv[l], cv_conv[l], cv_conv_b[l], cv_ln_g[l], cv_ln_b[l],
                        sg_ln_g[l], sg_ln_b[l], sg_w[l], sg_b[l], w_branch[l], w_gate[l],
                        b_gate[l], w_o[l])
        x = layer_norm(ALPHA * x + gate1 * h, ln1_g[l], ln1_b[l])
        u = x * (1.0 + scale2) + shift2
        h = grouped_moe(u, w_router, b_router, w1[l], w3[l], w2[l])
        x = layer_norm(ALPHA * x + gate2 * h, ln2_g[l], ln2_b[l])
    return x
```

```python
import functools

import jax
import jax.numpy as jnp
from jax import lax
from jax.experimental import pallas as pl
from jax.experimental.pallas import tpu as pltpu
from jax.experimental.pallas import tpu_sc as plsc

D_MODEL = 1024
DEPTH = 2
W_BRANCH = 1024
SC_KERNEL = 3
CV_KERNEL = 31
CHUNK = 128
SG_HEADS = 8
N_EXPERTS = 16
N_GROUPS = 4
EXPERTS_PER_GROUP = N_EXPERTS // N_GROUPS
TOP_K = 2
D_EXPERT = 512
ALPHA = (2.0 * DEPTH) ** 0.25
LN_EPS = 1e-5

F32 = jnp.float32
BF16 = jnp.bfloat16

V7X_VMEM_BYTES = 64 * 1024 * 1024
MIXER_VMEM_LIMIT = V7X_VMEM_BYTES - 6 * 1024 * 1024
SUBLANES = 8
LANES = 128

MIX_ROWS = 256
SC_HALO = SUBLANES
CV_HALO = 32
CONV_ROWS = 64
CONV_COLS = 256
MOE_ROWS = 256
COMB_ROWS = 512
SC_WINDOW = 32


def _dot(a, b):
    return jnp.dot(a, b, preferred_element_type=F32)


def _layer_norm(v, g, b):
    mu = jnp.mean(v, axis=-1, keepdims=True)
    vc = v - mu
    var = jnp.mean(vc * vc, axis=-1, keepdims=True)
    return vc * lax.rsqrt(var + LN_EPS) * g + b


def _ada_kernel(c_ref, w_ref, b_ref, o_ref):
    c = c_ref[...]
    c_act = c * jax.nn.sigmoid(c)
    o_ref[0] = jnp.dot(c_act, w_ref[0], preferred_element_type=F32,
                       precision=lax.Precision.HIGHEST) + b_ref[0]


def _ada(c, w_ada, b_ada):
    bsz, d = c.shape
    n = w_ada.shape[-1]
    tn = 1536
    return pl.pallas_call(
        _ada_kernel,
        grid=(DEPTH, n // tn),
        in_specs=[
            pl.BlockSpec((bsz, d), lambda l, j: (0, 0)),
            pl.BlockSpec((1, d, tn), lambda l, j: (l, 0, j)),
            pl.BlockSpec((1, 1, tn), lambda l, j: (l, 0, j)),
        ],
        out_specs=pl.BlockSpec((1, bsz, tn), lambda l, j: (l, 0, j)),
        out_shape=jax.ShapeDtypeStruct((DEPTH, bsz, n), F32),
    )(c, w_ada, b_ada.reshape(DEPTH, 1, n))


def _top2_of4(rows):
    m1 = rows[0]
    i1 = jnp.zeros(rows[0].shape, jnp.int32)
    for k in range(1, 4):
        gt = rows[k] > m1
        m1 = jnp.where(gt, rows[k], m1)
        i1 = jnp.where(gt, k, i1)
    m2 = jnp.full(rows[0].shape, -jnp.inf, F32)
    i2 = jnp.zeros(rows[0].shape, jnp.int32)
    for k in range(4):
        cand = jnp.where(i1 == k, -jnp.inf, rows[k])
        gt = cand > m2
        m2 = jnp.where(gt, cand, m2)
        i2 = jnp.where(gt, k, i2)
    return m1, i1, m2, i2


def _mixer_kernel(x_ref, mod_ref, w_in_ref, scw_ref, cvw_ref, cvb_ref, cvg_ref, cvbeta_ref,
                  sgg_ref, sgbeta_ref, sgw_ref, sgbt_ref, wbr_ref, wg_ref, bg_ref, wo_ref,
                  ln1g_ref, ln1b_ref, wrt_ref, brc_ref,
                  x1_ref, u2_ref, ri_ref, rf_ref, cnt_ref,
                  qbuf, cvbuf, ybuf, base_ref):
    ts = MIX_ROWS
    d = D_MODEL
    wb = W_BRANCH
    first_tile = pl.program_id(1) == 0

    @pl.when(jnp.logical_and(pl.program_id(0) == 0, first_tile))
    def _():
        base_ref[...] = jnp.zeros_like(base_ref)

    @pl.when(first_tile)
    def _():
        qbuf[0:SC_HALO, :] = jnp.zeros((SC_HALO, wb), F32)
        cvbuf[0:CV_HALO, :] = jnp.zeros((CV_HALO, wb), F32)

    x = x_ref[0]
    mod = mod_ref[0]
    shift1, scale1, gate1 = mod[:, 0:d], mod[:, d:2 * d], mod[:, 2 * d:3 * d]
    shift2, scale2 = mod[:, 3 * d:4 * d], mod[:, 4 * d:5 * d]
    ub = (x * (1.0 + scale1) + shift1).astype(BF16)

    pa = _dot(ub, w_in_ref[:, 0:3 * wb])
    qbuf[SC_HALO:SC_HALO + ts, :] = pa[:, wb:2 * wb] * pa[:, 2 * wb:3 * wb]
    conv = scw_ref[SC_KERNEL - 1:SC_KERNEL, :] * qbuf[SC_HALO:SC_HALO + ts, :]
    for k in range(SC_KERNEL - 1):
        off = SC_HALO - (SC_KERNEL - 1) + k
        conv = conv + scw_ref[k:k + 1, :] * qbuf[off:off + ts, :]
    ybuf[0] = (pa[:, 0:wb] * conv).astype(BF16)
    qbuf[0:SC_HALO, :] = qbuf[ts:ts + SC_HALO, :]

    pb = _dot(ub, w_in_ref[:, 3 * wb:5 * wb])
    cvbuf[CV_HALO:CV_HALO + ts, :] = pb[:, 0:wb] * jax.nn.sigmoid(pb[:, wb:2 * wb])
    for r0 in range(0, ts, CONV_ROWS):
        for c0 in range(0, wb, CONV_COLS):
            acc = jnp.zeros((CONV_ROWS, CONV_COLS), F32)
            for k in range(CV_KERNEL):
                off = r0 + CV_HALO - (CV_KERNEL - 1) + k
                acc = acc + (cvw_ref[k:k + 1, c0:c0 + CONV_COLS]
                             * cvbuf[off:off + CONV_ROWS, c0:c0 + CONV_COLS])
            qbuf[SC_HALO + r0:SC_HALO + r0 + CONV_ROWS, c0:c0 + CONV_COLS] = acc
    cvbuf[0:CV_HALO, :] = cvbuf[ts:ts + CV_HALO, :]
    cv = _layer_norm(qbuf[SC_HALO:SC_HALO + ts, :] + cvb_ref[...], cvg_ref[...], cvbeta_ref[...])
    ybuf[1] = (cv * jax.nn.sigmoid(cv)).astype(BF16)

    pc = _dot(ub, w_in_ref[:, 5 * wb:7 * wb])
    gu = jax.nn.gelu(pc[:, 0:wb])
    gv = _layer_norm(jax.nn.gelu(pc[:, wb:2 * wb]), sgg_ref[...], sgbeta_ref[...]).astype(BF16)
    row = lax.broadcasted_iota(jnp.int32, (CHUNK, CHUNK), 0)
    col = lax.broadcasted_iota(jnp.int32, (CHUNK, CHUNK), 1)
    hd = wb // SG_HEADS
    for h in range(SG_HEADS):
        wm = jnp.where(row >= col, sgw_ref[h], 0.0).astype(BF16)
        bias = sgbt_ref[:, h:h + 1]
        for n in range(ts // CHUNK):
            rs = slice(n * CHUNK, (n + 1) * CHUNK)
            cs = slice(h * hd, (h + 1) * hd)
            mixed = _dot(wm, gv[rs, cs]) + bias
            ybuf[2, rs, cs] = (gu[rs, cs] * mixed).astype(BF16)

    merged = jnp.zeros((ts, d), F32)
    for n in range(3):
        z = _dot(ybuf[n], wbr_ref[n])
        g = jax.nn.sigmoid(_dot(ub, wg_ref[:, n * d:(n + 1) * d]) + bg_ref[n:n + 1, :])
        merged = merged + g * z
    hmix = _dot(merged.astype(BF16), wo_ref[...])
    x1 = _layer_norm(ALPHA * x + gate1 * hmix, ln1g_ref[...], ln1b_ref[...])
    x1_ref[0] = x1
    u2 = x1 * (1.0 + scale2) + shift2
    u2_ref[0] = u2

    logits = lax.dot_general(wrt_ref[...], u2, (((1,), (1,)), ((), ())),
                             preferred_element_type=F32,
                             precision=lax.Precision.HIGHEST)
    mx = jnp.max(logits, axis=0, keepdims=True)
    ex = jnp.exp(logits - mx)
    scores = ex / jnp.sum(ex, axis=0, keepdims=True)
    sel = scores + brc_ref[...]
    tops = []
    for g in range(N_GROUPS):
        rows = [sel[g * EXPERTS_PER_GROUP + k:g * EXPERTS_PER_GROUP + k + 1, :]
                for k in range(EXPERTS_PER_GROUP)]
        tops.append(_top2_of4(rows))
    best = tops[0][0] + tops[0][2]
    g_idx = jnp.zeros(best.shape, jnp.int32)
    loc1, loc2 = tops[0][1], tops[0][3]
    for g in range(1, N_GROUPS):
        gs = tops[g][0] + tops[g][2]
        gt = gs > best
        best = jnp.where(gt, gs, best)
        g_idx = jnp.where(gt, g, g_idx)
        loc1 = jnp.where(gt, tops[g][1], loc1)
        loc2 = jnp.where(gt, tops[g][3], loc2)
    e0 = g_idx * EXPERTS_PER_GROUP + loc1
    e1 = g_idx * EXPERTS_PER_GROUP + loc2
    erow = lax.broadcasted_iota(jnp.int32, (N_EXPERTS, ts), 0)
    is0 = erow == e0
    is1 = erow == e1
    s0 = jnp.sum(jnp.where(is0, scores, 0.0), axis=0, keepdims=True)
    s1 = jnp.sum(jnp.where(is1, scores, 0.0), axis=0, keepdims=True)
    ssum = s0 + s1
    onehot = jnp.logical_or(is0, is1).astype(BF16)
    src = lax.broadcasted_iota(jnp.int32, (ts, ts), 0)
    dst = lax.broadcasted_iota(jnp.int32, (ts, ts), 1)
    earlier = (src < dst).astype(BF16)
    prior = _dot(onehot, earlier) + base_ref[:, 0:1]
    r0 = jnp.sum(jnp.where(is0, prior, 0.0), axis=0, keepdims=True)
    r1 = jnp.sum(jnp.where(is1, prior, 0.0), axis=0, keepdims=True)
    base_ref[...] = base_ref[...] + jnp.sum(onehot.astype(F32), axis=1, keepdims=True)

    zi = jnp.zeros((SUBLANES - 4, ts), jnp.int32)
    ri_ref[...] = jnp.concatenate([e0, e1, r0.astype(jnp.int32), r1.astype(jnp.int32), zi], axis=0)
    zf = jnp.zeros((SUBLANES - 2, ts), F32)
    rf_ref[...] = jnp.concatenate([s0 / ssum, s1 / ssum, zf], axis=0)
    cnt_ref[...] = base_ref[...]


def _mixer(x, mod3, w_in, scw, cvw, cvb, cvg, cvbeta, sgg, sgbeta, sgw, sgbt, wbr, wg, bg, wo,
           ln1g, ln1b, wrt, brc):
    bsz, seq, d = x.shape
    ts = MIX_ROWS
    ns = seq // ts
    n_tok = bsz * seq

    def const(shape):
        zeros = (0,) * len(shape)
        return pl.BlockSpec(shape, lambda b, j: zeros, pipeline_mode=pl.Buffered(1))

    in_specs = [
        pl.BlockSpec((1, ts, d), lambda b, j: (b, j, 0)),
        pl.BlockSpec((1, 1, 6 * d), lambda b, j: (b, 0, 0)),
        const(w_in.shape), const(scw.shape), const(cvw.shape), const(cvb.shape),
        const(cvg.shape), const(cvbeta.shape), const(sgg.shape), const(sgbeta.shape),
        const(sgw.shape), const(sgbt.shape), const(wbr.shape), const(wg.shape), const(bg.shape),
        const(wo.shape), const(ln1g.shape), const(ln1b.shape), const(wrt.shape), const(brc.shape),
    ]
    out_specs = [
        pl.BlockSpec((1, ts, d), lambda b, j: (b, j, 0)),
        pl.BlockSpec((1, ts, d), lambda b, j: (b, j, 0)),
        pl.BlockSpec((SUBLANES, ts), lambda b, j: (0, b * ns + j)),
        pl.BlockSpec((SUBLANES, ts), lambda b, j: (0, b * ns + j)),
        pl.BlockSpec((N_EXPERTS, LANES), lambda b, j: (0, 0)),
    ]
    out_shape = [
        jax.ShapeDtypeStruct((bsz, seq, d), F32),
        jax.ShapeDtypeStruct((bsz, seq, d), F32),
        jax.ShapeDtypeStruct((SUBLANES, n_tok), jnp.int32),
        jax.ShapeDtypeStruct((SUBLANES, n_tok), F32),
        jax.ShapeDtypeStruct((N_EXPERTS, LANES), F32),
    ]
    return pl.pallas_call(
        _mixer_kernel,
        grid=(bsz, ns),
        in_specs=in_specs,
        out_specs=out_specs,
        out_shape=out_shape,
        scratch_shapes=[
            pltpu.VMEM((SC_HALO + ts, W_BRANCH), F32),
            pltpu.VMEM((CV_HALO + ts, W_BRANCH), F32),
            pltpu.VMEM((3, ts, W_BRANCH), BF16),
            pltpu.VMEM((N_EXPERTS, LANES), F32),
        ],
        compiler_params=pltpu.CompilerParams(
            dimension_semantics=("arbitrary", "arbitrary"),
            vmem_limit_bytes=MIXER_VMEM_LIMIT),
    )(x, mod3, w_in, scw, cvw, cvb, cvg, cvbeta, sgg, sgbeta, sgw, sgbt, wbr, wg, bg, wo,
      ln1g, ln1b, wrt, brc)


def _sc_workers():
    info = plsc.get_sparse_core_info()
    return info.num_cores, info.num_cores * info.num_subcores


def _sc_scatter_rows(rows, dest_a, dest_b, n_out):
    n, d = rows.shape
    nc, nw = _sc_workers()
    per_w = n // nw
    n_win = per_w // SC_WINDOW
    ia = dest_a.reshape(nw, n_win, SC_WINDOW)
    ib = dest_b.reshape(nw, n_win, SC_WINDOW)
    mesh = plsc.VectorSubcoreMesh(core_axis_name="c", subcore_axis_name="s")

    @functools.partial(
        pl.kernel, mesh=mesh,
        out_type=jax.ShapeDtypeStruct((n_out, d), rows.dtype),
        scratch_types=[
            pltpu.VMEM((n_win, SC_WINDOW), jnp.int32),
            pltpu.VMEM((n_win, SC_WINDOW), jnp.int32),
            pltpu.VMEM((SC_WINDOW, d), rows.dtype),
        ],
    )
    def scatter(rows_hbm, ia_hbm, ib_hbm, out_hbm, ia_v, ib_v, rows_v):
        wid = lax.axis_index("s") * nc + lax.axis_index("c")
        pltpu.sync_copy(ia_hbm.at[wid], ia_v)
        pltpu.sync_copy(ib_hbm.at[wid], ib_v)
        base = wid * per_w

        @pl.loop(0, n_win)
        def _(j):
            pltpu.sync_copy(rows_hbm.at[pl.ds(base + j * SC_WINDOW, SC_WINDOW)], rows_v)
            pltpu.sync_copy(rows_v, out_hbm.at[ia_v.at[j]])
            pltpu.sync_copy(rows_v, out_hbm.at[ib_v.at[j]])

    return scatter(rows, ia, ib)


def _sc_gather_rows(table, idx):
    n = idx.shape[0]
    d = table.shape[1]
    nc, nw = _sc_workers()
    per_w = n // nw
    n_win = per_w // SC_WINDOW
    idx3 = idx.reshape(nw, n_win, SC_WINDOW)
    mesh = plsc.VectorSubcoreMesh(core_axis_name="c", subcore_axis_name="s")

    @functools.partial(
        pl.kernel, mesh=mesh,
        out_type=jax.ShapeDtypeStruct((n, d), table.dtype),
        scratch_types=[
            pltpu.VMEM((n_win, SC_WINDOW), jnp.int32),
            pltpu.VMEM((SC_WINDOW, d), table.dtype),
        ],
    )
    def gather(table_hbm, idx_hbm, out_hbm, idx_v, rows_v):
        wid = lax.axis_index("s") * nc + lax.axis_index("c")
        pltpu.sync_copy(idx_hbm.at[wid], idx_v)
        base = wid * per_w

        @pl.loop(0, n_win)
        def _(j):
            pltpu.sync_copy(table_hbm.at[idx_v.at[j]], rows_v)
            pltpu.sync_copy(rows_v, out_hbm.at[pl.ds(base + j * SC_WINDOW, SC_WINDOW)])

    return gather(table, idx3)


def _expert_kernel(be_ref, nused_ref, x_ref, w1_ref, w3_ref, w2_ref, o_ref):
    @pl.when(pl.program_id(0) < nused_ref[0])
    def _():
        xb = x_ref[...].astype(BF16)
        a = _dot(xb, w1_ref[0])
        h = a * jax.nn.sigmoid(a) * _dot(xb, w3_ref[0])
        o_ref[...] = _dot(h.astype(BF16), w2_ref[0])


def _experts(buf, block_e, n_used, w1, w3, w2):
    n_rows, d = buf.shape
    nb = n_rows // MOE_ROWS
    fe = w1.shape[-1]

    def row_map(i, be, nu):
        return (jnp.minimum(i, nu[0] - 1), 0)

    grid_spec = pltpu.PrefetchScalarGridSpec(
        num_scalar_prefetch=2,
        grid=(nb,),
        in_specs=[
            pl.BlockSpec((MOE_ROWS, d), row_map),
            pl.BlockSpec((1, d, fe), lambda i, be, nu: (be[i], 0, 0)),
            pl.BlockSpec((1, d, fe), lambda i, be, nu: (be[i], 0, 0)),
            pl.BlockSpec((1, fe, d), lambda i, be, nu: (be[i], 0, 0)),
        ],
        out_specs=pl.BlockSpec((MOE_ROWS, d), row_map),
    )
    return pl.pallas_call(
        _expert_kernel,
        grid_spec=grid_spec,
        out_shape=jax.ShapeDtypeStruct((n_rows, d), F32),
        compiler_params=pltpu.CompilerParams(dimension_semantics=("arbitrary",)),
    )(block_e, n_used, buf, w1, w3, w2)


def _combine_kernel(x1_ref, pair_ref, gates_ref, mod_ref, g_ref, b_ref, o_ref):
    d = D_MODEL
    gate2 = mod_ref[0][:, 5 * d:6 * d]
    gates = gates_ref[...]
    h = gates[:, 0:1] * pair_ref[:, 0:d] + gates[:, 1:2] * pair_ref[:, d:2 * d]
    o_ref[0] = _layer_norm(ALPHA * x1_ref[0] + gate2 * h, g_ref[...], b_ref[...])


def _combine(x1, pairs, gates, mod3, g, b):
    bsz, seq, d = x1.shape
    ts = COMB_ROWS
    ns = seq // ts
    return pl.pallas_call(
        _combine_kernel,
        grid=(bsz, ns),
        in_specs=[
            pl.BlockSpec((1, ts, d), lambda i, j: (i, j, 0)),
            pl.BlockSpec((ts, 2 * d), lambda i, j: (i * ns + j, 0)),
            pl.BlockSpec((ts, TOP_K), lambda i, j: (i * ns + j, 0)),
            pl.BlockSpec((1, 1, 6 * d), lambda i, j: (i, 0, 0)),
            pl.BlockSpec((1, d), lambda i, j: (0, 0)),
            pl.BlockSpec((1, d), lambda i, j: (0, 0)),
        ],
        out_specs=pl.BlockSpec((1, ts, d), lambda i, j: (i, j, 0)),
        out_shape=jax.ShapeDtypeStruct((bsz, seq, d), F32),
        compiler_params=pltpu.CompilerParams(dimension_semantics=("arbitrary", "arbitrary")),
    )(x1, pairs, gates, mod3, g, b)


def _dispatch_plan(ri, cnt):
    n_tok = ri.shape[1]
    n_blocks = (n_tok * TOP_K + N_EXPERTS * (MOE_ROWS - 1) + MOE_ROWS - 1) // MOE_ROWS
    counts = cnt[:, 0].astype(jnp.int32)
    padded = (counts + MOE_ROWS - 1) // MOE_ROWS * MOE_ROWS
    pad_end = jnp.cumsum(padded)
    pad_start = pad_end - padded
    dest_a = pad_start[ri[0]] + ri[2]
    dest_b = pad_start[ri[1]] + ri[3]
    block_start = jnp.arange(n_blocks, dtype=jnp.int32) * MOE_ROWS
    block_e = jnp.minimum(jnp.searchsorted(pad_end, block_start, side='right'),
                          N_EXPERTS - 1).astype(jnp.int32)
    n_used = (pad_end[-1:] // MOE_ROWS).astype(jnp.int32)
    return dest_a, dest_b, block_e, n_used, n_blocks * MOE_ROWS


def kernel(x, c, w_ada, b_ada, w_in, sc_conv, cv_conv, cv_conv_b, cv_ln_g, cv_ln_b, sg_ln_g, sg_ln_b, sg_w, sg_b, w_branch, w_gate, b_gate, w_o, ln1_g, ln1_b, w_router, b_router, w1, w3, w2, ln2_g, ln2_b):
    bsz, seq, d = x.shape
    n_tok = bsz * seq
    mod = _ada(c, w_ada, b_ada)
    wrt = w_router.T
    brc = b_router.reshape(N_EXPERTS, 1)
    for l in range(DEPTH):
        mod3 = mod[l].reshape(bsz, 1, 6 * d)
        x1, u2, ri, rf, cnt = _mixer(
            x, mod3, w_in[l].astype(BF16), sc_conv[l], cv_conv[l],
            cv_conv_b[l].reshape(1, -1), cv_ln_g[l].reshape(1, -1), cv_ln_b[l].reshape(1, -1),
            sg_ln_g[l].reshape(1, -1), sg_ln_b[l].reshape(1, -1), sg_w[l], sg_b[l].T,
            w_branch[l].astype(BF16), w_gate[l].reshape(d, 3 * d).astype(BF16), b_gate[l],
            w_o[l].astype(BF16), ln1_g[l].reshape(1, -1), ln1_b[l].reshape(1, -1), wrt, brc)
        dest_a, dest_b, block_e, n_used, n_rows = _dispatch_plan(ri, cnt)
        buf = _sc_scatter_rows(u2.reshape(n_tok, d), dest_a, dest_b, n_rows)
        obuf = _experts(buf, block_e, n_used, w1[l].astype(BF16), w3[l].astype(BF16),
                        w2[l].astype(BF16))
        pair_idx = jnp.stack([dest_a, dest_b], axis=1).reshape(n_tok * TOP_K)
        pairs = _sc_gather_rows(obuf, pair_idx).reshape(n_tok, TOP_K * d)
        x = _combine(x1, pairs, rf[0:TOP_K].T, mod3, ln2_g[l].reshape(1, -1),
                     ln2_b[l].reshape(1, -1))
    return x
```

```python
import functools

import jax
import jax.numpy as jnp
from jax import lax
from jax.experimental import pallas as pl
from jax.experimental.pallas import tpu as pltpu
from jax.experimental.pallas import tpu_sc as plsc

D_MODEL = 1024
DEPTH = 2
W_BRANCH = 1024
SC_KERNEL = 3
CV_KERNEL = 31
CHUNK = 128
SG_HEADS = 8
N_EXPERTS = 16
N_GROUPS = 4
EXPERTS_PER_GROUP = N_EXPERTS // N_GROUPS
TOP_K = 2
D_EXPERT = 512
ALPHA = (2.0 * DEPTH) ** 0.25
LN_EPS = 1e-5

F32 = jnp.float32
BF16 = jnp.bfloat16

V7X_VMEM_BYTES = 64 * 1024 * 1024
MIXER_VMEM_LIMIT = V7X_VMEM_BYTES - 6 * 1024 * 1024
SUBLANES = 8
LANES = 128

MIX_ROWS = 256
SC_HALO = SUBLANES
CV_HALO = 32
CONV_ROWS = 128
CONV_COLS = 128
MOE_ROWS = 256
COMB_ROWS = 512
SC_WINDOW = 32


def _dot(a, b):
    return jnp.dot(a, b, preferred_element_type=F32)


def _layer_norm(v, g, b):
    mu = jnp.mean(v, axis=-1, keepdims=True)
    vc = v - mu
    var = jnp.mean(vc * vc, axis=-1, keepdims=True)
    return vc * lax.rsqrt(var + LN_EPS) * g + b


def _ada_kernel(c_ref, w_ref, b_ref, o_ref):
    c = c_ref[...]
    c_act = c * jax.nn.sigmoid(c)
    o_ref[0] = jnp.dot(c_act, w_ref[0], preferred_element_type=F32,
                       precision=lax.Precision.HIGHEST) + b_ref[0]


def _ada(c, w_ada, b_ada):
    bsz, d = c.shape
    n = w_ada.shape[-1]
    tn = 1536
    return pl.pallas_call(
        _ada_kernel,
        grid=(DEPTH, n // tn),
        in_specs=[
            pl.BlockSpec((bsz, d), lambda l, j: (0, 0)),
            pl.BlockSpec((1, d, tn), lambda l, j: (l, 0, j)),
            pl.BlockSpec((1, 1, tn), lambda l, j: (l, 0, j)),
        ],
        out_specs=pl.BlockSpec((1, bsz, tn), lambda l, j: (l, 0, j)),
        out_shape=jax.ShapeDtypeStruct((DEPTH, bsz, n), F32),
    )(c, w_ada, b_ada.reshape(DEPTH, 1, n))


def _top2_of4(rows):
    m1 = rows[0]
    i1 = jnp.zeros(rows[0].shape, jnp.int32)
    for k in range(1, 4):
        gt = rows[k] > m1
        m1 = jnp.where(gt, rows[k], m1)
        i1 = jnp.where(gt, k, i1)
    m2 = jnp.full(rows[0].shape, -jnp.inf, F32)
    i2 = jnp.zeros(rows[0].shape, jnp.int32)
    for k in range(4):
        cand = jnp.where(i1 == k, -jnp.inf, rows[k])
        gt = cand > m2
        m2 = jnp.where(gt, cand, m2)
        i2 = jnp.where(gt, k, i2)
    return m1, i1, m2, i2


def _mixer_kernel(x_ref, mod_ref, w_in_ref, scw_ref, cvw_ref, cvb_ref, cvg_ref, cvbeta_ref,
                  sgg_ref, sgbeta_ref, sgw_ref, sgbt_ref, wbr_ref, wg_ref, bg_ref, wo_ref,
                  ln1g_ref, ln1b_ref, wrt_ref, brc_ref,
                  x1_ref, u2_ref, ri_ref, rf_ref, cnt_ref,
                  qbuf, cvbuf, ybuf, base_ref):
    ts = MIX_ROWS
    d = D_MODEL
    wb = W_BRANCH
    first_tile = pl.program_id(1) == 0

    @pl.when(jnp.logical_and(pl.program_id(0) == 0, first_tile))
    def _():
        base_ref[...] = jnp.zeros_like(base_ref)

    @pl.when(first_tile)
    def _():
        qbuf[0:SC_HALO, :] = jnp.zeros((SC_HALO, wb), F32)
        cvbuf[0:CV_HALO, :] = jnp.zeros((CV_HALO, wb), F32)

    x = x_ref[0]
    mod = mod_ref[0]
    shift1, scale1, gate1 = mod[:, 0:d], mod[:, d:2 * d], mod[:, 2 * d:3 * d]
    shift2, scale2 = mod[:, 3 * d:4 * d], mod[:, 4 * d:5 * d]
    ub = (x * (1.0 + scale1) + shift1).astype(BF16)

    pa = _dot(ub, w_in_ref[:, 0:3 * wb])
    qbuf[SC_HALO:SC_HALO + ts, :] = pa[:, wb:2 * wb] * pa[:, 2 * wb:3 * wb]
    conv = scw_ref[SC_KERNEL - 1:SC_KERNEL, :] * qbuf[SC_HALO:SC_HALO + ts, :]
    for k in range(SC_KERNEL - 1):
        off = SC_HALO - (SC_KERNEL - 1) + k
        conv = conv + scw_ref[k:k + 1, :] * qbuf[off:off + ts, :]
    ybuf[0] = (pa[:, 0:wb] * conv).astype(BF16)
    qbuf[0:SC_HALO, :] = qbuf[ts:ts + SC_HALO, :]

    pb = _dot(ub, w_in_ref[:, 3 * wb:5 * wb])
    cvbuf[CV_HALO:CV_HALO + ts, :] = pb[:, 0:wb] * jax.nn.sigmoid(pb[:, wb:2 * wb])
    for r0 in range(0, ts, CONV_ROWS):
        for c0 in range(0, wb, CONV_COLS):
            cs = slice(c0, c0 + CONV_COLS)
            acc = None
            for r in range(SUBLANES):
                lead = SUBLANES if r else 0
                part = None
                for m in range((CV_KERNEL - 1 - r) // SUBLANES + 1):
                    k = CV_KERNEL - 1 - (SUBLANES * m + r)
                    start = CV_HALO + r0 - lead - SUBLANES * m
                    term = cvw_ref[k:k + 1, cs] * cvbuf[start:start + lead + CONV_ROWS, cs]
                    part = term if part is None else part + term
                part = part[lead - r:lead - r + CONV_ROWS]
                acc = part if acc is None else acc + part
            qbuf[SC_HALO + r0:SC_HALO + r0 + CONV_ROWS, cs] = acc
    cvbuf[0:CV_HALO, :] = cvbuf[ts:ts + CV_HALO, :]
    cv = _layer_norm(qbuf[SC_HALO:SC_HALO + ts, :] + cvb_ref[...], cvg_ref[...], cvbeta_ref[...])
    ybuf[1] = (cv * jax.nn.sigmoid(cv)).astype(BF16)

    pc = _dot(ub, w_in_ref[:, 5 * wb:7 * wb])
    gu = jax.nn.gelu(pc[:, 0:wb])
    gv = _layer_norm(jax.nn.gelu(pc[:, wb:2 * wb]), sgg_ref[...], sgbeta_ref[...]).astype(BF16)
    row = lax.broadcasted_iota(jnp.int32, (CHUNK, CHUNK), 0)
    col = lax.broadcasted_iota(jnp.int32, (CHUNK, CHUNK), 1)
    hd = wb // SG_HEADS
    for h in range(SG_HEADS):
        wm = jnp.where(row >= col, sgw_ref[h], 0.0).astype(BF16)
        bias = sgbt_ref[:, h:h + 1]
        for n in range(ts // CHUNK):
            rs = slice(n * CHUNK, (n + 1) * CHUNK)
            cs = slice(h * hd, (h + 1) * hd)
            mixed = _dot(wm, gv[rs, cs]) + bias
            ybuf[2, rs, cs] = (gu[rs, cs] * mixed).astype(BF16)

    merged = jnp.zeros((ts, d), F32)
    for n in range(3):
        z = _dot(ybuf[n], wbr_ref[n])
        g = jax.nn.sigmoid(_dot(ub, wg_ref[:, n * d:(n + 1) * d]) + bg_ref[n:n + 1, :])
        merged = merged + g * z
    hmix = _dot(merged.astype(BF16), wo_ref[...])
    x1 = _layer_norm(ALPHA * x + gate1 * hmix, ln1g_ref[...], ln1b_ref[...])
    x1_ref[0] = x1
    u2 = x1 * (1.0 + scale2) + shift2
    u2_ref[0] = u2

    logits = lax.dot_general(wrt_ref[...], u2, (((1,), (1,)), ((), ())),
                             preferred_element_type=F32,
                             precision=lax.Precision.HIGHEST)
    mx = jnp.max(logits, axis=0, keepdims=True)
    ex = jnp.exp(logits - mx)
    scores = ex / jnp.sum(ex, axis=0, keepdims=True)
    sel = scores + brc_ref[...]
    tops = []
    for g in range(N_GROUPS):
        rows = [sel[g * EXPERTS_PER_GROUP + k:g * EXPERTS_PER_GROUP + k + 1, :]
                for k in range(EXPERTS_PER_GROUP)]
        tops.append(_top2_of4(rows))
    best = tops[0][0] + tops[0][2]
    g_idx = jnp.zeros(best.shape, jnp.int32)
    loc1, loc2 = tops[0][1], tops[0][3]
    for g in range(1, N_GROUPS):
        gs = tops[g][0] + tops[g][2]
        gt = gs > best
        best = jnp.where(gt, gs, best)
        g_idx = jnp.where(gt, g, g_idx)
        loc1 = jnp.where(gt, tops[g][1], loc1)
        loc2 = jnp.where(gt, tops[g][3], loc2)
    e0 = g_idx * EXPERTS_PER_GROUP + loc1
    e1 = g_idx * EXPERTS_PER_GROUP + loc2
    erow = lax.broadcasted_iota(jnp.int32, (N_EXPERTS, ts), 0)
    is0 = erow == e0
    is1 = erow == e1
    s0 = jnp.sum(jnp.where(is0, scores, 0.0), axis=0, keepdims=True)
    s1 = jnp.sum(jnp.where(is1, scores, 0.0), axis=0, keepdims=True)
    ssum = s0 + s1
    onehot = jnp.logical_or(is0, is1).astype(BF16)
    src = lax.broadcasted_iota(jnp.int32, (ts, ts), 0)
    dst = lax.broadcasted_iota(jnp.int32, (ts, ts), 1)
    earlier = (src < dst).astype(BF16)
    prior = _dot(onehot, earlier) + base_ref[:, 0:1]
    r0 = jnp.sum(jnp.where(is0, prior, 0.0), axis=0, keepdims=True)
    r1 = jnp.sum(jnp.where(is1, prior, 0.0), axis=0, keepdims=True)
    base_ref[...] = base_ref[...] + jnp.sum(onehot.astype(F32), axis=1, keepdims=True)

    zi = jnp.zeros((SUBLANES - 4, ts), jnp.int32)
    ri_ref[...] = jnp.concatenate([e0, e1, r0.astype(jnp.int32), r1.astype(jnp.int32), zi], axis=0)
    zf = jnp.zeros((SUBLANES - 2, ts), F32)
    rf_ref[...] = jnp.concatenate([s0 / ssum, s1 / ssum, zf], axis=0)
    cnt_ref[...] = base_ref[...]


def _mixer(x, mod3, w_in, scw, cvw, cvb, cvg, cvbeta, sgg, sgbeta, sgw, sgbt, wbr, wg, bg, wo,
           ln1g, ln1b, wrt, brc):
    bsz, seq, d = x.shape
    ts = MIX_ROWS
    ns = seq // ts
    n_tok = bsz * seq

    def const(shape):
        zeros = (0,) * len(shape)
        return pl.BlockSpec(shape, lambda b, j: zeros, pipeline_mode=pl.Buffered(1))

    in_specs = [
        pl.BlockSpec((1, ts, d), lambda b, j: (b, j, 0)),
        pl.BlockSpec((1, 1, 6 * d), lambda b, j: (b, 0, 0)),
        const(w_in.shape), const(scw.shape), const(cvw.shape), const(cvb.shape),
        const(cvg.shape), const(cvbeta.shape), const(sgg.shape), const(sgbeta.shape),
        const(sgw.shape), const(sgbt.shape), const(wbr.shape), const(wg.shape), const(bg.shape),
        const(wo.shape), const(ln1g.shape), const(ln1b.shape), const(wrt.shape), const(brc.shape),
    ]
    out_specs = [
        pl.BlockSpec((1, ts, d), lambda b, j: (b, j, 0)),
        pl.BlockSpec((1, ts, d), lambda b, j: (b, j, 0)),
        pl.BlockSpec((SUBLANES, ts), lambda b, j: (0, b * ns + j)),
        pl.BlockSpec((SUBLANES, ts), lambda b, j: (0, b * ns + j)),
        pl.BlockSpec((N_EXPERTS, LANES), lambda b, j: (0, 0)),
    ]
    out_shape = [
        jax.ShapeDtypeStruct((bsz, seq, d), F32),
        jax.ShapeDtypeStruct((bsz, seq, d), F32),
        jax.ShapeDtypeStruct((SUBLANES, n_tok), jnp.int32),
        jax.ShapeDtypeStruct((SUBLANES, n_tok), F32),
        jax.ShapeDtypeStruct((N_EXPERTS, LANES), F32),
    ]
    return pl.pallas_call(
        _mixer_kernel,
        grid=(bsz, ns),
        in_specs=in_specs,
        out_specs=out_specs,
        out_shape=out_shape,
        scratch_shapes=[
            pltpu.VMEM((SC_HALO + ts, W_BRANCH), F32),
            pltpu.VMEM((CV_HALO + ts, W_BRANCH), F32),
            pltpu.VMEM((3, ts, W_BRANCH), BF16),
            pltpu.VMEM((N_EXPERTS, LANES), F32),
        ],
        compiler_params=pltpu.CompilerParams(
            dimension_semantics=("arbitrary", "arbitrary"),
            vmem_limit_bytes=MIXER_VMEM_LIMIT),
    )(x, mod3, w_in, scw, cvw, cvb, cvg, cvbeta, sgg, sgbeta, sgw, sgbt, wbr, wg, bg, wo,
      ln1g, ln1b, wrt, brc)


def _sc_workers():
    info = plsc.get_sparse_core_info()
    return info.num_cores, info.num_cores * info.num_subcores


def _sc_scatter_rows(rows, dest_a, dest_b, n_out):
    n, d = rows.shape
    nc, nw = _sc_workers()
    per_w = n // nw
    n_win = per_w // SC_WINDOW
    ia = dest_a.reshape(nw, n_win, SC_WINDOW)
    ib = dest_b.reshape(nw, n_win, SC_WINDOW)
    mesh = plsc.VectorSubcoreMesh(core_axis_name="c", subcore_axis_name="s")

    @functools.partial(
        pl.kernel, mesh=mesh,
        out_type=jax.ShapeDtypeStruct((n_out, d), rows.dtype),
        scratch_types=[
            pltpu.VMEM((n_win, SC_WINDOW), jnp.int32),
            pltpu.VMEM((n_win, SC_WINDOW), jnp.int32),
            pltpu.VMEM((SC_WINDOW, d), rows.dtype),
        ],
    )
    def scatter(rows_hbm, ia_hbm, ib_hbm, out_hbm, ia_v, ib_v, rows_v):
        wid = lax.axis_index("s") * nc + lax.axis_index("c")
        pltpu.sync_copy(ia_hbm.at[wid], ia_v)
        pltpu.sync_copy(ib_hbm.at[wid], ib_v)
        base = wid * per_w

        @pl.loop(0, n_win)
        def _(j):
            pltpu.sync_copy(rows_hbm.at[pl.ds(base + j * SC_WINDOW, SC_WINDOW)], rows_v)
            pltpu.sync_copy(rows_v, out_hbm.at[ia_v.at[j]])
            pltpu.sync_copy(rows_v, out_hbm.at[ib_v.at[j]])

    return scatter(rows, ia, ib)


def _sc_gather_rows(table, idx):
    n = idx.shape[0]
    d = table.shape[1]
    nc, nw = _sc_workers()
    per_w = n // nw
    n_win = per_w // SC_WINDOW
    idx3 = idx.reshape(nw, n_win, SC_WINDOW)
    mesh = plsc.VectorSubcoreMesh(core_axis_name="c", subcore_axis_name="s")

    @functools.partial(
        pl.kernel, mesh=mesh,
        out_type=jax.ShapeDtypeStruct((n, d), table.dtype),
        scratch_types=[
            pltpu.VMEM((n_win, SC_WINDOW), jnp.int32),
            pltpu.VMEM((SC_WINDOW, d), table.dtype),
        ],
    )
    def gather(table_hbm, idx_hbm, out_hbm, idx_v, rows_v):
        wid = lax.axis_index("s") * nc + lax.axis_index("c")
        pltpu.sync_copy(idx_hbm.at[wid], idx_v)
        base = wid * per_w

        @pl.loop(0, n_win)
        def _(j):
            pltpu.sync_copy(table_hbm.at[idx_v.at[j]], rows_v)
            pltpu.sync_copy(rows_v, out_hbm.at[pl.ds(base + j * SC_WINDOW, SC_WINDOW)])

    return gather(table, idx3)


def _expert_kernel(be_ref, nused_ref, x_ref, w1_ref, w3_ref, w2_ref, o_ref):
    @pl.when(pl.program_id(0) < nused_ref[0])
    def _():
        xb = x_ref[...].astype(BF16)
        a = _dot(xb, w1_ref[0])
        h = a * jax.nn.sigmoid(a) * _dot(xb, w3_ref[0])
        o_ref[...] = _dot(h.astype(BF16), w2_ref[0])


def _experts(buf, block_e, n_used, w1, w3, w2):
    n_rows, d = buf.shape
    nb = n_rows // MOE_ROWS
    fe = w1.shape[-1]

    def row_map(i, be, nu):
        return (jnp.minimum(i, nu[0] - 1), 0)

    grid_spec = pltpu.PrefetchScalarGridSpec(
        num_scalar_prefetch=2,
        grid=(nb,),
        in_specs=[
            pl.BlockSpec((MOE_ROWS, d), row_map),
            pl.BlockSpec((1, d, fe), lambda i, be, nu: (be[i], 0, 0)),
            pl.BlockSpec((1, d, fe), lambda i, be, nu: (be[i], 0, 0)),
            pl.BlockSpec((1, fe, d), lambda i, be, nu: (be[i], 0, 0)),
        ],
        out_specs=pl.BlockSpec((MOE_ROWS, d), row_map),
    )
    return pl.pallas_call(
        _expert_kernel,
        grid_spec=grid_spec,
        out_shape=jax.ShapeDtypeStruct((n_rows, d), F32),
        compiler_params=pltpu.CompilerParams(dimension_semantics=("arbitrary",)),
    )(block_e, n_used, buf, w1, w3, w2)


def _combine_kernel(x1_ref, ya_ref, yb_ref, gates_ref, mod_ref, g_ref, b_ref, o_ref):
    d = D_MODEL
    gate2 = mod_ref[0][:, 5 * d:6 * d]
    gates = gates_ref[...]
    h = gates[:, 0:1] * ya_ref[0] + gates[:, 1:2] * yb_ref[0]
    o_ref[0] = _layer_norm(ALPHA * x1_ref[0] + gate2 * h, g_ref[...], b_ref[...])


def _combine(x1, pairs, gates, mod3, g, b):
    bsz, seq, d = x1.shape
    ts = COMB_ROWS
    ns = seq // ts
    return pl.pallas_call(
        _combine_kernel,
        grid=(bsz, ns),
        in_specs=[
            pl.BlockSpec((1, ts, d), lambda i, j: (i, j, 0)),
            pl.BlockSpec((1, ts, d), lambda i, j: (0, i * ns + j, 0)),
            pl.BlockSpec((1, ts, d), lambda i, j: (1, i * ns + j, 0)),
            pl.BlockSpec((ts, TOP_K), lambda i, j: (i * ns + j, 0)),
            pl.BlockSpec((1, 1, 6 * d), lambda i, j: (i, 0, 0)),
            pl.BlockSpec((1, d), lambda i, j: (0, 0)),
            pl.BlockSpec((1, d), lambda i, j: (0, 0)),
        ],
        out_specs=pl.BlockSpec((1, ts, d), lambda i, j: (i, j, 0)),
        out_shape=jax.ShapeDtypeStruct((bsz, seq, d), F32),
        compiler_params=pltpu.CompilerParams(dimension_semantics=("arbitrary", "arbitrary")),
    )(x1, pairs, pairs, gates, mod3, g, b)


def _dispatch_plan(ri, cnt):
    n_tok = ri.shape[1]
    n_blocks = (n_tok * TOP_K + N_EXPERTS * (MOE_ROWS - 1) + MOE_ROWS - 1) // MOE_ROWS
    counts = cnt[:, 0].astype(jnp.int32)
    padded = (counts + MOE_ROWS - 1) // MOE_ROWS * MOE_ROWS
    pad_end = jnp.cumsum(padded)
    pad_start = pad_end - padded
    dest_a = pad_start[ri[0]] + ri[2]
    dest_b = pad_start[ri[1]] + ri[3]
    block_start = jnp.arange(n_blocks, dtype=jnp.int32) * MOE_ROWS
    block_e = jnp.minimum(jnp.sum(block_start[:, None] >= pad_end[None, :], axis=1),
                          N_EXPERTS - 1).astype(jnp.int32)
    n_used = (pad_end[-1:] // MOE_ROWS).astype(jnp.int32)
    return dest_a, dest_b, block_e, n_used, n_blocks * MOE_ROWS


def kernel(x, c, w_ada, b_ada, w_in, sc_conv, cv_conv, cv_conv_b, cv_ln_g, cv_ln_b, sg_ln_g, sg_ln_b, sg_w, sg_b, w_branch, w_gate, b_gate, w_o, ln1_g, ln1_b, w_router, b_router, w1, w3, w2, ln2_g, ln2_b):
    bsz, seq, d = x.shape
    n_tok = bsz * seq
    mod = _ada(c, w_ada, b_ada)
    wrt = w_router.T
    brc = b_router.reshape(N_EXPERTS, 1)
    for l in range(DEPTH):
        mod3 = mod[l].reshape(bsz, 1, 6 * d)
        x1, u2, ri, rf, cnt = _mixer(
            x, mod3, w_in[l].astype(BF16), sc_conv[l], cv_conv[l],
            cv_conv_b[l].reshape(1, -1), cv_ln_g[l].reshape(1, -1), cv_ln_b[l].reshape(1, -1),
            sg_ln_g[l].reshape(1, -1), sg_ln_b[l].reshape(1, -1), sg_w[l], sg_b[l].T,
            w_branch[l].astype(BF16), w_gate[l].reshape(d, 3 * d).astype(BF16), b_gate[l],
            w_o[l].astype(BF16), ln1_g[l].reshape(1, -1), ln1_b[l].reshape(1, -1), wrt, brc)
        dest_a, dest_b, block_e, n_used, n_rows = _dispatch_plan(ri, cnt)
        buf = _sc_scatter_rows(u2.reshape(n_tok, d), dest_a, dest_b, n_rows)
        obuf = _experts(buf, block_e, n_used, w1[l].astype(BF16), w3[l].astype(BF16),
                        w2[l].astype(BF16))
        pair_idx = jnp.concatenate([dest_a, dest_b])
        pairs = _sc_gather_rows(obuf, pair_idx).reshape(TOP_K, n_tok, d)
        x = _combine(x1, pairs, rf[0:TOP_K].T, mod3, ln2_g[l].reshape(1, -1),
                     ln2_b[l].reshape(1, -1))
    return x
```

```python
import functools

import jax
import jax.numpy as jnp
from jax import lax
from jax.experimental import pallas as pl
from jax.experimental.pallas import tpu as pltpu
from jax.experimental.pallas import tpu_sc as plsc

D_MODEL = 1024
DEPTH = 2
W_BRANCH = 1024
SC_KERNEL = 3
CV_KERNEL = 31
CHUNK = 128
SG_HEADS = 8
N_EXPERTS = 16
N_GROUPS = 4
EXPERTS_PER_GROUP = N_EXPERTS // N_GROUPS
TOP_K = 2
D_EXPERT = 512
ALPHA = (2.0 * DEPTH) ** 0.25
LN_EPS = 1e-5

F32 = jnp.float32
BF16 = jnp.bfloat16

V7X_VMEM_BYTES = 64 * 1024 * 1024
MIXER_VMEM_LIMIT = V7X_VMEM_BYTES - 6 * 1024 * 1024
EXPERT_VMEM_LIMIT = V7X_VMEM_BYTES // 2
SUBLANES = 8
LANES = 128

MIX_ROWS = 256
SC_HALO = SUBLANES
CV_HALO = 32
CONV_ROWS = 128
TIE_LAG = 2
CONV_COLS = 128
PROJ_COLS = 512
STAGE_ELEMS = 128 * 1024
MOE_ROWS = 256
COMB_ROWS = 512
SC_WINDOW = 32


def _dot(a, b):
    return jnp.dot(a, b, preferred_element_type=F32)


def _layer_norm(v, g, b):
    mu = jnp.mean(v, axis=-1, keepdims=True)
    vc = v - mu
    var = jnp.mean(vc * vc, axis=-1, keepdims=True)
    return vc * lax.rsqrt(var + LN_EPS) * g + b


def _ada_kernel(c_ref, w_ref, b_ref, o_ref):
    c = c_ref[...]
    c_act = c * jax.nn.sigmoid(c)
    o_ref[0] = jnp.dot(c_act, w_ref[0], preferred_element_type=F32,
                       precision=lax.Precision.HIGHEST) + b_ref[0]


def _ada(c, w_ada, b_ada):
    bsz, d = c.shape
    n = w_ada.shape[-1]
    tn = 1536
    return pl.pallas_call(
        _ada_kernel,
        grid=(DEPTH, n // tn),
        in_specs=[
            pl.BlockSpec((bsz, d), lambda l, j: (0, 0)),
            pl.BlockSpec((1, d, tn), lambda l, j: (l, 0, j)),
            pl.BlockSpec((1, 1, tn), lambda l, j: (l, 0, j)),
        ],
        out_specs=pl.BlockSpec((1, bsz, tn), lambda l, j: (l, 0, j)),
        out_shape=jax.ShapeDtypeStruct((DEPTH, bsz, n), F32),
    )(c, w_ada, b_ada.reshape(DEPTH, 1, n))


def _top2_of4(rows):
    m1 = rows[0]
    i1 = jnp.zeros(rows[0].shape, jnp.int32)
    for k in range(1, 4):
        gt = rows[k] > m1
        m1 = jnp.where(gt, rows[k], m1)
        i1 = jnp.where(gt, k, i1)
    m2 = jnp.full(rows[0].shape, -jnp.inf, F32)
    i2 = jnp.zeros(rows[0].shape, jnp.int32)
    for k in range(4):
        cand = jnp.where(i1 == k, -jnp.inf, rows[k])
        gt = cand > m2
        m2 = jnp.where(gt, cand, m2)
        i2 = jnp.where(gt, k, i2)
    return m1, i1, m2, i2


def _zero_after(v):
    u = lax.bitcast_convert_type(v, jnp.uint32)
    u = lax.shift_right_logical(lax.shift_right_logical(u, jnp.uint32(16)), jnp.uint32(16))
    return lax.bitcast_convert_type(u, F32)


def _conv31_chunk(cvw_ref, cvbuf, cvout, r0, c0, tie):
    cs = slice(c0, c0 + CONV_COLS)
    acc = None
    for r in range(SUBLANES):
        lead = SUBLANES if r else 0
        part = None
        for m in range((CV_KERNEL - 1 - r) // SUBLANES + 1):
            k = CV_KERNEL - 1 - (SUBLANES * m + r)
            start = CV_HALO + r0 - lead - SUBLANES * m
            w_row = cvw_ref[k:k + 1, cs]
            if tie is not None and acc is None and part is None:
                w_row = w_row + tie
            term = w_row * cvbuf[start:start + lead + CONV_ROWS, cs]
            part = term if part is None else part + term
        part = part[lead - r:lead - r + CONV_ROWS]
        acc = part if acc is None else acc + part
    cvout[r0:r0 + CONV_ROWS, cs] = acc


def _stage_rows(cols):
    rows = SUBLANES
    while 2 * rows * cols <= STAGE_ELEMS:
        rows *= 2
    return rows


def _stage_weights_bf16(pairs, stages, sems):
    stage_ids = {cols: n for n, cols in enumerate(stages)}
    used = {cols: 0 for cols in stages}
    chunks = []
    for src, dst in pairs:
        rows, cols = src.shape
        stage = stages[cols]
        chunk_rows = stage.shape[1]
        for r0 in range(0, rows, chunk_rows):
            slot = used[cols] % 2
            used[cols] += 1
            copy = pltpu.make_async_copy(src.at[pl.ds(r0, chunk_rows), :], stage.at[slot],
                                         sems.at[stage_ids[cols], slot])
            chunks.append((copy, stage, slot, dst, r0, chunk_rows))
    chunks[0][0].start()
    for i, (copy, stage, slot, dst, r0, chunk_rows) in enumerate(chunks):
        if i + 1 < len(chunks):
            chunks[i + 1][0].start()
        copy.wait()
        dst[r0:r0 + chunk_rows, :] = stage[slot].astype(BF16)


def _mixer_kernel(layer, x_ref, mod_ref, w_in_hbm, scw_ref, cvw_ref, cvb_ref, cvg_ref,
                  cvbeta_ref, sgg_ref, sgbeta_ref, sgw_ref, sgbt_ref, wbr_hbm, wg_hbm, bg_ref,
                  wo_hbm, ln1g_ref, ln1b_ref, wrt_ref, brc_ref,
                  x1_ref, u2_ref, ri_ref, rf_ref, cnt_ref,
                  w_in_ref, wbr_ref, wg_ref, wo_ref, stage_in, stage_g, stage_d, wsems,
                  qbuf, cvbuf, cvout, ybuf, pa_buf, pc_buf, gl_buf, base_ref):
    ts = MIX_ROWS
    d = D_MODEL
    wb = W_BRANCH
    first_tile = pl.program_id(1) == 0

    @pl.when(jnp.logical_and(pl.program_id(0) == 0, first_tile))
    def _():
        base_ref[...] = jnp.zeros_like(base_ref)
        pairs = [(w_in_hbm.at[layer], w_in_ref), (wg_hbm.at[layer], wg_ref),
                 (wo_hbm.at[layer], wo_ref)]
        pairs += [(wbr_hbm.at[layer, n], wbr_ref.at[n]) for n in range(3)]
        stages = {w_in_ref.shape[1]: stage_in, wg_ref.shape[1]: stage_g, wo_ref.shape[1]: stage_d}
        _stage_weights_bf16(pairs, stages, wsems)

    @pl.when(first_tile)
    def _():
        qbuf[0:SC_HALO, :] = jnp.zeros((SC_HALO, wb), F32)
        cvbuf[0:CV_HALO, :] = jnp.zeros((CV_HALO, wb), F32)

    x = x_ref[0]
    mod = mod_ref[0]
    shift1, scale1, gate1 = mod[:, 0:d], mod[:, d:2 * d], mod[:, 2 * d:3 * d]
    shift2, scale2 = mod[:, 3 * d:4 * d], mod[:, 4 * d:5 * d]
    ub = (x * (1.0 + scale1) + shift1).astype(BF16)

    for c0 in range(0, wb, PROJ_COLS):
        a = _dot(ub, w_in_ref[:, 3 * wb + c0:3 * wb + c0 + PROJ_COLS])
        g = _dot(ub, w_in_ref[:, 4 * wb + c0:4 * wb + c0 + PROJ_COLS])
        cvbuf[CV_HALO:CV_HALO + ts, c0:c0 + PROJ_COLS] = a * jax.nn.sigmoid(g)

    mxu_jobs = ([(pa_buf, w_in_ref, c0, c0) for c0 in range(0, 3 * wb, PROJ_COLS)]
                + [(pc_buf, w_in_ref, c0, 5 * wb + c0) for c0 in range(0, 2 * wb, PROJ_COLS)]
                + [(gl_buf, wg_ref, c0, c0) for c0 in range(0, 3 * d, PROJ_COLS)])
    conv_jobs = [(r0, c0) for c0 in range(0, wb, CONV_COLS) for r0 in range(0, ts, CONV_ROWS)]
    ties = []
    for i in range(max(len(mxu_jobs), len(conv_jobs))):
        if i < len(conv_jobs):
            tie = ties[i - TIE_LAG] if TIE_LAG <= i < TIE_LAG + len(ties) else None
            _conv31_chunk(cvw_ref, cvbuf, cvout, *conv_jobs[i], tie)
        if i < len(mxu_jobs):
            dst, w_ref, dc, wc = mxu_jobs[i]
            res = _dot(ub, w_ref[:, wc:wc + PROJ_COLS])
            dst[:, dc:dc + PROJ_COLS] = res
            ties.append(_zero_after(res[ts - 1:ts, PROJ_COLS - CONV_COLS:PROJ_COLS]))
    cvbuf[0:CV_HALO, :] = cvbuf[ts:ts + CV_HALO, :]

    qbuf[SC_HALO:SC_HALO + ts, :] = pa_buf[:, wb:2 * wb] * pa_buf[:, 2 * wb:3 * wb]
    conv = scw_ref[SC_KERNEL - 1:SC_KERNEL, :] * qbuf[SC_HALO:SC_HALO + ts, :]
    for k in range(SC_KERNEL - 1):
        off = SC_HALO - (SC_KERNEL - 1) + k
        conv = conv + scw_ref[k:k + 1, :] * qbuf[off:off + ts, :]
    ybuf[0] = (pa_buf[:, 0:wb] * conv).astype(BF16)
    qbuf[0:SC_HALO, :] = qbuf[ts:ts + SC_HALO, :]
    z0 = _dot(ybuf[0], wbr_ref[0])

    cv = _layer_norm(cvout[...] + cvb_ref[...], cvg_ref[...], cvbeta_ref[...])
    ybuf[1] = (cv * jax.nn.sigmoid(cv)).astype(BF16)
    z1 = _dot(ybuf[1], wbr_ref[1])

    gu = jax.nn.gelu(pc_buf[:, 0:wb])
    gv = _layer_norm(jax.nn.gelu(pc_buf[:, wb:2 * wb]), sgg_ref[...], sgbeta_ref[...]).astype(BF16)
    merged = (jax.nn.sigmoid(gl_buf[:, 0:d] + bg_ref[0:1, :]) * z0
              + jax.nn.sigmoid(gl_buf[:, d:2 * d] + bg_ref[1:2, :]) * z1)
    g2 = jax.nn.sigmoid(gl_buf[:, 2 * d:3 * d] + bg_ref[2:3, :])
    row = lax.broadcasted_iota(jnp.int32, (CHUNK, CHUNK), 0)
    col = lax.broadcasted_iota(jnp.int32, (CHUNK, CHUNK), 1)
    hd = wb // SG_HEADS
    for h in range(SG_HEADS):
        wm = jnp.where(row >= col, sgw_ref[h], 0.0).astype(BF16)
        bias = sgbt_ref[:, h:h + 1]
        for n in range(ts // CHUNK):
            rs = slice(n * CHUNK, (n + 1) * CHUNK)
            cs = slice(h * hd, (h + 1) * hd)
            mixed = _dot(wm, gv[rs, cs]) + bias
            ybuf[2, rs, cs] = (gu[rs, cs] * mixed).astype(BF16)

    merged = merged + g2 * _dot(ybuf[2], wbr_ref[2])
    hmix = _dot(merged.astype(BF16), wo_ref[...])
    x1 = _layer_norm(ALPHA * x + gate1 * hmix, ln1g_ref[...], ln1b_ref[...])
    x1_ref[0] = x1
    u2 = x1 * (1.0 + scale2) + shift2
    u2_ref[0] = u2

    logits = lax.dot_general(wrt_ref[...], u2, (((1,), (1,)), ((), ())),
                             preferred_element_type=F32,
                             precision=lax.Precision.HIGHEST)
    mx = jnp.max(logits, axis=0, keepdims=True)
    ex = jnp.exp(logits - mx)
    scores = ex / jnp.sum(ex, axis=0, keepdims=True)
    sel = scores + brc_ref[...]
    tops = []
    for g in range(N_GROUPS):
        rows = [sel[g * EXPERTS_PER_GROUP + k:g * EXPERTS_PER_GROUP + k + 1, :]
                for k in range(EXPERTS_PER_GROUP)]
        tops.append(_top2_of4(rows))
    best = tops[0][0] + tops[0][2]
    g_idx = jnp.zeros(best.shape, jnp.int32)
    loc1, loc2 = tops[0][1], tops[0][3]
    for g in range(1, N_GROUPS):
        gs = tops[g][0] + tops[g][2]
        gt = gs > best
        best = jnp.where(gt, gs, best)
        g_idx = jnp.where(gt, g, g_idx)
        loc1 = jnp.where(gt, tops[g][1], loc1)
        loc2 = jnp.where(gt, tops[g][3], loc2)
    e0 = g_idx * EXPERTS_PER_GROUP + loc1
    e1 = g_idx * EXPERTS_PER_GROUP + loc2
    erow = lax.broadcasted_iota(jnp.int32, (N_EXPERTS, ts), 0)
    is0 = erow == e0
    is1 = erow == e1
    s0 = jnp.sum(jnp.where(is0, scores, 0.0), axis=0, keepdims=True)
    s1 = jnp.sum(jnp.where(is1, scores, 0.0), axis=0, keepdims=True)
    ssum = s0 + s1
    onehot = jnp.logical_or(is0, is1).astype(BF16)
    src = lax.broadcasted_iota(jnp.int32, (ts, ts), 0)
    dst = lax.broadcasted_iota(jnp.int32, (ts, ts), 1)
    earlier = (src < dst).astype(BF16)
    prior = _dot(onehot, earlier) + base_ref[:, 0:1]
    r0 = jnp.sum(jnp.where(is0, prior, 0.0), axis=0, keepdims=True)
    r1 = jnp.sum(jnp.where(is1, prior, 0.0), axis=0, keepdims=True)
    base_ref[...] = base_ref[...] + jnp.sum(onehot.astype(F32), axis=1, keepdims=True)

    zi = jnp.zeros((SUBLANES - 4, ts), jnp.int32)
    ri_ref[...] = jnp.concatenate([e0, e1, r0.astype(jnp.int32), r1.astype(jnp.int32), zi], axis=0)
    zf = jnp.zeros((SUBLANES - 2, ts), F32)
    rf_ref[...] = jnp.concatenate([s0 / ssum, s1 / ssum, zf], axis=0)
    cnt_ref[...] = base_ref[...]


def _mixer(layer, x, mod3, w_in, scw, cvw, cvb, cvg, cvbeta, sgg, sgbeta, sgw, sgbt, wbr, wg, bg,
           wo, ln1g, ln1b, wrt, brc):
    bsz, seq, d = x.shape
    ts = MIX_ROWS
    ns = seq // ts
    n_tok = bsz * seq

    def const(shape):
        zeros = (0,) * len(shape)
        return pl.BlockSpec(shape, lambda b, j: zeros, pipeline_mode=pl.Buffered(1))

    hbm = pl.BlockSpec(memory_space=pl.ANY)
    in_specs = [
        pl.BlockSpec((1, ts, d), lambda b, j: (b, j, 0)),
        pl.BlockSpec((1, 1, 6 * d), lambda b, j: (b, 0, 0)),
        hbm, const(scw.shape), const(cvw.shape), const(cvb.shape),
        const(cvg.shape), const(cvbeta.shape), const(sgg.shape), const(sgbeta.shape),
        const(sgw.shape), const(sgbt.shape), hbm, hbm, const(bg.shape),
        hbm, const(ln1g.shape), const(ln1b.shape), const(wrt.shape), const(brc.shape),
    ]
    out_specs = [
        pl.BlockSpec((1, ts, d), lambda b, j: (b, j, 0)),
        pl.BlockSpec((1, ts, d), lambda b, j: (b, j, 0)),
        pl.BlockSpec((SUBLANES, ts), lambda b, j: (0, b * ns + j)),
        pl.BlockSpec((SUBLANES, ts), lambda b, j: (0, b * ns + j)),
        pl.BlockSpec((N_EXPERTS, LANES), lambda b, j: (0, 0)),
    ]
    out_shape = [
        jax.ShapeDtypeStruct((bsz, seq, d), F32),
        jax.ShapeDtypeStruct((bsz, seq, d), F32),
        jax.ShapeDtypeStruct((SUBLANES, n_tok), jnp.int32),
        jax.ShapeDtypeStruct((SUBLANES, n_tok), F32),
        jax.ShapeDtypeStruct((N_EXPERTS, LANES), F32),
    ]
    n_in = w_in.shape[-1]
    return pl.pallas_call(
        functools.partial(_mixer_kernel, layer),
        grid=(bsz, ns),
        in_specs=in_specs,
        out_specs=out_specs,
        out_shape=out_shape,
        scratch_shapes=[
            pltpu.VMEM((d, n_in), BF16),
            pltpu.VMEM((3, W_BRANCH, d), BF16),
            pltpu.VMEM((d, 3 * d), BF16),
            pltpu.VMEM((d, d), BF16),
            pltpu.VMEM((2, _stage_rows(n_in), n_in), F32),
            pltpu.VMEM((2, _stage_rows(3 * d), 3 * d), F32),
            pltpu.VMEM((2, _stage_rows(d), d), F32),
            pltpu.SemaphoreType.DMA((3, 2)),
            pltpu.VMEM((SC_HALO + ts, W_BRANCH), F32),
            pltpu.VMEM((CV_HALO + ts, W_BRANCH), F32),
            pltpu.VMEM((ts, W_BRANCH), F32),
            pltpu.VMEM((3, ts, W_BRANCH), BF16),
            pltpu.VMEM((ts, 3 * W_BRANCH), F32),
            pltpu.VMEM((ts, 2 * W_BRANCH), F32),
            pltpu.VMEM((ts, 3 * D_MODEL), F32),
            pltpu.VMEM((N_EXPERTS, LANES), F32),
        ],
        compiler_params=pltpu.CompilerParams(
            dimension_semantics=("arbitrary", "arbitrary"),
            vmem_limit_bytes=MIXER_VMEM_LIMIT),
    )(x, mod3, w_in, scw, cvw, cvb, cvg, cvbeta, sgg, sgbeta, sgw, sgbt, wbr, wg, bg, wo,
      ln1g, ln1b, wrt, brc)


def _sc_workers():
    info = plsc.get_sparse_core_info()
    return info.num_cores, info.num_cores * info.num_subcores


def _sc_scatter_rows(rows, dest_a, dest_b, n_out):
    n, d = rows.shape
    nc, nw = _sc_workers()
    per_w = n // nw
    n_win = per_w // SC_WINDOW
    ia = dest_a.reshape(nw, n_win, SC_WINDOW)
    ib = dest_b.reshape(nw, n_win, SC_WINDOW)
    mesh = plsc.VectorSubcoreMesh(core_axis_name="c", subcore_axis_name="s")

    @functools.partial(
        pl.kernel, mesh=mesh,
        out_type=jax.ShapeDtypeStruct((n_out, d), rows.dtype),
        scratch_types=[
            pltpu.VMEM((n_win, SC_WINDOW), jnp.int32),
            pltpu.VMEM((n_win, SC_WINDOW), jnp.int32),
            pltpu.VMEM((SC_WINDOW, d), rows.dtype),
        ],
    )
    def scatter(rows_hbm, ia_hbm, ib_hbm, out_hbm, ia_v, ib_v, rows_v):
        wid = lax.axis_index("s") * nc + lax.axis_index("c")
        pltpu.sync_copy(ia_hbm.at[wid], ia_v)
        pltpu.sync_copy(ib_hbm.at[wid], ib_v)
        base = wid * per_w

        @pl.loop(0, n_win)
        def _(j):
            pltpu.sync_copy(rows_hbm.at[pl.ds(base + j * SC_WINDOW, SC_WINDOW)], rows_v)
            pltpu.sync_copy(rows_v, out_hbm.at[ia_v.at[j]])
            pltpu.sync_copy(rows_v, out_hbm.at[ib_v.at[j]])

    return scatter(rows, ia, ib)


def _sc_gather_rows(table, idx):
    n = idx.shape[0]
    d = table.shape[1]
    nc, nw = _sc_workers()
    per_w = n // nw
    n_win = per_w // SC_WINDOW
    idx3 = idx.reshape(nw, n_win, SC_WINDOW)
    mesh = plsc.VectorSubcoreMesh(core_axis_name="c", subcore_axis_name="s")

    @functools.partial(
        pl.kernel, mesh=mesh,
        out_type=jax.ShapeDtypeStruct((n, d), table.dtype),
        scratch_types=[
            pltpu.VMEM((n_win, SC_WINDOW), jnp.int32),
            pltpu.VMEM((SC_WINDOW, d), table.dtype),
        ],
    )
    def gather(table_hbm, idx_hbm, out_hbm, idx_v, rows_v):
        wid = lax.axis_index("s") * nc + lax.axis_index("c")
        pltpu.sync_copy(idx_hbm.at[wid], idx_v)
        base = wid * per_w

        @pl.loop(0, n_win)
        def _(j):
            pltpu.sync_copy(table_hbm.at[idx_v.at[j]], rows_v)
            pltpu.sync_copy(rows_v, out_hbm.at[pl.ds(base + j * SC_WINDOW, SC_WINDOW)])

    return gather(table, idx3)


def _expert_kernel(be_ref, nused_ref, x_ref, w1_ref, w3_ref, w2_ref, o_ref, w1s, w3s, w2s):
    i = pl.program_id(0)
    new_expert = jnp.logical_or(i == 0, be_ref[i] != be_ref[jnp.maximum(i - 1, 0)])

    @pl.when(new_expert)
    def _():
        w1s[...] = w1_ref[0, 0].astype(BF16)
        w3s[...] = w3_ref[0, 0].astype(BF16)
        w2s[...] = w2_ref[0, 0].astype(BF16)

    @pl.when(i < nused_ref[0])
    def _():
        xb = x_ref[...].astype(BF16)
        a = _dot(xb, w1s[...])
        h = a * jax.nn.sigmoid(a) * _dot(xb, w3s[...])
        o_ref[...] = _dot(h.astype(BF16), w2s[...])


def _experts(layer, buf, block_e, n_used, w1, w3, w2):
    n_rows, d = buf.shape
    nb = n_rows // MOE_ROWS
    fe = w1.shape[-1]

    def row_map(i, be, nu):
        return (jnp.minimum(i, nu[0] - 1), 0)

    grid_spec = pltpu.PrefetchScalarGridSpec(
        num_scalar_prefetch=2,
        grid=(nb,),
        in_specs=[
            pl.BlockSpec((MOE_ROWS, d), row_map),
            pl.BlockSpec((1, 1, d, fe), lambda i, be, nu: (layer, be[i], 0, 0)),
            pl.BlockSpec((1, 1, d, fe), lambda i, be, nu: (layer, be[i], 0, 0)),
            pl.BlockSpec((1, 1, fe, d), lambda i, be, nu: (layer, be[i], 0, 0)),
        ],
        out_specs=pl.BlockSpec((MOE_ROWS, d), row_map),
        scratch_shapes=[pltpu.VMEM((d, fe), BF16), pltpu.VMEM((d, fe), BF16),
                        pltpu.VMEM((fe, d), BF16)],
    )
    return pl.pallas_call(
        _expert_kernel,
        grid_spec=grid_spec,
        out_shape=jax.ShapeDtypeStruct((n_rows, d), F32),
        compiler_params=pltpu.CompilerParams(dimension_semantics=("arbitrary",),
                                             vmem_limit_bytes=EXPERT_VMEM_LIMIT),
    )(block_e, n_used, buf, w1, w3, w2)


def _combine_kernel(x1_ref, ya_ref, yb_ref, gates_ref, mod_ref, g_ref, b_ref, o_ref):
    d = D_MODEL
    gate2 = mod_ref[0][:, 5 * d:6 * d]
    gates = gates_ref[...]
    h = gates[:, 0:1] * ya_ref[0] + gates[:, 1:2] * yb_ref[0]
    o_ref[0] = _layer_norm(ALPHA * x1_ref[0] + gate2 * h, g_ref[...], b_ref[...])


def _combine(x1, pairs, gates, mod3, g, b):
    bsz, seq, d = x1.shape
    ts = COMB_ROWS
    ns = seq // ts
    return pl.pallas_call(
        _combine_kernel,
        grid=(bsz, ns),
        in_specs=[
            pl.BlockSpec((1, ts, d), lambda i, j: (i, j, 0)),
            pl.BlockSpec((1, ts, d), lambda i, j: (0, i * ns + j, 0)),
            pl.BlockSpec((1, ts, d), lambda i, j: (1, i * ns + j, 0)),
            pl.BlockSpec((ts, TOP_K), lambda i, j: (i * ns + j, 0)),
            pl.BlockSpec((1, 1, 6 * d), lambda i, j: (i, 0, 0)),
            pl.BlockSpec((1, d), lambda i, j: (0, 0)),
            pl.BlockSpec((1, d), lambda i, j: (0, 0)),
        ],
        out_specs=pl.BlockSpec((1, ts, d), lambda i, j: (i, j, 0)),
        out_shape=jax.ShapeDtypeStruct((bsz, seq, d), F32),
        compiler_params=pltpu.CompilerParams(dimension_semantics=("arbitrary", "arbitrary")),
    )(x1, pairs, pairs, gates, mod3, g, b)


def _dispatch_plan(ri, cnt):
    n_tok = ri.shape[1]
    n_blocks = (n_tok * TOP_K + N_EXPERTS * (MOE_ROWS - 1) + MOE_ROWS - 1) // MOE_ROWS
    counts = cnt[:, 0].astype(jnp.int32)
    padded = (counts + MOE_ROWS - 1) // MOE_ROWS * MOE_ROWS
    pad_end = jnp.cumsum(padded)
    pad_start = pad_end - padded
    dest_a = pad_start[ri[0]] + ri[2]
    dest_b = pad_start[ri[1]] + ri[3]
    block_start = jnp.arange(n_blocks, dtype=jnp.int32) * MOE_ROWS
    block_e = jnp.minimum(jnp.sum(block_start[:, None] >= pad_end[None, :], axis=1),
                          N_EXPERTS - 1).astype(jnp.int32)
    n_used = (pad_end[-1:] // MOE_ROWS).astype(jnp.int32)
    return dest_a, dest_b, block_e, n_used, n_blocks * MOE_ROWS


def kernel(x, c, w_ada, b_ada, w_in, sc_conv, cv_conv, cv_conv_b, cv_ln_g, cv_ln_b, sg_ln_g, sg_ln_b, sg_w, sg_b, w_branch, w_gate, b_gate, w_o, ln1_g, ln1_b, w_router, b_router, w1, w3, w2, ln2_g, ln2_b):
    bsz, seq, d = x.shape
    n_tok = bsz * seq
    mod = _ada(c, w_ada, b_ada)
    wrt = w_router.T
    brc = b_router.reshape(N_EXPERTS, 1)
    w_gate2 = w_gate.reshape(DEPTH, d, 3 * d)
    for l in range(DEPTH):
        mod3 = mod[l].reshape(bsz, 1, 6 * d)
        x1, u2, ri, rf, cnt = _mixer(
            l, x, mod3, w_in, sc_conv[l], cv_conv[l],
            cv_conv_b[l].reshape(1, -1), cv_ln_g[l].reshape(1, -1), cv_ln_b[l].reshape(1, -1),
            sg_ln_g[l].reshape(1, -1), sg_ln_b[l].reshape(1, -1), sg_w[l], sg_b[l].T,
            w_branch, w_gate2, b_gate[l],
            w_o, ln1_g[l].reshape(1, -1), ln1_b[l].reshape(1, -1), wrt, brc)
        dest_a, dest_b, block_e, n_used, n_rows = _dispatch_plan(ri, cnt)
        buf = _sc_scatter_rows(u2.reshape(n_tok, d), dest_a, dest_b, n_rows)
        obuf = _experts(l, buf, block_e, n_used, w1, w3, w2)
        pair_idx = jnp.concatenate([dest_a, dest_b])
        pairs = _sc_gather_rows(obuf, pair_idx).reshape(TOP_K, n_tok, d)
        x = _combine(x1, pairs, rf[0:TOP_K].T, mod3, ln2_g[l].reshape(1, -1),
                     ln2_b[l].reshape(1, -1))
    return x
```

```python
import functools

import jax
import jax.numpy as jnp
from jax import lax
from jax.experimental import pallas as pl
from jax.experimental.pallas import tpu as pltpu
from jax.experimental.pallas import tpu_sc as plsc

D_MODEL = 1024
DEPTH = 2
W_BRANCH = 1024
SC_KERNEL = 3
CV_KERNEL = 31
CHUNK = 128
SG_HEADS = 8
N_EXPERTS = 16
N_GROUPS = 4
EXPERTS_PER_GROUP = N_EXPERTS // N_GROUPS
TOP_K = 2
D_EXPERT = 512
ALPHA = (2.0 * DEPTH) ** 0.25
LN_EPS = 1e-5

F32 = jnp.float32
BF16 = jnp.bfloat16

V7X_VMEM_BYTES = 64 * 1024 * 1024
MIXER_VMEM_LIMIT = V7X_VMEM_BYTES - 6 * 1024 * 1024
EXPERT_VMEM_LIMIT = V7X_VMEM_BYTES // 2
SUBLANES = 8
LANES = 128

MIX_ROWS = 256
SC_HALO = SUBLANES
CV_HALO = 32
CONV_ROWS = 128
TIE_LAG = 2
CONV_COLS = 128
PROJ_COLS = 512
STAGE_ELEMS = 128 * 1024
MOE_ROWS = 256
COMB_ROWS = 512
SC_WINDOW = 32
N_CHAINS = 2


def _dot(a, b):
    return jnp.dot(a, b, preferred_element_type=F32)


def _pack_bf16_pairs(v):
    m = v.shape[1] // 2
    lo = lax.bitcast_convert_type(v[:, 0:m].astype(BF16).astype(F32), jnp.uint32)
    hi = lax.bitcast_convert_type(v[:, m:2 * m].astype(BF16).astype(F32), jnp.uint32)
    return jnp.bitwise_or(jnp.bitwise_and(hi, jnp.uint32(0xFFFF0000)),
                          lax.shift_right_logical(lo, jnp.uint32(16)))


def _unpack_bf16_pairs(w):
    lo = lax.bitcast_convert_type(lax.shift_left(w, jnp.uint32(16)), F32)
    hi = lax.bitcast_convert_type(jnp.bitwise_and(w, jnp.uint32(0xFFFF0000)), F32)
    return lo, hi


def _layer_norm(v, g, b):
    mu = jnp.mean(v, axis=-1, keepdims=True)
    vc = v - mu
    var = jnp.mean(vc * vc, axis=-1, keepdims=True)
    return vc * lax.rsqrt(var + LN_EPS) * g + b


def _ada_kernel(c_ref, w_ref, b_ref, o_ref):
    c = c_ref[...]
    c_act = c * jax.nn.sigmoid(c)
    o_ref[0] = jnp.dot(c_act, w_ref[0], preferred_element_type=F32,
                       precision=lax.Precision.HIGHEST) + b_ref[0]


def _ada(c, w_ada, b_ada):
    bsz, d = c.shape
    n = w_ada.shape[-1]
    tn = 1536
    return pl.pallas_call(
        _ada_kernel,
        grid=(DEPTH, n // tn),
        in_specs=[
            pl.BlockSpec((bsz, d), lambda l, j: (0, 0)),
            pl.BlockSpec((1, d, tn), lambda l, j: (l, 0, j)),
            pl.BlockSpec((1, 1, tn), lambda l, j: (l, 0, j)),
        ],
        out_specs=pl.BlockSpec((1, bsz, tn), lambda l, j: (l, 0, j)),
        out_shape=jax.ShapeDtypeStruct((DEPTH, bsz, n), F32),
    )(c, w_ada, b_ada.reshape(DEPTH, 1, n))


def _top2_of4(rows):
    m1 = rows[0]
    i1 = jnp.zeros(rows[0].shape, jnp.int32)
    for k in range(1, 4):
        gt = rows[k] > m1
        m1 = jnp.where(gt, rows[k], m1)
        i1 = jnp.where(gt, k, i1)
    m2 = jnp.full(rows[0].shape, -jnp.inf, F32)
    i2 = jnp.zeros(rows[0].shape, jnp.int32)
    for k in range(4):
        cand = jnp.where(i1 == k, -jnp.inf, rows[k])
        gt = cand > m2
        m2 = jnp.where(gt, cand, m2)
        i2 = jnp.where(gt, k, i2)
    return m1, i1, m2, i2


def _zero_after(v):
    u = lax.bitcast_convert_type(v, jnp.uint32)
    u = lax.shift_right_logical(lax.shift_right_logical(u, jnp.uint32(16)), jnp.uint32(16))
    return lax.bitcast_convert_type(u, F32)


def _conv31_chunk(cvw_ref, cvbuf, cvout, r0, c0, tie):
    cs = slice(c0, c0 + CONV_COLS)
    acc = None
    for r in range(SUBLANES):
        lead = SUBLANES if r else 0
        part = None
        for m in range((CV_KERNEL - 1 - r) // SUBLANES + 1):
            k = CV_KERNEL - 1 - (SUBLANES * m + r)
            start = CV_HALO + r0 - lead - SUBLANES * m
            w_row = cvw_ref[k:k + 1, cs]
            if tie is not None and acc is None and part is None:
                w_row = w_row + tie
            term = w_row * cvbuf[start:start + lead + CONV_ROWS, cs]
            part = term if part is None else part + term
        part = part[lead - r:lead - r + CONV_ROWS]
        acc = part if acc is None else acc + part
    cvout[r0:r0 + CONV_ROWS, cs] = acc


def _stage_rows(cols):
    rows = SUBLANES
    while 2 * rows * cols <= STAGE_ELEMS:
        rows *= 2
    return rows


def _stage_weights_bf16(pairs, stages, sems):
    stage_ids = {cols: n for n, cols in enumerate(stages)}
    used = {cols: 0 for cols in stages}
    chunks = []
    for src, dst in pairs:
        rows, cols = src.shape
        stage = stages[cols]
        chunk_rows = stage.shape[1]
        for r0 in range(0, rows, chunk_rows):
            slot = used[cols] % 2
            used[cols] += 1
            copy = pltpu.make_async_copy(src.at[pl.ds(r0, chunk_rows), :], stage.at[slot],
                                         sems.at[stage_ids[cols], slot])
            chunks.append((copy, stage, slot, dst, r0, chunk_rows))
    chunks[0][0].start()
    for i, (copy, stage, slot, dst, r0, chunk_rows) in enumerate(chunks):
        if i + 1 < len(chunks):
            chunks[i + 1][0].start()
        copy.wait()
        dst[r0:r0 + chunk_rows, :] = stage[slot].astype(BF16)


def _mixer_kernel(layer, x_ref, mod_ref, w_in_hbm, scw_ref, cvw_ref, cvb_ref, cvg_ref,
                  cvbeta_ref, sgg_ref, sgbeta_ref, sgw_ref, sgbt_ref, wbr_hbm, wg_hbm, bg_ref,
                  wo_hbm, ln1g_ref, ln1b_ref, wrt_ref, brc_ref,
                  x1_ref, u2_ref, ri_ref, rf_ref, cnt_ref,
                  w_in_ref, wbr_ref, wg_ref, wo_ref, stage_in, stage_g, stage_d, wsems,
                  qbuf, cvbuf, cvout, ybuf, pa_buf, pc_buf, gl_buf, base_ref):
    ts = MIX_ROWS
    d = D_MODEL
    wb = W_BRANCH
    first_tile = pl.program_id(1) == 0

    @pl.when(jnp.logical_and(pl.program_id(0) == 0, first_tile))
    def _():
        base_ref[...] = jnp.zeros_like(base_ref)
        pairs = [(w_in_hbm.at[layer], w_in_ref), (wg_hbm.at[layer], wg_ref),
                 (wo_hbm.at[layer], wo_ref)]
        pairs += [(wbr_hbm.at[layer, n], wbr_ref.at[n]) for n in range(3)]
        stages = {w_in_ref.shape[1]: stage_in, wg_ref.shape[1]: stage_g, wo_ref.shape[1]: stage_d}
        _stage_weights_bf16(pairs, stages, wsems)

    @pl.when(first_tile)
    def _():
        qbuf[0:SC_HALO, :] = jnp.zeros((SC_HALO, wb), F32)
        cvbuf[0:CV_HALO, :] = jnp.zeros((CV_HALO, wb), F32)

    x = x_ref[0]
    mod = mod_ref[0]
    shift1, scale1, gate1 = mod[:, 0:d], mod[:, d:2 * d], mod[:, 2 * d:3 * d]
    shift2, scale2 = mod[:, 3 * d:4 * d], mod[:, 4 * d:5 * d]
    ub = (x * (1.0 + scale1) + shift1).astype(BF16)

    for c0 in range(0, wb, PROJ_COLS):
        a = _dot(ub, w_in_ref[:, 3 * wb + c0:3 * wb + c0 + PROJ_COLS])
        g = _dot(ub, w_in_ref[:, 4 * wb + c0:4 * wb + c0 + PROJ_COLS])
        cvbuf[CV_HALO:CV_HALO + ts, c0:c0 + PROJ_COLS] = a * jax.nn.sigmoid(g)

    mxu_jobs = ([(pa_buf, w_in_ref, c0, c0) for c0 in range(0, 3 * wb, PROJ_COLS)]
                + [(pc_buf, w_in_ref, c0, 5 * wb + c0) for c0 in range(0, 2 * wb, PROJ_COLS)]
                + [(gl_buf, wg_ref, c0, c0) for c0 in range(0, 3 * d, PROJ_COLS)])
    conv_jobs = [(r0, c0) for c0 in range(0, wb, CONV_COLS) for r0 in range(0, ts, CONV_ROWS)]
    ties = []
    for i in range(max(len(mxu_jobs), len(conv_jobs))):
        if i < len(conv_jobs):
            tie = ties[i - TIE_LAG] if TIE_LAG <= i < TIE_LAG + len(ties) else None
            _conv31_chunk(cvw_ref, cvbuf, cvout, *conv_jobs[i], tie)
        if i < len(mxu_jobs):
            dst, w_ref, dc, wc = mxu_jobs[i]
            res = _dot(ub, w_ref[:, wc:wc + PROJ_COLS])
            dst[:, dc:dc + PROJ_COLS] = res
            ties.append(_zero_after(res[ts - 1:ts, PROJ_COLS - CONV_COLS:PROJ_COLS]))
    cvbuf[0:CV_HALO, :] = cvbuf[ts:ts + CV_HALO, :]

    qbuf[SC_HALO:SC_HALO + ts, :] = pa_buf[:, wb:2 * wb] * pa_buf[:, 2 * wb:3 * wb]
    conv = scw_ref[SC_KERNEL - 1:SC_KERNEL, :] * qbuf[SC_HALO:SC_HALO + ts, :]
    for k in range(SC_KERNEL - 1):
        off = SC_HALO - (SC_KERNEL - 1) + k
        conv = conv + scw_ref[k:k + 1, :] * qbuf[off:off + ts, :]
    ybuf[0] = (pa_buf[:, 0:wb] * conv).astype(BF16)
    qbuf[0:SC_HALO, :] = qbuf[ts:ts + SC_HALO, :]
    z0 = _dot(ybuf[0], wbr_ref[0])

    cv = _layer_norm(cvout[...] + cvb_ref[...], cvg_ref[...], cvbeta_ref[...])
    ybuf[1] = (cv * jax.nn.sigmoid(cv)).astype(BF16)
    z1 = _dot(ybuf[1], wbr_ref[1])

    gu = jax.nn.gelu(pc_buf[:, 0:wb])
    gv = _layer_norm(jax.nn.gelu(pc_buf[:, wb:2 * wb]), sgg_ref[...], sgbeta_ref[...]).astype(BF16)
    merged = (jax.nn.sigmoid(gl_buf[:, 0:d] + bg_ref[0:1, :]) * z0
              + jax.nn.sigmoid(gl_buf[:, d:2 * d] + bg_ref[1:2, :]) * z1)
    g2 = jax.nn.sigmoid(gl_buf[:, 2 * d:3 * d] + bg_ref[2:3, :])
    row = lax.broadcasted_iota(jnp.int32, (CHUNK, CHUNK), 0)
    col = lax.broadcasted_iota(jnp.int32, (CHUNK, CHUNK), 1)
    hd = wb // SG_HEADS
    for h in range(SG_HEADS):
        wm = jnp.where(row >= col, sgw_ref[h], 0.0).astype(BF16)
        bias = sgbt_ref[:, h:h + 1]
        for n in range(ts // CHUNK):
            rs = slice(n * CHUNK, (n + 1) * CHUNK)
            cs = slice(h * hd, (h + 1) * hd)
            mixed = _dot(wm, gv[rs, cs]) + bias
            ybuf[2, rs, cs] = (gu[rs, cs] * mixed).astype(BF16)

    merged = merged + g2 * _dot(ybuf[2], wbr_ref[2])
    hmix = _dot(merged.astype(BF16), wo_ref[...])
    x1 = _layer_norm(ALPHA * x + gate1 * hmix, ln1g_ref[...], ln1b_ref[...])
    x1_ref[0] = x1
    u2 = x1 * (1.0 + scale2) + shift2
    u2_ref[0] = _pack_bf16_pairs(u2)

    logits = lax.dot_general(wrt_ref[...], u2, (((1,), (1,)), ((), ())),
                             preferred_element_type=F32,
                             precision=lax.Precision.HIGHEST)
    mx = jnp.max(logits, axis=0, keepdims=True)
    ex = jnp.exp(logits - mx)
    scores = ex / jnp.sum(ex, axis=0, keepdims=True)
    sel = scores + brc_ref[...]
    tops = []
    for g in range(N_GROUPS):
        rows = [sel[g * EXPERTS_PER_GROUP + k:g * EXPERTS_PER_GROUP + k + 1, :]
                for k in range(EXPERTS_PER_GROUP)]
        tops.append(_top2_of4(rows))
    best = tops[0][0] + tops[0][2]
    g_idx = jnp.zeros(best.shape, jnp.int32)
    loc1, loc2 = tops[0][1], tops[0][3]
    for g in range(1, N_GROUPS):
        gs = tops[g][0] + tops[g][2]
        gt = gs > best
        best = jnp.where(gt, gs, best)
        g_idx = jnp.where(gt, g, g_idx)
        loc1 = jnp.where(gt, tops[g][1], loc1)
        loc2 = jnp.where(gt, tops[g][3], loc2)
    e0 = g_idx * EXPERTS_PER_GROUP + loc1
    e1 = g_idx * EXPERTS_PER_GROUP + loc2
    erow = lax.broadcasted_iota(jnp.int32, (N_EXPERTS, ts), 0)
    is0 = erow == e0
    is1 = erow == e1
    s0 = jnp.sum(jnp.where(is0, scores, 0.0), axis=0, keepdims=True)
    s1 = jnp.sum(jnp.where(is1, scores, 0.0), axis=0, keepdims=True)
    ssum = s0 + s1
    onehot = jnp.logical_or(is0, is1).astype(BF16)
    src = lax.broadcasted_iota(jnp.int32, (ts, ts), 0)
    dst = lax.broadcasted_iota(jnp.int32, (ts, ts), 1)
    earlier = (src < dst).astype(BF16)
    prior = _dot(onehot, earlier) + base_ref[:, 0:1]
    r0 = jnp.sum(jnp.where(is0, prior, 0.0), axis=0, keepdims=True)
    r1 = jnp.sum(jnp.where(is1, prior, 0.0), axis=0, keepdims=True)
    base_ref[...] = base_ref[...] + jnp.sum(onehot.astype(F32), axis=1, keepdims=True)

    zi = jnp.zeros((SUBLANES - 4, ts), jnp.int32)
    ri_ref[...] = jnp.concatenate([e0, e1, r0.astype(jnp.int32), r1.astype(jnp.int32), zi], axis=0)
    zf = jnp.zeros((SUBLANES - 2, ts), F32)
    rf_ref[...] = jnp.concatenate([s0 / ssum, s1 / ssum, zf], axis=0)
    cnt_ref[...] = base_ref[...]


def _mixer(layer, x, xb0, mod3, mb0, bsz, w_in, scw, cvw, cvb, cvg, cvbeta, sgg, sgbeta, sgw, sgbt,
           wbr, wg, bg, wo, ln1g, ln1b, wrt, brc):
    _, seq, d = x.shape
    ts = MIX_ROWS
    ns = seq // ts
    n_tok = bsz * seq

    def const(shape):
        zeros = (0,) * len(shape)
        return pl.BlockSpec(shape, lambda b, j: zeros, pipeline_mode=pl.Buffered(1))

    hbm = pl.BlockSpec(memory_space=pl.ANY)
    in_specs = [
        pl.BlockSpec((1, ts, d), lambda b, j: (b + xb0, j, 0)),
        pl.BlockSpec((1, 1, 6 * d), lambda b, j: (b + mb0, 0, 0)),
        hbm, const(scw.shape), const(cvw.shape), const(cvb.shape),
        const(cvg.shape), const(cvbeta.shape), const(sgg.shape), const(sgbeta.shape),
        const(sgw.shape), const(sgbt.shape), hbm, hbm, const(bg.shape),
        hbm, const(ln1g.shape), const(ln1b.shape), const(wrt.shape), const(brc.shape),
    ]
    out_specs = [
        pl.BlockSpec((1, ts, d), lambda b, j: (b, j, 0)),
        pl.BlockSpec((1, ts, d // 2), lambda b, j: (b, j, 0)),
        pl.BlockSpec((SUBLANES, ts), lambda b, j: (0, b * ns + j)),
        pl.BlockSpec((SUBLANES, ts), lambda b, j: (0, b * ns + j)),
        pl.BlockSpec((N_EXPERTS, LANES), lambda b, j: (0, 0)),
    ]
    out_shape = [
        jax.ShapeDtypeStruct((bsz, seq, d), F32),
        jax.ShapeDtypeStruct((bsz, seq, d // 2), jnp.uint32),
        jax.ShapeDtypeStruct((SUBLANES, n_tok), jnp.int32),
        jax.ShapeDtypeStruct((SUBLANES, n_tok), F32),
        jax.ShapeDtypeStruct((N_EXPERTS, LANES), F32),
    ]
    n_in = w_in.shape[-1]
    return pl.pallas_call(
        functools.partial(_mixer_kernel, layer),
        grid=(bsz, ns),
        in_specs=in_specs,
        out_specs=out_specs,
        out_shape=out_shape,
        scratch_shapes=[
            pltpu.VMEM((d, n_in), BF16),
            pltpu.VMEM((3, W_BRANCH, d), BF16),
            pltpu.VMEM((d, 3 * d), BF16),
            pltpu.VMEM((d, d), BF16),
            pltpu.VMEM((2, _stage_rows(n_in), n_in), F32),
            pltpu.VMEM((2, _stage_rows(3 * d), 3 * d), F32),
            pltpu.VMEM((2, _stage_rows(d), d), F32),
            pltpu.SemaphoreType.DMA((3, 2)),
            pltpu.VMEM((SC_HALO + ts, W_BRANCH), F32),
            pltpu.VMEM((CV_HALO + ts, W_BRANCH), F32),
            pltpu.VMEM((ts, W_BRANCH), F32),
            pltpu.VMEM((3, ts, W_BRANCH), BF16),
            pltpu.VMEM((ts, 3 * W_BRANCH), F32),
            pltpu.VMEM((ts, 2 * W_BRANCH), F32),
            pltpu.VMEM((ts, 3 * D_MODEL), F32),
            pltpu.VMEM((N_EXPERTS, LANES), F32),
        ],
        compiler_params=pltpu.CompilerParams(
            dimension_semantics=("arbitrary", "arbitrary"),
            vmem_limit_bytes=MIXER_VMEM_LIMIT),
    )(x, mod3, w_in, scw, cvw, cvb, cvg, cvbeta, sgg, sgbeta, sgw, sgbt, wbr, wg, bg, wo,
      ln1g, ln1b, wrt, brc)


def _sc_workers():
    info = plsc.get_sparse_core_info()
    return info.num_cores, info.num_cores * info.num_subcores


def _sc_scatter_rows(rows, dest_a, dest_b, n_out):
    n, d = rows.shape
    nc, nw = _sc_workers()
    per_w = n // nw
    n_win = per_w // SC_WINDOW
    ia = dest_a.reshape(nw, n_win, SC_WINDOW)
    ib = dest_b.reshape(nw, n_win, SC_WINDOW)
    mesh = plsc.VectorSubcoreMesh(core_axis_name="c", subcore_axis_name="s")

    @functools.partial(
        pl.kernel, mesh=mesh,
        out_type=jax.ShapeDtypeStruct((n_out, d), rows.dtype),
        scratch_types=[
            pltpu.VMEM((n_win, SC_WINDOW), jnp.int32),
            pltpu.VMEM((n_win, SC_WINDOW), jnp.int32),
            pltpu.VMEM((SC_WINDOW, d), rows.dtype),
        ],
    )
    def scatter(rows_hbm, ia_hbm, ib_hbm, out_hbm, ia_v, ib_v, rows_v):
        wid = lax.axis_index("s") * nc + lax.axis_index("c")
        pltpu.sync_copy(ia_hbm.at[wid], ia_v)
        pltpu.sync_copy(ib_hbm.at[wid], ib_v)
        base = wid * per_w

        @pl.loop(0, n_win)
        def _(j):
            pltpu.sync_copy(rows_hbm.at[pl.ds(base + j * SC_WINDOW, SC_WINDOW)], rows_v)
            pltpu.sync_copy(rows_v, out_hbm.at[ia_v.at[j]])
            pltpu.sync_copy(rows_v, out_hbm.at[ib_v.at[j]])

    return scatter(rows, ia, ib)


def _sc_gather_rows(table, idx):
    n = idx.shape[0]
    d = table.shape[1]
    nc, nw = _sc_workers()
    per_w = n // nw
    n_win = per_w // SC_WINDOW
    idx3 = idx.reshape(nw, n_win, SC_WINDOW)
    mesh = plsc.VectorSubcoreMesh(core_axis_name="c", subcore_axis_name="s")

    @functools.partial(
        pl.kernel, mesh=mesh,
        out_type=jax.ShapeDtypeStruct((n, d), table.dtype),
        scratch_types=[
            pltpu.VMEM((n_win, SC_WINDOW), jnp.int32),
            pltpu.VMEM((SC_WINDOW, d), table.dtype),
        ],
    )
    def gather(table_hbm, idx_hbm, out_hbm, idx_v, rows_v):
        wid = lax.axis_index("s") * nc + lax.axis_index("c")
        pltpu.sync_copy(idx_hbm.at[wid], idx_v)
        base = wid * per_w

        @pl.loop(0, n_win)
        def _(j):
            pltpu.sync_copy(table_hbm.at[idx_v.at[j]], rows_v)
            pltpu.sync_copy(rows_v, out_hbm.at[pl.ds(base + j * SC_WINDOW, SC_WINDOW)])

    return gather(table, idx3)


def _expert_kernel(be_ref, nused_ref, x_ref, w1_ref, w3_ref, w2_ref, o_ref, w1s, w3s, w2s):
    i = pl.program_id(0)
    new_expert = jnp.logical_or(i == 0, be_ref[i] != be_ref[jnp.maximum(i - 1, 0)])

    @pl.when(new_expert)
    def _():
        w1s[...] = w1_ref[0, 0].astype(BF16)
        w3s[...] = w3_ref[0, 0].astype(BF16)
        w2s[...] = w2_ref[0, 0].astype(BF16)

    @pl.when(i < nused_ref[0])
    def _():
        lo, hi = _unpack_bf16_pairs(x_ref[...])
        xb = jnp.concatenate([lo, hi], axis=1).astype(BF16)
        a = _dot(xb, w1s[...])
        h = a * jax.nn.sigmoid(a) * _dot(xb, w3s[...])
        o_ref[...] = _pack_bf16_pairs(_dot(h.astype(BF16), w2s[...]))


def _experts(layer, buf, block_e, n_used, w1, w3, w2):
    n_rows, dp = buf.shape
    nb = n_rows // MOE_ROWS
    d, fe = w1.shape[-2:]

    def row_map(i, be, nu):
        return (jnp.minimum(i, nu[0] - 1), 0)

    grid_spec = pltpu.PrefetchScalarGridSpec(
        num_scalar_prefetch=2,
        grid=(nb,),
        in_specs=[
            pl.BlockSpec((MOE_ROWS, dp), row_map),
            pl.BlockSpec((1, 1, d, fe), lambda i, be, nu: (layer, be[i], 0, 0)),
            pl.BlockSpec((1, 1, d, fe), lambda i, be, nu: (layer, be[i], 0, 0)),
            pl.BlockSpec((1, 1, fe, d), lambda i, be, nu: (layer, be[i], 0, 0)),
        ],
        out_specs=pl.BlockSpec((MOE_ROWS, dp), row_map),
        scratch_shapes=[pltpu.VMEM((d, fe), BF16), pltpu.VMEM((d, fe), BF16),
                        pltpu.VMEM((fe, d), BF16)],
    )
    return pl.pallas_call(
        _expert_kernel,
        grid_spec=grid_spec,
        out_shape=jax.ShapeDtypeStruct((n_rows, dp), jnp.uint32),
        compiler_params=pltpu.CompilerParams(dimension_semantics=("arbitrary",),
                                             vmem_limit_bytes=EXPERT_VMEM_LIMIT),
    )(block_e, n_used, buf, w1, w3, w2)


def _combine_kernel(x1_ref, ya_ref, yb_ref, gates_ref, mod_ref, g_ref, b_ref, *rest):
    o_ref = rest[-1]
    d = D_MODEL
    gate2 = mod_ref[0][:, 5 * d:6 * d]
    gates = gates_ref[...]
    ya = jnp.concatenate(_unpack_bf16_pairs(ya_ref[0]), axis=1)
    yb = jnp.concatenate(_unpack_bf16_pairs(yb_ref[0]), axis=1)
    h = gates[:, 0:1] * ya + gates[:, 1:2] * yb
    o_ref[0] = _layer_norm(ALPHA * x1_ref[0] + gate2 * h, g_ref[...], b_ref[...])


def _combine(x1, pairs, gates, mod3, mb0, g, b, out_bsz, ob0, prev):
    bsz, seq, d = x1.shape
    ts = COMB_ROWS
    ns = seq // ts
    in_specs = [
        pl.BlockSpec((1, ts, d), lambda i, j: (i, j, 0)),
        pl.BlockSpec((1, ts, d // 2), lambda i, j: (0, i * ns + j, 0)),
        pl.BlockSpec((1, ts, d // 2), lambda i, j: (1, i * ns + j, 0)),
        pl.BlockSpec((ts, TOP_K), lambda i, j: (i * ns + j, 0)),
        pl.BlockSpec((1, 1, 6 * d), lambda i, j: (i + mb0, 0, 0)),
        pl.BlockSpec((1, d), lambda i, j: (0, 0)),
        pl.BlockSpec((1, d), lambda i, j: (0, 0)),
    ]
    args = [x1, pairs, pairs, gates, mod3, g, b]
    aliases = {}
    if prev is not None:
        in_specs.append(pl.BlockSpec(memory_space=pl.ANY))
        aliases = {len(args): 0}
        args.append(prev)
    return pl.pallas_call(
        _combine_kernel,
        grid=(bsz, ns),
        in_specs=in_specs,
        out_specs=pl.BlockSpec((1, ts, d), lambda i, j: (i + ob0, j, 0)),
        out_shape=jax.ShapeDtypeStruct((out_bsz, seq, d), F32),
        input_output_aliases=aliases,
        compiler_params=pltpu.CompilerParams(dimension_semantics=("arbitrary", "arbitrary")),
    )(*args)


def _dispatch_plan(ri, cnt):
    n_tok = ri.shape[1]
    n_blocks = (n_tok * TOP_K + N_EXPERTS * (MOE_ROWS - 1) + MOE_ROWS - 1) // MOE_ROWS
    counts = cnt[:, 0].astype(jnp.int32)
    padded = (counts + MOE_ROWS - 1) // MOE_ROWS * MOE_ROWS
    pad_end = jnp.cumsum(padded)
    pad_start = pad_end - padded
    dest_a = pad_start[ri[0]] + ri[2]
    dest_b = pad_start[ri[1]] + ri[3]
    block_start = jnp.arange(n_blocks, dtype=jnp.int32) * MOE_ROWS
    block_e = jnp.minimum(jnp.sum(block_start[:, None] >= pad_end[None, :], axis=1),
                          N_EXPERTS - 1).astype(jnp.int32)
    n_used = (pad_end[-1:] // MOE_ROWS).astype(jnp.int32)
    return dest_a, dest_b, block_e, n_used, n_blocks * MOE_ROWS


def kernel(x, c, w_ada, b_ada, w_in, sc_conv, cv_conv, cv_conv_b, cv_ln_g, cv_ln_b, sg_ln_g, sg_ln_b, sg_w, sg_b, w_branch, w_gate, b_gate, w_o, ln1_g, ln1_b, w_router, b_router, w1, w3, w2, ln2_g, ln2_b):
    bsz, seq, d = x.shape
    mod3 = _ada(c, w_ada, b_ada).reshape(DEPTH, bsz, 1, 6 * d)
    wrt = w_router.T
    brc = b_router.reshape(N_EXPERTS, 1)
    w_gate2 = w_gate.reshape(DEPTH, d, 3 * d)
    cb = bsz // N_CHAINS
    n_tok = cb * seq
    chains = [(x, h * cb) for h in range(N_CHAINS)]
    for l in range(DEPTH):
        last = l == DEPTH - 1
        result = None
        for h in range(N_CHAINS):
            xin, xb0 = chains[h]
            x1, u2, ri, rf, cnt = _mixer(
                l, xin, xb0, mod3[l], h * cb, cb, w_in, sc_conv[l], cv_conv[l],
                cv_conv_b[l].reshape(1, -1), cv_ln_g[l].reshape(1, -1), cv_ln_b[l].reshape(1, -1),
                sg_ln_g[l].reshape(1, -1), sg_ln_b[l].reshape(1, -1), sg_w[l], sg_b[l].T,
                w_branch, w_gate2, b_gate[l],
                w_o, ln1_g[l].reshape(1, -1), ln1_b[l].reshape(1, -1), wrt, brc)
            dest_a, dest_b, block_e, n_used, n_rows = _dispatch_plan(ri, cnt)
            buf = _sc_scatter_rows(u2.reshape(n_tok, d // 2), dest_a, dest_b, n_rows)
            obuf = _experts(l, buf, block_e, n_used, w1, w3, w2)
            pair_idx = jnp.concatenate([dest_a, dest_b])
            pairs = _sc_gather_rows(obuf, pair_idx).reshape(TOP_K, n_tok, d // 2)
            gates = rf[0:TOP_K].T
            g2, b2 = ln2_g[l].reshape(1, -1), ln2_b[l].reshape(1, -1)
            if last:
                result = _combine(x1, pairs, gates, mod3[l], h * cb, g2, b2, bsz, h * cb, result)
            else:
                chains[h] = (_combine(x1, pairs, gates, mod3[l], h * cb, g2, b2, cb, 0, None), 0)
    return result
```

```python
import functools

import jax
import jax.numpy as jnp
from jax import lax
from jax.experimental import pallas as pl
from jax.experimental.pallas import tpu as pltpu
from jax.experimental.pallas import tpu_sc as plsc

D_MODEL = 1024
DEPTH = 2
W_BRANCH = 1024
SC_KERNEL = 3
CV_KERNEL = 31
CHUNK = 128
SG_HEADS = 8
N_EXPERTS = 16
N_GROUPS = 4
EXPERTS_PER_GROUP = N_EXPERTS // N_GROUPS
TOP_K = 2
D_EXPERT = 512
ALPHA = (2.0 * DEPTH) ** 0.25
LN_EPS = 1e-5

F32 = jnp.float32
BF16 = jnp.bfloat16

V7X_VMEM_BYTES = 64 * 1024 * 1024
MIXER_VMEM_LIMIT = V7X_VMEM_BYTES - 6 * 1024 * 1024
EXPERT_VMEM_LIMIT = V7X_VMEM_BYTES // 2
SUBLANES = 8
LANES = 128

MIX_ROWS = 256
SC_HALO = SUBLANES
CV_HALO = 32
CONV_ROWS = 128
TIE_LAG = 2
TIE_FREE_JOBS = 4
TAIL_STAGE_AFTER_JOBS = (3, 8)
CONV_COLS = 128
PROJ_COLS = 512
STAGE_ELEMS = 128 * 1024
MOE_ROWS = 256
COMB_ROWS = 512
SC_WINDOW = 32
N_CHAINS = 2


def _dot(a, b):
    return jnp.dot(a, b, preferred_element_type=F32)


def _pack_bf16_pairs(v):
    m = v.shape[1] // 2
    lo = lax.bitcast_convert_type(v[:, 0:m].astype(BF16).astype(F32), jnp.uint32)
    hi = lax.bitcast_convert_type(v[:, m:2 * m].astype(BF16).astype(F32), jnp.uint32)
    return jnp.bitwise_or(jnp.bitwise_and(hi, jnp.uint32(0xFFFF0000)),
                          lax.shift_right_logical(lo, jnp.uint32(16)))


def _unpack_bf16_pairs(w):
    lo = lax.bitcast_convert_type(lax.shift_left(w, jnp.uint32(16)), F32)
    hi = lax.bitcast_convert_type(jnp.bitwise_and(w, jnp.uint32(0xFFFF0000)), F32)
    return lo, hi


def _layer_norm(v, g, b):
    mu = jnp.mean(v, axis=-1, keepdims=True)
    vc = v - mu
    var = jnp.mean(vc * vc, axis=-1, keepdims=True)
    return vc * lax.rsqrt(var + LN_EPS) * g + b


def _ada_kernel(c_ref, w_ref, b_ref, o_ref):
    c = c_ref[...]
    c_act = c * jax.nn.sigmoid(c)
    o_ref[0] = jnp.dot(c_act, w_ref[0], preferred_element_type=F32,
                       precision=lax.Precision.HIGHEST) + b_ref[0]


def _ada(c, w_ada, b_ada):
    bsz, d = c.shape
    n = w_ada.shape[-1]
    tn = 1536
    return pl.pallas_call(
        _ada_kernel,
        grid=(DEPTH, n // tn),
        in_specs=[
            pl.BlockSpec((bsz, d), lambda l, j: (0, 0)),
            pl.BlockSpec((1, d, tn), lambda l, j: (l, 0, j)),
            pl.BlockSpec((1, 1, tn), lambda l, j: (l, 0, j)),
        ],
        out_specs=pl.BlockSpec((1, bsz, tn), lambda l, j: (l, 0, j)),
        out_shape=jax.ShapeDtypeStruct((DEPTH, bsz, n), F32),
    )(c, w_ada, b_ada.reshape(DEPTH, 1, n))


def _top2_of4(rows):
    m1 = rows[0]
    i1 = jnp.zeros(rows[0].shape, jnp.int32)
    for k in range(1, 4):
        gt = rows[k] > m1
        m1 = jnp.where(gt, rows[k], m1)
        i1 = jnp.where(gt, k, i1)
    m2 = jnp.full(rows[0].shape, -jnp.inf, F32)
    i2 = jnp.zeros(rows[0].shape, jnp.int32)
    for k in range(4):
        cand = jnp.where(i1 == k, -jnp.inf, rows[k])
        gt = cand > m2
        m2 = jnp.where(gt, cand, m2)
        i2 = jnp.where(gt, k, i2)
    return m1, i1, m2, i2


def _zero_after(v):
    u = lax.bitcast_convert_type(v, jnp.uint32)
    u = lax.shift_right_logical(lax.shift_right_logical(u, jnp.uint32(16)), jnp.uint32(16))
    return lax.bitcast_convert_type(u, F32)


def _conv31_chunk(cvw_ref, cvbuf, cvout, r0, c0, tie):
    cs = slice(c0, c0 + CONV_COLS)
    acc = None
    for r in range(SUBLANES):
        lead = SUBLANES if r else 0
        part = None
        for m in range((CV_KERNEL - 1 - r) // SUBLANES + 1):
            k = CV_KERNEL - 1 - (SUBLANES * m + r)
            start = CV_HALO + r0 - lead - SUBLANES * m
            w_row = cvw_ref[k:k + 1, cs]
            if tie is not None and acc is None and part is None:
                w_row = w_row + tie
            term = w_row * cvbuf[start:start + lead + CONV_ROWS, cs]
            part = term if part is None else part + term
        part = part[lead - r:lead - r + CONV_ROWS]
        acc = part if acc is None else acc + part
    cvout[r0:r0 + CONV_ROWS, cs] = acc


def _stage_rows(cols):
    rows = SUBLANES
    while 2 * rows * cols <= STAGE_ELEMS:
        rows *= 2
    return rows


def _stage_weights_bf16(pairs, stages, sems):
    stage_ids = {cols: n for n, cols in enumerate(stages)}
    used = {cols: 0 for cols in stages}
    chunks = []
    for src, dst in pairs:
        rows, cols = src.shape
        stage = stages[cols]
        chunk_rows = stage.shape[1]
        for r0 in range(0, rows, chunk_rows):
            slot = used[cols] % 2
            used[cols] += 1
            copy = pltpu.make_async_copy(src.at[pl.ds(r0, chunk_rows), :], stage.at[slot],
                                         sems.at[stage_ids[cols], slot])
            chunks.append((copy, stage, slot, dst, r0, chunk_rows))
    chunks[0][0].start()
    for i, (copy, stage, slot, dst, r0, chunk_rows) in enumerate(chunks):
        if i + 1 < len(chunks):
            chunks[i + 1][0].start()
        copy.wait()
        dst[r0:r0 + chunk_rows, :] = stage[slot].astype(BF16)


def _mixer_kernel(layer, tiles_per_seq, x_ref, mod_ref, modt_ref, w_in_hbm, scw_ref, cvw_ref,
                  cvb_ref, cvg_ref, cvbeta_ref, sgg_ref, sgbeta_ref, sgw_ref, sgbt_ref, wbr_hbm,
                  wg_hbm, bg_ref, wo_hbm, ln1g_ref, ln1b_ref, wrt_ref, brc_ref,
                  x1_ref, u2_ref, ri_ref, rf_ref, cnt_ref,
                  w_in_ref, wbr_ref, wg_ref, wo_ref, stage_in, stage_g, stage_d, wsems,
                  qbuf, cvbuf, cvout, ybuf, pa_buf, pc_buf, gl_buf, mg_buf, xs_buf, base_ref):
    ts = MIX_ROWS
    d = D_MODEL
    wb = W_BRANCH
    step = pl.program_id(0)
    tile = jnp.minimum(step, pl.num_programs(0) - 2)
    first_tile = tile % tiles_per_seq == 0

    @pl.when(step == 0)
    def _():
        base_ref[...] = jnp.zeros_like(base_ref)
        mg_buf[...] = jnp.zeros_like(mg_buf)
        xs_buf[...] = jnp.zeros_like(xs_buf)
        pairs = [(w_in_hbm.at[layer], w_in_ref), (wg_hbm.at[layer], wg_ref),
                 (wo_hbm.at[layer], wo_ref)]
        pairs += [(wbr_hbm.at[layer, n], wbr_ref.at[n]) for n in range(3)]
        stages = {w_in_ref.shape[1]: stage_in, wg_ref.shape[1]: stage_g, wo_ref.shape[1]: stage_d}
        _stage_weights_bf16(pairs, stages, wsems)

    @pl.when(first_tile)
    def _():
        qbuf[0:SC_HALO, :] = jnp.zeros((SC_HALO, wb), F32)
        cvbuf[0:CV_HALO, :] = jnp.zeros((CV_HALO, wb), F32)

    x = x_ref[0]
    mod = mod_ref[0]
    shift1, scale1 = mod[:, 0:d], mod[:, d:2 * d]
    ub = (x * (1.0 + scale1) + shift1).astype(BF16)

    for c0 in range(0, wb, PROJ_COLS):
        a = _dot(ub, w_in_ref[:, 3 * wb + c0:3 * wb + c0 + PROJ_COLS])
        g = _dot(ub, w_in_ref[:, 4 * wb + c0:4 * wb + c0 + PROJ_COLS])
        cvbuf[CV_HALO:CV_HALO + ts, c0:c0 + PROJ_COLS] = a * jax.nn.sigmoid(g)

    tail = _mixer_tail(step, mg_buf, xs_buf, modt_ref, wo_ref, ln1g_ref, ln1b_ref, wrt_ref,
                       brc_ref, x1_ref, u2_ref, ri_ref, rf_ref, cnt_ref, base_ref)
    next(tail)

    mxu_jobs = ([(pa_buf, w_in_ref, c0, c0) for c0 in range(0, 3 * wb, PROJ_COLS)]
                + [(pc_buf, w_in_ref, c0, 5 * wb + c0) for c0 in range(0, 2 * wb, PROJ_COLS)]
                + [(gl_buf, wg_ref, c0, c0) for c0 in range(0, 3 * d, PROJ_COLS)])
    conv_jobs = [(r0, c0) for c0 in range(0, wb, CONV_COLS) for r0 in range(0, ts, CONV_ROWS)]
    ties = {}
    for i in range(max(len(mxu_jobs), len(conv_jobs))):
        if i < len(conv_jobs):
            _conv31_chunk(cvw_ref, cvbuf, cvout, *conv_jobs[i], ties.get(i))
        if i < len(mxu_jobs):
            dst, w_ref, dc, wc = mxu_jobs[i]
            res = _dot(ub, w_ref[:, wc:wc + PROJ_COLS])
            dst[:, dc:dc + PROJ_COLS] = res
            if i < len(mxu_jobs) - TIE_FREE_JOBS:
                ties[i + TIE_LAG] = _zero_after(res[ts - 1:ts, PROJ_COLS - CONV_COLS:PROJ_COLS])
        if i in TAIL_STAGE_AFTER_JOBS:
            next(tail, None)
    cvbuf[0:CV_HALO, :] = cvbuf[ts:ts + CV_HALO, :]

    qbuf[SC_HALO:SC_HALO + ts, :] = pa_buf[:, wb:2 * wb] * pa_buf[:, 2 * wb:3 * wb]
    conv = scw_ref[SC_KERNEL - 1:SC_KERNEL, :] * qbuf[SC_HALO:SC_HALO + ts, :]
    for k in range(SC_KERNEL - 1):
        off = SC_HALO - (SC_KERNEL - 1) + k
        conv = conv + scw_ref[k:k + 1, :] * qbuf[off:off + ts, :]
    ybuf[0] = (pa_buf[:, 0:wb] * conv).astype(BF16)
    qbuf[0:SC_HALO, :] = qbuf[ts:ts + SC_HALO, :]
    z0 = _dot(ybuf[0], wbr_ref[0])

    cv = _layer_norm(cvout[...] + cvb_ref[...], cvg_ref[...], cvbeta_ref[...])
    ybuf[1] = (cv * jax.nn.sigmoid(cv)).astype(BF16)
    z1 = _dot(ybuf[1], wbr_ref[1])

    gu = jax.nn.gelu(pc_buf[:, 0:wb])
    gv = _layer_norm(jax.nn.gelu(pc_buf[:, wb:2 * wb]), sgg_ref[...], sgbeta_ref[...]).astype(BF16)
    merged = (jax.nn.sigmoid(gl_buf[:, 0:d] + bg_ref[0:1, :]) * z0
              + jax.nn.sigmoid(gl_buf[:, d:2 * d] + bg_ref[1:2, :]) * z1)
    g2 = jax.nn.sigmoid(gl_buf[:, 2 * d:3 * d] + bg_ref[2:3, :])
    row = lax.broadcasted_iota(jnp.int32, (CHUNK, CHUNK), 0)
    col = lax.broadcasted_iota(jnp.int32, (CHUNK, CHUNK), 1)
    hd = wb // SG_HEADS
    for h in range(SG_HEADS):
        wm = jnp.where(row >= col, sgw_ref[h], 0.0).astype(BF16)
        bias = sgbt_ref[:, h:h + 1]
        for n in range(ts // CHUNK):
            rs = slice(n * CHUNK, (n + 1) * CHUNK)
            cs = slice(h * hd, (h + 1) * hd)
            mixed = _dot(wm, gv[rs, cs]) + bias
            ybuf[2, rs, cs] = (gu[rs, cs] * mixed).astype(BF16)

    merged = merged + g2 * _dot(ybuf[2], wbr_ref[2])
    mg_buf[...] = merged.astype(BF16)
    xs_buf[...] = x


def _mixer_tail(step, mg_buf, xs_buf, modt_ref, wo_ref, ln1g_ref, ln1b_ref, wrt_ref, brc_ref,
                x1_ref, u2_ref, ri_ref, rf_ref, cnt_ref, base_ref):
    ts = MIX_ROWS
    d = D_MODEL
    modt = modt_ref[0]
    gate1, shift2, scale2 = modt[:, 2 * d:3 * d], modt[:, 3 * d:4 * d], modt[:, 4 * d:5 * d]
    hmix = _dot(mg_buf[...], wo_ref[...])
    x1 = _layer_norm(ALPHA * xs_buf[...] + gate1 * hmix, ln1g_ref[...], ln1b_ref[...])
    x1_ref[0] = x1
    u2 = x1 * (1.0 + scale2) + shift2
    u2_ref[0] = _pack_bf16_pairs(u2)
    yield

    logits = lax.dot_general(wrt_ref[...], u2, (((1,), (1,)), ((), ())),
                             preferred_element_type=F32,
                             precision=lax.Precision.HIGHEST)
    mx = jnp.max(logits, axis=0, keepdims=True)
    ex = jnp.exp(logits - mx)
    scores = ex / jnp.sum(ex, axis=0, keepdims=True)
    sel = scores + brc_ref[...]
    tops = []
    for g in range(N_GROUPS):
        rows = [sel[g * EXPERTS_PER_GROUP + k:g * EXPERTS_PER_GROUP + k + 1, :]
                for k in range(EXPERTS_PER_GROUP)]
        tops.append(_top2_of4(rows))
    best = tops[0][0] + tops[0][2]
    g_idx = jnp.zeros(best.shape, jnp.int32)
    loc1, loc2 = tops[0][1], tops[0][3]
    for g in range(1, N_GROUPS):
        gs = tops[g][0] + tops[g][2]
        gt = gs > best
        best = jnp.where(gt, gs, best)
        g_idx = jnp.where(gt, g, g_idx)
        loc1 = jnp.where(gt, tops[g][1], loc1)
        loc2 = jnp.where(gt, tops[g][3], loc2)
    e0 = g_idx * EXPERTS_PER_GROUP + loc1
    e1 = g_idx * EXPERTS_PER_GROUP + loc2
    erow = lax.broadcasted_iota(jnp.int32, (N_EXPERTS, ts), 0)
    is0 = erow == e0
    is1 = erow == e1
    s0 = jnp.sum(jnp.where(is0, scores, 0.0), axis=0, keepdims=True)
    s1 = jnp.sum(jnp.where(is1, scores, 0.0), axis=0, keepdims=True)
    ssum = s0 + s1
    yield

    onehot = jnp.logical_or(is0, is1).astype(BF16)
    src = lax.broadcasted_iota(jnp.int32, (ts, ts), 0)
    dst = lax.broadcasted_iota(jnp.int32, (ts, ts), 1)
    earlier = (src < dst).astype(BF16)
    prior = _dot(onehot, earlier) + base_ref[:, 0:1]
    r0 = jnp.sum(jnp.where(is0, prior, 0.0), axis=0, keepdims=True)
    r1 = jnp.sum(jnp.where(is1, prior, 0.0), axis=0, keepdims=True)
    counts = jnp.sum(onehot.astype(F32), axis=1, keepdims=True)
    base_ref[...] = base_ref[...] + jnp.where(step > 0, counts, 0.0)

    zi = jnp.zeros((SUBLANES - 4, ts), jnp.int32)
    ri_ref[...] = jnp.concatenate([e0, e1, r0.astype(jnp.int32), r1.astype(jnp.int32), zi], axis=0)
    zf = jnp.zeros((SUBLANES - 2, ts), F32)
    rf_ref[...] = jnp.concatenate([s0 / ssum, s1 / ssum, zf], axis=0)
    cnt_ref[...] = base_ref[...]


def _mixer(layer, x, xb0, mod3, mb0, bsz, w_in, scw, cvw, cvb, cvg, cvbeta, sgg, sgbeta, sgw, sgbt,
           wbr, wg, bg, wo, ln1g, ln1b, wrt, brc):
    _, seq, d = x.shape
    ts = MIX_ROWS
    ns = seq // ts
    n_tok = bsz * seq

    n_tiles = bsz * ns

    def const(shape):
        zeros = (0,) * len(shape)
        return pl.BlockSpec(shape, lambda s: zeros, pipeline_mode=pl.Buffered(1))

    def first_half(s):
        return jnp.minimum(s, n_tiles - 1)

    def second_half(s):
        return jnp.maximum(s - 1, 0)

    hbm = pl.BlockSpec(memory_space=pl.ANY)
    in_specs = [
        pl.BlockSpec((1, ts, d), lambda s: (first_half(s) // ns + xb0, first_half(s) % ns, 0)),
        pl.BlockSpec((1, 1, 6 * d), lambda s: (first_half(s) // ns + mb0, 0, 0)),
        pl.BlockSpec((1, 1, 6 * d), lambda s: (second_half(s) // ns + mb0, 0, 0)),
        hbm, const(scw.shape), const(cvw.shape), const(cvb.shape),
        const(cvg.shape), const(cvbeta.shape), const(sgg.shape), const(sgbeta.shape),
        const(sgw.shape), const(sgbt.shape), hbm, hbm, const(bg.shape),
        hbm, const(ln1g.shape), const(ln1b.shape), const(wrt.shape), const(brc.shape),
    ]
    out_specs = [
        pl.BlockSpec((1, ts, d), lambda s: (second_half(s) // ns, second_half(s) % ns, 0)),
        pl.BlockSpec((1, ts, d // 2), lambda s: (second_half(s) // ns, second_half(s) % ns, 0)),
        pl.BlockSpec((SUBLANES, ts), lambda s: (0, second_half(s))),
        pl.BlockSpec((SUBLANES, ts), lambda s: (0, second_half(s))),
        pl.BlockSpec((N_EXPERTS, LANES), lambda s: (0, 0)),
    ]
    out_shape = [
        jax.ShapeDtypeStruct((bsz, seq, d), F32),
        jax.ShapeDtypeStruct((bsz, seq, d // 2), jnp.uint32),
        jax.ShapeDtypeStruct((SUBLANES, n_tok), jnp.int32),
        jax.ShapeDtypeStruct((SUBLANES, n_tok), F32),
        jax.ShapeDtypeStruct((N_EXPERTS, LANES), F32),
    ]
    n_in = w_in.shape[-1]
    return pl.pallas_call(
        functools.partial(_mixer_kernel, layer, ns),
        grid=(n_tiles + 1,),
        in_specs=in_specs,
        out_specs=out_specs,
        out_shape=out_shape,
        scratch_shapes=[
            pltpu.VMEM((d, n_in), BF16),
            pltpu.VMEM((3, W_BRANCH, d), BF16),
            pltpu.VMEM((d, 3 * d), BF16),
            pltpu.VMEM((d, d), BF16),
            pltpu.VMEM((2, _stage_rows(n_in), n_in), F32),
            pltpu.VMEM((2, _stage_rows(3 * d), 3 * d), F32),
            pltpu.VMEM((2, _stage_rows(d), d), F32),
            pltpu.SemaphoreType.DMA((3, 2)),
            pltpu.VMEM((SC_HALO + ts, W_BRANCH), F32),
            pltpu.VMEM((CV_HALO + ts, W_BRANCH), F32),
            pltpu.VMEM((ts, W_BRANCH), F32),
            pltpu.VMEM((3, ts, W_BRANCH), BF16),
            pltpu.VMEM((ts, 3 * W_BRANCH), F32),
            pltpu.VMEM((ts, 2 * W_BRANCH), F32),
            pltpu.VMEM((ts, 3 * D_MODEL), F32),
            pltpu.VMEM((ts, d), BF16),
            pltpu.VMEM((ts, d), F32),
            pltpu.VMEM((N_EXPERTS, LANES), F32),
        ],
        compiler_params=pltpu.CompilerParams(
            dimension_semantics=("arbitrary",),
            vmem_limit_bytes=MIXER_VMEM_LIMIT),
    )(x, mod3, mod3, w_in, scw, cvw, cvb, cvg, cvbeta, sgg, sgbeta, sgw, sgbt, wbr, wg, bg, wo,
      ln1g, ln1b, wrt, brc)


def _sc_workers():
    info = plsc.get_sparse_core_info()
    return info.num_cores, info.num_cores * info.num_subcores


def _sc_scatter_rows(rows, dest_a, dest_b, n_out):
    n, d = rows.shape
    nc, nw = _sc_workers()
    per_w = n // nw
    n_win = per_w // SC_WINDOW
    ia = dest_a.reshape(nw, n_win, SC_WINDOW)
    ib = dest_b.reshape(nw, n_win, SC_WINDOW)
    mesh = plsc.VectorSubcoreMesh(core_axis_name="c", subcore_axis_name="s")

    @functools.partial(
        pl.kernel, mesh=mesh,
        out_type=jax.ShapeDtypeStruct((n_out, d), rows.dtype),
        scratch_types=[
            pltpu.VMEM((n_win, SC_WINDOW), jnp.int32),
            pltpu.VMEM((n_win, SC_WINDOW), jnp.int32),
            pltpu.VMEM((SC_WINDOW, d), rows.dtype),
        ],
    )
    def scatter(rows_hbm, ia_hbm, ib_hbm, out_hbm, ia_v, ib_v, rows_v):
        wid = lax.axis_index("s") * nc + lax.axis_index("c")
        pltpu.sync_copy(ia_hbm.at[wid], ia_v)
        pltpu.sync_copy(ib_hbm.at[wid], ib_v)
        base = wid * per_w

        @pl.loop(0, n_win)
        def _(j):
            pltpu.sync_copy(rows_hbm.at[pl.ds(base + j * SC_WINDOW, SC_WINDOW)], rows_v)
            pltpu.sync_copy(rows_v, out_hbm.at[ia_v.at[j]])
            pltpu.sync_copy(rows_v, out_hbm.at[ib_v.at[j]])

    return scatter(rows, ia, ib)


def _sc_gather_rows(table, idx):
    n = idx.shape[0]
    d = table.shape[1]
    nc, nw = _sc_workers()
    per_w = n // nw
    n_win = per_w // SC_WINDOW
    idx3 = idx.reshape(nw, n_win, SC_WINDOW)
    mesh = plsc.VectorSubcoreMesh(core_axis_name="c", subcore_axis_name="s")

    @functools.partial(
        pl.kernel, mesh=mesh,
        out_type=jax.ShapeDtypeStruct((n, d), table.dtype),
        scratch_types=[
            pltpu.VMEM((n_win, SC_WINDOW), jnp.int32),
            pltpu.VMEM((SC_WINDOW, d), table.dtype),
        ],
    )
    def gather(table_hbm, idx_hbm, out_hbm, idx_v, rows_v):
        wid = lax.axis_index("s") * nc + lax.axis_index("c")
        pltpu.sync_copy(idx_hbm.at[wid], idx_v)
        base = wid * per_w

        @pl.loop(0, n_win)
        def _(j):
            pltpu.sync_copy(table_hbm.at[idx_v.at[j]], rows_v)
            pltpu.sync_copy(rows_v, out_hbm.at[pl.ds(base + j * SC_WINDOW, SC_WINDOW)])

    return gather(table, idx3)


def _expert_kernel(be_ref, nused_ref, x_ref, w1_ref, w3_ref, w2_ref, o_ref, w1s, w3s, w2s):
    i = pl.program_id(0)
    new_expert = jnp.logical_or(i == 0, be_ref[i] != be_ref[jnp.maximum(i - 1, 0)])

    @pl.when(new_expert)
    def _():
        w1s[...] = w1_ref[0, 0].astype(BF16)
        w3s[...] = w3_ref[0, 0].astype(BF16)
        w2s[...] = w2_ref[0, 0].astype(BF16)

    @pl.when(i < nused_ref[0])
    def _():
        lo, hi = _unpack_bf16_pairs(x_ref[...])
        xb = jnp.concatenate([lo, hi], axis=1).astype(BF16)
        a = _dot(xb, w1s[...])
        h = a * jax.nn.sigmoid(a) * _dot(xb, w3s[...])
        o_ref[...] = _pack_bf16_pairs(_dot(h.astype(BF16), w2s[...]))


def _experts(layer, buf, block_e, n_used, w1, w3, w2):
    n_rows, dp = buf.shape
    nb = n_rows // MOE_ROWS
    d, fe = w1.shape[-2:]

    def row_map(i, be, nu):
        return (jnp.minimum(i, nu[0] - 1), 0)

    grid_spec = pltpu.PrefetchScalarGridSpec(
        num_scalar_prefetch=2,
        grid=(nb,),
        in_specs=[
            pl.BlockSpec((MOE_ROWS, dp), row_map),
            pl.BlockSpec((1, 1, d, fe), lambda i, be, nu: (layer, be[i], 0, 0)),
            pl.BlockSpec((1, 1, d, fe), lambda i, be, nu: (layer, be[i], 0, 0)),
            pl.BlockSpec((1, 1, fe, d), lambda i, be, nu: (layer, be[i], 0, 0)),
        ],
        out_specs=pl.BlockSpec((MOE_ROWS, dp), row_map),
        scratch_shapes=[pltpu.VMEM((d, fe), BF16), pltpu.VMEM((d, fe), BF16),
                        pltpu.VMEM((fe, d), BF16)],
    )
    return pl.pallas_call(
        _expert_kernel,
        grid_spec=grid_spec,
        out_shape=jax.ShapeDtypeStruct((n_rows, dp), jnp.uint32),
        compiler_params=pltpu.CompilerParams(dimension_semantics=("arbitrary",),
                                             vmem_limit_bytes=EXPERT_VMEM_LIMIT),
    )(block_e, n_used, buf, w1, w3, w2)


def _combine_kernel(x1_ref, ya_ref, yb_ref, gates_ref, mod_ref, g_ref, b_ref, *rest):
    o_ref = rest[-1]
    d = D_MODEL
    gate2 = mod_ref[0][:, 5 * d:6 * d]
    gates = gates_ref[...]
    ya = jnp.concatenate(_unpack_bf16_pairs(ya_ref[0]), axis=1)
    yb = jnp.concatenate(_unpack_bf16_pairs(yb_ref[0]), axis=1)
    h = gates[:, 0:1] * ya + gates[:, 1:2] * yb
    o_ref[0] = _layer_norm(ALPHA * x1_ref[0] + gate2 * h, g_ref[...], b_ref[...])


def _combine(x1, pairs, gates, mod3, mb0, g, b, out_bsz, ob0, prev):
    bsz, seq, d = x1.shape
    ts = COMB_ROWS
    ns = seq // ts
    in_specs = [
        pl.BlockSpec((1, ts, d), lambda i, j: (i, j, 0)),
        pl.BlockSpec((1, ts, d // 2), lambda i, j: (0, i * ns + j, 0)),
        pl.BlockSpec((1, ts, d // 2), lambda i, j: (1, i * ns + j, 0)),
        pl.BlockSpec((ts, TOP_K), lambda i, j: (i * ns + j, 0)),
        pl.BlockSpec((1, 1, 6 * d), lambda i, j: (i + mb0, 0, 0)),
        pl.BlockSpec((1, d), lambda i, j: (0, 0)),
        pl.BlockSpec((1, d), lambda i, j: (0, 0)),
    ]
    args = [x1, pairs, pairs, gates, mod3, g, b]
    aliases = {}
    if prev is not None:
        in_specs.append(pl.BlockSpec(memory_space=pl.ANY))
        aliases = {len(args): 0}
        args.append(prev)
    return pl.pallas_call(
        _combine_kernel,
        grid=(bsz, ns),
        in_specs=in_specs,
        out_specs=pl.BlockSpec((1, ts, d), lambda i, j: (i + ob0, j, 0)),
        out_shape=jax.ShapeDtypeStruct((out_bsz, seq, d), F32),
        input_output_aliases=aliases,
        compiler_params=pltpu.CompilerParams(dimension_semantics=("arbitrary", "arbitrary")),
    )(*args)


def _dispatch_plan(ri, cnt):
    n_tok = ri.shape[1]
    n_blocks = (n_tok * TOP_K + N_EXPERTS * (MOE_ROWS - 1) + MOE_ROWS - 1) // MOE_ROWS
    counts = cnt[:, 0].astype(jnp.int32)
    padded = (counts + MOE_ROWS - 1) // MOE_ROWS * MOE_ROWS
    pad_end = jnp.cumsum(padded)
    pad_start = pad_end - padded
    dest_a = pad_start[ri[0]] + ri[2]
    dest_b = pad_start[ri[1]] + ri[3]
    block_start = jnp.arange(n_blocks, dtype=jnp.int32) * MOE_ROWS
    block_e = jnp.minimum(jnp.sum(block_start[:, None] >= pad_end[None, :], axis=1),
                          N_EXPERTS - 1).astype(jnp.int32)
    n_used = (pad_end[-1:] // MOE_ROWS).astype(jnp.int32)
    return dest_a, dest_b, block_e, n_used, n_blocks * MOE_ROWS


def kernel(x, c, w_ada, b_ada, w_in, sc_conv, cv_conv, cv_conv_b, cv_ln_g, cv_ln_b, sg_ln_g, sg_ln_b, sg_w, sg_b, w_branch, w_gate, b_gate, w_o, ln1_g, ln1_b, w_router, b_router, w1, w3, w2, ln2_g, ln2_b):
    bsz, seq, d = x.shape
    mod3 = _ada(c, w_ada, b_ada).reshape(DEPTH, bsz, 1, 6 * d)
    wrt = w_router.T
    brc = b_router.reshape(N_EXPERTS, 1)
    w_gate2 = w_gate.reshape(DEPTH, d, 3 * d)
    cb = bsz // N_CHAINS
    n_tok = cb * seq
    chains = [(x, h * cb) for h in range(N_CHAINS)]
    for l in range(DEPTH):
        last = l == DEPTH - 1
        result = None
        for h in range(N_CHAINS):
            xin, xb0 = chains[h]
            x1, u2, ri, rf, cnt = _mixer(
                l, xin, xb0, mod3[l], h * cb, cb, w_in, sc_conv[l], cv_conv[l],
                cv_conv_b[l].reshape(1, -1), cv_ln_g[l].reshape(1, -1), cv_ln_b[l].reshape(1, -1),
                sg_ln_g[l].reshape(1, -1), sg_ln_b[l].reshape(1, -1), sg_w[l], sg_b[l].T,
                w_branch, w_gate2, b_gate[l],
                w_o, ln1_g[l].reshape(1, -1), ln1_b[l].reshape(1, -1), wrt, brc)
            dest_a, dest_b, block_e, n_used, n_rows = _dispatch_plan(ri, cnt)
            buf = _sc_scatter_rows(u2.reshape(n_tok, d // 2), dest_a, dest_b, n_rows)
            obuf = _experts(l, buf, block_e, n_used, w1, w3, w2)
            pair_idx = jnp.concatenate([dest_a, dest_b])
            pairs = _sc_gather_rows(obuf, pair_idx).reshape(TOP_K, n_tok, d // 2)
            gates = rf[0:TOP_K].T
            g2, b2 = ln2_g[l].reshape(1, -1), ln2_b[l].reshape(1, -1)
            if last:
                result = _combine(x1, pairs, gates, mod3[l], h * cb, g2, b2, bsz, h * cb, result)
            else:
                chains[h] = (_combine(x1, pairs, gates, mod3[l], h * cb, g2, b2, cb, 0, None), 0)
    return result
```

```python
import functools

import jax
import jax.numpy as jnp
from jax import lax
from jax.experimental import pallas as pl
from jax.experimental.pallas import tpu as pltpu
from jax.experimental.pallas import tpu_sc as plsc

D_MODEL = 1024
DEPTH = 2
W_BRANCH = 1024
SC_KERNEL = 3
CV_KERNEL = 31
CHUNK = 128
SG_HEADS = 8
N_EXPERTS = 16
N_GROUPS = 4
EXPERTS_PER_GROUP = N_EXPERTS // N_GROUPS
TOP_K = 2
D_EXPERT = 512
ALPHA = (2.0 * DEPTH) ** 0.25
LN_EPS = 1e-5

F32 = jnp.float32
BF16 = jnp.bfloat16

V7X_VMEM_BYTES = 64 * 1024 * 1024
MIXER_VMEM_LIMIT = V7X_VMEM_BYTES - 6 * 1024 * 1024
EXPERT_VMEM_LIMIT = V7X_VMEM_BYTES // 2
SUBLANES = 8
LANES = 128

MIX_ROWS = 256
SC_HALO = SUBLANES
CV_HALO = 32
CONV_ROWS = 128
TIE_LAG = 2
TIE_FREE_JOBS = 4
TAIL_STAGE_AFTER_JOBS = (3, 8)
CONV_COLS = 128
PROJ_COLS = 512
PACK_ROWS = 128
MOE_ROWS = 256
COMB_ROWS = 512
SC_WINDOW = 32
N_CHAINS = 2


def _dot(a, b):
    return jnp.dot(a, b, preferred_element_type=F32)


def _pack_bf16_pairs(v):
    m = v.shape[1] // 2
    lo = lax.bitcast_convert_type(v[:, 0:m].astype(BF16).astype(F32), jnp.uint32)
    hi = lax.bitcast_convert_type(v[:, m:2 * m].astype(BF16).astype(F32), jnp.uint32)
    return jnp.bitwise_or(jnp.bitwise_and(hi, jnp.uint32(0xFFFF0000)),
                          lax.shift_right_logical(lo, jnp.uint32(16)))


def _unpack_bf16_pairs(w):
    lo = lax.bitcast_convert_type(lax.shift_left(w, jnp.uint32(16)), F32)
    hi = lax.bitcast_convert_type(jnp.bitwise_and(w, jnp.uint32(0xFFFF0000)), F32)
    return lo, hi


def _layer_norm(v, g, b):
    mu = jnp.mean(v, axis=-1, keepdims=True)
    vc = v - mu
    var = jnp.mean(vc * vc, axis=-1, keepdims=True)
    return vc * lax.rsqrt(var + LN_EPS) * g + b


def _ada_kernel(c_ref, w_ref, b_ref, o_ref):
    c = c_ref[...]
    c_act = c * jax.nn.sigmoid(c)
    o_ref[0] = jnp.dot(c_act, w_ref[0], preferred_element_type=F32,
                       precision=lax.Precision.HIGHEST) + b_ref[0]


def _ada(c, w_ada, b_ada):
    bsz, d = c.shape
    n = w_ada.shape[-1]
    tn = 1536
    return pl.pallas_call(
        _ada_kernel,
        grid=(DEPTH, n // tn),
        in_specs=[
            pl.BlockSpec((bsz, d), lambda l, j: (0, 0)),
            pl.BlockSpec((1, d, tn), lambda l, j: (l, 0, j)),
            pl.BlockSpec((1, 1, tn), lambda l, j: (l, 0, j)),
        ],
        out_specs=pl.BlockSpec((1, bsz, tn), lambda l, j: (l, 0, j)),
        out_shape=jax.ShapeDtypeStruct((DEPTH, bsz, n), F32),
    )(c, w_ada, b_ada.reshape(DEPTH, 1, n))


def _top2_of4(rows):
    m1 = rows[0]
    i1 = jnp.zeros(rows[0].shape, jnp.int32)
    for k in range(1, 4):
        gt = rows[k] > m1
        m1 = jnp.where(gt, rows[k], m1)
        i1 = jnp.where(gt, k, i1)
    m2 = jnp.full(rows[0].shape, -jnp.inf, F32)
    i2 = jnp.zeros(rows[0].shape, jnp.int32)
    for k in range(4):
        cand = jnp.where(i1 == k, -jnp.inf, rows[k])
        gt = cand > m2
        m2 = jnp.where(gt, cand, m2)
        i2 = jnp.where(gt, k, i2)
    return m1, i1, m2, i2


def _zero_after(v):
    u = lax.bitcast_convert_type(v, jnp.uint32)
    u = lax.shift_right_logical(lax.shift_right_logical(u, jnp.uint32(16)), jnp.uint32(16))
    return lax.bitcast_convert_type(u, F32)


def _conv31_chunk(cvw_ref, cvbuf, cvout, r0, c0, tie):
    cs = slice(c0, c0 + CONV_COLS)
    acc = None
    for r in range(SUBLANES):
        lead = SUBLANES if r else 0
        part = None
        for m in range((CV_KERNEL - 1 - r) // SUBLANES + 1):
            k = CV_KERNEL - 1 - (SUBLANES * m + r)
            start = CV_HALO + r0 - lead - SUBLANES * m
            w_row = cvw_ref[k:k + 1, cs]
            if tie is not None and acc is None and part is None:
                w_row = w_row + tie
            term = w_row * cvbuf[start:start + lead + CONV_ROWS, cs]
            part = term if part is None else part + term
        part = part[lead - r:lead - r + CONV_ROWS]
        acc = part if acc is None else acc + part
    cvout[r0:r0 + CONV_ROWS, cs] = acc


def _bf16_weights(packed):
    return pltpu.bitcast(packed, BF16)


def _pack_weights_kernel(w_ref, o_ref):
    o_ref[0] = pltpu.bitcast(w_ref[0].astype(BF16), jnp.uint32)


def _pack_weights(w):
    g, k, n = w.shape
    kb = PACK_ROWS
    return pl.pallas_call(
        _pack_weights_kernel,
        grid=(g, k // kb),
        in_specs=[pl.BlockSpec((1, kb, n), lambda i, j: (i, j, 0))],
        out_specs=pl.BlockSpec((1, kb // 2, n), lambda i, j: (i, j, 0)),
        out_shape=jax.ShapeDtypeStruct((g, k // 2, n), jnp.uint32),
        compiler_params=pltpu.CompilerParams(vmem_limit_bytes=EXPERT_VMEM_LIMIT),
    )(w)


def _mixer_kernel(tiles_per_seq, x_ref, mod_ref, modt_ref, w_in_ref, scw_ref, cvw_ref,
                  cvb_ref, cvg_ref, cvbeta_ref, sgg_ref, sgbeta_ref, sgw_ref, sgbt_ref, wbr_ref,
                  wg_ref, bg_ref, wo_ref, ln1g_ref, ln1b_ref, wrt_ref, brc_ref,
                  x1_ref, u2_ref, ri_ref, rf_ref, cnt_ref,
                  qbuf, cvbuf, cvout, ybuf, pa_buf, pc_buf, gl_buf, mg_buf, xs_buf, base_ref):
    ts = MIX_ROWS
    d = D_MODEL
    wb = W_BRANCH
    step = pl.program_id(0)
    tile = jnp.minimum(step, pl.num_programs(0) - 2)
    first_tile = tile % tiles_per_seq == 0

    @pl.when(step == 0)
    def _():
        base_ref[...] = jnp.zeros_like(base_ref)
        mg_buf[...] = jnp.zeros_like(mg_buf)
        xs_buf[...] = jnp.zeros_like(xs_buf)

    def w_in(c0):
        return _bf16_weights(w_in_ref[0, :, c0:c0 + PROJ_COLS])

    def w_gate(c0):
        return _bf16_weights(wg_ref[0, :, c0:c0 + PROJ_COLS])

    @pl.when(first_tile)
    def _():
        qbuf[0:SC_HALO, :] = jnp.zeros((SC_HALO, wb), F32)
        cvbuf[0:CV_HALO, :] = jnp.zeros((CV_HALO, wb), F32)

    x = x_ref[0]
    mod = mod_ref[0]
    shift1, scale1 = mod[:, 0:d], mod[:, d:2 * d]
    ub = (x * (1.0 + scale1) + shift1).astype(BF16)

    for c0 in range(0, wb, PROJ_COLS):
        a = _dot(ub, w_in(3 * wb + c0))
        g = _dot(ub, w_in(4 * wb + c0))
        cvbuf[CV_HALO:CV_HALO + ts, c0:c0 + PROJ_COLS] = a * jax.nn.sigmoid(g)

    tail = _mixer_tail(step, mg_buf, xs_buf, modt_ref, wo_ref, ln1g_ref, ln1b_ref, wrt_ref,
                       brc_ref, x1_ref, u2_ref, ri_ref, rf_ref, cnt_ref, base_ref)
    next(tail)

    mxu_jobs = ([(pa_buf, w_in, c0, c0) for c0 in range(0, 3 * wb, PROJ_COLS)]
                + [(pc_buf, w_in, c0, 5 * wb + c0) for c0 in range(0, 2 * wb, PROJ_COLS)]
                + [(gl_buf, w_gate, c0, c0) for c0 in range(0, 3 * d, PROJ_COLS)])
    conv_jobs = [(r0, c0) for c0 in range(0, wb, CONV_COLS) for r0 in range(0, ts, CONV_ROWS)]
    ties = {}
    for i in range(max(len(mxu_jobs), len(conv_jobs))):
        if i < len(conv_jobs):
            _conv31_chunk(cvw_ref, cvbuf, cvout, *conv_jobs[i], ties.get(i))
        if i < len(mxu_jobs):
            dst, weights, dc, wc = mxu_jobs[i]
            res = _dot(ub, weights(wc))
            dst[:, dc:dc + PROJ_COLS] = res
            if i < len(mxu_jobs) - TIE_FREE_JOBS:
                ties[i + TIE_LAG] = _zero_after(res[ts - 1:ts, PROJ_COLS - CONV_COLS:PROJ_COLS])
        if i in TAIL_STAGE_AFTER_JOBS:
            next(tail, None)
    cvbuf[0:CV_HALO, :] = cvbuf[ts:ts + CV_HALO, :]

    qbuf[SC_HALO:SC_HALO + ts, :] = pa_buf[:, wb:2 * wb] * pa_buf[:, 2 * wb:3 * wb]
    conv = scw_ref[SC_KERNEL - 1:SC_KERNEL, :] * qbuf[SC_HALO:SC_HALO + ts, :]
    for k in range(SC_KERNEL - 1):
        off = SC_HALO - (SC_KERNEL - 1) + k
        conv = conv + scw_ref[k:k + 1, :] * qbuf[off:off + ts, :]
    ybuf[0] = (pa_buf[:, 0:wb] * conv).astype(BF16)
    qbuf[0:SC_HALO, :] = qbuf[ts:ts + SC_HALO, :]
    z0 = _dot(ybuf[0], _bf16_weights(wbr_ref[0, 0]))

    cv = _layer_norm(cvout[...] + cvb_ref[...], cvg_ref[...], cvbeta_ref[...])
    ybuf[1] = (cv * jax.nn.sigmoid(cv)).astype(BF16)
    z1 = _dot(ybuf[1], _bf16_weights(wbr_ref[0, 1]))

    gu = jax.nn.gelu(pc_buf[:, 0:wb])
    gv = _layer_norm(jax.nn.gelu(pc_buf[:, wb:2 * wb]), sgg_ref[...], sgbeta_ref[...]).astype(BF16)
    merged = (jax.nn.sigmoid(gl_buf[:, 0:d] + bg_ref[0:1, :]) * z0
              + jax.nn.sigmoid(gl_buf[:, d:2 * d] + bg_ref[1:2, :]) * z1)
    g2 = jax.nn.sigmoid(gl_buf[:, 2 * d:3 * d] + bg_ref[2:3, :])
    row = lax.broadcasted_iota(jnp.int32, (CHUNK, CHUNK), 0)
    col = lax.broadcasted_iota(jnp.int32, (CHUNK, CHUNK), 1)
    hd = wb // SG_HEADS
    for h in range(SG_HEADS):
        wm = jnp.where(row >= col, sgw_ref[h], 0.0).astype(BF16)
        bias = sgbt_ref[:, h:h + 1]
        for n in range(ts // CHUNK):
            rs = slice(n * CHUNK, (n + 1) * CHUNK)
            cs = slice(h * hd, (h + 1) * hd)
            mixed = _dot(wm, gv[rs, cs]) + bias
            ybuf[2, rs, cs] = (gu[rs, cs] * mixed).astype(BF16)

    merged = merged + g2 * _dot(ybuf[2], _bf16_weights(wbr_ref[0, 2]))
    mg_buf[...] = merged.astype(BF16)
    xs_buf[...] = x


def _mixer_tail(step, mg_buf, xs_buf, modt_ref, wo_ref, ln1g_ref, ln1b_ref, wrt_ref, brc_ref,
                x1_ref, u2_ref, ri_ref, rf_ref, cnt_ref, base_ref):
    ts = MIX_ROWS
    d = D_MODEL
    modt = modt_ref[0]
    gate1, shift2, scale2 = modt[:, 2 * d:3 * d], modt[:, 3 * d:4 * d], modt[:, 4 * d:5 * d]
    hmix = _dot(mg_buf[...], _bf16_weights(wo_ref[0]))
    x1 = _layer_norm(ALPHA * xs_buf[...] + gate1 * hmix, ln1g_ref[...], ln1b_ref[...])
    x1_ref[0] = x1
    u2 = x1 * (1.0 + scale2) + shift2
    u2_ref[0] = _pack_bf16_pairs(u2)
    yield

    logits = lax.dot_general(wrt_ref[...], u2, (((1,), (1,)), ((), ())),
                             preferred_element_type=F32,
                             precision=lax.Precision.HIGHEST)
    mx = jnp.max(logits, axis=0, keepdims=True)
    ex = jnp.exp(logits - mx)
    scores = ex / jnp.sum(ex, axis=0, keepdims=True)
    sel = scores + brc_ref[...]
    tops = []
    for g in range(N_GROUPS):
        rows = [sel[g * EXPERTS_PER_GROUP + k:g * EXPERTS_PER_GROUP + k + 1, :]
                for k in range(EXPERTS_PER_GROUP)]
        tops.append(_top2_of4(rows))
    best = tops[0][0] + tops[0][2]
    g_idx = jnp.zeros(best.shape, jnp.int32)
    loc1, loc2 = tops[0][1], tops[0][3]
    for g in range(1, N_GROUPS):
        gs = tops[g][0] + tops[g][2]
        gt = gs > best
        best = jnp.where(gt, gs, best)
        g_idx = jnp.where(gt, g, g_idx)
        loc1 = jnp.where(gt, tops[g][1], loc1)
        loc2 = jnp.where(gt, tops[g][3], loc2)
    e0 = g_idx * EXPERTS_PER_GROUP + loc1
    e1 = g_idx * EXPERTS_PER_GROUP + loc2
    erow = lax.broadcasted_iota(jnp.int32, (N_EXPERTS, ts), 0)
    is0 = erow == e0
    is1 = erow == e1
    s0 = jnp.sum(jnp.where(is0, scores, 0.0), axis=0, keepdims=True)
    s1 = jnp.sum(jnp.where(is1, scores, 0.0), axis=0, keepdims=True)
    ssum = s0 + s1
    yield

    onehot = jnp.logical_or(is0, is1).astype(BF16)
    src = lax.broadcasted_iota(jnp.int32, (ts, ts), 0)
    dst = lax.broadcasted_iota(jnp.int32, (ts, ts), 1)
    earlier = (src < dst).astype(BF16)
    prior = _dot(onehot, earlier) + base_ref[:, 0:1]
    r0 = jnp.sum(jnp.where(is0, prior, 0.0), axis=0, keepdims=True)
    r1 = jnp.sum(jnp.where(is1, prior, 0.0), axis=0, keepdims=True)
    counts = jnp.sum(onehot.astype(F32), axis=1, keepdims=True)
    base_ref[...] = base_ref[...] + jnp.where(step > 0, counts, 0.0)

    zi = jnp.zeros((SUBLANES - 4, ts), jnp.int32)
    ri_ref[...] = jnp.concatenate([e0, e1, r0.astype(jnp.int32), r1.astype(jnp.int32), zi], axis=0)
    zf = jnp.zeros((SUBLANES - 2, ts), F32)
    rf_ref[...] = jnp.concatenate([s0 / ssum, s1 / ssum, zf], axis=0)
    cnt_ref[...] = base_ref[...]


def _mixer(layer, x, xb0, mod3, mb0, bsz, w_in, scw, cvw, cvb, cvg, cvbeta, sgg, sgbeta, sgw, sgbt,
           wbr, wg, bg, wo, ln1g, ln1b, wrt, brc):
    _, seq, d = x.shape
    ts = MIX_ROWS
    ns = seq // ts
    n_tok = bsz * seq

    n_tiles = bsz * ns

    def const(shape):
        zeros = (0,) * len(shape)
        return pl.BlockSpec(shape, lambda s: zeros, pipeline_mode=pl.Buffered(1))

    def first_half(s):
        return jnp.minimum(s, n_tiles - 1)

    def second_half(s):
        return jnp.maximum(s - 1, 0)

    def layer_weights(shape):
        block = (1,) + tuple(shape[1:])
        index = (layer,) + (0,) * (len(shape) - 1)
        return pl.BlockSpec(block, lambda s: index, pipeline_mode=pl.Buffered(1))

    in_specs = [
        pl.BlockSpec((1, ts, d), lambda s: (first_half(s) // ns + xb0, first_half(s) % ns, 0)),
        pl.BlockSpec((1, 1, 6 * d), lambda s: (first_half(s) // ns + mb0, 0, 0)),
        pl.BlockSpec((1, 1, 6 * d), lambda s: (second_half(s) // ns + mb0, 0, 0)),
        layer_weights(w_in.shape), const(scw.shape), const(cvw.shape), const(cvb.shape),
        const(cvg.shape), const(cvbeta.shape), const(sgg.shape), const(sgbeta.shape),
        const(sgw.shape), const(sgbt.shape), layer_weights(wbr.shape), layer_weights(wg.shape),
        const(bg.shape), layer_weights(wo.shape), const(ln1g.shape), const(ln1b.shape),
        const(wrt.shape), const(brc.shape),
    ]
    out_specs = [
        pl.BlockSpec((1, ts, d), lambda s: (second_half(s) // ns, second_half(s) % ns, 0)),
        pl.BlockSpec((1, ts, d // 2), lambda s: (second_half(s) // ns, second_half(s) % ns, 0)),
        pl.BlockSpec((SUBLANES, ts), lambda s: (0, second_half(s))),
        pl.BlockSpec((SUBLANES, ts), lambda s: (0, second_half(s))),
        pl.BlockSpec((N_EXPERTS, LANES), lambda s: (0, 0)),
    ]
    out_shape = [
        jax.ShapeDtypeStruct((bsz, seq, d), F32),
        jax.ShapeDtypeStruct((bsz, seq, d // 2), jnp.uint32),
        jax.ShapeDtypeStruct((SUBLANES, n_tok), jnp.int32),
        jax.ShapeDtypeStruct((SUBLANES, n_tok), F32),
        jax.ShapeDtypeStruct((N_EXPERTS, LANES), F32),
    ]
    return pl.pallas_call(
        functools.partial(_mixer_kernel, ns),
        grid=(n_tiles + 1,),
        in_specs=in_specs,
        out_specs=out_specs,
        out_shape=out_shape,
        scratch_shapes=[
            pltpu.VMEM((SC_HALO + ts, W_BRANCH), F32),
            pltpu.VMEM((CV_HALO + ts, W_BRANCH), F32),
            pltpu.VMEM((ts, W_BRANCH), F32),
            pltpu.VMEM((3, ts, W_BRANCH), BF16),
            pltpu.VMEM((ts, 3 * W_BRANCH), F32),
            pltpu.VMEM((ts, 2 * W_BRANCH), F32),
            pltpu.VMEM((ts, 3 * D_MODEL), F32),
            pltpu.VMEM((ts, d), BF16),
            pltpu.VMEM((ts, d), F32),
            pltpu.VMEM((N_EXPERTS, LANES), F32),
        ],
        compiler_params=pltpu.CompilerParams(
            dimension_semantics=("arbitrary",),
            vmem_limit_bytes=MIXER_VMEM_LIMIT),
    )(x, mod3, mod3, w_in, scw, cvw, cvb, cvg, cvbeta, sgg, sgbeta, sgw, sgbt, wbr, wg, bg, wo,
      ln1g, ln1b, wrt, brc)


def _sc_workers():
    info = plsc.get_sparse_core_info()
    return info.num_cores, info.num_cores * info.num_subcores


def _sc_scatter_rows(rows, dest_a, dest_b, n_out):
    n, d = rows.shape
    nc, nw = _sc_workers()
    per_w = n // nw
    n_win = per_w // SC_WINDOW
    ia = dest_a.reshape(nw, n_win, SC_WINDOW)
    ib = dest_b.reshape(nw, n_win, SC_WINDOW)
    mesh = plsc.VectorSubcoreMesh(core_axis_name="c", subcore_axis_name="s")

    @functools.partial(
        pl.kernel, mesh=mesh,
        out_type=jax.ShapeDtypeStruct((n_out, d), rows.dtype),
        scratch_types=[
            pltpu.VMEM((n_win, SC_WINDOW), jnp.int32),
            pltpu.VMEM((n_win, SC_WINDOW), jnp.int32),
            pltpu.VMEM((SC_WINDOW, d), rows.dtype),
        ],
    )
    def scatter(rows_hbm, ia_hbm, ib_hbm, out_hbm, ia_v, ib_v, rows_v):
        wid = lax.axis_index("s") * nc + lax.axis_index("c")
        pltpu.sync_copy(ia_hbm.at[wid], ia_v)
        pltpu.sync_copy(ib_hbm.at[wid], ib_v)
        base = wid * per_w

        @pl.loop(0, n_win)
        def _(j):
            pltpu.sync_copy(rows_hbm.at[pl.ds(base + j * SC_WINDOW, SC_WINDOW)], rows_v)
            pltpu.sync_copy(rows_v, out_hbm.at[ia_v.at[j]])
            pltpu.sync_copy(rows_v, out_hbm.at[ib_v.at[j]])

    return scatter(rows, ia, ib)


def _sc_gather_rows(table, idx):
    n = idx.shape[0]
    d = table.shape[1]
    nc, nw = _sc_workers()
    per_w = n // nw
    n_win = per_w // SC_WINDOW
    idx3 = idx.reshape(nw, n_win, SC_WINDOW)
    mesh = plsc.VectorSubcoreMesh(core_axis_name="c", subcore_axis_name="s")

    @functools.partial(
        pl.kernel, mesh=mesh,
        out_type=jax.ShapeDtypeStruct((n, d), table.dtype),
        scratch_types=[
            pltpu.VMEM((n_win, SC_WINDOW), jnp.int32),
            pltpu.VMEM((SC_WINDOW, d), table.dtype),
        ],
    )
    def gather(table_hbm, idx_hbm, out_hbm, idx_v, rows_v):
        wid = lax.axis_index("s") * nc + lax.axis_index("c")
        pltpu.sync_copy(idx_hbm.at[wid], idx_v)
        base = wid * per_w

        @pl.loop(0, n_win)
        def _(j):
            pltpu.sync_copy(table_hbm.at[idx_v.at[j]], rows_v)
            pltpu.sync_copy(rows_v, out_hbm.at[pl.ds(base + j * SC_WINDOW, SC_WINDOW)])

    return gather(table, idx3)


def _expert_kernel(be_ref, nused_ref, x_ref, w1_ref, w3_ref, w2_ref, o_ref, w1s, w3s, w2s):
    i = pl.program_id(0)
    new_expert = jnp.logical_or(i == 0, be_ref[i] != be_ref[jnp.maximum(i - 1, 0)])

    @pl.when(new_expert)
    def _():
        w1s[...] = w1_ref[0, 0].astype(BF16)
        w3s[...] = w3_ref[0, 0].astype(BF16)
        w2s[...] = w2_ref[0, 0].astype(BF16)

    @pl.when(i < nused_ref[0])
    def _():
        lo, hi = _unpack_bf16_pairs(x_ref[...])
        xb = jnp.concatenate([lo, hi], axis=1).astype(BF16)
        a = _dot(xb, w1s[...])
        h = a * jax.nn.sigmoid(a) * _dot(xb, w3s[...])
        o_ref[...] = _pack_bf16_pairs(_dot(h.astype(BF16), w2s[...]))


def _experts(layer, buf, block_e, n_used, w1, w3, w2):
    n_rows, dp = buf.shape
    nb = n_rows // MOE_ROWS
    d, fe = w1.shape[-2:]

    def row_map(i, be, nu):
        return (jnp.minimum(i, nu[0] - 1), 0)

    grid_spec = pltpu.PrefetchScalarGridSpec(
        num_scalar_prefetch=2,
        grid=(nb,),
        in_specs=[
            pl.BlockSpec((MOE_ROWS, dp), row_map),
            pl.BlockSpec((1, 1, d, fe), lambda i, be, nu: (layer, be[i], 0, 0)),
            pl.BlockSpec((1, 1, d, fe), lambda i, be, nu: (layer, be[i], 0, 0)),
            pl.BlockSpec((1, 1, fe, d), lambda i, be, nu: (layer, be[i], 0, 0)),
        ],
        out_specs=pl.BlockSpec((MOE_ROWS, dp), row_map),
        scratch_shapes=[pltpu.VMEM((d, fe), BF16), pltpu.VMEM((d, fe), BF16),
                        pltpu.VMEM((fe, d), BF16)],
    )
    return pl.pallas_call(
        _expert_kernel,
        grid_spec=grid_spec,
        out_shape=jax.ShapeDtypeStruct((n_rows, dp), jnp.uint32),
        compiler_params=pltpu.CompilerParams(dimension_semantics=("arbitrary",),
                                             vmem_limit_bytes=EXPERT_VMEM_LIMIT),
    )(block_e, n_used, buf, w1, w3, w2)


def _combine_kernel(x1_ref, ya_ref, yb_ref, gates_ref, mod_ref, g_ref, b_ref, *rest):
    o_ref = rest[-1]
    d = D_MODEL
    gate2 = mod_ref[0][:, 5 * d:6 * d]
    gates = gates_ref[...]
    ya = jnp.concatenate(_unpack_bf16_pairs(ya_ref[0]), axis=1)
    yb = jnp.concatenate(_unpack_bf16_pairs(yb_ref[0]), axis=1)
    h = gates[:, 0:1] * ya + gates[:, 1:2] * yb
    o_ref[0] = _layer_norm(ALPHA * x1_ref[0] + gate2 * h, g_ref[...], b_ref[...])


def _combine(x1, pairs, gates, mod3, mb0, g, b, out_bsz, ob0, prev):
    bsz, seq, d = x1.shape
    ts = COMB_ROWS
    ns = seq // ts
    in_specs = [
        pl.BlockSpec((1, ts, d), lambda i, j: (i, j, 0)),
        pl.BlockSpec((1, ts, d // 2), lambda i, j: (0, i * ns + j, 0)),
        pl.BlockSpec((1, ts, d // 2), lambda i, j: (1, i * ns + j, 0)),
        pl.BlockSpec((ts, TOP_K), lambda i, j: (i * ns + j, 0)),
        pl.BlockSpec((1, 1, 6 * d), lambda i, j: (i + mb0, 0, 0)),
        pl.BlockSpec((1, d), lambda i, j: (0, 0)),
        pl.BlockSpec((1, d), lambda i, j: (0, 0)),
    ]
    args = [x1, pairs, pairs, gates, mod3, g, b]
    aliases = {}
    if prev is not None:
        in_specs.append(pl.BlockSpec(memory_space=pl.ANY))
        aliases = {len(args): 0}
        args.append(prev)
    return pl.pallas_call(
        _combine_kernel,
        grid=(bsz, ns),
        in_specs=in_specs,
        out_specs=pl.BlockSpec((1, ts, d), lambda i, j: (i + ob0, j, 0)),
        out_shape=jax.ShapeDtypeStruct((out_bsz, seq, d), F32),
        input_output_aliases=aliases,
        compiler_params=pltpu.CompilerParams(dimension_semantics=("arbitrary", "arbitrary")),
    )(*args)


def _dispatch_plan(ri, cnt):
    n_tok = ri.shape[1]
    n_blocks = (n_tok * TOP_K + N_EXPERTS * (MOE_ROWS - 1) + MOE_ROWS - 1) // MOE_ROWS
    counts = cnt[:, 0].astype(jnp.int32)
    padded = (counts + MOE_ROWS - 1) // MOE_ROWS * MOE_ROWS
    pad_end = jnp.cumsum(padded)
    pad_start = pad_end - padded
    dest_a = pad_start[ri[0]] + ri[2]
    dest_b = pad_start[ri[1]] + ri[3]
    block_start = jnp.arange(n_blocks, dtype=jnp.int32) * MOE_ROWS
    block_e = jnp.minimum(jnp.sum(block_start[:, None] >= pad_end[None, :], axis=1),
                          N_EXPERTS - 1).astype(jnp.int32)
    n_used = (pad_end[-1:] // MOE_ROWS).astype(jnp.int32)
    return dest_a, dest_b, block_e, n_used, n_blocks * MOE_ROWS


def kernel(x, c, w_ada, b_ada, w_in, sc_conv, cv_conv, cv_conv_b, cv_ln_g, cv_ln_b, sg_ln_g, sg_ln_b, sg_w, sg_b, w_branch, w_gate, b_gate, w_o, ln1_g, ln1_b, w_router, b_router, w1, w3, w2, ln2_g, ln2_b):
    bsz, seq, d = x.shape
    mod3 = _ada(c, w_ada, b_ada).reshape(DEPTH, bsz, 1, 6 * d)
    wrt = w_router.T
    brc = b_router.reshape(N_EXPERTS, 1)
    w_in_p = _pack_weights(w_in)
    w_gate_p = _pack_weights(w_gate.reshape(DEPTH, d, 3 * d))
    w_o_p = _pack_weights(w_o)
    w_branch_p = _pack_weights(w_branch.reshape(DEPTH * 3, W_BRANCH, d)).reshape(
        DEPTH, 3, W_BRANCH // 2, d)
    cb = bsz // N_CHAINS
    n_tok = cb * seq
    chains = [(x, h * cb) for h in range(N_CHAINS)]
    for l in range(DEPTH):
        last = l == DEPTH - 1
        result = None
        for h in range(N_CHAINS):
            xin, xb0 = chains[h]
            x1, u2, ri, rf, cnt = _mixer(
                l, xin, xb0, mod3[l], h * cb, cb, w_in_p, sc_conv[l], cv_conv[l],
                cv_conv_b[l].reshape(1, -1), cv_ln_g[l].reshape(1, -1), cv_ln_b[l].reshape(1, -1),
                sg_ln_g[l].reshape(1, -1), sg_ln_b[l].reshape(1, -1), sg_w[l], sg_b[l].T,
                w_branch_p, w_gate_p, b_gate[l],
                w_o_p, ln1_g[l].reshape(1, -1), ln1_b[l].reshape(1, -1), wrt, brc)
            dest_a, dest_b, block_e, n_used, n_rows = _dispatch_plan(ri, cnt)
            buf = _sc_scatter_rows(u2.reshape(n_tok, d // 2), dest_a, dest_b, n_rows)
            obuf = _experts(l, buf, block_e, n_used, w1, w3, w2)
            pair_idx = jnp.concatenate([dest_a, dest_b])
            pairs = _sc_gather_rows(obuf, pair_idx).reshape(TOP_K, n_tok, d // 2)
            gates = rf[0:TOP_K].T
            g2, b2 = ln2_g[l].reshape(1, -1), ln2_b[l].reshape(1, -1)
            if last:
                result = _combine(x1, pairs, gates, mod3[l], h * cb, g2, b2, bsz, h * cb, result)
            else:
                chains[h] = (_combine(x1, pairs, gates, mod3[l], h * cb, g2, b2, cb, 0, None), 0)
    return result
```

```python
import functools

import jax
import jax.numpy as jnp
from jax import lax
from jax.experimental import pallas as pl
from jax.experimental.pallas import tpu as pltpu
from jax.experimental.pallas import tpu_sc as plsc

D_MODEL = 1024
DEPTH = 2
W_BRANCH = 1024
SC_KERNEL = 3
CV_KERNEL = 31
CHUNK = 128
SG_HEADS = 8
N_EXPERTS = 16
N_GROUPS = 4
EXPERTS_PER_GROUP = N_EXPERTS // N_GROUPS
TOP_K = 2
D_EXPERT = 512
ALPHA = (2.0 * DEPTH) ** 0.25
LN_EPS = 1e-5

F32 = jnp.float32
BF16 = jnp.bfloat16

V7X_VMEM_BYTES = 64 * 1024 * 1024
MIXER_VMEM_LIMIT = V7X_VMEM_BYTES - 6 * 1024 * 1024
EXPERT_VMEM_LIMIT = V7X_VMEM_BYTES // 2
SUBLANES = 8
LANES = 128

MIX_ROWS = 256
SC_HALO = SUBLANES
CV_HALO = 32
CONV_ROWS = 128
TIE_LAG = 2
TIE_FREE_JOBS = 4
TAIL_STAGE_AFTER_JOBS = (7, 12)
CONV_COLS = 128
PROJ_COLS = 512
PACK_ROWS = 128
MOE_ROWS = 512
MOE_SLAB = 256
COMB_ROWS = 512
SC_WINDOW = 32
N_CHAINS = 2


def _dot(a, b):
    return jnp.dot(a, b, preferred_element_type=F32)


def _pack_bf16_pairs(v):
    m = v.shape[1] // 2
    lo = lax.bitcast_convert_type(v[:, 0:m].astype(BF16).astype(F32), jnp.uint32)
    hi = lax.bitcast_convert_type(v[:, m:2 * m].astype(BF16).astype(F32), jnp.uint32)
    return jnp.bitwise_or(jnp.bitwise_and(hi, jnp.uint32(0xFFFF0000)),
                          lax.shift_right_logical(lo, jnp.uint32(16)))


def _unpack_bf16_pairs(w):
    lo = lax.bitcast_convert_type(lax.shift_left(w, jnp.uint32(16)), F32)
    hi = lax.bitcast_convert_type(jnp.bitwise_and(w, jnp.uint32(0xFFFF0000)), F32)
    return lo, hi


def _layer_norm(v, g, b):
    mu = jnp.mean(v, axis=-1, keepdims=True)
    vc = v - mu
    var = jnp.mean(vc * vc, axis=-1, keepdims=True)
    return vc * lax.rsqrt(var + LN_EPS) * g + b


def _ada_kernel(c_ref, w_ref, b_ref, o_ref):
    c = c_ref[...]
    c_act = c * jax.nn.sigmoid(c)
    o_ref[0] = jnp.dot(c_act, w_ref[0], preferred_element_type=F32,
                       precision=lax.Precision.HIGHEST) + b_ref[0]


def _ada(c, w_ada, b_ada):
    bsz, d = c.shape
    n = w_ada.shape[-1]
    tn = 1536
    return pl.pallas_call(
        _ada_kernel,
        grid=(DEPTH, n // tn),
        in_specs=[
            pl.BlockSpec((bsz, d), lambda l, j: (0, 0)),
            pl.BlockSpec((1, d, tn), lambda l, j: (l, 0, j)),
            pl.BlockSpec((1, 1, tn), lambda l, j: (l, 0, j)),
        ],
        out_specs=pl.BlockSpec((1, bsz, tn), lambda l, j: (l, 0, j)),
        out_shape=jax.ShapeDtypeStruct((DEPTH, bsz, n), F32),
    )(c, w_ada, b_ada.reshape(DEPTH, 1, n))


def _top2_of4(rows):
    m1 = rows[0]
    i1 = jnp.zeros(rows[0].shape, jnp.int32)
    for k in range(1, 4):
        gt = rows[k] > m1
        m1 = jnp.where(gt, rows[k], m1)
        i1 = jnp.where(gt, k, i1)
    m2 = jnp.full(rows[0].shape, -jnp.inf, F32)
    i2 = jnp.zeros(rows[0].shape, jnp.int32)
    for k in range(4):
        cand = jnp.where(i1 == k, -jnp.inf, rows[k])
        gt = cand > m2
        m2 = jnp.where(gt, cand, m2)
        i2 = jnp.where(gt, k, i2)
    return m1, i1, m2, i2


def _zero_after(v):
    u = lax.bitcast_convert_type(v, jnp.uint32)
    u = lax.shift_right_logical(lax.shift_right_logical(u, jnp.uint32(16)), jnp.uint32(16))
    return lax.bitcast_convert_type(u, F32)


def _conv31_chunk(cvw_ref, cvbuf, cvout, r0, c0, tie):
    cs = slice(c0, c0 + CONV_COLS)
    acc = None
    for r in range(SUBLANES):
        lead = SUBLANES if r else 0
        part = None
        for m in range((CV_KERNEL - 1 - r) // SUBLANES + 1):
            k = CV_KERNEL - 1 - (SUBLANES * m + r)
            start = CV_HALO + r0 - lead - SUBLANES * m
            w_row = cvw_ref[k:k + 1, cs]
            if tie is not None and acc is None and part is None:
                w_row = w_row + tie
            term = w_row * cvbuf[start:start + lead + CONV_ROWS, cs]
            part = term if part is None else part + term
        part = part[lead - r:lead - r + CONV_ROWS]
        acc = part if acc is None else acc + part
    cvout[r0:r0 + CONV_ROWS, cs] = acc


def _bf16_weights(packed):
    return pltpu.bitcast(packed, BF16)


def _pack_weights_kernel(w_ref, o_ref):
    o_ref[0] = pltpu.bitcast(w_ref[0].astype(BF16), jnp.uint32)


def _pack_weights(w):
    g, k, n = w.shape
    kb = PACK_ROWS
    return pl.pallas_call(
        _pack_weights_kernel,
        grid=(g, k // kb),
        in_specs=[pl.BlockSpec((1, kb, n), lambda i, j: (i, j, 0))],
        out_specs=pl.BlockSpec((1, kb // 2, n), lambda i, j: (i, j, 0)),
        out_shape=jax.ShapeDtypeStruct((g, k // 2, n), jnp.uint32),
        compiler_params=pltpu.CompilerParams(vmem_limit_bytes=EXPERT_VMEM_LIMIT),
    )(w)


def _mixer_kernel(tiles_per_seq, x_ref, mod_ref, modt_ref, w_in_ref, scw_ref, cvw_ref,
                  cvb_ref, cvg_ref, cvbeta_ref, sgg_ref, sgbeta_ref, sgw_ref, sgbt_ref, wbr_ref,
                  wg_ref, bg_ref, wo_ref, ln1g_ref, ln1b_ref, wrt_ref, brc_ref,
                  x1_ref, u2_ref, ri_ref, rf_ref, cnt_ref,
                  qbuf, cvbuf, cvout, ybuf, pa_buf, pc_buf, gl_buf, mg_buf, xs_buf, base_ref):
    ts = MIX_ROWS
    d = D_MODEL
    wb = W_BRANCH
    step = pl.program_id(0)
    tile = jnp.minimum(step, pl.num_programs(0) - 2)
    first_tile = tile % tiles_per_seq == 0

    @pl.when(step == 0)
    def _():
        base_ref[...] = jnp.zeros_like(base_ref)
        mg_buf[...] = jnp.zeros_like(mg_buf)
        xs_buf[...] = jnp.zeros_like(xs_buf)

    def w_in(c0):
        return _bf16_weights(w_in_ref[0, :, c0:c0 + PROJ_COLS])

    def w_gate(c0):
        return _bf16_weights(wg_ref[0, :, c0:c0 + PROJ_COLS])

    @pl.when(first_tile)
    def _():
        qbuf[0:SC_HALO, :] = jnp.zeros((SC_HALO, wb), F32)
        cvbuf[0:CV_HALO, :] = jnp.zeros((CV_HALO, wb), F32)

    x = x_ref[0]
    mod = mod_ref[0]
    shift1, scale1 = mod[:, 0:d], mod[:, d:2 * d]
    ub = (x * (1.0 + scale1) + shift1).astype(BF16)

    for c0 in range(0, wb, PROJ_COLS):
        a = _dot(ub, w_in(3 * wb + c0))
        g = _dot(ub, w_in(4 * wb + c0))
        cvbuf[CV_HALO:CV_HALO + ts, c0:c0 + PROJ_COLS] = a * jax.nn.sigmoid(g)

    tail = _mixer_tail(step, mg_buf, xs_buf, modt_ref, wo_ref, ln1g_ref, ln1b_ref, wrt_ref,
                       brc_ref, x1_ref, u2_ref, ri_ref, rf_ref, cnt_ref, base_ref)
    next(tail)

    mxu_jobs = ([(pa_buf, w_in, c0, c0) for c0 in range(0, 3 * wb, PROJ_COLS)]
                + [(pc_buf, w_in, c0, 5 * wb + c0) for c0 in range(0, 2 * wb, PROJ_COLS)]
                + [(gl_buf, w_gate, c0, c0) for c0 in range(0, 3 * d, PROJ_COLS)])
    conv_jobs = [(r0, c0) for c0 in range(0, wb, CONV_COLS) for r0 in range(0, ts, CONV_ROWS)]
    ties = {}
    for i in range(max(len(mxu_jobs), len(conv_jobs))):
        if i < len(conv_jobs):
            _conv31_chunk(cvw_ref, cvbuf, cvout, *conv_jobs[i], ties.get(i))
        if i < len(mxu_jobs):
            dst, weights, dc, wc = mxu_jobs[i]
            res = _dot(ub, weights(wc))
            dst[:, dc:dc + PROJ_COLS] = res
            if i < len(mxu_jobs) - TIE_FREE_JOBS:
                ties[i + TIE_LAG] = _zero_after(res[ts - 1:ts, PROJ_COLS - CONV_COLS:PROJ_COLS])
        if i in TAIL_STAGE_AFTER_JOBS:
            next(tail, None)
    cvbuf[0:CV_HALO, :] = cvbuf[ts:ts + CV_HALO, :]

    qbuf[SC_HALO:SC_HALO + ts, :] = pa_buf[:, wb:2 * wb] * pa_buf[:, 2 * wb:3 * wb]
    conv = scw_ref[SC_KERNEL - 1:SC_KERNEL, :] * qbuf[SC_HALO:SC_HALO + ts, :]
    for k in range(SC_KERNEL - 1):
        off = SC_HALO - (SC_KERNEL - 1) + k
        conv = conv + scw_ref[k:k + 1, :] * qbuf[off:off + ts, :]
    ybuf[0] = (pa_buf[:, 0:wb] * conv).astype(BF16)
    qbuf[0:SC_HALO, :] = qbuf[ts:ts + SC_HALO, :]
    z0 = _dot(ybuf[0], _bf16_weights(wbr_ref[0, 0]))

    cv = _layer_norm(cvout[...] + cvb_ref[...], cvg_ref[...], cvbeta_ref[...])
    ybuf[1] = (cv * jax.nn.sigmoid(cv)).astype(BF16)
    z1 = _dot(ybuf[1], _bf16_weights(wbr_ref[0, 1]))

    gu = jax.nn.gelu(pc_buf[:, 0:wb])
    gv = _layer_norm(jax.nn.gelu(pc_buf[:, wb:2 * wb]), sgg_ref[...], sgbeta_ref[...]).astype(BF16)
    merged = (jax.nn.sigmoid(gl_buf[:, 0:d] + bg_ref[0:1, :]) * z0
              + jax.nn.sigmoid(gl_buf[:, d:2 * d] + bg_ref[1:2, :]) * z1)
    g2 = jax.nn.sigmoid(gl_buf[:, 2 * d:3 * d] + bg_ref[2:3, :])
    row = lax.broadcasted_iota(jnp.int32, (CHUNK, CHUNK), 0)
    col = lax.broadcasted_iota(jnp.int32, (CHUNK, CHUNK), 1)
    hd = wb // SG_HEADS
    for h in range(SG_HEADS):
        wm = jnp.where(row >= col, sgw_ref[h], 0.0).astype(BF16)
        bias = sgbt_ref[:, h:h + 1]
        for n in range(ts // CHUNK):
            rs = slice(n * CHUNK, (n + 1) * CHUNK)
            cs = slice(h * hd, (h + 1) * hd)
            mixed = _dot(wm, gv[rs, cs]) + bias
            ybuf[2, rs, cs] = (gu[rs, cs] * mixed).astype(BF16)

    merged = merged + g2 * _dot(ybuf[2], _bf16_weights(wbr_ref[0, 2]))
    mg_buf[...] = merged.astype(BF16)
    xs_buf[...] = x


def _mixer_tail(step, mg_buf, xs_buf, modt_ref, wo_ref, ln1g_ref, ln1b_ref, wrt_ref, brc_ref,
                x1_ref, u2_ref, ri_ref, rf_ref, cnt_ref, base_ref):
    ts = MIX_ROWS
    d = D_MODEL
    modt = modt_ref[0]
    gate1, shift2, scale2 = modt[:, 2 * d:3 * d], modt[:, 3 * d:4 * d], modt[:, 4 * d:5 * d]
    hmix = _dot(mg_buf[...], _bf16_weights(wo_ref[0]))
    x1 = _layer_norm(ALPHA * xs_buf[...] + gate1 * hmix, ln1g_ref[...], ln1b_ref[...])
    x1_ref[0] = x1
    u2 = x1 * (1.0 + scale2) + shift2
    u2_ref[0] = _pack_bf16_pairs(u2)
    yield

    logits = lax.dot_general(wrt_ref[...], u2, (((1,), (1,)), ((), ())),
                             preferred_element_type=F32,
                             precision=lax.Precision.HIGHEST)
    mx = jnp.max(logits, axis=0, keepdims=True)
    ex = jnp.exp(logits - mx)
    scores = ex / jnp.sum(ex, axis=0, keepdims=True)
    sel = scores + brc_ref[...]
    tops = []
    for g in range(N_GROUPS):
        rows = [sel[g * EXPERTS_PER_GROUP + k:g * EXPERTS_PER_GROUP + k + 1, :]
                for k in range(EXPERTS_PER_GROUP)]
        tops.append(_top2_of4(rows))
    best = tops[0][0] + tops[0][2]
    g_idx = jnp.zeros(best.shape, jnp.int32)
    loc1, loc2 = tops[0][1], tops[0][3]
    for g in range(1, N_GROUPS):
        gs = tops[g][0] + tops[g][2]
        gt = gs > best
        best = jnp.where(gt, gs, best)
        g_idx = jnp.where(gt, g, g_idx)
        loc1 = jnp.where(gt, tops[g][1], loc1)
        loc2 = jnp.where(gt, tops[g][3], loc2)
    e0 = g_idx * EXPERTS_PER_GROUP + loc1
    e1 = g_idx * EXPERTS_PER_GROUP + loc2
    erow = lax.broadcasted_iota(jnp.int32, (N_EXPERTS, ts), 0)
    is0 = erow == e0
    is1 = erow == e1
    s0 = jnp.sum(jnp.where(is0, scores, 0.0), axis=0, keepdims=True)
    s1 = jnp.sum(jnp.where(is1, scores, 0.0), axis=0, keepdims=True)
    ssum = s0 + s1
    yield

    onehot = jnp.logical_or(is0, is1).astype(BF16)
    src = lax.broadcasted_iota(jnp.int32, (ts, ts), 0)
    dst = lax.broadcasted_iota(jnp.int32, (ts, ts), 1)
    earlier = (src < dst).astype(BF16)
    prior = _dot(onehot, earlier) + base_ref[:, 0:1]
    r0 = jnp.sum(jnp.where(is0, prior, 0.0), axis=0, keepdims=True)
    r1 = jnp.sum(jnp.where(is1, prior, 0.0), axis=0, keepdims=True)
    counts = jnp.sum(onehot.astype(F32), axis=1, keepdims=True)
    base_ref[...] = base_ref[...] + jnp.where(step > 0, counts, 0.0)

    zi = jnp.zeros((SUBLANES - 4, ts), jnp.int32)
    ri_ref[...] = jnp.concatenate([e0, e1, r0.astype(jnp.int32), r1.astype(jnp.int32), zi], axis=0)
    zf = jnp.zeros((SUBLANES - 2, ts), F32)
    rf_ref[...] = jnp.concatenate([s0 / ssum, s1 / ssum, zf], axis=0)
    cnt_ref[...] = base_ref[...]


def _mixer(layer, x, xb0, mod3, mb0, bsz, w_in, scw, cvw, cvb, cvg, cvbeta, sgg, sgbeta, sgw, sgbt,
           wbr, wg, bg, wo, ln1g, ln1b, wrt, brc):
    _, seq, d = x.shape
    ts = MIX_ROWS
    ns = seq // ts
    n_tok = bsz * seq

    n_tiles = bsz * ns

    def const(shape):
        zeros = (0,) * len(shape)
        return pl.BlockSpec(shape, lambda s: zeros, pipeline_mode=pl.Buffered(1))

    def first_half(s):
        return jnp.minimum(s, n_tiles - 1)

    def second_half(s):
        return jnp.maximum(s - 1, 0)

    def layer_weights(shape):
        block = (1,) + tuple(shape[1:])
        index = (layer,) + (0,) * (len(shape) - 1)
        return pl.BlockSpec(block, lambda s: index, pipeline_mode=pl.Buffered(1))

    in_specs = [
        pl.BlockSpec((1, ts, d), lambda s: (first_half(s) // ns + xb0, first_half(s) % ns, 0)),
        pl.BlockSpec((1, 1, 6 * d), lambda s: (first_half(s) // ns + mb0, 0, 0)),
        pl.BlockSpec((1, 1, 6 * d), lambda s: (second_half(s) // ns + mb0, 0, 0)),
        layer_weights(w_in.shape), const(scw.shape), const(cvw.shape), const(cvb.shape),
        const(cvg.shape), const(cvbeta.shape), const(sgg.shape), const(sgbeta.shape),
        const(sgw.shape), const(sgbt.shape), layer_weights(wbr.shape), layer_weights(wg.shape),
        const(bg.shape), layer_weights(wo.shape), const(ln1g.shape), const(ln1b.shape),
        const(wrt.shape), const(brc.shape),
    ]
    out_specs = [
        pl.BlockSpec((1, ts, d), lambda s: (second_half(s) // ns, second_half(s) % ns, 0)),
        pl.BlockSpec((1, ts, d // 2), lambda s: (second_half(s) // ns, second_half(s) % ns, 0)),
        pl.BlockSpec((SUBLANES, ts), lambda s: (0, second_half(s))),
        pl.BlockSpec((SUBLANES, ts), lambda s: (0, second_half(s))),
        pl.BlockSpec((N_EXPERTS, LANES), lambda s: (0, 0)),
    ]
    out_shape = [
        jax.ShapeDtypeStruct((bsz, seq, d), F32),
        jax.ShapeDtypeStruct((bsz, seq, d // 2), jnp.uint32),
        jax.ShapeDtypeStruct((SUBLANES, n_tok), jnp.int32),
        jax.ShapeDtypeStruct((SUBLANES, n_tok), F32),
        jax.ShapeDtypeStruct((N_EXPERTS, LANES), F32),
    ]
    return pl.pallas_call(
        functools.partial(_mixer_kernel, ns),
        grid=(n_tiles + 1,),
        in_specs=in_specs,
        out_specs=out_specs,
        out_shape=out_shape,
        scratch_shapes=[
            pltpu.VMEM((SC_HALO + ts, W_BRANCH), F32),
            pltpu.VMEM((CV_HALO + ts, W_BRANCH), F32),
            pltpu.VMEM((ts, W_BRANCH), F32),
            pltpu.VMEM((3, ts, W_BRANCH), BF16),
            pltpu.VMEM((ts, 3 * W_BRANCH), F32),
            pltpu.VMEM((ts, 2 * W_BRANCH), F32),
            pltpu.VMEM((ts, 3 * D_MODEL), F32),
            pltpu.VMEM((ts, d), BF16),
            pltpu.VMEM((ts, d), F32),
            pltpu.VMEM((N_EXPERTS, LANES), F32),
        ],
        compiler_params=pltpu.CompilerParams(
            dimension_semantics=("arbitrary",),
            vmem_limit_bytes=MIXER_VMEM_LIMIT),
    )(x, mod3, mod3, w_in, scw, cvw, cvb, cvg, cvbeta, sgg, sgbeta, sgw, sgbt, wbr, wg, bg, wo,
      ln1g, ln1b, wrt, brc)


def _sc_workers():
    info = plsc.get_sparse_core_info()
    return info.num_cores, info.num_cores * info.num_subcores


def _sc_scatter_rows(rows, dest_a, dest_b, n_out):
    n, d = rows.shape
    nc, nw = _sc_workers()
    per_w = n // nw
    n_win = per_w // SC_WINDOW
    ia = dest_a.reshape(nw, n_win, SC_WINDOW)
    ib = dest_b.reshape(nw, n_win, SC_WINDOW)
    mesh = plsc.VectorSubcoreMesh(core_axis_name="c", subcore_axis_name="s")

    @functools.partial(
        pl.kernel, mesh=mesh,
        out_type=jax.ShapeDtypeStruct((n_out, d), rows.dtype),
        scratch_types=[
            pltpu.VMEM((n_win, SC_WINDOW), jnp.int32),
            pltpu.VMEM((n_win, SC_WINDOW), jnp.int32),
            pltpu.VMEM((SC_WINDOW, d), rows.dtype),
        ],
    )
    def scatter(rows_hbm, ia_hbm, ib_hbm, out_hbm, ia_v, ib_v, rows_v):
        wid = lax.axis_index("s") * nc + lax.axis_index("c")
        pltpu.sync_copy(ia_hbm.at[wid], ia_v)
        pltpu.sync_copy(ib_hbm.at[wid], ib_v)
        base = wid * per_w

        @pl.loop(0, n_win)
        def _(j):
            pltpu.sync_copy(rows_hbm.at[pl.ds(base + j * SC_WINDOW, SC_WINDOW)], rows_v)
            pltpu.sync_copy(rows_v, out_hbm.at[ia_v.at[j]])
            pltpu.sync_copy(rows_v, out_hbm.at[ib_v.at[j]])

    return scatter(rows, ia, ib)


def _sc_gather_rows(table, idx):
    n = idx.shape[0]
    d = table.shape[1]
    nc, nw = _sc_workers()
    per_w = n // nw
    n_win = per_w // SC_WINDOW
    idx3 = idx.reshape(nw, n_win, SC_WINDOW)
    mesh = plsc.VectorSubcoreMesh(core_axis_name="c", subcore_axis_name="s")

    @functools.partial(
        pl.kernel, mesh=mesh,
        out_type=jax.ShapeDtypeStruct((n, d), table.dtype),
        scratch_types=[
            pltpu.VMEM((n_win, SC_WINDOW), jnp.int32),
            pltpu.VMEM((SC_WINDOW, d), table.dtype),
        ],
    )
    def gather(table_hbm, idx_hbm, out_hbm, idx_v, rows_v):
        wid = lax.axis_index("s") * nc + lax.axis_index("c")
        pltpu.sync_copy(idx_hbm.at[wid], idx_v)
        base = wid * per_w

        @pl.loop(0, n_win)
        def _(j):
            pltpu.sync_copy(table_hbm.at[idx_v.at[j]], rows_v)
            pltpu.sync_copy(rows_v, out_hbm.at[pl.ds(base + j * SC_WINDOW, SC_WINDOW)])

    return gather(table, idx3)


def _expert_kernel(be_ref, nused_ref, x_ref, w1_ref, w3_ref, w2_ref, o_ref, w1s, w3s, w2s):
    i = pl.program_id(0)
    new_expert = jnp.logical_or(i == 0, be_ref[i] != be_ref[jnp.maximum(i - 1, 0)])

    @pl.when(new_expert)
    def _():
        w1s[...] = w1_ref[0, 0].astype(BF16)
        w3s[...] = w3_ref[0, 0].astype(BF16)
        w2s[...] = w2_ref[0, 0].astype(BF16)

    @pl.when(i < nused_ref[0])
    def _():
        lo, hi = _unpack_bf16_pairs(x_ref[...])
        xb = jnp.concatenate([lo, hi], axis=1).astype(BF16)
        a = _dot(xb, w1s[...])
        b = _dot(xb, w3s[...])
        for r0 in range(0, MOE_ROWS, MOE_SLAB):
            rs = slice(r0, r0 + MOE_SLAB)
            h = a[rs] * jax.nn.sigmoid(a[rs]) * b[rs]
            o_ref[rs, :] = _pack_bf16_pairs(_dot(h.astype(BF16), w2s[...]))


def _experts(layer, buf, block_e, n_used, w1, w3, w2):
    n_rows, dp = buf.shape
    nb = n_rows // MOE_ROWS
    d, fe = w1.shape[-2:]

    def row_map(i, be, nu):
        return (jnp.minimum(i, nu[0] - 1), 0)

    grid_spec = pltpu.PrefetchScalarGridSpec(
        num_scalar_prefetch=2,
        grid=(nb,),
        in_specs=[
            pl.BlockSpec((MOE_ROWS, dp), row_map),
            pl.BlockSpec((1, 1, d, fe), lambda i, be, nu: (layer, be[i], 0, 0)),
            pl.BlockSpec((1, 1, d, fe), lambda i, be, nu: (layer, be[i], 0, 0)),
            pl.BlockSpec((1, 1, fe, d), lambda i, be, nu: (layer, be[i], 0, 0)),
        ],
        out_specs=pl.BlockSpec((MOE_ROWS, dp), row_map),
        scratch_shapes=[pltpu.VMEM((d, fe), BF16), pltpu.VMEM((d, fe), BF16),
                        pltpu.VMEM((fe, d), BF16)],
    )
    return pl.pallas_call(
        _expert_kernel,
        grid_spec=grid_spec,
        out_shape=jax.ShapeDtypeStruct((n_rows, dp), jnp.uint32),
        compiler_params=pltpu.CompilerParams(dimension_semantics=("arbitrary",),
                                             vmem_limit_bytes=EXPERT_VMEM_LIMIT),
    )(block_e, n_used, buf, w1, w3, w2)


def _combine_kernel(x1_ref, ya_ref, yb_ref, gates_ref, mod_ref, g_ref, b_ref, *rest):
    o_ref = rest[-1]
    d = D_MODEL
    gate2 = mod_ref[0][:, 5 * d:6 * d]
    gates = gates_ref[...]
    ya = jnp.concatenate(_unpack_bf16_pairs(ya_ref[0]), axis=1)
    yb = jnp.concatenate(_unpack_bf16_pairs(yb_ref[0]), axis=1)
    h = gates[:, 0:1] * ya + gates[:, 1:2] * yb
    o_ref[0] = _layer_norm(ALPHA * x1_ref[0] + gate2 * h, g_ref[...], b_ref[...])


def _combine(x1, pairs, gates, mod3, mb0, g, b, out_bsz, ob0, prev):
    bsz, seq, d = x1.shape
    ts = COMB_ROWS
    ns = seq // ts
    in_specs = [
        pl.BlockSpec((1, ts, d), lambda i, j: (i, j, 0)),
        pl.BlockSpec((1, ts, d // 2), lambda i, j: (0, i * ns + j, 0)),
        pl.BlockSpec((1, ts, d // 2), lambda i, j: (1, i * ns + j, 0)),
        pl.BlockSpec((ts, TOP_K), lambda i, j: (i * ns + j, 0)),
        pl.BlockSpec((1, 1, 6 * d), lambda i, j: (i + mb0, 0, 0)),
        pl.BlockSpec((1, d), lambda i, j: (0, 0)),
        pl.BlockSpec((1, d), lambda i, j: (0, 0)),
    ]
    args = [x1, pairs, pairs, gates, mod3, g, b]
    aliases = {}
    if prev is not None:
        in_specs.append(pl.BlockSpec(memory_space=pl.ANY))
        aliases = {len(args): 0}
        args.append(prev)
    return pl.pallas_call(
        _combine_kernel,
        grid=(bsz, ns),
        in_specs=in_specs,
        out_specs=pl.BlockSpec((1, ts, d), lambda i, j: (i + ob0, j, 0)),
        out_shape=jax.ShapeDtypeStruct((out_bsz, seq, d), F32),
        input_output_aliases=aliases,
        compiler_params=pltpu.CompilerParams(dimension_semantics=("arbitrary", "arbitrary")),
    )(*args)


def _dispatch_plan(ri, cnt):
    n_tok = ri.shape[1]
    n_blocks = (n_tok * TOP_K + N_EXPERTS * (MOE_ROWS - 1) + MOE_ROWS - 1) // MOE_ROWS
    counts = cnt[:, 0].astype(jnp.int32)
    padded = (counts + MOE_ROWS - 1) // MOE_ROWS * MOE_ROWS
    pad_end = jnp.cumsum(padded)
    pad_start = pad_end - padded
    dest_a = pad_start[ri[0]] + ri[2]
    dest_b = pad_start[ri[1]] + ri[3]
    block_start = jnp.arange(n_blocks, dtype=jnp.int32) * MOE_ROWS
    block_e = jnp.minimum(jnp.sum(block_start[:, None] >= pad_end[None, :], axis=1),
                          N_EXPERTS - 1).astype(jnp.int32)
    n_used = (pad_end[-1:] // MOE_ROWS).astype(jnp.int32)
    return dest_a, dest_b, block_e, n_used, n_blocks * MOE_ROWS


def kernel(x, c, w_ada, b_ada, w_in, sc_conv, cv_conv, cv_conv_b, cv_ln_g, cv_ln_b, sg_ln_g, sg_ln_b, sg_w, sg_b, w_branch, w_gate, b_gate, w_o, ln1_g, ln1_b, w_router, b_router, w1, w3, w2, ln2_g, ln2_b):
    bsz, seq, d = x.shape
    mod3 = _ada(c, w_ada, b_ada).reshape(DEPTH, bsz, 1, 6 * d)
    wrt = w_router.T
    brc = b_router.reshape(N_EXPERTS, 1)
    w_in_p = _pack_weights(w_in)
    w_gate_p = _pack_weights(w_gate.reshape(DEPTH, d, 3 * d))
    w_o_p = _pack_weights(w_o)
    w_branch_p = _pack_weights(w_branch.reshape(DEPTH * 3, W_BRANCH, d)).reshape(
        DEPTH, 3, W_BRANCH // 2, d)
    cb = bsz // N_CHAINS
    n_tok = cb * seq
    chains = [(x, h * cb) for h in range(N_CHAINS)]
    for l in range(DEPTH):
        last = l == DEPTH - 1
        result = None
        for h in range(N_CHAINS):
            xin, xb0 = chains[h]
            x1, u2, ri, rf, cnt = _mixer(
                l, xin, xb0, mod3[l], h * cb, cb, w_in_p, sc_conv[l], cv_conv[l],
                cv_conv_b[l].reshape(1, -1), cv_ln_g[l].reshape(1, -1), cv_ln_b[l].reshape(1, -1),
                sg_ln_g[l].reshape(1, -1), sg_ln_b[l].reshape(1, -1), sg_w[l], sg_b[l].T,
                w_branch_p, w_gate_p, b_gate[l],
                w_o_p, ln1_g[l].reshape(1, -1), ln1_b[l].reshape(1, -1), wrt, brc)
            dest_a, dest_b, block_e, n_used, n_rows = _dispatch_plan(ri, cnt)
            buf = _sc_scatter_rows(u2.reshape(n_tok, d // 2), dest_a, dest_b, n_rows)
            obuf = _experts(l, buf, block_e, n_used, w1, w3, w2)
            pair_idx = jnp.concatenate([dest_a, dest_b])
            pairs = _sc_gather_rows(obuf, pair_idx).reshape(TOP_K, n_tok, d // 2)
            gates = rf[0:TOP_K].T
            g2, b2 = ln2_g[l].reshape(1, -1), ln2_b[l].reshape(1, -1)
            if last:
                result = _combine(x1, pairs, gates, mod3[l], h * cb, g2, b2, bsz, h * cb, result)
            else:
                chains[h] = (_combine(x1, pairs, gates, mod3[l], h * cb, g2, b2, cb, 0, None), 0)
    return result
```

```python
import functools

import jax
import jax.numpy as jnp
from jax import lax
from jax.experimental import pallas as pl
from jax.experimental.pallas import tpu as pltpu
from jax.experimental.pallas import tpu_sc as plsc

D_MODEL = 1024
DEPTH = 2
W_BRANCH = 1024
SC_KERNEL = 3
CV_KERNEL = 31
CHUNK = 128
SG_HEADS = 8
N_EXPERTS = 16
N_GROUPS = 4
EXPERTS_PER_GROUP = N_EXPERTS // N_GROUPS
TOP_K = 2
D_EXPERT = 512
ALPHA = (2.0 * DEPTH) ** 0.25
LN_EPS = 1e-5

F32 = jnp.float32
BF16 = jnp.bfloat16

V7X_VMEM_BYTES = 64 * 1024 * 1024
MIXER_VMEM_LIMIT = V7X_VMEM_BYTES - 6 * 1024 * 1024
EXPERT_VMEM_LIMIT = V7X_VMEM_BYTES // 2
SUBLANES = 8
LANES = 128

MIX_ROWS = 256
SC_HALO = SUBLANES
CV_HALO = 32
CONV_ROWS = 64
TIE_LAG = 4
TIE_FREE_JOBS = 4
TAIL_STAGE_AFTER_JOBS = (7, 12)
CONV_COLS = 128
PROJ_COLS = 512
PACK_ROWS = 256
MOE_ROWS = 512
MOE_SLAB = 256
COMB_ROWS = 512
SC_WINDOW = 32
N_CHAINS = 2


def _dot(a, b):
    return jnp.dot(a, b, preferred_element_type=F32)


def _pack_bf16_pairs(v):
    m = v.shape[1] // 2
    lo = lax.bitcast_convert_type(v[:, 0:m].astype(BF16).astype(F32), jnp.uint32)
    hi = lax.bitcast_convert_type(v[:, m:2 * m].astype(BF16).astype(F32), jnp.uint32)
    return jnp.bitwise_or(jnp.bitwise_and(hi, jnp.uint32(0xFFFF0000)),
                          lax.shift_right_logical(lo, jnp.uint32(16)))


def _unpack_bf16_pairs(w):
    lo = lax.bitcast_convert_type(lax.shift_left(w, jnp.uint32(16)), F32)
    hi = lax.bitcast_convert_type(jnp.bitwise_and(w, jnp.uint32(0xFFFF0000)), F32)
    return lo, hi


def _layer_norm(v, g, b):
    mu = jnp.mean(v, axis=-1, keepdims=True)
    vc = v - mu
    var = jnp.mean(vc * vc, axis=-1, keepdims=True)
    return vc * lax.rsqrt(var + LN_EPS) * g + b


def _ada_kernel(c_ref, w_ref, b_ref, o_ref):
    c = c_ref[...]
    c_act = c * jax.nn.sigmoid(c)
    o_ref[0] = jnp.dot(c_act, w_ref[0], preferred_element_type=F32,
                       precision=lax.Precision.HIGHEST) + b_ref[0]


def _ada(c, w_ada, b_ada):
    bsz, d = c.shape
    n = w_ada.shape[-1]
    tn = 1536
    return pl.pallas_call(
        _ada_kernel,
        grid=(DEPTH, n // tn),
        in_specs=[
            pl.BlockSpec((bsz, d), lambda l, j: (0, 0)),
            pl.BlockSpec((1, d, tn), lambda l, j: (l, 0, j)),
            pl.BlockSpec((1, 1, tn), lambda l, j: (l, 0, j)),
        ],
        out_specs=pl.BlockSpec((1, bsz, tn), lambda l, j: (l, 0, j)),
        out_shape=jax.ShapeDtypeStruct((DEPTH, bsz, n), F32),
    )(c, w_ada, b_ada.reshape(DEPTH, 1, n))


def _top2_of4(rows):
    m1 = rows[0]
    i1 = jnp.zeros(rows[0].shape, jnp.int32)
    for k in range(1, 4):
        gt = rows[k] > m1
        m1 = jnp.where(gt, rows[k], m1)
        i1 = jnp.where(gt, k, i1)
    m2 = jnp.full(rows[0].shape, -jnp.inf, F32)
    i2 = jnp.zeros(rows[0].shape, jnp.int32)
    for k in range(4):
        cand = jnp.where(i1 == k, -jnp.inf, rows[k])
        gt = cand > m2
        m2 = jnp.where(gt, cand, m2)
        i2 = jnp.where(gt, k, i2)
    return m1, i1, m2, i2


def _zero_after(v):
    u = lax.bitcast_convert_type(v, jnp.uint32)
    u = lax.shift_right_logical(lax.shift_right_logical(u, jnp.uint32(16)), jnp.uint32(16))
    return lax.bitcast_convert_type(u, F32)


def _conv31_chunk(cvw_ref, cvbuf, cvout, r0, c0, tie):
    cs = slice(c0, c0 + CONV_COLS)
    n_out = CONV_ROWS // SUBLANES
    n_halo = CV_HALO // SUBLANES
    xin = [cvbuf[r0 + SUBLANES * j:r0 + SUBLANES * (j + 1), cs] for j in range(n_halo + n_out)]
    if tie is not None:
        xin = [blk + tie for blk in xin]
    acc = [None] * n_out
    for r in range(SUBLANES):
        taps = []
        for m in range((CV_KERNEL - 1 - r) // SUBLANES + 1):
            k = CV_KERNEL - 1 - (SUBLANES * m + r)
            taps.append(cvw_ref[k:k + 1, cs])
        part = []
        for p in range(-1 if r else 0, n_out):
            blk = None
            for m, w_row in enumerate(taps):
                term = w_row * xin[n_halo + p - m]
                blk = term if blk is None else blk + term
            part.append(blk)
        for q in range(n_out):
            if r:
                shifted = jnp.concatenate([part[q], part[q + 1]], axis=0)[
                    SUBLANES - r:2 * SUBLANES - r]
            else:
                shifted = part[q]
            acc[q] = shifted if acc[q] is None else acc[q] + shifted
    cvout[r0:r0 + CONV_ROWS, cs] = jnp.concatenate(acc, axis=0)


def _bf16_weights(packed):
    return pltpu.bitcast(packed, BF16)


def _pack_weights_kernel(w_ref, o_ref):
    o_ref[0] = pltpu.bitcast(w_ref[0].astype(BF16), jnp.uint32)


def _pack_weights(w):
    g, k, n = w.shape
    kb = PACK_ROWS
    return pl.pallas_call(
        _pack_weights_kernel,
        grid=(g, k // kb),
        in_specs=[pl.BlockSpec((1, kb, n), lambda i, j: (i, j, 0))],
        out_specs=pl.BlockSpec((1, kb // 2, n), lambda i, j: (i, j, 0)),
        out_shape=jax.ShapeDtypeStruct((g, k // 2, n), jnp.uint32),
        compiler_params=pltpu.CompilerParams(vmem_limit_bytes=EXPERT_VMEM_LIMIT),
    )(w)


def _mixer_kernel(tiles_per_seq, x_ref, mod_ref, modt_ref, w_in_ref, scw_ref, cvw_ref,
                  cvb_ref, cvg_ref, cvbeta_ref, sgg_ref, sgbeta_ref, sgw_ref, sgbt_ref, wbr_ref,
                  wg_ref, bg_ref, wo_ref, ln1g_ref, ln1b_ref, wrt_ref, brc_ref,
                  x1_ref, u2_ref, ri_ref, rf_ref, cnt_ref,
                  qbuf, cvbuf, cvout, ybuf, pa_buf, pc_buf, gl_buf, mg_buf, xs_buf, base_ref):
    ts = MIX_ROWS
    d = D_MODEL
    wb = W_BRANCH
    step = pl.program_id(0)
    tile = jnp.minimum(step, pl.num_programs(0) - 2)
    first_tile = tile % tiles_per_seq == 0

    @pl.when(step == 0)
    def _():
        base_ref[...] = jnp.zeros_like(base_ref)
        mg_buf[...] = jnp.zeros_like(mg_buf)
        xs_buf[...] = jnp.zeros_like(xs_buf)

    def w_in(c0):
        return _bf16_weights(w_in_ref[0, :, c0:c0 + PROJ_COLS])

    def w_gate(c0):
        return _bf16_weights(wg_ref[0, :, c0:c0 + PROJ_COLS])

    @pl.when(first_tile)
    def _():
        qbuf[0:SC_HALO, :] = jnp.zeros((SC_HALO, wb), F32)
        cvbuf[0:CV_HALO, :] = jnp.zeros((CV_HALO, wb), F32)

    x = x_ref[0]
    mod = mod_ref[0]
    shift1, scale1 = mod[:, 0:d], mod[:, d:2 * d]
    ub = (x * (1.0 + scale1) + shift1).astype(BF16)

    for c0 in range(0, wb, PROJ_COLS):
        a = _dot(ub, w_in(3 * wb + c0))
        g = _dot(ub, w_in(4 * wb + c0))
        cvbuf[CV_HALO:CV_HALO + ts, c0:c0 + PROJ_COLS] = a * jax.nn.sigmoid(g)

    tail = _mixer_tail(step, mg_buf, xs_buf, modt_ref, wo_ref, ln1g_ref, ln1b_ref, wrt_ref,
                       brc_ref, x1_ref, u2_ref, ri_ref, rf_ref, cnt_ref, base_ref)
    ties = {}

    def add_tie(chunk, tie):
        ties[chunk] = tie + ties[chunk] if chunk in ties else tie

    next(tail)

    mxu_jobs = ([(pa_buf, w_in, c0, c0) for c0 in range(0, 3 * wb, PROJ_COLS)]
                + [(pc_buf, w_in, c0, 5 * wb + c0) for c0 in range(0, 2 * wb, PROJ_COLS)]
                + [(gl_buf, w_gate, c0, c0) for c0 in range(0, 3 * d, PROJ_COLS)])
    conv_jobs = [(r0, c0) for c0 in range(0, wb, CONV_COLS) for r0 in range(0, ts, CONV_ROWS)]
    chunks_per_job = len(conv_jobs) // len(mxu_jobs)
    for c, conv_job in enumerate(conv_jobs):
        _conv31_chunk(cvw_ref, cvbuf, cvout, *conv_job, ties.get(c))
        if (c + 1) % chunks_per_job:
            continue
        i = c // chunks_per_job
        dst, weights, dc, wc = mxu_jobs[i]
        res = _dot(ub, weights(wc))
        dst[:, dc:dc + PROJ_COLS] = res
        if i < len(mxu_jobs) - TIE_FREE_JOBS:
            add_tie(c + TIE_LAG, _zero_after(res[ts - 1:ts, PROJ_COLS - CONV_COLS:PROJ_COLS]))
        if i in TAIL_STAGE_AFTER_JOBS:
            next(tail, None)
    cvbuf[0:CV_HALO, :] = cvbuf[ts:ts + CV_HALO, :]

    qbuf[SC_HALO:SC_HALO + ts, :] = pa_buf[:, wb:2 * wb] * pa_buf[:, 2 * wb:3 * wb]
    conv = scw_ref[SC_KERNEL - 1:SC_KERNEL, :] * qbuf[SC_HALO:SC_HALO + ts, :]
    for k in range(SC_KERNEL - 1):
        off = SC_HALO - (SC_KERNEL - 1) + k
        conv = conv + scw_ref[k:k + 1, :] * qbuf[off:off + ts, :]
    ybuf[0] = (pa_buf[:, 0:wb] * conv).astype(BF16)
    qbuf[0:SC_HALO, :] = qbuf[ts:ts + SC_HALO, :]
    z0 = _dot(ybuf[0], _bf16_weights(wbr_ref[0, 0]))

    cv = _layer_norm(cvout[...] + cvb_ref[...], cvg_ref[...], cvbeta_ref[...])
    ybuf[1] = (cv * jax.nn.sigmoid(cv)).astype(BF16)
    z1 = _dot(ybuf[1], _bf16_weights(wbr_ref[0, 1]))

    gu = jax.nn.gelu(pc_buf[:, 0:wb])
    gv = _layer_norm(jax.nn.gelu(pc_buf[:, wb:2 * wb]), sgg_ref[...], sgbeta_ref[...]).astype(BF16)
    merged = (jax.nn.sigmoid(gl_buf[:, 0:d] + bg_ref[0:1, :]) * z0
              + jax.nn.sigmoid(gl_buf[:, d:2 * d] + bg_ref[1:2, :]) * z1)
    g2 = jax.nn.sigmoid(gl_buf[:, 2 * d:3 * d] + bg_ref[2:3, :])
    row = lax.broadcasted_iota(jnp.int32, (CHUNK, CHUNK), 0)
    col = lax.broadcasted_iota(jnp.int32, (CHUNK, CHUNK), 1)
    hd = wb // SG_HEADS
    for h in range(SG_HEADS):
        wm = jnp.where(row >= col, sgw_ref[h], 0.0).astype(BF16)
        bias = sgbt_ref[:, h:h + 1]
        for n in range(ts // CHUNK):
            rs = slice(n * CHUNK, (n + 1) * CHUNK)
            cs = slice(h * hd, (h + 1) * hd)
            mixed = _dot(wm, gv[rs, cs]) + bias
            ybuf[2, rs, cs] = (gu[rs, cs] * mixed).astype(BF16)

    merged = merged + g2 * _dot(ybuf[2], _bf16_weights(wbr_ref[0, 2]))
    mg_buf[...] = merged.astype(BF16)
    xs_buf[...] = x


def _mixer_tail(step, mg_buf, xs_buf, modt_ref, wo_ref, ln1g_ref, ln1b_ref, wrt_ref, brc_ref,
                x1_ref, u2_ref, ri_ref, rf_ref, cnt_ref, base_ref):
    ts = MIX_ROWS
    d = D_MODEL
    modt = modt_ref[0]
    gate1, shift2, scale2 = modt[:, 2 * d:3 * d], modt[:, 3 * d:4 * d], modt[:, 4 * d:5 * d]
    hmix = _dot(mg_buf[...], _bf16_weights(wo_ref[0]))
    x1 = _layer_norm(ALPHA * xs_buf[...] + gate1 * hmix, ln1g_ref[...], ln1b_ref[...])
    x1_ref[0] = x1
    u2 = x1 * (1.0 + scale2) + shift2
    u2_ref[0] = _pack_bf16_pairs(u2)
    yield

    logits = lax.dot_general(wrt_ref[...], u2, (((1,), (1,)), ((), ())),
                             preferred_element_type=F32,
                             precision=lax.Precision.HIGHEST)
    mx = jnp.max(logits, axis=0, keepdims=True)
    ex = jnp.exp(logits - mx)
    scores = ex / jnp.sum(ex, axis=0, keepdims=True)
    sel = scores + brc_ref[...]
    tops = []
    for g in range(N_GROUPS):
        rows = [sel[g * EXPERTS_PER_GROUP + k:g * EXPERTS_PER_GROUP + k + 1, :]
                for k in range(EXPERTS_PER_GROUP)]
        tops.append(_top2_of4(rows))
    best = tops[0][0] + tops[0][2]
    g_idx = jnp.zeros(best.shape, jnp.int32)
    loc1, loc2 = tops[0][1], tops[0][3]
    for g in range(1, N_GROUPS):
        gs = tops[g][0] + tops[g][2]
        gt = gs > best
        best = jnp.where(gt, gs, best)
        g_idx = jnp.where(gt, g, g_idx)
        loc1 = jnp.where(gt, tops[g][1], loc1)
        loc2 = jnp.where(gt, tops[g][3], loc2)
    e0 = g_idx * EXPERTS_PER_GROUP + loc1
    e1 = g_idx * EXPERTS_PER_GROUP + loc2
    erow = lax.broadcasted_iota(jnp.int32, (N_EXPERTS, ts), 0)
    is0 = erow == e0
    is1 = erow == e1
    s0 = jnp.sum(jnp.where(is0, scores, 0.0), axis=0, keepdims=True)
    s1 = jnp.sum(jnp.where(is1, scores, 0.0), axis=0, keepdims=True)
    ssum = s0 + s1
    yield

    onehot = jnp.logical_or(is0, is1).astype(BF16)
    src = lax.broadcasted_iota(jnp.int32, (ts, ts), 0)
    dst = lax.broadcasted_iota(jnp.int32, (ts, ts), 1)
    earlier = (src < dst).astype(BF16)
    prior = _dot(onehot, earlier) + base_ref[:, 0:1]
    r0 = jnp.sum(jnp.where(is0, prior, 0.0), axis=0, keepdims=True)
    r1 = jnp.sum(jnp.where(is1, prior, 0.0), axis=0, keepdims=True)
    counts = jnp.sum(onehot.astype(F32), axis=1, keepdims=True)
    base_ref[...] = base_ref[...] + jnp.where(step > 0, counts, 0.0)

    zi = jnp.zeros((SUBLANES - 4, ts), jnp.int32)
    ri_ref[...] = jnp.concatenate([e0, e1, r0.astype(jnp.int32), r1.astype(jnp.int32), zi], axis=0)
    zf = jnp.zeros((SUBLANES - 2, ts), F32)
    rf_ref[...] = jnp.concatenate([s0 / ssum, s1 / ssum, zf], axis=0)
    cnt_ref[...] = base_ref[...]


def _mixer(layer, x, xb0, mod3, mb0, bsz, w_in, scw, cvw, cvb, cvg, cvbeta, sgg, sgbeta, sgw, sgbt,
           wbr, wg, bg, wo, ln1g, ln1b, wrt, brc):
    _, seq, d = x.shape
    ts = MIX_ROWS
    ns = seq // ts
    n_tok = bsz * seq

    n_tiles = bsz * ns

    def const(shape):
        zeros = (0,) * len(shape)
        return pl.BlockSpec(shape, lambda s: zeros, pipeline_mode=pl.Buffered(1))

    def first_half(s):
        return jnp.minimum(s, n_tiles - 1)

    def second_half(s):
        return jnp.maximum(s - 1, 0)

    def layer_weights(shape):
        block = (1,) + tuple(shape[1:])
        index = (layer,) + (0,) * (len(shape) - 1)
        return pl.BlockSpec(block, lambda s: index, pipeline_mode=pl.Buffered(1))

    in_specs = [
        pl.BlockSpec((1, ts, d), lambda s: (first_half(s) // ns + xb0, first_half(s) % ns, 0)),
        pl.BlockSpec((1, 1, 6 * d), lambda s: (first_half(s) // ns + mb0, 0, 0)),
        pl.BlockSpec((1, 1, 6 * d), lambda s: (second_half(s) // ns + mb0, 0, 0)),
        layer_weights(w_in.shape), const(scw.shape), const(cvw.shape), const(cvb.shape),
        const(cvg.shape), const(cvbeta.shape), const(sgg.shape), const(sgbeta.shape),
        const(sgw.shape), const(sgbt.shape), layer_weights(wbr.shape), layer_weights(wg.shape),
        const(bg.shape), layer_weights(wo.shape), const(ln1g.shape), const(ln1b.shape),
        const(wrt.shape), const(brc.shape),
    ]
    out_specs = [
        pl.BlockSpec((1, ts, d), lambda s: (second_half(s) // ns, second_half(s) % ns, 0)),
        pl.BlockSpec((1, ts, d // 2), lambda s: (second_half(s) // ns, second_half(s) % ns, 0)),
        pl.BlockSpec((SUBLANES, ts), lambda s: (0, second_half(s))),
        pl.BlockSpec((SUBLANES, ts), lambda s: (0, second_half(s))),
        pl.BlockSpec((N_EXPERTS, LANES), lambda s: (0, 0)),
    ]
    out_shape = [
        jax.ShapeDtypeStruct((bsz, seq, d), F32),
        jax.ShapeDtypeStruct((bsz, seq, d // 2), jnp.uint32),
        jax.ShapeDtypeStruct((SUBLANES, n_tok), jnp.int32),
        jax.ShapeDtypeStruct((SUBLANES, n_tok), F32),
        jax.ShapeDtypeStruct((N_EXPERTS, LANES), F32),
    ]
    return pl.pallas_call(
        functools.partial(_mixer_kernel, ns),
        grid=(n_tiles + 1,),
        in_specs=in_specs,
        out_specs=out_specs,
        out_shape=out_shape,
        scratch_shapes=[
            pltpu.VMEM((SC_HALO + ts, W_BRANCH), F32),
            pltpu.VMEM((CV_HALO + ts, W_BRANCH), F32),
            pltpu.VMEM((ts, W_BRANCH), F32),
            pltpu.VMEM((3, ts, W_BRANCH), BF16),
            pltpu.VMEM((ts, 3 * W_BRANCH), F32),
            pltpu.VMEM((ts, 2 * W_BRANCH), F32),
            pltpu.VMEM((ts, 3 * D_MODEL), F32),
            pltpu.VMEM((ts, d), BF16),
            pltpu.VMEM((ts, d), F32),
            pltpu.VMEM((N_EXPERTS, LANES), F32),
        ],
        compiler_params=pltpu.CompilerParams(
            dimension_semantics=("arbitrary",),
            vmem_limit_bytes=MIXER_VMEM_LIMIT),
    )(x, mod3, mod3, w_in, scw, cvw, cvb, cvg, cvbeta, sgg, sgbeta, sgw, sgbt, wbr, wg, bg, wo,
      ln1g, ln1b, wrt, brc)


def _sc_workers():
    info = plsc.get_sparse_core_info()
    return info.num_cores, info.num_cores * info.num_subcores


def _sc_scatter_rows(rows, dest_a, dest_b, n_out):
    n, d = rows.shape
    nc, nw = _sc_workers()
    per_w = n // nw
    n_win = per_w // SC_WINDOW
    ia = dest_a.reshape(nw, n_win, SC_WINDOW)
    ib = dest_b.reshape(nw, n_win, SC_WINDOW)
    mesh = plsc.VectorSubcoreMesh(core_axis_name="c", subcore_axis_name="s")

    @functools.partial(
        pl.kernel, mesh=mesh,
        out_type=jax.ShapeDtypeStruct((n_out, d), rows.dtype),
        scratch_types=[
            pltpu.VMEM((n_win, SC_WINDOW), jnp.int32),
            pltpu.VMEM((n_win, SC_WINDOW), jnp.int32),
            pltpu.VMEM((SC_WINDOW, d), rows.dtype),
        ],
    )
    def scatter(rows_hbm, ia_hbm, ib_hbm, out_hbm, ia_v, ib_v, rows_v):
        wid = lax.axis_index("s") * nc + lax.axis_index("c")
        pltpu.sync_copy(ia_hbm.at[wid], ia_v)
        pltpu.sync_copy(ib_hbm.at[wid], ib_v)
        base = wid * per_w

        @pl.loop(0, n_win)
        def _(j):
            pltpu.sync_copy(rows_hbm.at[pl.ds(base + j * SC_WINDOW, SC_WINDOW)], rows_v)
            pltpu.sync_copy(rows_v, out_hbm.at[ia_v.at[j]])
            pltpu.sync_copy(rows_v, out_hbm.at[ib_v.at[j]])

    return scatter(rows, ia, ib)


def _sc_gather_rows(table, idx):
    n = idx.shape[0]
    d = table.shape[1]
    nc, nw = _sc_workers()
    per_w = n // nw
    n_win = per_w // SC_WINDOW
    idx3 = idx.reshape(nw, n_win, SC_WINDOW)
    mesh = plsc.VectorSubcoreMesh(core_axis_name="c", subcore_axis_name="s")

    @functools.partial(
        pl.kernel, mesh=mesh,
        out_type=jax.ShapeDtypeStruct((n, d), table.dtype),
        scratch_types=[
            pltpu.VMEM((n_win, SC_WINDOW), jnp.int32),
            pltpu.VMEM((SC_WINDOW, d), table.dtype),
        ],
    )
    def gather(table_hbm, idx_hbm, out_hbm, idx_v, rows_v):
        wid = lax.axis_index("s") * nc + lax.axis_index("c")
        pltpu.sync_copy(idx_hbm.at[wid], idx_v)
        base = wid * per_w

        @pl.loop(0, n_win)
        def _(j):
            pltpu.sync_copy(table_hbm.at[idx_v.at[j]], rows_v)
            pltpu.sync_copy(rows_v, out_hbm.at[pl.ds(base + j * SC_WINDOW, SC_WINDOW)])

    return gather(table, idx3)


def _expert_kernel(be_ref, nused_ref, x_ref, w1_ref, w3_ref, w2_ref, o_ref, w1s, w3s, w2s):
    i = pl.program_id(0)
    new_expert = jnp.logical_or(i == 0, be_ref[i] != be_ref[jnp.maximum(i - 1, 0)])

    @pl.when(new_expert)
    def _():
        w1s[...] = w1_ref[0, 0].astype(BF16)
        w3s[...] = w3_ref[0, 0].astype(BF16)
        w2s[...] = w2_ref[0, 0].astype(BF16)

    @pl.when(i < nused_ref[0])
    def _():
        lo, hi = _unpack_bf16_pairs(x_ref[...])
        xb = jnp.concatenate([lo, hi], axis=1).astype(BF16)
        a = _dot(xb, w1s[...])
        b = _dot(xb, w3s[...])
        for r0 in range(0, MOE_ROWS, MOE_SLAB):
            rs = slice(r0, r0 + MOE_SLAB)
            h = a[rs] * jax.nn.sigmoid(a[rs]) * b[rs]
            o_ref[rs, :] = _pack_bf16_pairs(_dot(h.astype(BF16), w2s[...]))


def _experts(layer, buf, block_e, n_used, w1, w3, w2):
    n_rows, dp = buf.shape
    nb = n_rows // MOE_ROWS
    d, fe = w1.shape[-2:]

    def row_map(i, be, nu):
        return (jnp.minimum(i, nu[0] - 1), 0)

    grid_spec = pltpu.PrefetchScalarGridSpec(
        num_scalar_prefetch=2,
        grid=(nb,),
        in_specs=[
            pl.BlockSpec((MOE_ROWS, dp), row_map),
            pl.BlockSpec((1, 1, d, fe), lambda i, be, nu: (layer, be[i], 0, 0)),
            pl.BlockSpec((1, 1, d, fe), lambda i, be, nu: (layer, be[i], 0, 0)),
            pl.BlockSpec((1, 1, fe, d), lambda i, be, nu: (layer, be[i], 0, 0)),
        ],
        out_specs=pl.BlockSpec((MOE_ROWS, dp), row_map),
        scratch_shapes=[pltpu.VMEM((d, fe), BF16), pltpu.VMEM((d, fe), BF16),
                        pltpu.VMEM((fe, d), BF16)],
    )
    return pl.pallas_call(
        _expert_kernel,
        grid_spec=grid_spec,
        out_shape=jax.ShapeDtypeStruct((n_rows, dp), jnp.uint32),
        compiler_params=pltpu.CompilerParams(dimension_semantics=("arbitrary",),
                                             vmem_limit_bytes=EXPERT_VMEM_LIMIT),
    )(block_e, n_used, buf, w1, w3, w2)


def _combine_kernel(x1_ref, ya_ref, yb_ref, gates_ref, mod_ref, g_ref, b_ref, *rest):
    o_ref = rest[-1]
    d = D_MODEL
    gate2 = mod_ref[0][:, 5 * d:6 * d]
    gates = gates_ref[...]
    ya = jnp.concatenate(_unpack_bf16_pairs(ya_ref[0]), axis=1)
    yb = jnp.concatenate(_unpack_bf16_pairs(yb_ref[0]), axis=1)
    h = gates[:, 0:1] * ya + gates[:, 1:2] * yb
    o_ref[0] = _layer_norm(ALPHA * x1_ref[0] + gate2 * h, g_ref[...], b_ref[...])


def _combine(x1, pairs, gates, mod3, mb0, g, b, out_bsz, ob0, prev):
    bsz, seq, d = x1.shape
    ts = COMB_ROWS
    ns = seq // ts
    in_specs = [
        pl.BlockSpec((1, ts, d), lambda i, j: (i, j, 0)),
        pl.BlockSpec((1, ts, d // 2), lambda i, j: (0, i * ns + j, 0)),
        pl.BlockSpec((1, ts, d // 2), lambda i, j: (1, i * ns + j, 0)),
        pl.BlockSpec((ts, TOP_K), lambda i, j: (i * ns + j, 0)),
        pl.BlockSpec((1, 1, 6 * d), lambda i, j: (i + mb0, 0, 0)),
        pl.BlockSpec((1, d), lambda i, j: (0, 0)),
        pl.BlockSpec((1, d), lambda i, j: (0, 0)),
    ]
    args = [x1, pairs, pairs, gates, mod3, g, b]
    aliases = {}
    if prev is not None:
        in_specs.append(pl.BlockSpec(memory_space=pl.ANY))
        aliases = {len(args): 0}
        args.append(prev)
    return pl.pallas_call(
        _combine_kernel,
        grid=(bsz, ns),
        in_specs=in_specs,
        out_specs=pl.BlockSpec((1, ts, d), lambda i, j: (i + ob0, j, 0)),
        out_shape=jax.ShapeDtypeStruct((out_bsz, seq, d), F32),
        input_output_aliases=aliases,
        compiler_params=pltpu.CompilerParams(dimension_semantics=("arbitrary", "arbitrary")),
    )(*args)


def _dispatch_plan(ri, cnt):
    n_tok = ri.shape[1]
    n_blocks = (n_tok * TOP_K + N_EXPERTS * (MOE_ROWS - 1) + MOE_ROWS - 1) // MOE_ROWS
    counts = cnt[:, 0].astype(jnp.int32)
    padded = (counts + MOE_ROWS - 1) // MOE_ROWS * MOE_ROWS
    pad_end = jnp.cumsum(padded)
    pad_start = pad_end - padded
    dest_a = pad_start[ri[0]] + ri[2]
    dest_b = pad_start[ri[1]] + ri[3]
    block_start = jnp.arange(n_blocks, dtype=jnp.int32) * MOE_ROWS
    block_e = jnp.minimum(jnp.sum(block_start[:, None] >= pad_end[None, :], axis=1),
                          N_EXPERTS - 1).astype(jnp.int32)
    n_used = (pad_end[-1:] // MOE_ROWS).astype(jnp.int32)
    return dest_a, dest_b, block_e, n_used, n_blocks * MOE_ROWS


def kernel(x, c, w_ada, b_ada, w_in, sc_conv, cv_conv, cv_conv_b, cv_ln_g, cv_ln_b, sg_ln_g, sg_ln_b, sg_w, sg_b, w_branch, w_gate, b_gate, w_o, ln1_g, ln1_b, w_router, b_router, w1, w3, w2, ln2_g, ln2_b):
    bsz, seq, d = x.shape
    mod3 = _ada(c, w_ada, b_ada).reshape(DEPTH, bsz, 1, 6 * d)
    wrt = w_router.T
    brc = b_router.reshape(N_EXPERTS, 1)
    w_in_p = _pack_weights(w_in)
    w_gate_p = _pack_weights(w_gate.reshape(DEPTH, d, 3 * d))
    w_o_p = _pack_weights(w_o)
    w_branch_p = _pack_weights(w_branch.reshape(DEPTH * 3, W_BRANCH, d)).reshape(
        DEPTH, 3, W_BRANCH // 2, d)
    cb = bsz // N_CHAINS
    n_tok = cb * seq
    chains = [(x, h * cb) for h in range(N_CHAINS)]
    for l in range(DEPTH):
        last = l == DEPTH - 1
        result = None
        for h in range(N_CHAINS):
            xin, xb0 = chains[h]
            x1, u2, ri, rf, cnt = _mixer(
                l, xin, xb0, mod3[l], h * cb, cb, w_in_p, sc_conv[l], cv_conv[l],
                cv_conv_b[l].reshape(1, -1), cv_ln_g[l].reshape(1, -1), cv_ln_b[l].reshape(1, -1),
                sg_ln_g[l].reshape(1, -1), sg_ln_b[l].reshape(1, -1), sg_w[l], sg_b[l].T,
                w_branch_p, w_gate_p, b_gate[l],
                w_o_p, ln1_g[l].reshape(1, -1), ln1_b[l].reshape(1, -1), wrt, brc)
            dest_a, dest_b, block_e, n_used, n_rows = _dispatch_plan(ri, cnt)
            buf = _sc_scatter_rows(u2.reshape(n_tok, d // 2), dest_a, dest_b, n_rows)
            obuf = _experts(l, buf, block_e, n_used, w1, w3, w2)
            pair_idx = jnp.concatenate([dest_a, dest_b])
            pairs = _sc_gather_rows(obuf, pair_idx).reshape(TOP_K, n_tok, d // 2)
            gates = rf[0:TOP_K].T
            g2, b2 = ln2_g[l].reshape(1, -1), ln2_b[l].reshape(1, -1)
            if last:
                result = _combine(x1, pairs, gates, mod3[l], h * cb, g2, b2, bsz, h * cb, result)
            else:
                chains[h] = (_combine(x1, pairs, gates, mod3[l], h * cb, g2, b2, cb, 0, None), 0)
    return result
```

```python
import functools

import jax
import jax.numpy as jnp
from jax import lax
from jax.experimental import pallas as pl
from jax.experimental.pallas import tpu as pltpu
from jax.experimental.pallas import tpu_sc as plsc

D_MODEL = 1024
DEPTH = 2
W_BRANCH = 1024
SC_KERNEL = 3
CV_KERNEL = 31
CHUNK = 128
SG_HEADS = 8
N_EXPERTS = 16
N_GROUPS = 4
EXPERTS_PER_GROUP = N_EXPERTS // N_GROUPS
TOP_K = 2
D_EXPERT = 512
ALPHA = (2.0 * DEPTH) ** 0.25
LN_EPS = 1e-5

F32 = jnp.float32
BF16 = jnp.bfloat16

V7X_VMEM_BYTES = 64 * 1024 * 1024
MIXER_VMEM_LIMIT = V7X_VMEM_BYTES - 6 * 1024 * 1024
EXPERT_VMEM_LIMIT = V7X_VMEM_BYTES // 2
SUBLANES = 8
LANES = 128

MIX_ROWS = 256
SC_HALO = SUBLANES
CV_HALO = 32
CONV_ROWS = 128
TIE_LAG = 2
TIE_FREE_JOBS = 4
TAIL_STAGE_AFTER_JOBS = (7, 12)
CONV_COLS = 128
PROJ_COLS = 512
PACK_ROWS = 256
MOE_ROWS = 512
MOE_SLAB = 256
COMB_ROWS = 512
SC_WINDOW = 32
N_CHAINS = 2


def _dot(a, b):
    return jnp.dot(a, b, preferred_element_type=F32)


def _pack_bf16_pairs(v):
    m = v.shape[1] // 2
    lo = lax.bitcast_convert_type(v[:, 0:m].astype(BF16).astype(F32), jnp.uint32)
    hi = lax.bitcast_convert_type(v[:, m:2 * m].astype(BF16).astype(F32), jnp.uint32)
    return jnp.bitwise_or(jnp.bitwise_and(hi, jnp.uint32(0xFFFF0000)),
                          lax.shift_right_logical(lo, jnp.uint32(16)))


def _unpack_bf16_pairs(w):
    lo = lax.bitcast_convert_type(lax.shift_left(w, jnp.uint32(16)), F32)
    hi = lax.bitcast_convert_type(jnp.bitwise_and(w, jnp.uint32(0xFFFF0000)), F32)
    return lo, hi


def _layer_norm(v, g, b):
    mu = jnp.mean(v, axis=-1, keepdims=True)
    vc = v - mu
    var = jnp.mean(vc * vc, axis=-1, keepdims=True)
    return vc * lax.rsqrt(var + LN_EPS) * g + b


def _ada_kernel(c_ref, w_ref, b_ref, o_ref):
    c = c_ref[...]
    c_act = c * jax.nn.sigmoid(c)
    o_ref[0] = jnp.dot(c_act, w_ref[0], preferred_element_type=F32,
                       precision=lax.Precision.HIGHEST) + b_ref[0]


def _ada(c, w_ada, b_ada):
    bsz, d = c.shape
    n = w_ada.shape[-1]
    tn = 1536
    return pl.pallas_call(
        _ada_kernel,
        grid=(DEPTH, n // tn),
        in_specs=[
            pl.BlockSpec((bsz, d), lambda l, j: (0, 0)),
            pl.BlockSpec((1, d, tn), lambda l, j: (l, 0, j)),
            pl.BlockSpec((1, 1, tn), lambda l, j: (l, 0, j)),
        ],
        out_specs=pl.BlockSpec((1, bsz, tn), lambda l, j: (l, 0, j)),
        out_shape=jax.ShapeDtypeStruct((DEPTH, bsz, n), F32),
    )(c, w_ada, b_ada.reshape(DEPTH, 1, n))


def _top2_of4(rows):
    m1 = rows[0]
    i1 = jnp.zeros(rows[0].shape, jnp.int32)
    for k in range(1, 4):
        gt = rows[k] > m1
        m1 = jnp.where(gt, rows[k], m1)
        i1 = jnp.where(gt, k, i1)
    m2 = jnp.full(rows[0].shape, -jnp.inf, F32)
    i2 = jnp.zeros(rows[0].shape, jnp.int32)
    for k in range(4):
        cand = jnp.where(i1 == k, -jnp.inf, rows[k])
        gt = cand > m2
        m2 = jnp.where(gt, cand, m2)
        i2 = jnp.where(gt, k, i2)
    return m1, i1, m2, i2


def _zero_after(v):
    u = lax.bitcast_convert_type(v, jnp.uint32)
    u = lax.shift_right_logical(lax.shift_right_logical(u, jnp.uint32(16)), jnp.uint32(16))
    return lax.bitcast_convert_type(u, F32)


def _conv31_chunk(cvw_ref, cvbuf, cvout, r0, c0, tie):
    cs = slice(c0, c0 + CONV_COLS)
    acc = None
    for r in range(SUBLANES):
        lead = SUBLANES if r else 0
        part = None
        for m in range((CV_KERNEL - 1 - r) // SUBLANES + 1):
            k = CV_KERNEL - 1 - (SUBLANES * m + r)
            start = CV_HALO + r0 - lead - SUBLANES * m
            w_row = cvw_ref[k:k + 1, cs]
            if tie is not None and acc is None and part is None:
                w_row = w_row + tie
            term = w_row * cvbuf[start:start + lead + CONV_ROWS, cs]
            part = term if part is None else part + term
        part = part[lead - r:lead - r + CONV_ROWS]
        acc = part if acc is None else acc + part
    cvout[r0:r0 + CONV_ROWS, cs] = acc


def _bf16_weights(packed):
    return pltpu.bitcast(packed, BF16)


def _pack_weights_kernel(w_ref, o_ref):
    o_ref[0] = pltpu.bitcast(w_ref[0].astype(BF16), jnp.uint32)


def _pack_weights(w):
    g, k, n = w.shape
    kb = PACK_ROWS
    return pl.pallas_call(
        _pack_weights_kernel,
        grid=(g, k // kb),
        in_specs=[pl.BlockSpec((1, kb, n), lambda i, j: (i, j, 0))],
        out_specs=pl.BlockSpec((1, kb // 2, n), lambda i, j: (i, j, 0)),
        out_shape=jax.ShapeDtypeStruct((g, k // 2, n), jnp.uint32),
        compiler_params=pltpu.CompilerParams(vmem_limit_bytes=EXPERT_VMEM_LIMIT),
    )(w)


def _mixer_kernel(tiles_per_seq, x_ref, mod_ref, modt_ref, w_in_ref, scw_ref, cvw_ref,
                  cvb_ref, cvg_ref, cvbeta_ref, sgg_ref, sgbeta_ref, sgw_ref, sgbt_ref, wbr_ref,
                  wg_ref, bg_ref, wo_ref, ln1g_ref, ln1b_ref, wrt_ref, brc_ref,
                  x1_ref, u2_ref, ri_ref, rf_ref, cnt_ref,
                  qbuf, cvbuf, cvout, ybuf, pa_buf, pc_buf, gl_buf, mg_buf, xs_buf, base_ref):
    ts = MIX_ROWS
    d = D_MODEL
    wb = W_BRANCH
    step = pl.program_id(0)
    tile = jnp.minimum(step, pl.num_programs(0) - 2)
    first_tile = tile % tiles_per_seq == 0

    @pl.when(step == 0)
    def _():
        base_ref[...] = jnp.zeros_like(base_ref)
        mg_buf[...] = jnp.zeros_like(mg_buf)
        xs_buf[...] = jnp.zeros_like(xs_buf)

    def w_in(c0):
        return _bf16_weights(w_in_ref[0, :, c0:c0 + PROJ_COLS])

    def w_gate(c0):
        n, col = divmod(c0, d)
        return _bf16_weights(wg_ref[0, n, :, col:col + PROJ_COLS])

    @pl.when(first_tile)
    def _():
        qbuf[0:SC_HALO, :] = jnp.zeros((SC_HALO, wb), F32)
        cvbuf[0:CV_HALO, :] = jnp.zeros((CV_HALO, wb), F32)

    x = x_ref[0]
    mod = mod_ref[0]
    shift1, scale1 = mod[:, 0:d], mod[:, d:2 * d]
    ub = (x * (1.0 + scale1) + shift1).astype(BF16)

    for c0 in range(0, wb, PROJ_COLS):
        a = _dot(ub, w_in(3 * wb + c0))
        g = _dot(ub, w_in(4 * wb + c0))
        cvbuf[CV_HALO:CV_HALO + ts, c0:c0 + PROJ_COLS] = a * jax.nn.sigmoid(g)

    tail = _mixer_tail(step, mg_buf, xs_buf, modt_ref, wo_ref, ln1g_ref, ln1b_ref, wrt_ref,
                       brc_ref, x1_ref, u2_ref, ri_ref, rf_ref, cnt_ref, base_ref)
    next(tail)
    ties = {}

    def add_tie(chunk, tie):
        ties[chunk] = tie + ties[chunk] if chunk in ties else tie

    mxu_jobs = ([(pa_buf, w_in, c0, c0) for c0 in range(0, 3 * wb, PROJ_COLS)]
                + [(pc_buf, w_in, c0, 5 * wb + c0) for c0 in range(0, 2 * wb, PROJ_COLS)]
                + [(gl_buf, w_gate, c0, c0) for c0 in range(0, 3 * d, PROJ_COLS)])
    conv_jobs = [(r0, c0) for c0 in range(0, wb, CONV_COLS) for r0 in range(0, ts, CONV_ROWS)]
    chunks_per_job = len(conv_jobs) // len(mxu_jobs)
    for c, conv_job in enumerate(conv_jobs):
        _conv31_chunk(cvw_ref, cvbuf, cvout, *conv_job, ties.get(c))
        if (c + 1) % chunks_per_job:
            continue
        i = c // chunks_per_job
        dst, weights, dc, wc = mxu_jobs[i]
        res = _dot(ub, weights(wc))
        dst[:, dc:dc + PROJ_COLS] = res
        if i < len(mxu_jobs) - TIE_FREE_JOBS:
            add_tie(c + TIE_LAG, _zero_after(res[ts - 1:ts, PROJ_COLS - CONV_COLS:PROJ_COLS]))
        if i in TAIL_STAGE_AFTER_JOBS:
            next(tail, None)
    cvbuf[0:CV_HALO, :] = cvbuf[ts:ts + CV_HALO, :]

    qbuf[SC_HALO:SC_HALO + ts, :] = pa_buf[:, wb:2 * wb] * pa_buf[:, 2 * wb:3 * wb]
    conv = scw_ref[SC_KERNEL - 1:SC_KERNEL, :] * qbuf[SC_HALO:SC_HALO + ts, :]
    for k in range(SC_KERNEL - 1):
        off = SC_HALO - (SC_KERNEL - 1) + k
        conv = conv + scw_ref[k:k + 1, :] * qbuf[off:off + ts, :]
    ybuf[0] = (pa_buf[:, 0:wb] * conv).astype(BF16)
    qbuf[0:SC_HALO, :] = qbuf[ts:ts + SC_HALO, :]
    z0 = _dot(ybuf[0], _bf16_weights(wbr_ref[0, 0]))

    cv = _layer_norm(cvout[...] + cvb_ref[...], cvg_ref[...], cvbeta_ref[...])
    ybuf[1] = (cv * jax.nn.sigmoid(cv)).astype(BF16)
    z1 = _dot(ybuf[1], _bf16_weights(wbr_ref[0, 1]))

    gu = jax.nn.gelu(pc_buf[:, 0:wb])
    gv = _layer_norm(jax.nn.gelu(pc_buf[:, wb:2 * wb]), sgg_ref[...], sgbeta_ref[...]).astype(BF16)
    merged = (jax.nn.sigmoid(gl_buf[:, 0:d] + bg_ref[0:1, :]) * z0
              + jax.nn.sigmoid(gl_buf[:, d:2 * d] + bg_ref[1:2, :]) * z1)
    g2 = jax.nn.sigmoid(gl_buf[:, 2 * d:3 * d] + bg_ref[2:3, :])
    row = lax.broadcasted_iota(jnp.int32, (CHUNK, CHUNK), 0)
    col = lax.broadcasted_iota(jnp.int32, (CHUNK, CHUNK), 1)
    hd = wb // SG_HEADS
    for h in range(SG_HEADS):
        wm = jnp.where(row >= col, sgw_ref[h], 0.0).astype(BF16)
        bias = sgbt_ref[:, h:h + 1]
        for n in range(ts // CHUNK):
            rs = slice(n * CHUNK, (n + 1) * CHUNK)
            cs = slice(h * hd, (h + 1) * hd)
            mixed = _dot(wm, gv[rs, cs]) + bias
            ybuf[2, rs, cs] = (gu[rs, cs] * mixed).astype(BF16)
    merged = merged + g2 * _dot(ybuf[2], _bf16_weights(wbr_ref[0, 2]))
    mg_buf[...] = merged.astype(BF16)
    xs_buf[...] = x


def _mixer_tail(step, mg_buf, xs_buf, modt_ref, wo_ref, ln1g_ref, ln1b_ref, wrt_ref, brc_ref,
                x1_ref, u2_ref, ri_ref, rf_ref, cnt_ref, base_ref):
    ts = MIX_ROWS
    d = D_MODEL
    modt = modt_ref[0]
    gate1, shift2, scale2 = modt[:, 2 * d:3 * d], modt[:, 3 * d:4 * d], modt[:, 4 * d:5 * d]
    hmix = _dot(mg_buf[...], _bf16_weights(wo_ref[0]))
    x1 = _layer_norm(ALPHA * xs_buf[...] + gate1 * hmix, ln1g_ref[...], ln1b_ref[...])
    x1_ref[0] = x1
    u2 = x1 * (1.0 + scale2) + shift2
    u2_ref[0] = _pack_bf16_pairs(u2)
    yield

    logits = lax.dot_general(wrt_ref[...], u2, (((1,), (1,)), ((), ())),
                             preferred_element_type=F32,
                             precision=lax.Precision.HIGHEST)
    mx = jnp.max(logits, axis=0, keepdims=True)
    ex = jnp.exp(logits - mx)
    scores = ex / jnp.sum(ex, axis=0, keepdims=True)
    sel = scores + brc_ref[...]
    tops = []
    for g in range(N_GROUPS):
        rows = [sel[g * EXPERTS_PER_GROUP + k:g * EXPERTS_PER_GROUP + k + 1, :]
                for k in range(EXPERTS_PER_GROUP)]
        tops.append(_top2_of4(rows))
    best = tops[0][0] + tops[0][2]
    g_idx = jnp.zeros(best.shape, jnp.int32)
    loc1, loc2 = tops[0][1], tops[0][3]
    for g in range(1, N_GROUPS):
        gs = tops[g][0] + tops[g][2]
        gt = gs > best
        best = jnp.where(gt, gs, best)
        g_idx = jnp.where(gt, g, g_idx)
        loc1 = jnp.where(gt, tops[g][1], loc1)
        loc2 = jnp.where(gt, tops[g][3], loc2)
    e0 = g_idx * EXPERTS_PER_GROUP + loc1
    e1 = g_idx * EXPERTS_PER_GROUP + loc2
    erow = lax.broadcasted_iota(jnp.int32, (N_EXPERTS, ts), 0)
    is0 = erow == e0
    is1 = erow == e1
    s0 = jnp.sum(jnp.where(is0, scores, 0.0), axis=0, keepdims=True)
    s1 = jnp.sum(jnp.where(is1, scores, 0.0), axis=0, keepdims=True)
    ssum = s0 + s1
    yield

    onehot = jnp.logical_or(is0, is1).astype(BF16)
    src = lax.broadcasted_iota(jnp.int32, (ts, ts), 0)
    dst = lax.broadcasted_iota(jnp.int32, (ts, ts), 1)
    earlier = (src < dst).astype(BF16)
    prior = _dot(onehot, earlier) + base_ref[:, 0:1]
    r0 = jnp.sum(jnp.where(is0, prior, 0.0), axis=0, keepdims=True)
    r1 = jnp.sum(jnp.where(is1, prior, 0.0), axis=0, keepdims=True)
    counts = jnp.sum(onehot.astype(F32), axis=1, keepdims=True)
    base_ref[...] = base_ref[...] + jnp.where(step > 0, counts, 0.0)

    zi = jnp.zeros((SUBLANES - 4, ts), jnp.int32)
    ri_ref[...] = jnp.concatenate([e0, e1, r0.astype(jnp.int32), r1.astype(jnp.int32), zi], axis=0)
    zf = jnp.zeros((SUBLANES - 2, ts), F32)
    rf_ref[...] = jnp.concatenate([s0 / ssum, s1 / ssum, zf], axis=0)
    cnt_ref[...] = base_ref[...]


def _mixer(layer, x, xb0, mod3, mb0, bsz, w_in, scw, cvw, cvb, cvg, cvbeta, sgg, sgbeta, sgw, sgbt,
           wbr, wg, bg, wo, ln1g, ln1b, wrt, brc):
    _, seq, d = x.shape
    ts = MIX_ROWS
    ns = seq // ts
    n_tok = bsz * seq

    n_tiles = bsz * ns

    def const(shape):
        zeros = (0,) * len(shape)
        return pl.BlockSpec(shape, lambda s: zeros, pipeline_mode=pl.Buffered(1))

    def first_half(s):
        return jnp.minimum(s, n_tiles - 1)

    def second_half(s):
        return jnp.maximum(s - 1, 0)

    def layer_weights(shape):
        block = (1,) + tuple(shape[1:])
        index = (layer,) + (0,) * (len(shape) - 1)
        return pl.BlockSpec(block, lambda s: index, pipeline_mode=pl.Buffered(1))

    in_specs = [
        pl.BlockSpec((1, ts, d), lambda s: (first_half(s) // ns + xb0, first_half(s) % ns, 0)),
        pl.BlockSpec((1, 1, 6 * d), lambda s: (first_half(s) // ns + mb0, 0, 0)),
        pl.BlockSpec((1, 1, 6 * d), lambda s: (second_half(s) // ns + mb0, 0, 0)),
        layer_weights(w_in.shape), const(scw.shape), const(cvw.shape), const(cvb.shape),
        const(cvg.shape), const(cvbeta.shape), const(sgg.shape), const(sgbeta.shape),
        const(sgw.shape), const(sgbt.shape), layer_weights(wbr.shape), layer_weights(wg.shape),
        const(bg.shape), layer_weights(wo.shape), const(ln1g.shape), const(ln1b.shape),
        const(wrt.shape), const(brc.shape),
    ]
    out_specs = [
        pl.BlockSpec((1, ts, d), lambda s: (second_half(s) // ns, second_half(s) % ns, 0)),
        pl.BlockSpec((1, ts, d // 2), lambda s: (second_half(s) // ns, second_half(s) % ns, 0)),
        pl.BlockSpec((SUBLANES, ts), lambda s: (0, second_half(s))),
        pl.BlockSpec((SUBLANES, ts), lambda s: (0, second_half(s))),
        pl.BlockSpec((N_EXPERTS, LANES), lambda s: (0, 0)),
    ]
    out_shape = [
        jax.ShapeDtypeStruct((bsz, seq, d), F32),
        jax.ShapeDtypeStruct((bsz, seq, d // 2), jnp.uint32),
        jax.ShapeDtypeStruct((SUBLANES, n_tok), jnp.int32),
        jax.ShapeDtypeStruct((SUBLANES, n_tok), F32),
        jax.ShapeDtypeStruct((N_EXPERTS, LANES), F32),
    ]
    return pl.pallas_call(
        functools.partial(_mixer_kernel, ns),
        grid=(n_tiles + 1,),
        in_specs=in_specs,
        out_specs=out_specs,
        out_shape=out_shape,
        scratch_shapes=[
            pltpu.VMEM((SC_HALO + ts, W_BRANCH), F32),
            pltpu.VMEM((CV_HALO + ts, W_BRANCH), F32),
            pltpu.VMEM((ts, W_BRANCH), F32),
            pltpu.VMEM((3, ts, W_BRANCH), BF16),
            pltpu.VMEM((ts, 3 * W_BRANCH), F32),
            pltpu.VMEM((ts, 2 * W_BRANCH), F32),
            pltpu.VMEM((ts, 3 * D_MODEL), F32),
            pltpu.VMEM((ts, d), BF16),
            pltpu.VMEM((ts, d), F32),
            pltpu.VMEM((N_EXPERTS, LANES), F32),
        ],
        compiler_params=pltpu.CompilerParams(
            dimension_semantics=("arbitrary",),
            vmem_limit_bytes=MIXER_VMEM_LIMIT),
    )(x, mod3, mod3, w_in, scw, cvw, cvb, cvg, cvbeta, sgg, sgbeta, sgw, sgbt, wbr, wg, bg, wo,
      ln1g, ln1b, wrt, brc)


def _sc_workers():
    info = plsc.get_sparse_core_info()
    return info.num_cores, info.num_cores * info.num_subcores


def _sc_scatter_rows(rows, dest_a, dest_b, n_out):
    n, d = rows.shape
    nc, nw = _sc_workers()
    per_w = n // nw
    n_win = per_w // SC_WINDOW
    ia = dest_a.reshape(nw, n_win, SC_WINDOW)
    ib = dest_b.reshape(nw, n_win, SC_WINDOW)
    mesh = plsc.VectorSubcoreMesh(core_axis_name="c", subcore_axis_name="s")

    @functools.partial(
        pl.kernel, mesh=mesh,
        out_type=jax.ShapeDtypeStruct((n_out, d), rows.dtype),
        scratch_types=[
            pltpu.VMEM((n_win, SC_WINDOW), jnp.int32),
            pltpu.VMEM((n_win, SC_WINDOW), jnp.int32),
            pltpu.VMEM((SC_WINDOW, d), rows.dtype),
        ],
    )
    def scatter(rows_hbm, ia_hbm, ib_hbm, out_hbm, ia_v, ib_v, rows_v):
        wid = lax.axis_index("s") * nc + lax.axis_index("c")
        pltpu.sync_copy(ia_hbm.at[wid], ia_v)
        pltpu.sync_copy(ib_hbm.at[wid], ib_v)
        base = wid * per_w

        @pl.loop(0, n_win)
        def _(j):
            pltpu.sync_copy(rows_hbm.at[pl.ds(base + j * SC_WINDOW, SC_WINDOW)], rows_v)
            pltpu.sync_copy(rows_v, out_hbm.at[ia_v.at[j]])
            pltpu.sync_copy(rows_v, out_hbm.at[ib_v.at[j]])

    return scatter(rows, ia, ib)


def _sc_gather_rows(table, idx):
    n = idx.shape[0]
    d = table.shape[1]
    nc, nw = _sc_workers()
    per_w = n // nw
    n_win = per_w // SC_WINDOW
    idx3 = idx.reshape(nw, n_win, SC_WINDOW)
    mesh = plsc.VectorSubcoreMesh(core_axis_name="c", subcore_axis_name="s")

    @functools.partial(
        pl.kernel, mesh=mesh,
        out_type=jax.ShapeDtypeStruct((n, d), table.dtype),
        scratch_types=[
            pltpu.VMEM((n_win, SC_WINDOW), jnp.int32),
            pltpu.VMEM((SC_WINDOW, d), table.dtype),
        ],
    )
    def gather(table_hbm, idx_hbm, out_hbm, idx_v, rows_v):
        wid = lax.axis_index("s") * nc + lax.axis_index("c")
        pltpu.sync_copy(idx_hbm.at[wid], idx_v)
        base = wid * per_w

        @pl.loop(0, n_win)
        def _(j):
            pltpu.sync_copy(table_hbm.at[idx_v.at[j]], rows_v)
            pltpu.sync_copy(rows_v, out_hbm.at[pl.ds(base + j * SC_WINDOW, SC_WINDOW)])

    return gather(table, idx3)


def _expert_kernel(be_ref, nused_ref, x_ref, w1_ref, w3_ref, w2_ref, o_ref, w1s, w3s, w2s):
    i = pl.program_id(0)
    new_expert = jnp.logical_or(i == 0, be_ref[i] != be_ref[jnp.maximum(i - 1, 0)])

    @pl.when(new_expert)
    def _():
        w1s[...] = w1_ref[0, 0].astype(BF16)
        w3s[...] = w3_ref[0, 0].astype(BF16)
        w2s[...] = w2_ref[0, 0].astype(BF16)

    @pl.when(i < nused_ref[0])
    def _():
        lo, hi = _unpack_bf16_pairs(x_ref[...])
        xb = jnp.concatenate([lo, hi], axis=1).astype(BF16)
        a = _dot(xb, w1s[...])
        b = _dot(xb, w3s[...])
        for r0 in range(0, MOE_ROWS, MOE_SLAB):
            rs = slice(r0, r0 + MOE_SLAB)
            h = a[rs] * jax.nn.sigmoid(a[rs]) * b[rs]
            o_ref[rs, :] = _pack_bf16_pairs(_dot(h.astype(BF16), w2s[...]))


def _experts(layer, buf, block_e, n_used, w1, w3, w2):
    n_rows, dp = buf.shape
    nb = n_rows // MOE_ROWS
    d, fe = w1.shape[-2:]

    def row_map(i, be, nu):
        return (jnp.minimum(i, nu[0] - 1), 0)

    grid_spec = pltpu.PrefetchScalarGridSpec(
        num_scalar_prefetch=2,
        grid=(nb,),
        in_specs=[
            pl.BlockSpec((MOE_ROWS, dp), row_map),
            pl.BlockSpec((1, 1, d, fe), lambda i, be, nu: (layer, be[i], 0, 0)),
            pl.BlockSpec((1, 1, d, fe), lambda i, be, nu: (layer, be[i], 0, 0)),
            pl.BlockSpec((1, 1, fe, d), lambda i, be, nu: (layer, be[i], 0, 0)),
        ],
        out_specs=pl.BlockSpec((MOE_ROWS, dp), row_map),
        scratch_shapes=[pltpu.VMEM((d, fe), BF16), pltpu.VMEM((d, fe), BF16),
                        pltpu.VMEM((fe, d), BF16)],
    )
    return pl.pallas_call(
        _expert_kernel,
        grid_spec=grid_spec,
        out_shape=jax.ShapeDtypeStruct((n_rows, dp), jnp.uint32),
        compiler_params=pltpu.CompilerParams(dimension_semantics=("arbitrary",),
                                             vmem_limit_bytes=EXPERT_VMEM_LIMIT),
    )(block_e, n_used, buf, w1, w3, w2)


def _combine_kernel(x1_ref, ya_ref, yb_ref, gates_ref, mod_ref, g_ref, b_ref, *rest):
    o_ref = rest[-1]
    d = D_MODEL
    gate2 = mod_ref[0][:, 5 * d:6 * d]
    gates = gates_ref[...]
    ya = jnp.concatenate(_unpack_bf16_pairs(ya_ref[0]), axis=1)
    yb = jnp.concatenate(_unpack_bf16_pairs(yb_ref[0]), axis=1)
    h = gates[:, 0:1] * ya + gates[:, 1:2] * yb
    o_ref[0] = _layer_norm(ALPHA * x1_ref[0] + gate2 * h, g_ref[...], b_ref[...])


def _combine(x1, pairs, gates, mod3, mb0, g, b, out_bsz, ob0, prev):
    bsz, seq, d = x1.shape
    ts = COMB_ROWS
    ns = seq // ts
    in_specs = [
        pl.BlockSpec((1, ts, d), lambda i, j: (i, j, 0)),
        pl.BlockSpec((1, ts, d // 2), lambda i, j: (0, i * ns + j, 0)),
        pl.BlockSpec((1, ts, d // 2), lambda i, j: (1, i * ns + j, 0)),
        pl.BlockSpec((ts, TOP_K), lambda i, j: (i * ns + j, 0)),
        pl.BlockSpec((1, 1, 6 * d), lambda i, j: (i + mb0, 0, 0)),
        pl.BlockSpec((1, d), lambda i, j: (0, 0)),
        pl.BlockSpec((1, d), lambda i, j: (0, 0)),
    ]
    args = [x1, pairs, pairs, gates, mod3, g, b]
    aliases = {}
    if prev is not None:
        in_specs.append(pl.BlockSpec(memory_space=pl.ANY))
        aliases = {len(args): 0}
        args.append(prev)
    return pl.pallas_call(
        _combine_kernel,
        grid=(bsz, ns),
        in_specs=in_specs,
        out_specs=pl.BlockSpec((1, ts, d), lambda i, j: (i + ob0, j, 0)),
        out_shape=jax.ShapeDtypeStruct((out_bsz, seq, d), F32),
        input_output_aliases=aliases,
        compiler_params=pltpu.CompilerParams(dimension_semantics=("arbitrary", "arbitrary")),
    )(*args)


def _plan_kernel(ri_ref, cnt_ref, dest_ref, blk_ref):
    n_tok = ri_ref.shape[1]
    counts = cnt_ref[:, 0:1].astype(jnp.int32)
    shift = MOE_ROWS.bit_length() - 1
    padded = lax.shift_left(lax.shift_right_logical(counts + (MOE_ROWS - 1), shift), shift)
    e_out = lax.broadcasted_iota(jnp.int32, (N_EXPERTS, N_EXPERTS), 0)
    e_in = lax.broadcasted_iota(jnp.int32, (N_EXPERTS, N_EXPERTS), 1)
    upto = (e_in <= e_out).astype(BF16)
    padded_f = jnp.broadcast_to(padded.astype(F32), (N_EXPERTS, LANES))
    pad_end = _dot(upto, padded_f.astype(BF16))
    pad_start = pad_end[:, 0:1] - padded.astype(F32)
    erow = lax.broadcasted_iota(jnp.int32, (N_EXPERTS, n_tok), 0)
    rows = []
    for k in range(TOP_K):
        start = jnp.sum(jnp.where(erow == ri_ref[k:k + 1, :], pad_start, 0.0), axis=0, keepdims=True)
        rows.append(start.astype(jnp.int32) + ri_ref[TOP_K + k:TOP_K + k + 1, :])
    rows.append(jnp.zeros((SUBLANES - TOP_K, n_tok), jnp.int32))
    dest_ref[...] = jnp.concatenate(rows, axis=0)
    block_start = (lax.broadcasted_iota(jnp.int32, (N_EXPERTS, LANES), 1) * MOE_ROWS).astype(F32)
    block_e = jnp.sum((block_start >= pad_end).astype(jnp.int32), axis=0, keepdims=True)
    n_used = lax.shift_right_logical(pad_end[N_EXPERTS - 1:N_EXPERTS, :].astype(jnp.int32), shift)
    blk_ref[...] = jnp.concatenate(
        [jnp.minimum(block_e, N_EXPERTS - 1), n_used,
         jnp.zeros((SUBLANES - 2, LANES), jnp.int32)], axis=0)


def _dispatch_plan(ri, cnt):
    n_tok = ri.shape[1]
    n_blocks = (n_tok * TOP_K + N_EXPERTS * (MOE_ROWS - 1) + MOE_ROWS - 1) // MOE_ROWS
    assert n_blocks <= LANES and MOE_ROWS & (MOE_ROWS - 1) == 0
    dest, blk = pl.pallas_call(
        _plan_kernel,
        out_shape=[jax.ShapeDtypeStruct((SUBLANES, n_tok), jnp.int32),
                   jax.ShapeDtypeStruct((SUBLANES, LANES), jnp.int32)],
    )(ri, cnt)
    return dest, blk[0, 0:n_blocks], blk[1, 0:1], n_blocks * MOE_ROWS


def kernel(x, c, w_ada, b_ada, w_in, sc_conv, cv_conv, cv_conv_b, cv_ln_g, cv_ln_b, sg_ln_g, sg_ln_b, sg_w, sg_b, w_branch, w_gate, b_gate, w_o, ln1_g, ln1_b, w_router, b_router, w1, w3, w2, ln2_g, ln2_b):
    bsz, seq, d = x.shape
    mod3 = _ada(c, w_ada, b_ada).reshape(DEPTH, bsz, 1, 6 * d)
    wrt = w_router.T
    brc = b_router.reshape(N_EXPERTS, 1)
    w_in_p = _pack_weights(w_in)
    w_gate_p = _pack_weights(jnp.transpose(w_gate, (0, 2, 1, 3)).reshape(DEPTH * 3, d, d)).reshape(
        DEPTH, 3, d // 2, d)
    w_o_p = _pack_weights(w_o)
    w_branch_p = _pack_weights(w_branch.reshape(DEPTH * 3, W_BRANCH, d)).reshape(
        DEPTH, 3, W_BRANCH // 2, d)
    cb = bsz // N_CHAINS
    n_tok = cb * seq
    chains = [(x, h * cb) for h in range(N_CHAINS)]
    for l in range(DEPTH):
        last = l == DEPTH - 1
        result = None
        for h in range(N_CHAINS):
            xin, xb0 = chains[h]
            x1, u2, ri, rf, cnt = _mixer(
                l, xin, xb0, mod3[l], h * cb, cb, w_in_p, sc_conv[l], cv_conv[l],
                cv_conv_b[l].reshape(1, -1), cv_ln_g[l].reshape(1, -1), cv_ln_b[l].reshape(1, -1),
                sg_ln_g[l].reshape(1, -1), sg_ln_b[l].reshape(1, -1), sg_w[l], sg_b[l].T,
                w_branch_p, w_gate_p, b_gate[l],
                w_o_p, ln1_g[l].reshape(1, -1), ln1_b[l].reshape(1, -1), wrt, brc)
            dest, block_e, n_used, n_rows = _dispatch_plan(ri, cnt)
            buf = _sc_scatter_rows(u2.reshape(n_tok, d // 2), dest[0], dest[1], n_rows)
            obuf = _experts(l, buf, block_e, n_used, w1, w3, w2)
            pair_idx = dest[0:TOP_K].reshape(TOP_K * n_tok)
            pairs = _sc_gather_rows(obuf, pair_idx).reshape(TOP_K, n_tok, d // 2)
            gates = rf[0:TOP_K].T
            g2, b2 = ln2_g[l].reshape(1, -1), ln2_b[l].reshape(1, -1)
            if last:
                result = _combine(x1, pairs, gates, mod3[l], h * cb, g2, b2, bsz, h * cb, result)
            else:
                chains[h] = (_combine(x1, pairs, gates, mod3[l], h * cb, g2, b2, cb, 0, None), 0)
    return result
```

```python
import functools

import jax
import jax.numpy as jnp
from jax import lax
from jax.experimental import pallas as pl
from jax.experimental.pallas import tpu as pltpu
from jax.experimental.pallas import tpu_sc as plsc

D_MODEL = 1024
DEPTH = 2
W_BRANCH = 1024
SC_KERNEL = 3
CV_KERNEL = 31
CHUNK = 128
SG_HEADS = 8
N_EXPERTS = 16
N_GROUPS = 4
EXPERTS_PER_GROUP = N_EXPERTS // N_GROUPS
TOP_K = 2
D_EXPERT = 512
ALPHA = (2.0 * DEPTH) ** 0.25
LN_EPS = 1e-5

F32 = jnp.float32
BF16 = jnp.bfloat16

V7X_VMEM_BYTES = 64 * 1024 * 1024
MIXER_VMEM_LIMIT = V7X_VMEM_BYTES - 6 * 1024 * 1024
EXPERT_VMEM_LIMIT = V7X_VMEM_BYTES // 2
SUBLANES = 8
LANES = 128

MIX_ROWS = 256
SC_HALO = SUBLANES
CV_HALO = 32
CONV_ROWS = 128
TIE_LAG = 2
TIE_FREE_JOBS = 4
TAIL_STAGE_AFTER_JOBS = (7, 12)
CONV_COLS = 128
PROJ_COLS = 512
PACK_ROWS = 256
MOE_ROWS = 512
MOE_SLAB = 256
COMB_ROWS = 512
SC_WINDOW = 32
N_CHAINS = 2


def _dot(a, b):
    return jnp.dot(a, b, preferred_element_type=F32)


def _pack_bf16_pairs(v):
    m = v.shape[1] // 2
    lo = lax.bitcast_convert_type(v[:, 0:m].astype(BF16).astype(F32), jnp.uint32)
    hi = lax.bitcast_convert_type(v[:, m:2 * m].astype(BF16).astype(F32), jnp.uint32)
    return jnp.bitwise_or(jnp.bitwise_and(hi, jnp.uint32(0xFFFF0000)),
                          lax.shift_right_logical(lo, jnp.uint32(16)))


def _unpack_bf16_pairs(w):
    lo = lax.bitcast_convert_type(lax.shift_left(w, jnp.uint32(16)), F32)
    hi = lax.bitcast_convert_type(jnp.bitwise_and(w, jnp.uint32(0xFFFF0000)), F32)
    return lo, hi


def _layer_norm(v, g, b):
    mu = jnp.mean(v, axis=-1, keepdims=True)
    vc = v - mu
    var = jnp.mean(vc * vc, axis=-1, keepdims=True)
    return vc * lax.rsqrt(var + LN_EPS) * g + b


def _ada_kernel(c_ref, w_ref, b_ref, o_ref):
    c = c_ref[...]
    c_act = c * jax.nn.sigmoid(c)
    o_ref[0] = jnp.dot(c_act, w_ref[0], preferred_element_type=F32,
                       precision=lax.Precision.HIGHEST) + b_ref[0]


def _ada(c, w_ada, b_ada):
    bsz, d = c.shape
    n = w_ada.shape[-1]
    tn = 1536
    return pl.pallas_call(
        _ada_kernel,
        grid=(DEPTH, n // tn),
        in_specs=[
            pl.BlockSpec((bsz, d), lambda l, j: (0, 0)),
            pl.BlockSpec((1, d, tn), lambda l, j: (l, 0, j)),
            pl.BlockSpec((1, 1, tn), lambda l, j: (l, 0, j)),
        ],
        out_specs=pl.BlockSpec((1, bsz, tn), lambda l, j: (l, 0, j)),
        out_shape=jax.ShapeDtypeStruct((DEPTH, bsz, n), F32),
    )(c, w_ada, b_ada.reshape(DEPTH, 1, n))


def _top2_of4(rows):
    m1 = rows[0]
    i1 = jnp.zeros(rows[0].shape, jnp.int32)
    for k in range(1, 4):
        gt = rows[k] > m1
        m1 = jnp.where(gt, rows[k], m1)
        i1 = jnp.where(gt, k, i1)
    m2 = jnp.full(rows[0].shape, -jnp.inf, F32)
    i2 = jnp.zeros(rows[0].shape, jnp.int32)
    for k in range(4):
        cand = jnp.where(i1 == k, -jnp.inf, rows[k])
        gt = cand > m2
        m2 = jnp.where(gt, cand, m2)
        i2 = jnp.where(gt, k, i2)
    return m1, i1, m2, i2


def _zero_after(v):
    u = lax.bitcast_convert_type(v, jnp.uint32)
    u = lax.shift_right_logical(lax.shift_right_logical(u, jnp.uint32(16)), jnp.uint32(16))
    return lax.bitcast_convert_type(u, F32)


def _conv31_chunk(cvw_ref, cvbuf, cvout, r0, c0, tie):
    cs = slice(c0, c0 + CONV_COLS)
    acc = None
    for r in range(SUBLANES):
        lead = SUBLANES if r else 0
        part = None
        for m in range((CV_KERNEL - 1 - r) // SUBLANES + 1):
            k = CV_KERNEL - 1 - (SUBLANES * m + r)
            start = CV_HALO + r0 - lead - SUBLANES * m
            w_row = cvw_ref[k:k + 1, cs]
            if tie is not None and acc is None and part is None:
                w_row = w_row + tie
            term = w_row * cvbuf[start:start + lead + CONV_ROWS, cs]
            part = term if part is None else part + term
        part = part[lead - r:lead - r + CONV_ROWS]
        acc = part if acc is None else acc + part
    cvout[r0:r0 + CONV_ROWS, cs] = acc


def _bf16_weights(packed):
    return pltpu.bitcast(packed, BF16)


def _pack_weights_kernel(w_ref, o_ref):
    o_ref[0] = pltpu.bitcast(w_ref[0].astype(BF16), jnp.uint32)


def _pack_weights(w):
    g, k, n = w.shape
    kb = PACK_ROWS
    return pl.pallas_call(
        _pack_weights_kernel,
        grid=(g, k // kb),
        in_specs=[pl.BlockSpec((1, kb, n), lambda i, j: (i, j, 0))],
        out_specs=pl.BlockSpec((1, kb // 2, n), lambda i, j: (i, j, 0)),
        out_shape=jax.ShapeDtypeStruct((g, k // 2, n), jnp.uint32),
        compiler_params=pltpu.CompilerParams(vmem_limit_bytes=EXPERT_VMEM_LIMIT),
    )(w)


def _moe_output(x1, ya_packed, yb_packed, gates, gate2, g, b):
    ya = jnp.concatenate(_unpack_bf16_pairs(ya_packed), axis=1)
    yb = jnp.concatenate(_unpack_bf16_pairs(yb_packed), axis=1)
    h = gates[:, 0:1] * ya + gates[:, 1:2] * yb
    return _layer_norm(ALPHA * x1 + gate2 * h, g, b)


def _mixer_kernel(tiles_per_seq, n_x_refs, *refs):
    x_refs = refs[:n_x_refs]
    (mod_ref, modt_ref, w_in_ref, scw_ref, cvw_ref, cvb_ref, cvg_ref, cvbeta_ref, sgg_ref,
     sgbeta_ref, sgw_ref, sgbt_ref, wbr_ref, wg_ref, bg_ref, wo_ref, ln1g_ref, ln1b_ref, wrt_ref,
     brc_ref,
     x1_ref, u2_ref, ri_ref, rf_ref, cnt_ref,
     qbuf, cvbuf, cvout, ybuf, pa_buf, pc_buf, gl_buf, mg_buf, xs_buf, base_ref) = refs[n_x_refs:]
    ts = MIX_ROWS
    d = D_MODEL
    wb = W_BRANCH
    step = pl.program_id(0)
    tile = jnp.minimum(step, pl.num_programs(0) - 2)
    first_tile = tile % tiles_per_seq == 0

    @pl.when(step == 0)
    def _():
        base_ref[...] = jnp.zeros_like(base_ref)
        mg_buf[...] = jnp.zeros_like(mg_buf)
        xs_buf[...] = jnp.zeros_like(xs_buf)

    def w_in(c0):
        return _bf16_weights(w_in_ref[0, :, c0:c0 + PROJ_COLS])

    def w_gate(c0):
        n, col = divmod(c0, d)
        return _bf16_weights(wg_ref[0, n, :, col:col + PROJ_COLS])

    @pl.when(first_tile)
    def _():
        qbuf[0:SC_HALO, :] = jnp.zeros((SC_HALO, wb), F32)
        cvbuf[0:CV_HALO, :] = jnp.zeros((CV_HALO, wb), F32)

    if n_x_refs == 1:
        x = x_refs[0][0]
    else:
        x1p_ref, ya_ref, yb_ref, gates_ref, modp_ref, g2_ref, b2_ref = x_refs
        x = _moe_output(x1p_ref[0], ya_ref[0], yb_ref[0], gates_ref[...],
                        modp_ref[0][:, 5 * d:6 * d], g2_ref[...], b2_ref[...])
    mod = mod_ref[0]
    shift1, scale1 = mod[:, 0:d], mod[:, d:2 * d]
    ub = (x * (1.0 + scale1) + shift1).astype(BF16)

    for c0 in range(0, wb, PROJ_COLS):
        a = _dot(ub, w_in(3 * wb + c0))
        g = _dot(ub, w_in(4 * wb + c0))
        cvbuf[CV_HALO:CV_HALO + ts, c0:c0 + PROJ_COLS] = a * jax.nn.sigmoid(g)

    tail = _mixer_tail(step, mg_buf, xs_buf, modt_ref, wo_ref, ln1g_ref, ln1b_ref, wrt_ref,
                       brc_ref, x1_ref, u2_ref, ri_ref, rf_ref, cnt_ref, base_ref)
    next(tail)
    ties = {}

    def add_tie(chunk, tie):
        ties[chunk] = tie + ties[chunk] if chunk in ties else tie

    mxu_jobs = ([(pa_buf, w_in, c0, c0) for c0 in range(0, 3 * wb, PROJ_COLS)]
                + [(pc_buf, w_in, c0, 5 * wb + c0) for c0 in range(0, 2 * wb, PROJ_COLS)]
                + [(gl_buf, w_gate, c0, c0) for c0 in range(0, 3 * d, PROJ_COLS)])
    conv_jobs = [(r0, c0) for c0 in range(0, wb, CONV_COLS) for r0 in range(0, ts, CONV_ROWS)]
    chunks_per_job = len(conv_jobs) // len(mxu_jobs)
    for c, conv_job in enumerate(conv_jobs):
        _conv31_chunk(cvw_ref, cvbuf, cvout, *conv_job, ties.get(c))
        if (c + 1) % chunks_per_job:
            continue
        i = c // chunks_per_job
        dst, weights, dc, wc = mxu_jobs[i]
        res = _dot(ub, weights(wc))
        dst[:, dc:dc + PROJ_COLS] = res
        if i < len(mxu_jobs) - TIE_FREE_JOBS:
            add_tie(c + TIE_LAG, _zero_after(res[ts - 1:ts, PROJ_COLS - CONV_COLS:PROJ_COLS]))
        if i in TAIL_STAGE_AFTER_JOBS:
            next(tail, None)
    cvbuf[0:CV_HALO, :] = cvbuf[ts:ts + CV_HALO, :]

    qbuf[SC_HALO:SC_HALO + ts, :] = pa_buf[:, wb:2 * wb] * pa_buf[:, 2 * wb:3 * wb]
    conv = scw_ref[SC_KERNEL - 1:SC_KERNEL, :] * qbuf[SC_HALO:SC_HALO + ts, :]
    for k in range(SC_KERNEL - 1):
        off = SC_HALO - (SC_KERNEL - 1) + k
        conv = conv + scw_ref[k:k + 1, :] * qbuf[off:off + ts, :]
    ybuf[0] = (pa_buf[:, 0:wb] * conv).astype(BF16)
    qbuf[0:SC_HALO, :] = qbuf[ts:ts + SC_HALO, :]
    z0 = _dot(ybuf[0], _bf16_weights(wbr_ref[0, 0]))

    cv = _layer_norm(cvout[...] + cvb_ref[...], cvg_ref[...], cvbeta_ref[...])
    ybuf[1] = (cv * jax.nn.sigmoid(cv)).astype(BF16)
    z1 = _dot(ybuf[1], _bf16_weights(wbr_ref[0, 1]))

    gu = jax.nn.gelu(pc_buf[:, 0:wb])
    gv = _layer_norm(jax.nn.gelu(pc_buf[:, wb:2 * wb]), sgg_ref[...], sgbeta_ref[...]).astype(BF16)
    merged = (jax.nn.sigmoid(gl_buf[:, 0:d] + bg_ref[0:1, :]) * z0
              + jax.nn.sigmoid(gl_buf[:, d:2 * d] + bg_ref[1:2, :]) * z1)
    g2 = jax.nn.sigmoid(gl_buf[:, 2 * d:3 * d] + bg_ref[2:3, :])
    row = lax.broadcasted_iota(jnp.int32, (CHUNK, CHUNK), 0)
    col = lax.broadcasted_iota(jnp.int32, (CHUNK, CHUNK), 1)
    hd = wb // SG_HEADS
    for h in range(SG_HEADS):
        wm = jnp.where(row >= col, sgw_ref[h], 0.0).astype(BF16)
        bias = sgbt_ref[:, h:h + 1]
        for n in range(ts // CHUNK):
            rs = slice(n * CHUNK, (n + 1) * CHUNK)
            cs = slice(h * hd, (h + 1) * hd)
            mixed = _dot(wm, gv[rs, cs]) + bias
            ybuf[2, rs, cs] = (gu[rs, cs] * mixed).astype(BF16)
    merged = merged + g2 * _dot(ybuf[2], _bf16_weights(wbr_ref[0, 2]))
    mg_buf[...] = merged.astype(BF16)
    xs_buf[...] = x


def _mixer_tail(step, mg_buf, xs_buf, modt_ref, wo_ref, ln1g_ref, ln1b_ref, wrt_ref, brc_ref,
                x1_ref, u2_ref, ri_ref, rf_ref, cnt_ref, base_ref):
    ts = MIX_ROWS
    d = D_MODEL
    modt = modt_ref[0]
    gate1, shift2, scale2 = modt[:, 2 * d:3 * d], modt[:, 3 * d:4 * d], modt[:, 4 * d:5 * d]
    hmix = _dot(mg_buf[...], _bf16_weights(wo_ref[0]))
    x1 = _layer_norm(ALPHA * xs_buf[...] + gate1 * hmix, ln1g_ref[...], ln1b_ref[...])
    x1_ref[0] = x1
    u2 = x1 * (1.0 + scale2) + shift2
    u2_ref[0] = _pack_bf16_pairs(u2)
    yield

    logits = lax.dot_general(wrt_ref[...], u2, (((1,), (1,)), ((), ())),
                             preferred_element_type=F32,
                             precision=lax.Precision.HIGHEST)
    mx = jnp.max(logits, axis=0, keepdims=True)
    ex = jnp.exp(logits - mx)
    scores = ex / jnp.sum(ex, axis=0, keepdims=True)
    sel = scores + brc_ref[...]
    tops = []
    for g in range(N_GROUPS):
        rows = [sel[g * EXPERTS_PER_GROUP + k:g * EXPERTS_PER_GROUP + k + 1, :]
                for k in range(EXPERTS_PER_GROUP)]
        tops.append(_top2_of4(rows))
    best = tops[0][0] + tops[0][2]
    g_idx = jnp.zeros(best.shape, jnp.int32)
    loc1, loc2 = tops[0][1], tops[0][3]
    for g in range(1, N_GROUPS):
        gs = tops[g][0] + tops[g][2]
        gt = gs > best
        best = jnp.where(gt, gs, best)
        g_idx = jnp.where(gt, g, g_idx)
        loc1 = jnp.where(gt, tops[g][1], loc1)
        loc2 = jnp.where(gt, tops[g][3], loc2)
    e0 = g_idx * EXPERTS_PER_GROUP + loc1
    e1 = g_idx * EXPERTS_PER_GROUP + loc2
    erow = lax.broadcasted_iota(jnp.int32, (N_EXPERTS, ts), 0)
    is0 = erow == e0
    is1 = erow == e1
    s0 = jnp.sum(jnp.where(is0, scores, 0.0), axis=0, keepdims=True)
    s1 = jnp.sum(jnp.where(is1, scores, 0.0), axis=0, keepdims=True)
    ssum = s0 + s1
    yield

    onehot = jnp.logical_or(is0, is1).astype(BF16)
    src = lax.broadcasted_iota(jnp.int32, (ts, ts), 0)
    dst = lax.broadcasted_iota(jnp.int32, (ts, ts), 1)
    earlier = (src < dst).astype(BF16)
    prior = _dot(onehot, earlier) + base_ref[:, 0:1]
    r0 = jnp.sum(jnp.where(is0, prior, 0.0), axis=0, keepdims=True)
    r1 = jnp.sum(jnp.where(is1, prior, 0.0), axis=0, keepdims=True)
    counts = jnp.sum(onehot.astype(F32), axis=1, keepdims=True)
    base_ref[...] = base_ref[...] + jnp.where(step > 0, counts, 0.0)

    zi = jnp.zeros((SUBLANES - 4, ts), jnp.int32)
    ri_ref[...] = jnp.concatenate([e0, e1, r0.astype(jnp.int32), r1.astype(jnp.int32), zi], axis=0)
    zf = jnp.zeros((SUBLANES - 2, ts), F32)
    rf_ref[...] = jnp.concatenate([s0 / ssum, s1 / ssum, zf], axis=0)
    cnt_ref[...] = base_ref[...]


def _mixer(layer, xsrc, mod3, mb0, bsz, w_in, scw, cvw, cvb, cvg, cvbeta, sgg, sgbeta, sgw, sgbt,
           wbr, wg, bg, wo, ln1g, ln1b, wrt, brc):
    _, seq, d = xsrc[0].shape
    ts = MIX_ROWS
    ns = seq // ts
    n_tok = bsz * seq

    n_tiles = bsz * ns

    def const(shape):
        zeros = (0,) * len(shape)
        return pl.BlockSpec(shape, lambda s: zeros, pipeline_mode=pl.Buffered(1))

    def first_half(s):
        return jnp.minimum(s, n_tiles - 1)

    def second_half(s):
        return jnp.maximum(s - 1, 0)

    def layer_weights(shape):
        block = (1,) + tuple(shape[1:])
        index = (layer,) + (0,) * (len(shape) - 1)
        return pl.BlockSpec(block, lambda s: index, pipeline_mode=pl.Buffered(1))

    if len(xsrc) == 2:
        x, xb0 = xsrc
        x_args = [x]
        x_specs = [pl.BlockSpec((1, ts, d),
                                lambda s: (first_half(s) // ns + xb0, first_half(s) % ns, 0))]
    else:
        x1p, pairs, gates, mod3p, g2, b2 = xsrc
        x_args = [x1p, pairs, pairs, gates, mod3p, g2, b2]
        x_specs = [
            pl.BlockSpec((1, ts, d), lambda s: (first_half(s) // ns, first_half(s) % ns, 0)),
            pl.BlockSpec((1, ts, d // 2), lambda s: (0, first_half(s), 0)),
            pl.BlockSpec((1, ts, d // 2), lambda s: (1, first_half(s), 0)),
            pl.BlockSpec((ts, TOP_K), lambda s: (first_half(s), 0)),
            pl.BlockSpec((1, 1, 6 * d), lambda s: (first_half(s) // ns + mb0, 0, 0)),
            const(g2.shape), const(b2.shape),
        ]
    in_specs = x_specs + [
        pl.BlockSpec((1, 1, 6 * d), lambda s: (first_half(s) // ns + mb0, 0, 0)),
        pl.BlockSpec((1, 1, 6 * d), lambda s: (second_half(s) // ns + mb0, 0, 0)),
        layer_weights(w_in.shape), const(scw.shape), const(cvw.shape), const(cvb.shape),
        const(cvg.shape), const(cvbeta.shape), const(sgg.shape), const(sgbeta.shape),
        const(sgw.shape), const(sgbt.shape), layer_weights(wbr.shape), layer_weights(wg.shape),
        const(bg.shape), layer_weights(wo.shape), const(ln1g.shape), const(ln1b.shape),
        const(wrt.shape), const(brc.shape),
    ]
    out_specs = [
        pl.BlockSpec((1, ts, d), lambda s: (second_half(s) // ns, second_half(s) % ns, 0)),
        pl.BlockSpec((1, ts, d // 2), lambda s: (second_half(s) // ns, second_half(s) % ns, 0)),
        pl.BlockSpec((SUBLANES, ts), lambda s: (0, second_half(s))),
        pl.BlockSpec((SUBLANES, ts), lambda s: (0, second_half(s))),
        pl.BlockSpec((N_EXPERTS, LANES), lambda s: (0, 0)),
    ]
    out_shape = [
        jax.ShapeDtypeStruct((bsz, seq, d), F32),
        jax.ShapeDtypeStruct((bsz, seq, d // 2), jnp.uint32),
        jax.ShapeDtypeStruct((SUBLANES, n_tok), jnp.int32),
        jax.ShapeDtypeStruct((SUBLANES, n_tok), F32),
        jax.ShapeDtypeStruct((N_EXPERTS, LANES), F32),
    ]
    return pl.pallas_call(
        functools.partial(_mixer_kernel, ns, len(x_args)),
        grid=(n_tiles + 1,),
        in_specs=in_specs,
        out_specs=out_specs,
        out_shape=out_shape,
        scratch_shapes=[
            pltpu.VMEM((SC_HALO + ts, W_BRANCH), F32),
            pltpu.VMEM((CV_HALO + ts, W_BRANCH), F32),
            pltpu.VMEM((ts, W_BRANCH), F32),
            pltpu.VMEM((3, ts, W_BRANCH), BF16),
            pltpu.VMEM((ts, 3 * W_BRANCH), F32),
            pltpu.VMEM((ts, 2 * W_BRANCH), F32),
            pltpu.VMEM((ts, 3 * D_MODEL), F32),
            pltpu.VMEM((ts, d), BF16),
            pltpu.VMEM((ts, d), F32),
            pltpu.VMEM((N_EXPERTS, LANES), F32),
        ],
        compiler_params=pltpu.CompilerParams(
            dimension_semantics=("arbitrary",),
            vmem_limit_bytes=MIXER_VMEM_LIMIT),
    )(*x_args, mod3, mod3, w_in, scw, cvw, cvb, cvg, cvbeta, sgg, sgbeta, sgw, sgbt, wbr, wg, bg,
      wo, ln1g, ln1b, wrt, brc)


def _sc_workers():
    info = plsc.get_sparse_core_info()
    return info.num_cores, info.num_cores * info.num_subcores


def _sc_scatter_rows(rows, dest_a, dest_b, n_out):
    n, d = rows.shape
    nc, nw = _sc_workers()
    per_w = n // nw
    n_win = per_w // SC_WINDOW
    ia = dest_a.reshape(nw, n_win, SC_WINDOW)
    ib = dest_b.reshape(nw, n_win, SC_WINDOW)
    mesh = plsc.VectorSubcoreMesh(core_axis_name="c", subcore_axis_name="s")

    @functools.partial(
        pl.kernel, mesh=mesh,
        out_type=jax.ShapeDtypeStruct((n_out, d), rows.dtype),
        scratch_types=[
            pltpu.VMEM((n_win, SC_WINDOW), jnp.int32),
            pltpu.VMEM((n_win, SC_WINDOW), jnp.int32),
            pltpu.VMEM((SC_WINDOW, d), rows.dtype),
        ],
    )
    def scatter(rows_hbm, ia_hbm, ib_hbm, out_hbm, ia_v, ib_v, rows_v):
        wid = lax.axis_index("s") * nc + lax.axis_index("c")
        pltpu.sync_copy(ia_hbm.at[wid], ia_v)
        pltpu.sync_copy(ib_hbm.at[wid], ib_v)
        base = wid * per_w

        @pl.loop(0, n_win)
        def _(j):
            pltpu.sync_copy(rows_hbm.at[pl.ds(base + j * SC_WINDOW, SC_WINDOW)], rows_v)
            pltpu.sync_copy(rows_v, out_hbm.at[ia_v.at[j]])
            pltpu.sync_copy(rows_v, out_hbm.at[ib_v.at[j]])

    return scatter(rows, ia, ib)


def _sc_gather_rows(table, idx):
    n = idx.shape[0]
    d = table.shape[1]
    nc, nw = _sc_workers()
    per_w = n // nw
    n_win = per_w // SC_WINDOW
    idx3 = idx.reshape(nw, n_win, SC_WINDOW)
    mesh = plsc.VectorSubcoreMesh(core_axis_name="c", subcore_axis_name="s")

    @functools.partial(
        pl.kernel, mesh=mesh,
        out_type=jax.ShapeDtypeStruct((n, d), table.dtype),
        scratch_types=[
            pltpu.VMEM((n_win, SC_WINDOW), jnp.int32),
            pltpu.VMEM((SC_WINDOW, d), table.dtype),
        ],
    )
    def gather(table_hbm, idx_hbm, out_hbm, idx_v, rows_v):
        wid = lax.axis_index("s") * nc + lax.axis_index("c")
        pltpu.sync_copy(idx_hbm.at[wid], idx_v)
        base = wid * per_w

        @pl.loop(0, n_win)
        def _(j):
            pltpu.sync_copy(table_hbm.at[idx_v.at[j]], rows_v)
            pltpu.sync_copy(rows_v, out_hbm.at[pl.ds(base + j * SC_WINDOW, SC_WINDOW)])

    return gather(table, idx3)


def _expert_kernel(be_ref, nused_ref, x_ref, w1_ref, w3_ref, w2_ref, o_ref, w1s, w3s, w2s):
    i = pl.program_id(0)
    new_expert = jnp.logical_or(i == 0, be_ref[i] != be_ref[jnp.maximum(i - 1, 0)])

    @pl.when(new_expert)
    def _():
        w1s[...] = w1_ref[0, 0].astype(BF16)
        w3s[...] = w3_ref[0, 0].astype(BF16)
        w2s[...] = w2_ref[0, 0].astype(BF16)

    @pl.when(i < nused_ref[0])
    def _():
        lo, hi = _unpack_bf16_pairs(x_ref[...])
        xb = jnp.concatenate([lo, hi], axis=1).astype(BF16)
        a = _dot(xb, w1s[...])
        b = _dot(xb, w3s[...])
        for r0 in range(0, MOE_ROWS, MOE_SLAB):
            rs = slice(r0, r0 + MOE_SLAB)
            h = a[rs] * jax.nn.sigmoid(a[rs]) * b[rs]
            o_ref[rs, :] = _pack_bf16_pairs(_dot(h.astype(BF16), w2s[...]))


def _experts(layer, buf, block_e, n_used, w1, w3, w2):
    n_rows, dp = buf.shape
    nb = n_rows // MOE_ROWS
    d, fe = w1.shape[-2:]

    def row_map(i, be, nu):
        return (jnp.minimum(i, nu[0] - 1), 0)

    grid_spec = pltpu.PrefetchScalarGridSpec(
        num_scalar_prefetch=2,
        grid=(nb,),
        in_specs=[
            pl.BlockSpec((MOE_ROWS, dp), row_map),
            pl.BlockSpec((1, 1, d, fe), lambda i, be, nu: (layer, be[i], 0, 0)),
            pl.BlockSpec((1, 1, d, fe), lambda i, be, nu: (layer, be[i], 0, 0)),
            pl.BlockSpec((1, 1, fe, d), lambda i, be, nu: (layer, be[i], 0, 0)),
        ],
        out_specs=pl.BlockSpec((MOE_ROWS, dp), row_map),
        scratch_shapes=[pltpu.VMEM((d, fe), BF16), pltpu.VMEM((d, fe), BF16),
                        pltpu.VMEM((fe, d), BF16)],
    )
    return pl.pallas_call(
        _expert_kernel,
        grid_spec=grid_spec,
        out_shape=jax.ShapeDtypeStruct((n_rows, dp), jnp.uint32),
        compiler_params=pltpu.CompilerParams(dimension_semantics=("arbitrary",),
                                             vmem_limit_bytes=EXPERT_VMEM_LIMIT),
    )(block_e, n_used, buf, w1, w3, w2)


def _combine_kernel(x1_ref, ya_ref, yb_ref, gates_ref, mod_ref, g_ref, b_ref, *rest):
    o_ref = rest[-1]
    d = D_MODEL
    o_ref[0] = _moe_output(x1_ref[0], ya_ref[0], yb_ref[0], gates_ref[...],
                           mod_ref[0][:, 5 * d:6 * d], g_ref[...], b_ref[...])


def _combine(x1, pairs, gates, mod3, mb0, g, b, out_bsz, ob0, prev):
    bsz, seq, d = x1.shape
    ts = COMB_ROWS
    ns = seq // ts
    in_specs = [
        pl.BlockSpec((1, ts, d), lambda i, j: (i, j, 0)),
        pl.BlockSpec((1, ts, d // 2), lambda i, j: (0, i * ns + j, 0)),
        pl.BlockSpec((1, ts, d // 2), lambda i, j: (1, i * ns + j, 0)),
        pl.BlockSpec((ts, TOP_K), lambda i, j: (i * ns + j, 0)),
        pl.BlockSpec((1, 1, 6 * d), lambda i, j: (i + mb0, 0, 0)),
        pl.BlockSpec((1, d), lambda i, j: (0, 0)),
        pl.BlockSpec((1, d), lambda i, j: (0, 0)),
    ]
    args = [x1, pairs, pairs, gates, mod3, g, b]
    aliases = {}
    if prev is not None:
        in_specs.append(pl.BlockSpec(memory_space=pl.ANY))
        aliases = {len(args): 0}
        args.append(prev)
    return pl.pallas_call(
        _combine_kernel,
        grid=(bsz, ns),
        in_specs=in_specs,
        out_specs=pl.BlockSpec((1, ts, d), lambda i, j: (i + ob0, j, 0)),
        out_shape=jax.ShapeDtypeStruct((out_bsz, seq, d), F32),
        input_output_aliases=aliases,
        compiler_params=pltpu.CompilerParams(dimension_semantics=("arbitrary", "arbitrary")),
    )(*args)


def _plan_kernel(ri_ref, cnt_ref, dest_ref, blk_ref):
    n_tok = ri_ref.shape[1]
    counts = cnt_ref[:, 0:1].astype(jnp.int32)
    shift = MOE_ROWS.bit_length() - 1
    padded = lax.shift_left(lax.shift_right_logical(counts + (MOE_ROWS - 1), shift), shift)
    e_out = lax.broadcasted_iota(jnp.int32, (N_EXPERTS, N_EXPERTS), 0)
    e_in = lax.broadcasted_iota(jnp.int32, (N_EXPERTS, N_EXPERTS), 1)
    upto = (e_in <= e_out).astype(BF16)
    padded_f = jnp.broadcast_to(padded.astype(F32), (N_EXPERTS, LANES))
    pad_end = _dot(upto, padded_f.astype(BF16))
    pad_start = pad_end[:, 0:1] - padded.astype(F32)
    erow = lax.broadcasted_iota(jnp.int32, (N_EXPERTS, n_tok), 0)
    rows = []
    for k in range(TOP_K):
        start = jnp.sum(jnp.where(erow == ri_ref[k:k + 1, :], pad_start, 0.0), axis=0, keepdims=True)
        rows.append(start.astype(jnp.int32) + ri_ref[TOP_K + k:TOP_K + k + 1, :])
    rows.append(jnp.zeros((SUBLANES - TOP_K, n_tok), jnp.int32))
    dest_ref[...] = jnp.concatenate(rows, axis=0)
    block_start = (lax.broadcasted_iota(jnp.int32, (N_EXPERTS, LANES), 1) * MOE_ROWS).astype(F32)
    block_e = jnp.sum((block_start >= pad_end).astype(jnp.int32), axis=0, keepdims=True)
    n_used = lax.shift_right_logical(pad_end[N_EXPERTS - 1:N_EXPERTS, :].astype(jnp.int32), shift)
    blk_ref[...] = jnp.concatenate(
        [jnp.minimum(block_e, N_EXPERTS - 1), n_used,
         jnp.zeros((SUBLANES - 2, LANES), jnp.int32)], axis=0)


def _dispatch_plan(ri, cnt):
    n_tok = ri.shape[1]
    n_blocks = (n_tok * TOP_K + N_EXPERTS * (MOE_ROWS - 1) + MOE_ROWS - 1) // MOE_ROWS
    assert n_blocks <= LANES and MOE_ROWS & (MOE_ROWS - 1) == 0
    dest, blk = pl.pallas_call(
        _plan_kernel,
        out_shape=[jax.ShapeDtypeStruct((SUBLANES, n_tok), jnp.int32),
                   jax.ShapeDtypeStruct((SUBLANES, LANES), jnp.int32)],
    )(ri, cnt)
    return dest, blk[0, 0:n_blocks], blk[1, 0:1], n_blocks * MOE_ROWS


def kernel(x, c, w_ada, b_ada, w_in, sc_conv, cv_conv, cv_conv_b, cv_ln_g, cv_ln_b, sg_ln_g, sg_ln_b, sg_w, sg_b, w_branch, w_gate, b_gate, w_o, ln1_g, ln1_b, w_router, b_router, w1, w3, w2, ln2_g, ln2_b):
    bsz, seq, d = x.shape
    mod3 = _ada(c, w_ada, b_ada).reshape(DEPTH, bsz, 1, 6 * d)
    wrt = w_router.T
    brc = b_router.reshape(N_EXPERTS, 1)
    w_in_p = _pack_weights(w_in)
    w_gate_p = _pack_weights(jnp.transpose(w_gate, (0, 2, 1, 3)).reshape(DEPTH * 3, d, d)).reshape(
        DEPTH, 3, d // 2, d)
    w_o_p = _pack_weights(w_o)
    w_branch_p = _pack_weights(w_branch.reshape(DEPTH * 3, W_BRANCH, d)).reshape(
        DEPTH, 3, W_BRANCH // 2, d)
    cb = bsz // N_CHAINS
    n_tok = cb * seq
    chains = [(x, h * cb) for h in range(N_CHAINS)]
    for l in range(DEPTH):
        last = l == DEPTH - 1
        result = None
        for h in range(N_CHAINS):
            x1, u2, ri, rf, cnt = _mixer(
                l, chains[h], mod3[l], h * cb, cb, w_in_p, sc_conv[l], cv_conv[l],
                cv_conv_b[l].reshape(1, -1), cv_ln_g[l].reshape(1, -1), cv_ln_b[l].reshape(1, -1),
                sg_ln_g[l].reshape(1, -1), sg_ln_b[l].reshape(1, -1), sg_w[l], sg_b[l].T,
                w_branch_p, w_gate_p, b_gate[l],
                w_o_p, ln1_g[l].reshape(1, -1), ln1_b[l].reshape(1, -1), wrt, brc)
            dest, block_e, n_used, n_rows = _dispatch_plan(ri, cnt)
            buf = _sc_scatter_rows(u2.reshape(n_tok, d // 2), dest[0], dest[1], n_rows)
            obuf = _experts(l, buf, block_e, n_used, w1, w3, w2)
            pair_idx = dest[0:TOP_K].reshape(TOP_K * n_tok)
            pairs = _sc_gather_rows(obuf, pair_idx).reshape(TOP_K, n_tok, d // 2)
            gates = rf[0:TOP_K].T
            g2, b2 = ln2_g[l].reshape(1, -1), ln2_b[l].reshape(1, -1)
            if last:
                result = _combine(x1, pairs, gates, mod3[l], h * cb, g2, b2, bsz, h * cb, result)
            else:
                chains[h] = (x1, pairs, gates, mod3[l], g2, b2)
    return result
```

```python
import functools

import jax
import jax.numpy as jnp
from jax import lax
from jax.experimental import pallas as pl
from jax.experimental.pallas import tpu as pltpu
from jax.experimental.pallas import tpu_sc as plsc

D_MODEL = 1024
DEPTH = 2
W_BRANCH = 1024
SC_KERNEL = 3
CV_KERNEL = 31
CHUNK = 128
SG_HEADS = 8
N_EXPERTS = 16
N_GROUPS = 4
EXPERTS_PER_GROUP = N_EXPERTS // N_GROUPS
TOP_K = 2
D_EXPERT = 512
ALPHA = (2.0 * DEPTH) ** 0.25
LN_EPS = 1e-5

F32 = jnp.float32
BF16 = jnp.bfloat16

V7X_VMEM_BYTES = 64 * 1024 * 1024
MIXER_VMEM_LIMIT = V7X_VMEM_BYTES - 6 * 1024 * 1024
EXPERT_VMEM_LIMIT = V7X_VMEM_BYTES // 2
SUBLANES = 8
LANES = 128

MIX_ROWS = 256
SC_HALO = SUBLANES
CV_HALO = 32
CONV_ROWS = 128
TIE_LAG = 2
TIE_FREE_JOBS = 4
TAIL_STAGE_AFTER_JOBS = (7, 12)
CONV_COLS = 128
PROJ_COLS = 512
PACK_ROWS = 256
MOE_ROWS = 512
MOE_SLAB = 256
COMB_ROWS = 512
SC_WINDOW = 32
N_CHAINS = 2


def _dot(a, b):
    return jnp.dot(a, b, preferred_element_type=F32)


def _pack_bf16_pairs(v):
    m = v.shape[1] // 2
    lo = lax.bitcast_convert_type(v[:, 0:m].astype(BF16).astype(F32), jnp.uint32)
    hi = lax.bitcast_convert_type(v[:, m:2 * m].astype(BF16).astype(F32), jnp.uint32)
    return jnp.bitwise_or(jnp.bitwise_and(hi, jnp.uint32(0xFFFF0000)),
                          lax.shift_right_logical(lo, jnp.uint32(16)))


def _unpack_bf16_pairs(w):
    lo = lax.bitcast_convert_type(lax.shift_left(w, jnp.uint32(16)), F32)
    hi = lax.bitcast_convert_type(jnp.bitwise_and(w, jnp.uint32(0xFFFF0000)), F32)
    return lo, hi


def _layer_norm(v, g, b):
    mu = jnp.mean(v, axis=-1, keepdims=True)
    vc = v - mu
    var = jnp.mean(vc * vc, axis=-1, keepdims=True)
    return vc * lax.rsqrt(var + LN_EPS) * g + b


def _ada_kernel(c_ref, w_ref, b_ref, o_ref):
    c = c_ref[...]
    c_act = c * jax.nn.sigmoid(c)
    o_ref[0] = jnp.dot(c_act, w_ref[0], preferred_element_type=F32,
                       precision=lax.Precision.HIGHEST) + b_ref[0]


def _ada(c, w_ada, b_ada):
    bsz, d = c.shape
    n = w_ada.shape[-1]
    tn = 1536
    return pl.pallas_call(
        _ada_kernel,
        grid=(DEPTH, n // tn),
        in_specs=[
            pl.BlockSpec((bsz, d), lambda l, j: (0, 0)),
            pl.BlockSpec((1, d, tn), lambda l, j: (l, 0, j)),
            pl.BlockSpec((1, 1, tn), lambda l, j: (l, 0, j)),
        ],
        out_specs=pl.BlockSpec((1, bsz, tn), lambda l, j: (l, 0, j)),
        out_shape=jax.ShapeDtypeStruct((DEPTH, bsz, n), F32),
    )(c, w_ada, b_ada.reshape(DEPTH, 1, n))


def _top2_of4(rows):
    m1 = rows[0]
    i1 = jnp.zeros(rows[0].shape, jnp.int32)
    for k in range(1, 4):
        gt = rows[k] > m1
        m1 = jnp.where(gt, rows[k], m1)
        i1 = jnp.where(gt, k, i1)
    m2 = jnp.full(rows[0].shape, -jnp.inf, F32)
    i2 = jnp.zeros(rows[0].shape, jnp.int32)
    for k in range(4):
        cand = jnp.where(i1 == k, -jnp.inf, rows[k])
        gt = cand > m2
        m2 = jnp.where(gt, cand, m2)
        i2 = jnp.where(gt, k, i2)
    return m1, i1, m2, i2


def _zero_after(v):
    u = lax.bitcast_convert_type(v, jnp.uint32)
    u = lax.shift_right_logical(lax.shift_right_logical(u, jnp.uint32(16)), jnp.uint32(16))
    return lax.bitcast_convert_type(u, F32)


def _conv31_chunk(cvw_ref, cvbuf, cvout, r0, c0, tie):
    cs = slice(c0, c0 + CONV_COLS)
    acc = None
    for r in range(SUBLANES):
        lead = SUBLANES if r else 0
        part = None
        for m in range((CV_KERNEL - 1 - r) // SUBLANES + 1):
            k = CV_KERNEL - 1 - (SUBLANES * m + r)
            start = CV_HALO + r0 - lead - SUBLANES * m
            w_row = cvw_ref[k:k + 1, cs]
            if tie is not None and acc is None and part is None:
                w_row = w_row + tie
            term = w_row * cvbuf[start:start + lead + CONV_ROWS, cs]
            part = term if part is None else part + term
        part = part[lead - r:lead - r + CONV_ROWS]
        acc = part if acc is None else acc + part
    cvout[r0:r0 + CONV_ROWS, cs] = acc


def _bf16_weights(packed):
    return pltpu.bitcast(packed, BF16)


def _pack_weights_kernel(w_ref, o_ref):
    o_ref[0] = pltpu.bitcast(w_ref[0].astype(BF16), jnp.uint32)


def _pack_weights(w):
    g, k, n = w.shape
    kb = PACK_ROWS
    return pl.pallas_call(
        _pack_weights_kernel,
        grid=(g, k // kb),
        in_specs=[pl.BlockSpec((1, kb, n), lambda i, j: (i, j, 0))],
        out_specs=pl.BlockSpec((1, kb // 2, n), lambda i, j: (i, j, 0)),
        out_shape=jax.ShapeDtypeStruct((g, k // 2, n), jnp.uint32),
        compiler_params=pltpu.CompilerParams(vmem_limit_bytes=EXPERT_VMEM_LIMIT),
    )(w)


def _moe_output(x1, ya_packed, yb_packed, gates, gate2, g, b):
    ya = jnp.concatenate(_unpack_bf16_pairs(ya_packed), axis=1)
    yb = jnp.concatenate(_unpack_bf16_pairs(yb_packed), axis=1)
    h = gates[:, 0:1] * ya + gates[:, 1:2] * yb
    return _layer_norm(ALPHA * x1 + gate2 * h, g, b)


def _mixer_kernel(tiles_per_seq, n_x_refs, *refs):
    last_step = pl.num_programs(0) - 1

    @pl.when(pl.program_id(0) < last_step)
    def _():
        _mixer_step(tiles_per_seq, n_x_refs, True, *refs)

    @pl.when(pl.program_id(0) == last_step)
    def _():
        _mixer_step(tiles_per_seq, n_x_refs, False, *refs)


def _mixer_step(tiles_per_seq, n_x_refs, run_first_half, *refs):
    x_refs = refs[:n_x_refs]
    (mod_ref, modt_ref, w_in_ref, scw_ref, cvw_ref, cvb_ref, cvg_ref, cvbeta_ref, sgg_ref,
     sgbeta_ref, sgw_ref, sgbt_ref, wbr_ref, wg_ref, bg_ref, wo_ref, ln1g_ref, ln1b_ref, wrt_ref,
     brc_ref,
     x1_ref, u2_ref, ri_ref, rf_ref, cnt_ref,
     qbuf, cvbuf, cvout, ybuf, pa_buf, pc_buf, gl_buf, mg_buf, xs_buf, base_ref) = refs[n_x_refs:]
    ts = MIX_ROWS
    d = D_MODEL
    wb = W_BRANCH
    step = pl.program_id(0)
    tile = jnp.minimum(step, pl.num_programs(0) - 2)
    first_tile = tile % tiles_per_seq == 0

    @pl.when(step == 0)
    def _():
        base_ref[...] = jnp.zeros_like(base_ref)
        mg_buf[...] = jnp.zeros_like(mg_buf)
        xs_buf[...] = jnp.zeros_like(xs_buf)

    def w_in(c0):
        return _bf16_weights(w_in_ref[0, :, c0:c0 + PROJ_COLS])

    def w_gate(c0):
        n, col = divmod(c0, d)
        return _bf16_weights(wg_ref[0, n, :, col:col + PROJ_COLS])

    if not run_first_half:
        for _ in _mixer_tail(step, mg_buf, xs_buf, modt_ref, wo_ref, ln1g_ref, ln1b_ref, wrt_ref,
                             brc_ref, x1_ref, u2_ref, ri_ref, rf_ref, cnt_ref, base_ref):
            pass
        return

    @pl.when(first_tile)
    def _():
        qbuf[0:SC_HALO, :] = jnp.zeros((SC_HALO, wb), F32)
        cvbuf[0:CV_HALO, :] = jnp.zeros((CV_HALO, wb), F32)

    if n_x_refs == 1:
        x = x_refs[0][0]
    else:
        x1p_ref, ya_ref, yb_ref, gates_ref, modp_ref, g2_ref, b2_ref = x_refs
        x = _moe_output(x1p_ref[0], ya_ref[0], yb_ref[0], gates_ref[...],
                        modp_ref[0][:, 5 * d:6 * d], g2_ref[...], b2_ref[...])
    mod = mod_ref[0]
    shift1, scale1 = mod[:, 0:d], mod[:, d:2 * d]
    ub = (x * (1.0 + scale1) + shift1).astype(BF16)

    for c0 in range(0, wb, PROJ_COLS):
        a = _dot(ub, w_in(3 * wb + c0))
        g = _dot(ub, w_in(4 * wb + c0))
        cvbuf[CV_HALO:CV_HALO + ts, c0:c0 + PROJ_COLS] = a * jax.nn.sigmoid(g)

    tail = _mixer_tail(step, mg_buf, xs_buf, modt_ref, wo_ref, ln1g_ref, ln1b_ref, wrt_ref,
                       brc_ref, x1_ref, u2_ref, ri_ref, rf_ref, cnt_ref, base_ref)
    next(tail)
    ties = {}

    def add_tie(chunk, tie):
        ties[chunk] = tie + ties[chunk] if chunk in ties else tie

    mxu_jobs = ([(pa_buf, w_in, c0, c0) for c0 in range(0, 3 * wb, PROJ_COLS)]
                + [(pc_buf, w_in, c0, 5 * wb + c0) for c0 in range(0, 2 * wb, PROJ_COLS)]
                + [(gl_buf, w_gate, c0, c0) for c0 in range(0, 3 * d, PROJ_COLS)])
    conv_jobs = [(r0, c0) for c0 in range(0, wb, CONV_COLS) for r0 in range(0, ts, CONV_ROWS)]
    chunks_per_job = len(conv_jobs) // len(mxu_jobs)
    for c, conv_job in enumerate(conv_jobs):
        _conv31_chunk(cvw_ref, cvbuf, cvout, *conv_job, ties.get(c))
        if (c + 1) % chunks_per_job:
            continue
        i = c // chunks_per_job
        dst, weights, dc, wc = mxu_jobs[i]
        res = _dot(ub, weights(wc))
        dst[:, dc:dc + PROJ_COLS] = res
        if i < len(mxu_jobs) - TIE_FREE_JOBS:
            add_tie(c + TIE_LAG, _zero_after(res[ts - 1:ts, PROJ_COLS - CONV_COLS:PROJ_COLS]))
        if i in TAIL_STAGE_AFTER_JOBS:
            next(tail, None)
    cvbuf[0:CV_HALO, :] = cvbuf[ts:ts + CV_HALO, :]

    qbuf[SC_HALO:SC_HALO + ts, :] = pa_buf[:, wb:2 * wb] * pa_buf[:, 2 * wb:3 * wb]
    conv = scw_ref[SC_KERNEL - 1:SC_KERNEL, :] * qbuf[SC_HALO:SC_HALO + ts, :]
    for k in range(SC_KERNEL - 1):
        off = SC_HALO - (SC_KERNEL - 1) + k
        conv = conv + scw_ref[k:k + 1, :] * qbuf[off:off + ts, :]
    ybuf[0] = (pa_buf[:, 0:wb] * conv).astype(BF16)
    qbuf[0:SC_HALO, :] = qbuf[ts:ts + SC_HALO, :]
    z0 = _dot(ybuf[0], _bf16_weights(wbr_ref[0, 0]))

    cv = _layer_norm(cvout[...] + cvb_ref[...], cvg_ref[...], cvbeta_ref[...])
    ybuf[1] = (cv * jax.nn.sigmoid(cv)).astype(BF16)
    z1 = _dot(ybuf[1], _bf16_weights(wbr_ref[0, 1]))

    gu = jax.nn.gelu(pc_buf[:, 0:wb])
    gv = _layer_norm(jax.nn.gelu(pc_buf[:, wb:2 * wb]), sgg_ref[...], sgbeta_ref[...]).astype(BF16)
    merged = (jax.nn.sigmoid(gl_buf[:, 0:d] + bg_ref[0:1, :]) * z0
              + jax.nn.sigmoid(gl_buf[:, d:2 * d] + bg_ref[1:2, :]) * z1)
    g2 = jax.nn.sigmoid(gl_buf[:, 2 * d:3 * d] + bg_ref[2:3, :])
    row = lax.broadcasted_iota(jnp.int32, (CHUNK, CHUNK), 0)
    col = lax.broadcasted_iota(jnp.int32, (CHUNK, CHUNK), 1)
    hd = wb // SG_HEADS
    for h in range(SG_HEADS):
        wm = jnp.where(row >= col, sgw_ref[h], 0.0).astype(BF16)
        bias = sgbt_ref[:, h:h + 1]
        for n in range(ts // CHUNK):
            rs = slice(n * CHUNK, (n + 1) * CHUNK)
            cs = slice(h * hd, (h + 1) * hd)
            mixed = _dot(wm, gv[rs, cs]) + bias
            ybuf[2, rs, cs] = (gu[rs, cs] * mixed).astype(BF16)
    merged = merged + g2 * _dot(ybuf[2], _bf16_weights(wbr_ref[0, 2]))
    mg_buf[...] = merged.astype(BF16)
    xs_buf[...] = x


def _mixer_tail(step, mg_buf, xs_buf, modt_ref, wo_ref, ln1g_ref, ln1b_ref, wrt_ref, brc_ref,
                x1_ref, u2_ref, ri_ref, rf_ref, cnt_ref, base_ref):
    ts = MIX_ROWS
    d = D_MODEL
    modt = modt_ref[0]
    gate1, shift2, scale2 = modt[:, 2 * d:3 * d], modt[:, 3 * d:4 * d], modt[:, 4 * d:5 * d]
    hmix = _dot(mg_buf[...], _bf16_weights(wo_ref[0]))
    x1 = _layer_norm(ALPHA * xs_buf[...] + gate1 * hmix, ln1g_ref[...], ln1b_ref[...])
    x1_ref[0] = x1
    u2 = x1 * (1.0 + scale2) + shift2
    u2_ref[0] = _pack_bf16_pairs(u2)
    yield

    logits = lax.dot_general(wrt_ref[...], u2, (((1,), (1,)), ((), ())),
                             preferred_element_type=F32,
                             precision=lax.Precision.HIGHEST)
    mx = jnp.max(logits, axis=0, keepdims=True)
    ex = jnp.exp(logits - mx)
    scores = ex / jnp.sum(ex, axis=0, keepdims=True)
    sel = scores + brc_ref[...]
    tops = []
    for g in range(N_GROUPS):
        rows = [sel[g * EXPERTS_PER_GROUP + k:g * EXPERTS_PER_GROUP + k + 1, :]
                for k in range(EXPERTS_PER_GROUP)]
        tops.append(_top2_of4(rows))
    best = tops[0][0] + tops[0][2]
    g_idx = jnp.zeros(best.shape, jnp.int32)
    loc1, loc2 = tops[0][1], tops[0][3]
    for g in range(1, N_GROUPS):
        gs = tops[g][0] + tops[g][2]
        gt = gs > best
        best = jnp.where(gt, gs, best)
        g_idx = jnp.where(gt, g, g_idx)
        loc1 = jnp.where(gt, tops[g][1], loc1)
        loc2 = jnp.where(gt, tops[g][3], loc2)
    e0 = g_idx * EXPERTS_PER_GROUP + loc1
    e1 = g_idx * EXPERTS_PER_GROUP + loc2
    erow = lax.broadcasted_iota(jnp.int32, (N_EXPERTS, ts), 0)
    is0 = erow == e0
    is1 = erow == e1
    s0 = jnp.sum(jnp.where(is0, scores, 0.0), axis=0, keepdims=True)
    s1 = jnp.sum(jnp.where(is1, scores, 0.0), axis=0, keepdims=True)
    ssum = s0 + s1
    yield

    onehot = jnp.logical_or(is0, is1).astype(BF16)
    src = lax.broadcasted_iota(jnp.int32, (ts, ts), 0)
    dst = lax.broadcasted_iota(jnp.int32, (ts, ts), 1)
    earlier = (src < dst).astype(BF16)
    prior = _dot(onehot, earlier) + base_ref[:, 0:1]
    r0 = jnp.sum(jnp.where(is0, prior, 0.0), axis=0, keepdims=True)
    r1 = jnp.sum(jnp.where(is1, prior, 0.0), axis=0, keepdims=True)
    counts = jnp.sum(onehot.astype(F32), axis=1, keepdims=True)
    base_ref[...] = base_ref[...] + jnp.where(step > 0, counts, 0.0)

    zi = jnp.zeros((SUBLANES - 4, ts), jnp.int32)
    ri_ref[...] = jnp.concatenate([e0, e1, r0.astype(jnp.int32), r1.astype(jnp.int32), zi], axis=0)
    zf = jnp.zeros((SUBLANES - 2, ts), F32)
    rf_ref[...] = jnp.concatenate([s0 / ssum, s1 / ssum, zf], axis=0)
    cnt_ref[...] = base_ref[...]


def _mixer(layer, xsrc, mod3, mb0, bsz, w_in, scw, cvw, cvb, cvg, cvbeta, sgg, sgbeta, sgw, sgbt,
           wbr, wg, bg, wo, ln1g, ln1b, wrt, brc):
    _, seq, d = xsrc[0].shape
    ts = MIX_ROWS
    ns = seq // ts
    n_tok = bsz * seq

    n_tiles = bsz * ns

    def const(shape):
        zeros = (0,) * len(shape)
        return pl.BlockSpec(shape, lambda s: zeros, pipeline_mode=pl.Buffered(1))

    def first_half(s):
        return jnp.minimum(s, n_tiles - 1)

    def second_half(s):
        return jnp.maximum(s - 1, 0)

    def layer_weights(shape):
        block = (1,) + tuple(shape[1:])
        index = (layer,) + (0,) * (len(shape) - 1)
        return pl.BlockSpec(block, lambda s: index, pipeline_mode=pl.Buffered(1))

    if len(xsrc) == 2:
        x, xb0 = xsrc
        x_args = [x]
        x_specs = [pl.BlockSpec((1, ts, d),
                                lambda s: (first_half(s) // ns + xb0, first_half(s) % ns, 0))]
    else:
        x1p, pairs, gates, mod3p, g2, b2 = xsrc
        x_args = [x1p, pairs, pairs, gates, mod3p, g2, b2]
        x_specs = [
            pl.BlockSpec((1, ts, d), lambda s: (first_half(s) // ns, first_half(s) % ns, 0)),
            pl.BlockSpec((1, ts, d // 2), lambda s: (0, first_half(s), 0)),
            pl.BlockSpec((1, ts, d // 2), lambda s: (1, first_half(s), 0)),
            pl.BlockSpec((ts, TOP_K), lambda s: (first_half(s), 0)),
            pl.BlockSpec((1, 1, 6 * d), lambda s: (first_half(s) // ns + mb0, 0, 0)),
            const(g2.shape), const(b2.shape),
        ]
    in_specs = x_specs + [
        pl.BlockSpec((1, 1, 6 * d), lambda s: (first_half(s) // ns + mb0, 0, 0)),
        pl.BlockSpec((1, 1, 6 * d), lambda s: (second_half(s) // ns + mb0, 0, 0)),
        layer_weights(w_in.shape), const(scw.shape), const(cvw.shape), const(cvb.shape),
        const(cvg.shape), const(cvbeta.shape), const(sgg.shape), const(sgbeta.shape),
        const(sgw.shape), const(sgbt.shape), layer_weights(wbr.shape), layer_weights(wg.shape),
        const(bg.shape), layer_weights(wo.shape), const(ln1g.shape), const(ln1b.shape),
        const(wrt.shape), const(brc.shape),
    ]
    out_specs = [
        pl.BlockSpec((1, ts, d), lambda s: (second_half(s) // ns, second_half(s) % ns, 0)),
        pl.BlockSpec((1, ts, d // 2), lambda s: (second_half(s) // ns, second_half(s) % ns, 0)),
        pl.BlockSpec((SUBLANES, ts), lambda s: (0, second_half(s))),
        pl.BlockSpec((SUBLANES, ts), lambda s: (0, second_half(s))),
        pl.BlockSpec((N_EXPERTS, LANES), lambda s: (0, 0)),
    ]
    out_shape = [
        jax.ShapeDtypeStruct((bsz, seq, d), F32),
        jax.ShapeDtypeStruct((bsz, seq, d // 2), jnp.uint32),
        jax.ShapeDtypeStruct((SUBLANES, n_tok), jnp.int32),
        jax.ShapeDtypeStruct((SUBLANES, n_tok), F32),
        jax.ShapeDtypeStruct((N_EXPERTS, LANES), F32),
    ]
    return pl.pallas_call(
        functools.partial(_mixer_kernel, ns, len(x_args)),
        grid=(n_tiles + 1,),
        in_specs=in_specs,
        out_specs=out_specs,
        out_shape=out_shape,
        scratch_shapes=[
            pltpu.VMEM((SC_HALO + ts, W_BRANCH), F32),
            pltpu.VMEM((CV_HALO + ts, W_BRANCH), F32),
            pltpu.VMEM((ts, W_BRANCH), F32),
            pltpu.VMEM((3, ts, W_BRANCH), BF16),
            pltpu.VMEM((ts, 3 * W_BRANCH), F32),
            pltpu.VMEM((ts, 2 * W_BRANCH), F32),
            pltpu.VMEM((ts, 3 * D_MODEL), F32),
            pltpu.VMEM((ts, d), BF16),
            pltpu.VMEM((ts, d), F32),
            pltpu.VMEM((N_EXPERTS, LANES), F32),
        ],
        compiler_params=pltpu.CompilerParams(
            dimension_semantics=("arbitrary",),
            vmem_limit_bytes=MIXER_VMEM_LIMIT),
    )(*x_args, mod3, mod3, w_in, scw, cvw, cvb, cvg, cvbeta, sgg, sgbeta, sgw, sgbt, wbr, wg, bg,
      wo, ln1g, ln1b, wrt, brc)


def _sc_workers():
    info = plsc.get_sparse_core_info()
    return info.num_cores, info.num_cores * info.num_subcores


def _sc_scatter_rows(rows, dest_a, dest_b, n_out):
    n, d = rows.shape
    nc, nw = _sc_workers()
    per_w = n // nw
    n_win = per_w // SC_WINDOW
    ia = dest_a.reshape(nw, n_win, SC_WINDOW)
    ib = dest_b.reshape(nw, n_win, SC_WINDOW)
    mesh = plsc.VectorSubcoreMesh(core_axis_name="c", subcore_axis_name="s")

    @functools.partial(
        pl.kernel, mesh=mesh,
        out_type=jax.ShapeDtypeStruct((n_out, d), rows.dtype),
        scratch_types=[
            pltpu.VMEM((n_win, SC_WINDOW), jnp.int32),
            pltpu.VMEM((n_win, SC_WINDOW), jnp.int32),
            pltpu.VMEM((SC_WINDOW, d), rows.dtype),
        ],
    )
    def scatter(rows_hbm, ia_hbm, ib_hbm, out_hbm, ia_v, ib_v, rows_v):
        wid = lax.axis_index("s") * nc + lax.axis_index("c")
        pltpu.sync_copy(ia_hbm.at[wid], ia_v)
        pltpu.sync_copy(ib_hbm.at[wid], ib_v)
        base = wid * per_w

        @pl.loop(0, n_win)
        def _(j):
            pltpu.sync_copy(rows_hbm.at[pl.ds(base + j * SC_WINDOW, SC_WINDOW)], rows_v)
            pltpu.sync_copy(rows_v, out_hbm.at[ia_v.at[j]])
            pltpu.sync_copy(rows_v, out_hbm.at[ib_v.at[j]])

    return scatter(rows, ia, ib)


def _sc_gather_rows(table, idx):
    n = idx.shape[0]
    d = table.shape[1]
    nc, nw = _sc_workers()
    per_w = n // nw
    n_win = per_w // SC_WINDOW
    idx3 = idx.reshape(nw, n_win, SC_WINDOW)
    mesh = plsc.VectorSubcoreMesh(core_axis_name="c", subcore_axis_name="s")

    @functools.partial(
        pl.kernel, mesh=mesh,
        out_type=jax.ShapeDtypeStruct((n, d), table.dtype),
        scratch_types=[
            pltpu.VMEM((n_win, SC_WINDOW), jnp.int32),
            pltpu.VMEM((SC_WINDOW, d), table.dtype),
        ],
    )
    def gather(table_hbm, idx_hbm, out_hbm, idx_v, rows_v):
        wid = lax.axis_index("s") * nc + lax.axis_index("c")
        pltpu.sync_copy(idx_hbm.at[wid], idx_v)
        base = wid * per_w

        @pl.loop(0, n_win)
        def _(j):
            pltpu.sync_copy(table_hbm.at[idx_v.at[j]], rows_v)
            pltpu.sync_copy(rows_v, out_hbm.at[pl.ds(base + j * SC_WINDOW, SC_WINDOW)])

    return gather(table, idx3)


def _expert_kernel(be_ref, nused_ref, x_ref, w1_ref, w3_ref, w2_ref, o_ref, w1s, w3s, w2s):
    i = pl.program_id(0)
    new_expert = jnp.logical_or(i == 0, be_ref[i] != be_ref[jnp.maximum(i - 1, 0)])

    @pl.when(new_expert)
    def _():
        w1s[...] = w1_ref[0, 0].astype(BF16)
        w3s[...] = w3_ref[0, 0].astype(BF16)
        w2s[...] = w2_ref[0, 0].astype(BF16)

    @pl.when(i < nused_ref[0])
    def _():
        lo, hi = _unpack_bf16_pairs(x_ref[...])
        xb = jnp.concatenate([lo, hi], axis=1).astype(BF16)
        a = _dot(xb, w1s[...])
        b = _dot(xb, w3s[...])
        for r0 in range(0, MOE_ROWS, MOE_SLAB):
            rs = slice(r0, r0 + MOE_SLAB)
            h = a[rs] * jax.nn.sigmoid(a[rs]) * b[rs]
            o_ref[rs, :] = _pack_bf16_pairs(_dot(h.astype(BF16), w2s[...]))


def _experts(layer, buf, block_e, n_used, w1, w3, w2):
    n_rows, dp = buf.shape
    nb = n_rows // MOE_ROWS
    d, fe = w1.shape[-2:]

    def row_map(i, be, nu):
        return (jnp.minimum(i, nu[0] - 1), 0)

    grid_spec = pltpu.PrefetchScalarGridSpec(
        num_scalar_prefetch=2,
        grid=(nb,),
        in_specs=[
            pl.BlockSpec((MOE_ROWS, dp), row_map),
            pl.BlockSpec((1, 1, d, fe), lambda i, be, nu: (layer, be[i], 0, 0)),
            pl.BlockSpec((1, 1, d, fe), lambda i, be, nu: (layer, be[i], 0, 0)),
            pl.BlockSpec((1, 1, fe, d), lambda i, be, nu: (layer, be[i], 0, 0)),
        ],
        out_specs=pl.BlockSpec((MOE_ROWS, dp), row_map),
        scratch_shapes=[pltpu.VMEM((d, fe), BF16), pltpu.VMEM((d, fe), BF16),
                        pltpu.VMEM((fe, d), BF16)],
    )
    return pl.pallas_call(
        _expert_kernel,
        grid_spec=grid_spec,
        out_shape=jax.ShapeDtypeStruct((n_rows, dp), jnp.uint32),
        compiler_params=pltpu.CompilerParams(dimension_semantics=("arbitrary",),
                                             vmem_limit_bytes=EXPERT_VMEM_LIMIT),
    )(block_e, n_used, buf, w1, w3, w2)


def _combine_kernel(x1_ref, ya_ref, yb_ref, gates_ref, mod_ref, g_ref, b_ref, *rest):
    o_ref = rest[-1]
    d = D_MODEL
    o_ref[0] = _moe_output(x1_ref[0], ya_ref[0], yb_ref[0], gates_ref[...],
                           mod_ref[0][:, 5 * d:6 * d], g_ref[...], b_ref[...])


def _combine(x1, pairs, gates, mod3, mb0, g, b, out_bsz, ob0, prev):
    bsz, seq, d = x1.shape
    ts = COMB_ROWS
    ns = seq // ts
    in_specs = [
        pl.BlockSpec((1, ts, d), lambda i, j: (i, j, 0)),
        pl.BlockSpec((1, ts, d // 2), lambda i, j: (0, i * ns + j, 0)),
        pl.BlockSpec((1, ts, d // 2), lambda i, j: (1, i * ns + j, 0)),
        pl.BlockSpec((ts, TOP_K), lambda i, j: (i * ns + j, 0)),
        pl.BlockSpec((1, 1, 6 * d), lambda i, j: (i + mb0, 0, 0)),
        pl.BlockSpec((1, d), lambda i, j: (0, 0)),
        pl.BlockSpec((1, d), lambda i, j: (0, 0)),
    ]
    args = [x1, pairs, pairs, gates, mod3, g, b]
    aliases = {}
    if prev is not None:
        in_specs.append(pl.BlockSpec(memory_space=pl.ANY))
        aliases = {len(args): 0}
        args.append(prev)
    return pl.pallas_call(
        _combine_kernel,
        grid=(bsz, ns),
        in_specs=in_specs,
        out_specs=pl.BlockSpec((1, ts, d), lambda i, j: (i + ob0, j, 0)),
        out_shape=jax.ShapeDtypeStruct((out_bsz, seq, d), F32),
        input_output_aliases=aliases,
        compiler_params=pltpu.CompilerParams(dimension_semantics=("arbitrary", "arbitrary")),
    )(*args)


def _plan_kernel(ri_ref, cnt_ref, dest_ref, blk_ref):
    n_tok = ri_ref.shape[1]
    counts = cnt_ref[:, 0:1].astype(jnp.int32)
    shift = MOE_ROWS.bit_length() - 1
    padded = lax.shift_left(lax.shift_right_logical(counts + (MOE_ROWS - 1), shift), shift)
    e_out = lax.broadcasted_iota(jnp.int32, (N_EXPERTS, N_EXPERTS), 0)
    e_in = lax.broadcasted_iota(jnp.int32, (N_EXPERTS, N_EXPERTS), 1)
    upto = (e_in <= e_out).astype(BF16)
    padded_f = jnp.broadcast_to(padded.astype(F32), (N_EXPERTS, LANES))
    pad_end = _dot(upto, padded_f.astype(BF16))
    pad_start = pad_end[:, 0:1] - padded.astype(F32)
    erow = lax.broadcasted_iota(jnp.int32, (N_EXPERTS, n_tok), 0)
    rows = []
    for k in range(TOP_K):
        start = jnp.sum(jnp.where(erow == ri_ref[k:k + 1, :], pad_start, 0.0), axis=0, keepdims=True)
        rows.append(start.astype(jnp.int32) + ri_ref[TOP_K + k:TOP_K + k + 1, :])
    rows.append(jnp.zeros((SUBLANES - TOP_K, n_tok), jnp.int32))
    dest_ref[...] = jnp.concatenate(rows, axis=0)
    block_start = (lax.broadcasted_iota(jnp.int32, (N_EXPERTS, LANES), 1) * MOE_ROWS).astype(F32)
    block_e = jnp.sum((block_start >= pad_end).astype(jnp.int32), axis=0, keepdims=True)
    n_used = lax.shift_right_logical(pad_end[N_EXPERTS - 1:N_EXPERTS, :].astype(jnp.int32), shift)
    blk_ref[...] = jnp.concatenate(
        [jnp.minimum(block_e, N_EXPERTS - 1), n_used,
         jnp.zeros((SUBLANES - 2, LANES), jnp.int32)], axis=0)


def _dispatch_plan(ri, cnt):
    n_tok = ri.shape[1]
    n_blocks = (n_tok * TOP_K + N_EXPERTS * (MOE_ROWS - 1) + MOE_ROWS - 1) // MOE_ROWS
    assert n_blocks <= LANES and MOE_ROWS & (MOE_ROWS - 1) == 0
    dest, blk = pl.pallas_call(
        _plan_kernel,
        out_shape=[jax.ShapeDtypeStruct((SUBLANES, n_tok), jnp.int32),
                   jax.ShapeDtypeStruct((SUBLANES, LANES), jnp.int32)],
    )(ri, cnt)
    return dest, blk[0, 0:n_blocks], blk[1, 0:1], n_blocks * MOE_ROWS


def kernel(x, c, w_ada, b_ada, w_in, sc_conv, cv_conv, cv_conv_b, cv_ln_g, cv_ln_b, sg_ln_g, sg_ln_b, sg_w, sg_b, w_branch, w_gate, b_gate, w_o, ln1_g, ln1_b, w_router, b_router, w1, w3, w2, ln2_g, ln2_b):
    bsz, seq, d = x.shape
    mod3 = _ada(c, w_ada, b_ada).reshape(DEPTH, bsz, 1, 6 * d)
    wrt = w_router.T
    brc = b_router.reshape(N_EXPERTS, 1)
    w_in_p = _pack_weights(w_in)
    w_gate_p = _pack_weights(jnp.transpose(w_gate, (0, 2, 1, 3)).reshape(DEPTH * 3, d, d)).reshape(
        DEPTH, 3, d // 2, d)
    w_o_p = _pack_weights(w_o)
    w_branch_p = _pack_weights(w_branch.reshape(DEPTH * 3, W_BRANCH, d)).reshape(
        DEPTH, 3, W_BRANCH // 2, d)
    cb = bsz // N_CHAINS
    n_tok = cb * seq
    chains = [(x, h * cb) for h in range(N_CHAINS)]
    for l in range(DEPTH):
        last = l == DEPTH - 1
        result = None
        for h in range(N_CHAINS):
            x1, u2, ri, rf, cnt = _mixer(
                l, chains[h], mod3[l], h * cb, cb, w_in_p, sc_conv[l], cv_conv[l],
                cv_conv_b[l].reshape(1, -1), cv_ln_g[l].reshape(1, -1), cv_ln_b[l].reshape(1, -1),
                sg_ln_g[l].reshape(1, -1), sg_ln_b[l].reshape(1, -1), sg_w[l], sg_b[l].T,
                w_branch_p, w_gate_p, b_gate[l],
                w_o_p, ln1_g[l].reshape(1, -1), ln1_b[l].reshape(1, -1), wrt, brc)
            dest, block_e, n_used, n_rows = _dispatch_plan(ri, cnt)
            buf = _sc_scatter_rows(u2.reshape(n_tok, d // 2), dest[0], dest[1], n_rows)
            obuf = _experts(l, buf, block_e, n_used, w1, w3, w2)
            pair_idx = dest[0:TOP_K].reshape(TOP_K * n_tok)
            pairs = _sc_gather_rows(obuf, pair_idx).reshape(TOP_K, n_tok, d // 2)
            gates = rf[0:TOP_K].T
            g2, b2 = ln2_g[l].reshape(1, -1), ln2_b[l].reshape(1, -1)
            if last:
                result = _combine(x1, pairs, gates, mod3[l], h * cb, g2, b2, bsz, h * cb, result)
            else:
                chains[h] = (x1, pairs, gates, mod3[l], g2, b2)
    return result
```

```python
import functools

import jax
import jax.numpy as jnp
from jax import lax
from jax.experimental import pallas as pl
from jax.experimental.pallas import tpu as pltpu
from jax.experimental.pallas import tpu_sc as plsc

D_MODEL = 1024
DEPTH = 2
W_BRANCH = 1024
SC_KERNEL = 3
CV_KERNEL = 31
CHUNK = 128
SG_HEADS = 8
N_EXPERTS = 16
N_GROUPS = 4
EXPERTS_PER_GROUP = N_EXPERTS // N_GROUPS
TOP_K = 2
D_EXPERT = 512
ALPHA = (2.0 * DEPTH) ** 0.25
LN_EPS = 1e-5

F32 = jnp.float32
BF16 = jnp.bfloat16

V7X_VMEM_BYTES = 64 * 1024 * 1024
MIXER_VMEM_LIMIT = V7X_VMEM_BYTES - 6 * 1024 * 1024
EXPERT_VMEM_LIMIT = V7X_VMEM_BYTES // 2
SUBLANES = 8
LANES = 128

MIX_ROWS = 256
SC_HALO = SUBLANES
CV_HALO = 32
CONV_ROWS = 128
TIE_LAG = 2
TIE_FREE_JOBS = 4
CONV_COLS = 128
PROJ_COLS = 512
PACK_ROWS = 256
MOE_ROWS = 512
MOE_SLAB = 256
COMB_ROWS = 512
SC_WINDOW = 32
N_CHAINS = 2


def _dot(a, b):
    return jnp.dot(a, b, preferred_element_type=F32)


def _pack_bf16_pairs(v):
    m = v.shape[1] // 2
    lo = lax.bitcast_convert_type(v[:, 0:m].astype(BF16).astype(F32), jnp.uint32)
    hi = lax.bitcast_convert_type(v[:, m:2 * m].astype(BF16).astype(F32), jnp.uint32)
    return jnp.bitwise_or(jnp.bitwise_and(hi, jnp.uint32(0xFFFF0000)),
                          lax.shift_right_logical(lo, jnp.uint32(16)))


def _unpack_bf16_pairs(w):
    lo = lax.bitcast_convert_type(lax.shift_left(w, jnp.uint32(16)), F32)
    hi = lax.bitcast_convert_type(jnp.bitwise_and(w, jnp.uint32(0xFFFF0000)), F32)
    return lo, hi


def _layer_norm(v, g, b):
    mu = jnp.mean(v, axis=-1, keepdims=True)
    vc = v - mu
    var = jnp.mean(vc * vc, axis=-1, keepdims=True)
    return vc * lax.rsqrt(var + LN_EPS) * g + b


def _ada_kernel(c_ref, w_ref, b_ref, o_ref):
    c = c_ref[...]
    c_act = c * jax.nn.sigmoid(c)
    o_ref[0] = jnp.dot(c_act, w_ref[0], preferred_element_type=F32,
                       precision=lax.Precision.HIGHEST) + b_ref[0]


def _ada(c, w_ada, b_ada):
    bsz, d = c.shape
    n = w_ada.shape[-1]
    tn = 1536
    return pl.pallas_call(
        _ada_kernel,
        grid=(DEPTH, n // tn),
        in_specs=[
            pl.BlockSpec((bsz, d), lambda l, j: (0, 0)),
            pl.BlockSpec((1, d, tn), lambda l, j: (l, 0, j)),
            pl.BlockSpec((1, 1, tn), lambda l, j: (l, 0, j)),
        ],
        out_specs=pl.BlockSpec((1, bsz, tn), lambda l, j: (l, 0, j)),
        out_shape=jax.ShapeDtypeStruct((DEPTH, bsz, n), F32),
    )(c, w_ada, b_ada.reshape(DEPTH, 1, n))


def _top2_of4(rows):
    m1 = rows[0]
    i1 = jnp.zeros(rows[0].shape, jnp.int32)
    for k in range(1, 4):
        gt = rows[k] > m1
        m1 = jnp.where(gt, rows[k], m1)
        i1 = jnp.where(gt, k, i1)
    m2 = jnp.full(rows[0].shape, -jnp.inf, F32)
    i2 = jnp.zeros(rows[0].shape, jnp.int32)
    for k in range(4):
        cand = jnp.where(i1 == k, -jnp.inf, rows[k])
        gt = cand > m2
        m2 = jnp.where(gt, cand, m2)
        i2 = jnp.where(gt, k, i2)
    return m1, i1, m2, i2


def _zero_after(v):
    u = lax.bitcast_convert_type(v, jnp.uint32)
    u = lax.shift_right_logical(lax.shift_right_logical(u, jnp.uint32(16)), jnp.uint32(16))
    return lax.bitcast_convert_type(u, F32)


def _conv31_chunk(cvw_ref, cvbufs, cvout, r0, c0, tie):
    cs = slice(c0, c0 + CONV_COLS)
    cvbuf = cvbufs[c0 // PROJ_COLS]
    bs = slice(c0 % PROJ_COLS, c0 % PROJ_COLS + CONV_COLS)
    acc = None
    for r in range(SUBLANES):
        lead = SUBLANES if r else 0
        part = None
        for m in range((CV_KERNEL - 1 - r) // SUBLANES + 1):
            k = CV_KERNEL - 1 - (SUBLANES * m + r)
            start = CV_HALO + r0 - lead - SUBLANES * m
            w_row = cvw_ref[k:k + 1, cs]
            if tie is not None and acc is None and part is None:
                w_row = w_row + tie
            term = w_row * cvbuf[start:start + lead + CONV_ROWS, bs]
            part = term if part is None else part + term
        part = part[lead - r:lead - r + CONV_ROWS]
        acc = part if acc is None else acc + part
    cvout[r0:r0 + CONV_ROWS, cs] = acc


def _bf16_weights(packed):
    return pltpu.bitcast(packed, BF16)


def _pack_weights_kernel(w_ref, o_ref):
    o_ref[0] = pltpu.bitcast(w_ref[0].astype(BF16), jnp.uint32)


def _pack_weights(w):
    g, k, n = w.shape
    kb = PACK_ROWS
    return pl.pallas_call(
        _pack_weights_kernel,
        grid=(g, k // kb),
        in_specs=[pl.BlockSpec((1, kb, n), lambda i, j: (i, j, 0))],
        out_specs=pl.BlockSpec((1, kb // 2, n), lambda i, j: (i, j, 0)),
        out_shape=jax.ShapeDtypeStruct((g, k // 2, n), jnp.uint32),
        compiler_params=pltpu.CompilerParams(vmem_limit_bytes=EXPERT_VMEM_LIMIT),
    )(w)


def _moe_output(x1, ya_packed, yb_packed, gates, gate2, g, b):
    ya = jnp.concatenate(_unpack_bf16_pairs(ya_packed), axis=1)
    yb = jnp.concatenate(_unpack_bf16_pairs(yb_packed), axis=1)
    h = gates[:, 0:1] * ya + gates[:, 1:2] * yb
    return _layer_norm(ALPHA * x1 + gate2 * h, g, b)


def _mixer_kernel(tiles_per_seq, n_x_refs, *refs):
    last_step = pl.num_programs(0) - 1

    @pl.when(pl.program_id(0) < last_step)
    def _():
        _mixer_step(tiles_per_seq, n_x_refs, True, *refs)

    @pl.when(pl.program_id(0) == last_step)
    def _():
        _mixer_step(tiles_per_seq, n_x_refs, False, *refs)


def _mixer_step(tiles_per_seq, n_x_refs, run_first_half, *refs):
    x_refs = refs[:n_x_refs]
    (mod_ref, modt_ref, w_in_ref, scw_ref, cvw_ref, cvb_ref, cvg_ref, cvbeta_ref, sgg_ref,
     sgbeta_ref, sgw_ref, sgbt_ref, wbr_ref, wg_ref, bg_ref, wo_ref, ln1g_ref, ln1b_ref, wrt_ref,
     brc_ref,
     x1_ref, u2_ref, ri_ref, rf_ref, cnt_ref,
     qbuf, cvbuf0, cvbuf1, cvout, ybuf, pa_buf, pc_buf, gl_buf, mg_buf, xs_buf,
     base_ref) = refs[n_x_refs:]
    cvbufs = (cvbuf0, cvbuf1)
    ts = MIX_ROWS
    d = D_MODEL
    wb = W_BRANCH
    step = pl.program_id(0)
    tile = jnp.minimum(step, pl.num_programs(0) - 2)
    first_tile = tile % tiles_per_seq == 0

    @pl.when(step == 0)
    def _():
        base_ref[...] = jnp.zeros_like(base_ref)
        mg_buf[...] = jnp.zeros_like(mg_buf)
        xs_buf[...] = jnp.zeros_like(xs_buf)

    def w_in(c0):
        return _bf16_weights(w_in_ref[0, :, c0:c0 + PROJ_COLS])

    def w_gate(c0):
        n, col = divmod(c0, d)
        return _bf16_weights(wg_ref[0, n, :, col:col + PROJ_COLS])

    if not run_first_half:
        for _ in _mixer_tail(step, mg_buf, xs_buf, modt_ref, wo_ref, ln1g_ref, ln1b_ref, wrt_ref,
                             brc_ref, x1_ref, u2_ref, ri_ref, rf_ref, cnt_ref, base_ref):
            pass
        return

    @pl.when(first_tile)
    def _():
        qbuf[0:SC_HALO, :] = jnp.zeros((SC_HALO, wb), F32)
        for cvbuf in cvbufs:
            cvbuf[0:CV_HALO, :] = jnp.zeros((CV_HALO, PROJ_COLS), F32)

    if n_x_refs == 1:
        x = x_refs[0][0]
    else:
        x1p_ref, ya_ref, yb_ref, gates_ref, modp_ref, g2_ref, b2_ref = x_refs
        x = _moe_output(x1p_ref[0], ya_ref[0], yb_ref[0], gates_ref[...],
                        modp_ref[0][:, 5 * d:6 * d], g2_ref[...], b2_ref[...])
    mod = mod_ref[0]
    shift1, scale1 = mod[:, 0:d], mod[:, d:2 * d]
    ub = (x * (1.0 + scale1) + shift1).astype(BF16)

    def glu_block(c0):
        a = _dot(ub, w_in(3 * wb + c0))
        g = _dot(ub, w_in(4 * wb + c0))
        cvbufs[c0 // PROJ_COLS][CV_HALO:CV_HALO + ts, :] = a * jax.nn.sigmoid(g)

    glu_block(0)
    tail = _mixer_tail(step, mg_buf, xs_buf, modt_ref, wo_ref, ln1g_ref, ln1b_ref, wrt_ref,
                       brc_ref, x1_ref, u2_ref, ri_ref, rf_ref, cnt_ref, base_ref)
    ties = {}

    def add_tie(chunk, tie):
        ties[chunk] = tie + ties[chunk] if chunk in ties else tie

    branch_z = {}

    def mixer_a():
        qbuf[SC_HALO:SC_HALO + ts, :] = pa_buf[:, wb:2 * wb] * pa_buf[:, 2 * wb:3 * wb]
        conv = scw_ref[SC_KERNEL - 1:SC_KERNEL, :] * qbuf[SC_HALO:SC_HALO + ts, :]
        for k in range(SC_KERNEL - 1):
            off = SC_HALO - (SC_KERNEL - 1) + k
            conv = conv + scw_ref[k:k + 1, :] * qbuf[off:off + ts, :]
        ybuf[0] = (pa_buf[:, 0:wb] * conv).astype(BF16)
        qbuf[0:SC_HALO, :] = qbuf[ts:ts + SC_HALO, :]
        branch_z[0] = _dot(ybuf[0], _bf16_weights(wbr_ref[0, 0]))

    def mixer_c():
        gu = jax.nn.gelu(pc_buf[:, 0:wb])
        gv = _layer_norm(jax.nn.gelu(pc_buf[:, wb:2 * wb]), sgg_ref[...],
                         sgbeta_ref[...]).astype(BF16)
        row = lax.broadcasted_iota(jnp.int32, (CHUNK, CHUNK), 0)
        col = lax.broadcasted_iota(jnp.int32, (CHUNK, CHUNK), 1)
        hd = wb // SG_HEADS
        for h in range(SG_HEADS):
            wm = jnp.where(row >= col, sgw_ref[h], 0.0).astype(BF16)
            bias = sgbt_ref[:, h:h + 1]
            for n in range(ts // CHUNK):
                rs = slice(n * CHUNK, (n + 1) * CHUNK)
                cs = slice(h * hd, (h + 1) * hd)
                mixed = _dot(wm, gv[rs, cs]) + bias
                ybuf[2, rs, cs] = (gu[rs, cs] * mixed).astype(BF16)
        branch_z[2] = _dot(ybuf[2], _bf16_weights(wbr_ref[0, 2]))

    def glu_rest():
        for c0 in range(PROJ_COLS, wb, PROJ_COLS):
            glu_block(c0)

    def tail_stage():
        next(tail, None)

    mxu_jobs = ([(pa_buf, w_in, c0, c0) for c0 in range(0, 3 * wb, PROJ_COLS)]
                + [(pc_buf, w_in, c0, 5 * wb + c0) for c0 in range(0, 2 * wb, PROJ_COLS)]
                + [(gl_buf, w_gate, c0, c0) for c0 in range(0, 3 * d, PROJ_COLS)])
    glu_rest()
    tail_stage()
    after_job = {7: tail_stage, 12: tail_stage}
    conv_jobs = [(r0, c0) for c0 in range(0, wb, CONV_COLS) for r0 in range(0, ts, CONV_ROWS)]
    chunks_per_job = len(conv_jobs) // len(mxu_jobs)
    for c, conv_job in enumerate(conv_jobs):
        _conv31_chunk(cvw_ref, cvbufs, cvout, *conv_job, ties.get(c))
        if (c + 1) % chunks_per_job:
            continue
        i = c // chunks_per_job
        dst, weights, dc, wc = mxu_jobs[i]
        res = _dot(ub, weights(wc))
        dst[:, dc:dc + PROJ_COLS] = res
        if i < len(mxu_jobs) - TIE_FREE_JOBS:
            add_tie(c + TIE_LAG, _zero_after(res[ts - 1:ts, PROJ_COLS - CONV_COLS:PROJ_COLS]))
        if i in after_job:
            after_job[i]()
    for cvbuf in cvbufs:
        cvbuf[0:CV_HALO, :] = cvbuf[ts:ts + CV_HALO, :]

    mixer_a()
    cv = _layer_norm(cvout[...] + cvb_ref[...], cvg_ref[...], cvbeta_ref[...])
    ybuf[1] = (cv * jax.nn.sigmoid(cv)).astype(BF16)
    branch_z[1] = _dot(ybuf[1], _bf16_weights(wbr_ref[0, 1]))
    mixer_c()

    merged = None
    for n in range(3):
        gated = jax.nn.sigmoid(gl_buf[:, n * d:(n + 1) * d] + bg_ref[n:n + 1, :]) * branch_z[n]
        merged = gated if merged is None else merged + gated
    mg_buf[...] = merged.astype(BF16)
    xs_buf[...] = x


def _mixer_tail(step, mg_buf, xs_buf, modt_ref, wo_ref, ln1g_ref, ln1b_ref, wrt_ref, brc_ref,
                x1_ref, u2_ref, ri_ref, rf_ref, cnt_ref, base_ref):
    ts = MIX_ROWS
    d = D_MODEL
    modt = modt_ref[0]
    gate1, shift2, scale2 = modt[:, 2 * d:3 * d], modt[:, 3 * d:4 * d], modt[:, 4 * d:5 * d]
    hmix = _dot(mg_buf[...], _bf16_weights(wo_ref[0]))
    x1 = _layer_norm(ALPHA * xs_buf[...] + gate1 * hmix, ln1g_ref[...], ln1b_ref[...])
    x1_ref[0] = x1
    u2 = x1 * (1.0 + scale2) + shift2
    u2_ref[0] = _pack_bf16_pairs(u2)
    yield

    logits = lax.dot_general(wrt_ref[...], u2, (((1,), (1,)), ((), ())),
                             preferred_element_type=F32,
                             precision=lax.Precision.HIGHEST)
    mx = jnp.max(logits, axis=0, keepdims=True)
    ex = jnp.exp(logits - mx)
    scores = ex / jnp.sum(ex, axis=0, keepdims=True)
    sel = scores + brc_ref[...]
    tops = []
    for g in range(N_GROUPS):
        rows = [sel[g * EXPERTS_PER_GROUP + k:g * EXPERTS_PER_GROUP + k + 1, :]
                for k in range(EXPERTS_PER_GROUP)]
        tops.append(_top2_of4(rows))
    best = tops[0][0] + tops[0][2]
    g_idx = jnp.zeros(best.shape, jnp.int32)
    loc1, loc2 = tops[0][1], tops[0][3]
    for g in range(1, N_GROUPS):
        gs = tops[g][0] + tops[g][2]
        gt = gs > best
        best = jnp.where(gt, gs, best)
        g_idx = jnp.where(gt, g, g_idx)
        loc1 = jnp.where(gt, tops[g][1], loc1)
        loc2 = jnp.where(gt, tops[g][3], loc2)
    e0 = g_idx * EXPERTS_PER_GROUP + loc1
    e1 = g_idx * EXPERTS_PER_GROUP + loc2
    erow = lax.broadcasted_iota(jnp.int32, (N_EXPERTS, ts), 0)
    is0 = erow == e0
    is1 = erow == e1
    s0 = jnp.sum(jnp.where(is0, scores, 0.0), axis=0, keepdims=True)
    s1 = jnp.sum(jnp.where(is1, scores, 0.0), axis=0, keepdims=True)
    ssum = s0 + s1
    yield

    onehot = jnp.logical_or(is0, is1).astype(BF16)
    src = lax.broadcasted_iota(jnp.int32, (ts, ts), 0)
    dst = lax.broadcasted_iota(jnp.int32, (ts, ts), 1)
    earlier = (src < dst).astype(BF16)
    prior = _dot(onehot, earlier) + base_ref[:, 0:1]
    r0 = jnp.sum(jnp.where(is0, prior, 0.0), axis=0, keepdims=True)
    r1 = jnp.sum(jnp.where(is1, prior, 0.0), axis=0, keepdims=True)
    counts = jnp.sum(onehot.astype(F32), axis=1, keepdims=True)
    base_ref[...] = base_ref[...] + jnp.where(step > 0, counts, 0.0)

    zi = jnp.zeros((SUBLANES - 4, ts), jnp.int32)
    ri_ref[...] = jnp.concatenate([e0, e1, r0.astype(jnp.int32), r1.astype(jnp.int32), zi], axis=0)
    zf = jnp.zeros((SUBLANES - 2, ts), F32)
    rf_ref[...] = jnp.concatenate([s0 / ssum, s1 / ssum, zf], axis=0)
    cnt_ref[...] = base_ref[...]


def _mixer(layer, xsrc, mod3, mb0, bsz, w_in, scw, cvw, cvb, cvg, cvbeta, sgg, sgbeta, sgw, sgbt,
           wbr, wg, bg, wo, ln1g, ln1b, wrt, brc):
    _, seq, d = xsrc[0].shape
    ts = MIX_ROWS
    ns = seq // ts
    n_tok = bsz * seq

    n_tiles = bsz * ns

    def const(shape):
        zeros = (0,) * len(shape)
        return pl.BlockSpec(shape, lambda s: zeros, pipeline_mode=pl.Buffered(1))

    def first_half(s):
        return jnp.minimum(s, n_tiles - 1)

    def second_half(s):
        return jnp.maximum(s - 1, 0)

    def layer_weights(shape):
        block = (1,) + tuple(shape[1:])
        index = (layer,) + (0,) * (len(shape) - 1)
        return pl.BlockSpec(block, lambda s: index, pipeline_mode=pl.Buffered(1))

    if len(xsrc) == 2:
        x, xb0 = xsrc
        x_args = [x]
        x_specs = [pl.BlockSpec((1, ts, d),
                                lambda s: (first_half(s) // ns + xb0, first_half(s) % ns, 0))]
    else:
        x1p, pairs, gates, mod3p, g2, b2 = xsrc
        x_args = [x1p, pairs, pairs, gates, mod3p, g2, b2]
        x_specs = [
            pl.BlockSpec((1, ts, d), lambda s: (first_half(s) // ns, first_half(s) % ns, 0)),
            pl.BlockSpec((1, ts, d // 2), lambda s: (0, first_half(s), 0)),
            pl.BlockSpec((1, ts, d // 2), lambda s: (1, first_half(s), 0)),
            pl.BlockSpec((ts, TOP_K), lambda s: (first_half(s), 0)),
            pl.BlockSpec((1, 1, 6 * d), lambda s: (first_half(s) // ns + mb0, 0, 0)),
            const(g2.shape), const(b2.shape),
        ]
    in_specs = x_specs + [
        pl.BlockSpec((1, 1, 6 * d), lambda s: (first_half(s) // ns + mb0, 0, 0)),
        pl.BlockSpec((1, 1, 6 * d), lambda s: (second_half(s) // ns + mb0, 0, 0)),
        layer_weights(w_in.shape), const(scw.shape), const(cvw.shape), const(cvb.shape),
        const(cvg.shape), const(cvbeta.shape), const(sgg.shape), const(sgbeta.shape),
        const(sgw.shape), const(sgbt.shape), layer_weights(wbr.shape), layer_weights(wg.shape),
        const(bg.shape), layer_weights(wo.shape), const(ln1g.shape), const(ln1b.shape),
        const(wrt.shape), const(brc.shape),
    ]
    out_specs = [
        pl.BlockSpec((1, ts, d), lambda s: (second_half(s) // ns, second_half(s) % ns, 0)),
        pl.BlockSpec((1, ts, d // 2), lambda s: (second_half(s) // ns, second_half(s) % ns, 0)),
        pl.BlockSpec((SUBLANES, ts), lambda s: (0, second_half(s))),
        pl.BlockSpec((SUBLANES, ts), lambda s: (0, second_half(s))),
        pl.BlockSpec((N_EXPERTS, LANES), lambda s: (0, 0)),
    ]
    out_shape = [
        jax.ShapeDtypeStruct((bsz, seq, d), F32),
        jax.ShapeDtypeStruct((bsz, seq, d // 2), jnp.uint32),
        jax.ShapeDtypeStruct((SUBLANES, n_tok), jnp.int32),
        jax.ShapeDtypeStruct((SUBLANES, n_tok), F32),
        jax.ShapeDtypeStruct((N_EXPERTS, LANES), F32),
    ]
    return pl.pallas_call(
        functools.partial(_mixer_kernel, ns, len(x_args)),
        grid=(n_tiles + 1,),
        in_specs=in_specs,
        out_specs=out_specs,
        out_shape=out_shape,
        scratch_shapes=[
            pltpu.VMEM((SC_HALO + ts, W_BRANCH), F32),
            pltpu.VMEM((CV_HALO + ts, PROJ_COLS), F32),
            pltpu.VMEM((CV_HALO + ts, PROJ_COLS), F32),
            pltpu.VMEM((ts, W_BRANCH), F32),
            pltpu.VMEM((3, ts, W_BRANCH), BF16),
            pltpu.VMEM((ts, 3 * W_BRANCH), F32),
            pltpu.VMEM((ts, 2 * W_BRANCH), F32),
            pltpu.VMEM((ts, 3 * D_MODEL), F32),
            pltpu.VMEM((ts, d), BF16),
            pltpu.VMEM((ts, d), F32),
            pltpu.VMEM((N_EXPERTS, LANES), F32),
        ],
        compiler_params=pltpu.CompilerParams(
            dimension_semantics=("arbitrary",),
            vmem_limit_bytes=MIXER_VMEM_LIMIT),
    )(*x_args, mod3, mod3, w_in, scw, cvw, cvb, cvg, cvbeta, sgg, sgbeta, sgw, sgbt, wbr, wg, bg,
      wo, ln1g, ln1b, wrt, brc)


def _sc_workers():
    info = plsc.get_sparse_core_info()
    return info.num_cores, info.num_cores * info.num_subcores


def _sc_scatter_rows(rows, dest_a, dest_b, n_out):
    n, d = rows.shape
    nc, nw = _sc_workers()
    per_w = n // nw
    n_win = per_w // SC_WINDOW
    ia = dest_a.reshape(nw, n_win, SC_WINDOW)
    ib = dest_b.reshape(nw, n_win, SC_WINDOW)
    mesh = plsc.VectorSubcoreMesh(core_axis_name="c", subcore_axis_name="s")

    @functools.partial(
        pl.kernel, mesh=mesh,
        out_type=jax.ShapeDtypeStruct((n_out, d), rows.dtype),
        scratch_types=[
            pltpu.VMEM((n_win, SC_WINDOW), jnp.int32),
            pltpu.VMEM((n_win, SC_WINDOW), jnp.int32),
            pltpu.VMEM((SC_WINDOW, d), rows.dtype),
        ],
    )
    def scatter(rows_hbm, ia_hbm, ib_hbm, out_hbm, ia_v, ib_v, rows_v):
        wid = lax.axis_index("s") * nc + lax.axis_index("c")
        pltpu.sync_copy(ia_hbm.at[wid], ia_v)
        pltpu.sync_copy(ib_hbm.at[wid], ib_v)
        base = wid * per_w

        @pl.loop(0, n_win)
        def _(j):
            pltpu.sync_copy(rows_hbm.at[pl.ds(base + j * SC_WINDOW, SC_WINDOW)], rows_v)
            pltpu.sync_copy(rows_v, out_hbm.at[ia_v.at[j]])
            pltpu.sync_copy(rows_v, out_hbm.at[ib_v.at[j]])

    return scatter(rows, ia, ib)


def _sc_gather_rows(table, idx):
    n = idx.shape[0]
    d = table.shape[1]
    nc, nw = _sc_workers()
    per_w = n // nw
    n_win = per_w // SC_WINDOW
    idx3 = idx.reshape(nw, n_win, SC_WINDOW)
    mesh = plsc.VectorSubcoreMesh(core_axis_name="c", subcore_axis_name="s")

    @functools.partial(
        pl.kernel, mesh=mesh,
        out_type=jax.ShapeDtypeStruct((n, d), table.dtype),
        scratch_types=[
            pltpu.VMEM((n_win, SC_WINDOW), jnp.int32),
            pltpu.VMEM((SC_WINDOW, d), table.dtype),
        ],
    )
    def gather(table_hbm, idx_hbm, out_hbm, idx_v, rows_v):
        wid = lax.axis_index("s") * nc + lax.axis_index("c")
        pltpu.sync_copy(idx_hbm.at[wid], idx_v)
        base = wid * per_w

        @pl.loop(0, n_win)
        def _(j):
            pltpu.sync_copy(table_hbm.at[idx_v.at[j]], rows_v)
            pltpu.sync_copy(rows_v, out_hbm.at[pl.ds(base + j * SC_WINDOW, SC_WINDOW)])

    return gather(table, idx3)


def _expert_kernel(be_ref, nused_ref, x_ref, w1_ref, w3_ref, w2_ref, o_ref, w1s, w3s, w2s):
    i = pl.program_id(0)
    new_expert = jnp.logical_or(i == 0, be_ref[i] != be_ref[jnp.maximum(i - 1, 0)])

    @pl.when(new_expert)
    def _():
        w1s[...] = w1_ref[0, 0].astype(BF16)
        w3s[...] = w3_ref[0, 0].astype(BF16)
        w2s[...] = w2_ref[0, 0].astype(BF16)

    @pl.when(i < nused_ref[0])
    def _():
        lo, hi = _unpack_bf16_pairs(x_ref[...])
        xb = jnp.concatenate([lo, hi], axis=1).astype(BF16)
        a = _dot(xb, w1s[...])
        b = _dot(xb, w3s[...])
        for r0 in range(0, MOE_ROWS, MOE_SLAB):
            rs = slice(r0, r0 + MOE_SLAB)
            h = a[rs] * jax.nn.sigmoid(a[rs]) * b[rs]
            o_ref[rs, :] = _pack_bf16_pairs(_dot(h.astype(BF16), w2s[...]))


def _experts(layer, buf, block_e, n_used, w1, w3, w2):
    n_rows, dp = buf.shape
    nb = n_rows // MOE_ROWS
    d, fe = w1.shape[-2:]

    def row_map(i, be, nu):
        return (jnp.minimum(i, nu[0] - 1), 0)

    grid_spec = pltpu.PrefetchScalarGridSpec(
        num_scalar_prefetch=2,
        grid=(nb,),
        in_specs=[
            pl.BlockSpec((MOE_ROWS, dp), row_map),
            pl.BlockSpec((1, 1, d, fe), lambda i, be, nu: (layer, be[i], 0, 0)),
            pl.BlockSpec((1, 1, d, fe), lambda i, be, nu: (layer, be[i], 0, 0)),
            pl.BlockSpec((1, 1, fe, d), lambda i, be, nu: (layer, be[i], 0, 0)),
        ],
        out_specs=pl.BlockSpec((MOE_ROWS, dp), row_map),
        scratch_shapes=[pltpu.VMEM((d, fe), BF16), pltpu.VMEM((d, fe), BF16),
                        pltpu.VMEM((fe, d), BF16)],
    )
    return pl.pallas_call(
        _expert_kernel,
        grid_spec=grid_spec,
        out_shape=jax.ShapeDtypeStruct((n_rows, dp), jnp.uint32),
        compiler_params=pltpu.CompilerParams(dimension_semantics=("arbitrary",),
                                             vmem_limit_bytes=EXPERT_VMEM_LIMIT),
    )(block_e, n_used, buf, w1, w3, w2)


def _combine_kernel(x1_ref, ya_ref, yb_ref, gates_ref, mod_ref, g_ref, b_ref, *rest):
    o_ref = rest[-1]
    d = D_MODEL
    o_ref[0] = _moe_output(x1_ref[0], ya_ref[0], yb_ref[0], gates_ref[...],
                           mod_ref[0][:, 5 * d:6 * d], g_ref[...], b_ref[...])


def _combine(x1, pairs, gates, mod3, mb0, g, b, out_bsz, ob0, prev):
    bsz, seq, d = x1.shape
    ts = COMB_ROWS
    ns = seq // ts
    in_specs = [
        pl.BlockSpec((1, ts, d), lambda i, j: (i, j, 0)),
        pl.BlockSpec((1, ts, d // 2), lambda i, j: (0, i * ns + j, 0)),
        pl.BlockSpec((1, ts, d // 2), lambda i, j: (1, i * ns + j, 0)),
        pl.BlockSpec((ts, TOP_K), lambda i, j: (i * ns + j, 0)),
        pl.BlockSpec((1, 1, 6 * d), lambda i, j: (i + mb0, 0, 0)),
        pl.BlockSpec((1, d), lambda i, j: (0, 0)),
        pl.BlockSpec((1, d), lambda i, j: (0, 0)),
    ]
    args = [x1, pairs, pairs, gates, mod3, g, b]
    aliases = {}
    if prev is not None:
        in_specs.append(pl.BlockSpec(memory_space=pl.ANY))
        aliases = {len(args): 0}
        args.append(prev)
    return pl.pallas_call(
        _combine_kernel,
        grid=(bsz, ns),
        in_specs=in_specs,
        out_specs=pl.BlockSpec((1, ts, d), lambda i, j: (i + ob0, j, 0)),
        out_shape=jax.ShapeDtypeStruct((out_bsz, seq, d), F32),
        input_output_aliases=aliases,
        compiler_params=pltpu.CompilerParams(dimension_semantics=("arbitrary", "arbitrary")),
    )(*args)


def _plan_kernel(ri_ref, cnt_ref, dest_ref, blk_ref):
    n_tok = ri_ref.shape[1]
    counts = cnt_ref[:, 0:1].astype(jnp.int32)
    shift = MOE_ROWS.bit_length() - 1
    padded = lax.shift_left(lax.shift_right_logical(counts + (MOE_ROWS - 1), shift), shift)
    e_out = lax.broadcasted_iota(jnp.int32, (N_EXPERTS, N_EXPERTS), 0)
    e_in = lax.broadcasted_iota(jnp.int32, (N_EXPERTS, N_EXPERTS), 1)
    upto = (e_in <= e_out).astype(BF16)
    padded_f = jnp.broadcast_to(padded.astype(F32), (N_EXPERTS, LANES))
    pad_end = _dot(upto, padded_f.astype(BF16))
    pad_start = pad_end[:, 0:1] - padded.astype(F32)
    erow = lax.broadcasted_iota(jnp.int32, (N_EXPERTS, n_tok), 0)
    rows = []
    for k in range(TOP_K):
        start = jnp.sum(jnp.where(erow == ri_ref[k:k + 1, :], pad_start, 0.0), axis=0, keepdims=True)
        rows.append(start.astype(jnp.int32) + ri_ref[TOP_K + k:TOP_K + k + 1, :])
    rows.append(jnp.zeros((SUBLANES - TOP_K, n_tok), jnp.int32))
    dest_ref[...] = jnp.concatenate(rows, axis=0)
    block_start = (lax.broadcasted_iota(jnp.int32, (N_EXPERTS, LANES), 1) * MOE_ROWS).astype(F32)
    block_e = jnp.sum((block_start >= pad_end).astype(jnp.int32), axis=0, keepdims=True)
    n_used = lax.shift_right_logical(pad_end[N_EXPERTS - 1:N_EXPERTS, :].astype(jnp.int32), shift)
    blk_ref[...] = jnp.concatenate(
        [jnp.minimum(block_e, N_EXPERTS - 1), n_used,
         jnp.zeros((SUBLANES - 2, LANES), jnp.int32)], axis=0)


def _dispatch_plan(ri, cnt):
    n_tok = ri.shape[1]
    n_blocks = (n_tok * TOP_K + N_EXPERTS * (MOE_ROWS - 1) + MOE_ROWS - 1) // MOE_ROWS
    assert n_blocks <= LANES and MOE_ROWS & (MOE_ROWS - 1) == 0
    dest, blk = pl.pallas_call(
        _plan_kernel,
        out_shape=[jax.ShapeDtypeStruct((SUBLANES, n_tok), jnp.int32),
                   jax.ShapeDtypeStruct((SUBLANES, LANES), jnp.int32)],
    )(ri, cnt)
    return dest, blk[0, 0:n_blocks], blk[1, 0:1], n_blocks * MOE_ROWS


def kernel(x, c, w_ada, b_ada, w_in, sc_conv, cv_conv, cv_conv_b, cv_ln_g, cv_ln_b, sg_ln_g, sg_ln_b, sg_w, sg_b, w_branch, w_gate, b_gate, w_o, ln1_g, ln1_b, w_router, b_router, w1, w3, w2, ln2_g, ln2_b):
    bsz, seq, d = x.shape
    mod3 = _ada(c, w_ada, b_ada).reshape(DEPTH, bsz, 1, 6 * d)
    wrt = w_router.T
    brc = b_router.reshape(N_EXPERTS, 1)
    w_in_p = _pack_weights(w_in)
    w_gate_p = _pack_weights(jnp.transpose(w_gate, (0, 2, 1, 3)).reshape(DEPTH * 3, d, d)).reshape(
        DEPTH, 3, d // 2, d)
    w_o_p = _pack_weights(w_o)
    w_branch_p = _pack_weights(w_branch.reshape(DEPTH * 3, W_BRANCH, d)).reshape(
        DEPTH, 3, W_BRANCH // 2, d)
    cb = bsz // N_CHAINS
    n_tok = cb * seq
    chains = [(x, h * cb) for h in range(N_CHAINS)]
    for l in range(DEPTH):
        last = l == DEPTH - 1
        result = None
        for h in range(N_CHAINS):
            x1, u2, ri, rf, cnt = _mixer(
                l, chains[h], mod3[l], h * cb, cb, w_in_p, sc_conv[l], cv_conv[l],
                cv_conv_b[l].reshape(1, -1), cv_ln_g[l].reshape(1, -1), cv_ln_b[l].reshape(1, -1),
                sg_ln_g[l].reshape(1, -1), sg_ln_b[l].reshape(1, -1), sg_w[l], sg_b[l].T,
                w_branch_p, w_gate_p, b_gate[l],
                w_o_p, ln1_g[l].reshape(1, -1), ln1_b[l].reshape(1, -1), wrt, brc)
            dest, block_e, n_used, n_rows = _dispatch_plan(ri, cnt)
            buf = _sc_scatter_rows(u2.reshape(n_tok, d // 2), dest[0], dest[1], n_rows)
            obuf = _experts(l, buf, block_e, n_used, w1, w3, w2)
            pair_idx = dest[0:TOP_K].reshape(TOP_K * n_tok)
            pairs = _sc_gather_rows(obuf, pair_idx).reshape(TOP_K, n_tok, d // 2)
            gates = rf[0:TOP_K].T
            g2, b2 = ln2_g[l].reshape(1, -1), ln2_b[l].reshape(1, -1)
            if last:
                result = _combine(x1, pairs, gates, mod3[l], h * cb, g2, b2, bsz, h * cb, result)
            else:
                chains[h] = (x1, pairs, gates, mod3[l], g2, b2)
    return result
```

```python
import functools

import jax
import jax.numpy as jnp
from jax import lax
from jax.experimental import pallas as pl
from jax.experimental.pallas import tpu as pltpu
from jax.experimental.pallas import tpu_sc as plsc

D_MODEL = 1024
DEPTH = 2
W_BRANCH = 1024
SC_KERNEL = 3
CV_KERNEL = 31
CHUNK = 128
SG_HEADS = 8
N_EXPERTS = 16
N_GROUPS = 4
EXPERTS_PER_GROUP = N_EXPERTS // N_GROUPS
TOP_K = 2
D_EXPERT = 512
ALPHA = (2.0 * DEPTH) ** 0.25
LN_EPS = 1e-5

F32 = jnp.float32
BF16 = jnp.bfloat16

V7X_VMEM_BYTES = 64 * 1024 * 1024
MIXER_VMEM_LIMIT = V7X_VMEM_BYTES - 6 * 1024 * 1024
EXPERT_VMEM_LIMIT = V7X_VMEM_BYTES // 2
SUBLANES = 8
LANES = 128

MIX_ROWS = 256
SC_HALO = SUBLANES
CV_HALO = 32
CONV_ROWS = 128
TIE_LAG = 2
TIE_FREE_JOBS = 4
CONV_COLS = 128
PROJ_COLS = 512
PACK_ROWS = 256
MOE_ROWS = 512
MOE_SLAB = 256
COMB_ROWS = 512
SC_WINDOW = 32
N_CHAINS = 2


def _dot(a, b):
    return jnp.dot(a, b, preferred_element_type=F32)


def _pack_bf16_pairs(v):
    m = v.shape[1] // 2
    lo = lax.bitcast_convert_type(v[:, 0:m].astype(BF16).astype(F32), jnp.uint32)
    hi = lax.bitcast_convert_type(v[:, m:2 * m].astype(BF16).astype(F32), jnp.uint32)
    return jnp.bitwise_or(jnp.bitwise_and(hi, jnp.uint32(0xFFFF0000)),
                          lax.shift_right_logical(lo, jnp.uint32(16)))


def _unpack_bf16_pairs(w):
    lo = lax.bitcast_convert_type(lax.shift_left(w, jnp.uint32(16)), F32)
    hi = lax.bitcast_convert_type(jnp.bitwise_and(w, jnp.uint32(0xFFFF0000)), F32)
    return lo, hi


def _layer_norm(v, g, b):
    mu = jnp.mean(v, axis=-1, keepdims=True)
    vc = v - mu
    var = jnp.mean(vc * vc, axis=-1, keepdims=True)
    return vc * lax.rsqrt(var + LN_EPS) * g + b


def _ada_kernel(c_ref, w_ref, b_ref, o_ref):
    c = c_ref[...]
    c_act = c * jax.nn.sigmoid(c)
    o_ref[0] = jnp.dot(c_act, w_ref[0], preferred_element_type=F32,
                       precision=lax.Precision.HIGHEST) + b_ref[0]


def _ada(c, w_ada, b_ada):
    bsz, d = c.shape
    n = w_ada.shape[-1]
    tn = 1536
    return pl.pallas_call(
        _ada_kernel,
        grid=(DEPTH, n // tn),
        in_specs=[
            pl.BlockSpec((bsz, d), lambda l, j: (0, 0)),
            pl.BlockSpec((1, d, tn), lambda l, j: (l, 0, j)),
            pl.BlockSpec((1, 1, tn), lambda l, j: (l, 0, j)),
        ],
        out_specs=pl.BlockSpec((1, bsz, tn), lambda l, j: (l, 0, j)),
        out_shape=jax.ShapeDtypeStruct((DEPTH, bsz, n), F32),
    )(c, w_ada, b_ada.reshape(DEPTH, 1, n))


def _top2_of4(rows):
    m1 = rows[0]
    i1 = jnp.zeros(rows[0].shape, jnp.int32)
    for k in range(1, 4):
        gt = rows[k] > m1
        m1 = jnp.where(gt, rows[k], m1)
        i1 = jnp.where(gt, k, i1)
    m2 = jnp.full(rows[0].shape, -jnp.inf, F32)
    i2 = jnp.zeros(rows[0].shape, jnp.int32)
    for k in range(4):
        cand = jnp.where(i1 == k, -jnp.inf, rows[k])
        gt = cand > m2
        m2 = jnp.where(gt, cand, m2)
        i2 = jnp.where(gt, k, i2)
    return m1, i1, m2, i2


def _zero_after(v):
    u = lax.bitcast_convert_type(v, jnp.uint32)
    u = lax.shift_right_logical(lax.shift_right_logical(u, jnp.uint32(16)), jnp.uint32(16))
    return lax.bitcast_convert_type(u, F32)


def _conv31_chunk(cvw_ref, cvbufs, cvout, r0, c0, tie):
    cs = slice(c0, c0 + CONV_COLS)
    cvbuf = cvbufs[c0 // PROJ_COLS]
    bs = slice(c0 % PROJ_COLS, c0 % PROJ_COLS + CONV_COLS)
    acc = None
    for r in range(SUBLANES):
        lead = SUBLANES if r else 0
        part = None
        for m in range((CV_KERNEL - 1 - r) // SUBLANES + 1):
            k = CV_KERNEL - 1 - (SUBLANES * m + r)
            start = CV_HALO + r0 - lead - SUBLANES * m
            w_row = cvw_ref[k:k + 1, cs]
            if tie is not None and acc is None and part is None:
                w_row = w_row + tie
            term = w_row * cvbuf[start:start + lead + CONV_ROWS, bs]
            part = term if part is None else part + term
        part = part[lead - r:lead - r + CONV_ROWS]
        acc = part if acc is None else acc + part
    cvout[r0:r0 + CONV_ROWS, cs] = acc


def _bf16_weights(packed):
    return pltpu.bitcast(packed, BF16)


def _pack_weights_kernel(w_ref, o_ref):
    o_ref[0] = pltpu.bitcast(w_ref[0].astype(BF16), jnp.uint32)


def _pack_weights(w):
    g, k, n = w.shape
    kb = PACK_ROWS
    return pl.pallas_call(
        _pack_weights_kernel,
        grid=(g, k // kb),
        in_specs=[pl.BlockSpec((1, kb, n), lambda i, j: (i, j, 0))],
        out_specs=pl.BlockSpec((1, kb // 2, n), lambda i, j: (i, j, 0)),
        out_shape=jax.ShapeDtypeStruct((g, k // 2, n), jnp.uint32),
        compiler_params=pltpu.CompilerParams(vmem_limit_bytes=EXPERT_VMEM_LIMIT),
    )(w)


def _moe_output(x1, ya_packed, yb_packed, gates, gate2, g, b):
    ya = jnp.concatenate(_unpack_bf16_pairs(ya_packed), axis=1)
    yb = jnp.concatenate(_unpack_bf16_pairs(yb_packed), axis=1)
    h = gates[:, 0:1] * ya + gates[:, 1:2] * yb
    return _layer_norm(ALPHA * x1 + gate2 * h, g, b)


def _mixer_kernel(tiles_per_seq, n_x_refs, *refs):
    last_step = pl.num_programs(0) - 1

    @pl.when(pl.program_id(0) < last_step)
    def _():
        _mixer_step(tiles_per_seq, n_x_refs, True, *refs)

    @pl.when(pl.program_id(0) == last_step)
    def _():
        _mixer_step(tiles_per_seq, n_x_refs, False, *refs)


def _mixer_step(tiles_per_seq, n_x_refs, run_first_half, *refs):
    x_refs = refs[:n_x_refs]
    (mod_ref, modt_ref, w_in_ref, scw_ref, cvw_ref, cvb_ref, cvg_ref, cvbeta_ref, sgg_ref,
     sgbeta_ref, sgw_ref, sgbt_ref, wbr_ref, wg_ref, bg_ref, wo_ref, ln1g_ref, ln1b_ref, wrt_ref,
     brc_ref,
     x1_ref, u2_ref, ri_ref, rf_ref, cnt_ref,
     qbuf, cvbuf0, cvbuf1, cvout, ybuf, pa_buf, pc_buf, gl_buf, mg_buf, xs_buf,
     base_ref) = refs[n_x_refs:]
    cvbufs = (cvbuf0, cvbuf1)
    ts = MIX_ROWS
    d = D_MODEL
    wb = W_BRANCH
    step = pl.program_id(0)
    tile = jnp.minimum(step, pl.num_programs(0) - 2)
    first_tile = tile % tiles_per_seq == 0

    @pl.when(step == 0)
    def _():
        base_ref[...] = jnp.zeros_like(base_ref)
        mg_buf[...] = jnp.zeros_like(mg_buf)
        xs_buf[...] = jnp.zeros_like(xs_buf)

    def w_in(c0):
        return _bf16_weights(w_in_ref[0, :, c0:c0 + PROJ_COLS])

    def w_gate(c0):
        n, col = divmod(c0, d)
        return _bf16_weights(wg_ref[0, n, :, col:col + PROJ_COLS])

    if not run_first_half:
        for _ in _mixer_tail(step, mg_buf, xs_buf, modt_ref, wo_ref, ln1g_ref, ln1b_ref, wrt_ref,
                             brc_ref, x1_ref, u2_ref, ri_ref, rf_ref, cnt_ref, base_ref):
            pass
        return

    @pl.when(first_tile)
    def _():
        qbuf[0:SC_HALO, :] = jnp.zeros((SC_HALO, wb), F32)
        for cvbuf in cvbufs:
            cvbuf[0:CV_HALO, :] = jnp.zeros((CV_HALO, PROJ_COLS), F32)

    if n_x_refs == 1:
        x = x_refs[0][0]
    else:
        x1p_ref, ya_ref, yb_ref, gates_ref, modp_ref, g2_ref, b2_ref = x_refs
        x = _moe_output(x1p_ref[0], ya_ref[0], yb_ref[0], gates_ref[...],
                        modp_ref[0][:, 5 * d:6 * d], g2_ref[...], b2_ref[...])
    mod = mod_ref[0]
    shift1, scale1 = mod[:, 0:d], mod[:, d:2 * d]
    ub = (x * (1.0 + scale1) + shift1).astype(BF16)

    def glu_block(c0):
        a = _dot(ub, w_in(3 * wb + c0))
        g = _dot(ub, w_in(4 * wb + c0))
        cvbufs[c0 // PROJ_COLS][CV_HALO:CV_HALO + ts, :] = a * jax.nn.sigmoid(g)

    glu_block(0)
    tail = _mixer_tail(step, mg_buf, xs_buf, modt_ref, wo_ref, ln1g_ref, ln1b_ref, wrt_ref,
                       brc_ref, x1_ref, u2_ref, ri_ref, rf_ref, cnt_ref, base_ref)
    ties = {}

    def add_tie(chunk, tie):
        ties[chunk] = tie + ties[chunk] if chunk in ties else tie

    branch_z = {}

    def mixer_a():
        qbuf[SC_HALO:SC_HALO + ts, :] = pa_buf[:, wb:2 * wb] * pa_buf[:, 2 * wb:3 * wb]
        conv = scw_ref[SC_KERNEL - 1:SC_KERNEL, :] * qbuf[SC_HALO:SC_HALO + ts, :]
        for k in range(SC_KERNEL - 1):
            off = SC_HALO - (SC_KERNEL - 1) + k
            conv = conv + scw_ref[k:k + 1, :] * qbuf[off:off + ts, :]
        ybuf[0] = (pa_buf[:, 0:wb] * conv).astype(BF16)
        qbuf[0:SC_HALO, :] = qbuf[ts:ts + SC_HALO, :]
        branch_z[0] = _dot(ybuf[0], _bf16_weights(wbr_ref[0, 0]))

    def mixer_c():
        gu = jax.nn.gelu(pc_buf[:, 0:wb])
        gv = _layer_norm(jax.nn.gelu(pc_buf[:, wb:2 * wb]), sgg_ref[...],
                         sgbeta_ref[...]).astype(BF16)
        row = lax.broadcasted_iota(jnp.int32, (CHUNK, CHUNK), 0)
        col = lax.broadcasted_iota(jnp.int32, (CHUNK, CHUNK), 1)
        hd = wb // SG_HEADS
        for h in range(SG_HEADS):
            wm = jnp.where(row >= col, sgw_ref[h], 0.0).astype(BF16)
            bias = sgbt_ref[:, h:h + 1]
            for n in range(ts // CHUNK):
                rs = slice(n * CHUNK, (n + 1) * CHUNK)
                cs = slice(h * hd, (h + 1) * hd)
                mixed = _dot(wm, gv[rs, cs]) + bias
                ybuf[2, rs, cs] = (gu[rs, cs] * mixed).astype(BF16)
        branch_z[2] = _dot(ybuf[2], _bf16_weights(wbr_ref[0, 2]))

    def glu_rest():
        for c0 in range(PROJ_COLS, wb, PROJ_COLS):
            glu_block(c0)

    def tail_stage():
        next(tail, None)

    mxu_jobs = ([(pa_buf, w_in, c0, c0) for c0 in range(0, 3 * wb, PROJ_COLS)]
                + [(pc_buf, w_in, c0, 5 * wb + c0) for c0 in range(0, 2 * wb, PROJ_COLS)]
                + [(gl_buf, w_gate, c0, c0) for c0 in range(0, 3 * d, PROJ_COLS)])
    glu_rest()
    tail_stage()
    after_job = {7: tail_stage, 12: tail_stage}
    conv_jobs = [(r0, c0) for c0 in range(0, wb, CONV_COLS) for r0 in range(0, ts, CONV_ROWS)]
    chunks_per_job = len(conv_jobs) // len(mxu_jobs)
    for c, conv_job in enumerate(conv_jobs):
        _conv31_chunk(cvw_ref, cvbufs, cvout, *conv_job, ties.get(c))
        if (c + 1) % chunks_per_job:
            continue
        i = c // chunks_per_job
        dst, weights, dc, wc = mxu_jobs[i]
        res = _dot(ub, weights(wc))
        dst[:, dc:dc + PROJ_COLS] = res
        if i < len(mxu_jobs) - TIE_FREE_JOBS:
            add_tie(c + TIE_LAG, _zero_after(res[ts - 1:ts, PROJ_COLS - CONV_COLS:PROJ_COLS]))
        if i in after_job:
            after_job[i]()
    for cvbuf in cvbufs:
        cvbuf[0:CV_HALO, :] = cvbuf[ts:ts + CV_HALO, :]

    mixer_a()
    cv = _layer_norm(cvout[...] + cvb_ref[...], cvg_ref[...], cvbeta_ref[...])
    ybuf[1] = (cv * jax.nn.sigmoid(cv)).astype(BF16)
    branch_z[1] = _dot(ybuf[1], _bf16_weights(wbr_ref[0, 1]))
    mixer_c()

    merged = None
    for n in range(3):
        gated = jax.nn.sigmoid(gl_buf[:, n * d:(n + 1) * d] + bg_ref[n:n + 1, :]) * branch_z[n]
        merged = gated if merged is None else merged + gated
    mg_buf[...] = merged.astype(BF16)
    xs_buf[...] = x


def _mixer_tail(step, mg_buf, xs_buf, modt_ref, wo_ref, ln1g_ref, ln1b_ref, wrt_ref, brc_ref,
                x1_ref, u2_ref, ri_ref, rf_ref, cnt_ref, base_ref):
    ts = MIX_ROWS
    d = D_MODEL
    modt = modt_ref[0]
    gate1, shift2, scale2 = modt[:, 2 * d:3 * d], modt[:, 3 * d:4 * d], modt[:, 4 * d:5 * d]
    hmix = _dot(mg_buf[...], _bf16_weights(wo_ref[0]))
    x1 = _layer_norm(ALPHA * xs_buf[...] + gate1 * hmix, ln1g_ref[...], ln1b_ref[...])
    x1_ref[0] = x1
    u2 = x1 * (1.0 + scale2) + shift2
    u2_ref[0] = _pack_bf16_pairs(u2)
    yield

    logits = lax.dot_general(wrt_ref[...], u2, (((1,), (1,)), ((), ())),
                             preferred_element_type=F32,
                             precision=lax.Precision.HIGHEST)
    mx = jnp.max(logits, axis=0, keepdims=True)
    ex = jnp.exp(logits - mx)
    scores = ex / jnp.sum(ex, axis=0, keepdims=True)
    sel = scores + brc_ref[...]
    tops = []
    for g in range(N_GROUPS):
        rows = [sel[g * EXPERTS_PER_GROUP + k:g * EXPERTS_PER_GROUP + k + 1, :]
                for k in range(EXPERTS_PER_GROUP)]
        tops.append(_top2_of4(rows))
    best = tops[0][0] + tops[0][2]
    g_idx = jnp.zeros(best.shape, jnp.int32)
    loc1, loc2 = tops[0][1], tops[0][3]
    for g in range(1, N_GROUPS):
        gs = tops[g][0] + tops[g][2]
        gt = gs > best
        best = jnp.where(gt, gs, best)
        g_idx = jnp.where(gt, g, g_idx)
        loc1 = jnp.where(gt, tops[g][1], loc1)
        loc2 = jnp.where(gt, tops[g][3], loc2)
    e0 = g_idx * EXPERTS_PER_GROUP + loc1
    e1 = g_idx * EXPERTS_PER_GROUP + loc2
    erow = lax.broadcasted_iota(jnp.int32, (N_EXPERTS, ts), 0)
    is0 = erow == e0
    is1 = erow == e1
    s0 = jnp.sum(jnp.where(is0, scores, 0.0), axis=0, keepdims=True)
    s1 = jnp.sum(jnp.where(is1, scores, 0.0), axis=0, keepdims=True)
    ssum = s0 + s1
    yield

    onehot = jnp.logical_or(is0, is1).astype(BF16)
    src = lax.broadcasted_iota(jnp.int32, (ts, ts), 0)
    dst = lax.broadcasted_iota(jnp.int32, (ts, ts), 1)
    earlier = (src < dst).astype(BF16)
    prior = _dot(onehot, earlier) + base_ref[:, 0:1]
    r0 = jnp.sum(jnp.where(is0, prior, 0.0), axis=0, keepdims=True)
    r1 = jnp.sum(jnp.where(is1, prior, 0.0), axis=0, keepdims=True)
    counts = jnp.sum(onehot.astype(F32), axis=1, keepdims=True)
    base_ref[...] = base_ref[...] + jnp.where(step > 0, counts, 0.0)

    zi = jnp.zeros((SUBLANES - 4, ts), jnp.int32)
    ri_ref[...] = jnp.concatenate([e0, e1, r0.astype(jnp.int32), r1.astype(jnp.int32), zi], axis=0)
    zf = jnp.zeros((SUBLANES - 2, ts), F32)
    rf_ref[...] = jnp.concatenate([s0 / ssum, s1 / ssum, zf], axis=0)
    cnt_ref[...] = base_ref[...]


def _mixer(layer, xsrc, mod3, mb0, bsz, w_in, scw, cvw, cvb, cvg, cvbeta, sgg, sgbeta, sgw, sgbt,
           wbr, wg, bg, wo, ln1g, ln1b, wrt, brc):
    _, seq, d = xsrc[0].shape
    ts = MIX_ROWS
    ns = seq // ts
    n_tok = bsz * seq

    n_tiles = bsz * ns

    def const(shape):
        zeros = (0,) * len(shape)
        return pl.BlockSpec(shape, lambda s: zeros, pipeline_mode=pl.Buffered(1))

    def first_half(s):
        return jnp.minimum(s, n_tiles - 1)

    def second_half(s):
        return jnp.maximum(s - 1, 0)

    def layer_weights(shape):
        block = (1,) + tuple(shape[1:])
        index = (layer,) + (0,) * (len(shape) - 1)
        return pl.BlockSpec(block, lambda s: index, pipeline_mode=pl.Buffered(1))

    if len(xsrc) == 2:
        x, xb0 = xsrc
        x_args = [x]
        x_specs = [pl.BlockSpec((1, ts, d),
                                lambda s: (first_half(s) // ns + xb0, first_half(s) % ns, 0))]
    else:
        x1p, pairs, gates, mod3p, g2, b2 = xsrc
        x_args = [x1p, pairs, pairs, gates, mod3p, g2, b2]
        x_specs = [
            pl.BlockSpec((1, ts, d), lambda s: (first_half(s) // ns, first_half(s) % ns, 0)),
            pl.BlockSpec((1, ts, d // 2), lambda s: (0, first_half(s), 0)),
            pl.BlockSpec((1, ts, d // 2), lambda s: (1, first_half(s), 0)),
            pl.BlockSpec((ts, TOP_K), lambda s: (first_half(s), 0)),
            pl.BlockSpec((1, 1, 6 * d), lambda s: (first_half(s) // ns + mb0, 0, 0)),
            const(g2.shape), const(b2.shape),
        ]
    in_specs = x_specs + [
        pl.BlockSpec((1, 1, 6 * d), lambda s: (first_half(s) // ns + mb0, 0, 0)),
        pl.BlockSpec((1, 1, 6 * d), lambda s: (second_half(s) // ns + mb0, 0, 0)),
        layer_weights(w_in.shape), const(scw.shape), const(cvw.shape), const(cvb.shape),
        const(cvg.shape), const(cvbeta.shape), const(sgg.shape), const(sgbeta.shape),
        const(sgw.shape), const(sgbt.shape), layer_weights(wbr.shape), layer_weights(wg.shape),
        const(bg.shape), layer_weights(wo.shape), const(ln1g.shape), const(ln1b.shape),
        const(wrt.shape), const(brc.shape),
    ]
    out_specs = [
        pl.BlockSpec((1, ts, d), lambda s: (second_half(s) // ns, second_half(s) % ns, 0)),
        pl.BlockSpec((1, ts, d // 2), lambda s: (second_half(s) // ns, second_half(s) % ns, 0)),
        pl.BlockSpec((SUBLANES, ts), lambda s: (0, second_half(s))),
        pl.BlockSpec((SUBLANES, ts), lambda s: (0, second_half(s))),
        pl.BlockSpec((N_EXPERTS, LANES), lambda s: (0, 0)),
    ]
    out_shape = [
        jax.ShapeDtypeStruct((bsz, seq, d), F32),
        jax.ShapeDtypeStruct((bsz, seq, d // 2), jnp.uint32),
        jax.ShapeDtypeStruct((SUBLANES, n_tok), jnp.int32),
        jax.ShapeDtypeStruct((SUBLANES, n_tok), F32),
        jax.ShapeDtypeStruct((N_EXPERTS, LANES), F32),
    ]
    return pl.pallas_call(
        functools.partial(_mixer_kernel, ns, len(x_args)),
        grid=(n_tiles + 1,),
        in_specs=in_specs,
        out_specs=out_specs,
        out_shape=out_shape,
        scratch_shapes=[
            pltpu.VMEM((SC_HALO + ts, W_BRANCH), F32),
            pltpu.VMEM((CV_HALO + ts, PROJ_COLS), F32),
            pltpu.VMEM((CV_HALO + ts, PROJ_COLS), F32),
            pltpu.VMEM((ts, W_BRANCH), F32),
            pltpu.VMEM((3, ts, W_BRANCH), BF16),
            pltpu.VMEM((ts, 3 * W_BRANCH), F32),
            pltpu.VMEM((ts, 2 * W_BRANCH), F32),
            pltpu.VMEM((ts, 3 * D_MODEL), F32),
            pltpu.VMEM((ts, d), BF16),
            pltpu.VMEM((ts, d), F32),
            pltpu.VMEM((N_EXPERTS, LANES), F32),
        ],
        compiler_params=pltpu.CompilerParams(
            dimension_semantics=("arbitrary",),
            vmem_limit_bytes=MIXER_VMEM_LIMIT),
    )(*x_args, mod3, mod3, w_in, scw, cvw, cvb, cvg, cvbeta, sgg, sgbeta, sgw, sgbt, wbr, wg, bg,
      wo, ln1g, ln1b, wrt, brc)


def _sc_workers():
    info = plsc.get_sparse_core_info()
    return info.num_cores, info.num_cores * info.num_subcores


def _sc_scatter_rows(rows, dest_a, dest_b, n_out):
    n, d = rows.shape
    nc, nw = _sc_workers()
    per_w = n // nw
    n_win = per_w // SC_WINDOW
    ia = dest_a.reshape(nw, n_win, SC_WINDOW)
    ib = dest_b.reshape(nw, n_win, SC_WINDOW)
    mesh = plsc.VectorSubcoreMesh(core_axis_name="c", subcore_axis_name="s")

    @functools.partial(
        pl.kernel, mesh=mesh,
        out_type=jax.ShapeDtypeStruct((n_out, d), rows.dtype),
        scratch_types=[
            pltpu.VMEM((n_win, SC_WINDOW), jnp.int32),
            pltpu.VMEM((n_win, SC_WINDOW), jnp.int32),
            pltpu.VMEM((SC_WINDOW, d), rows.dtype),
        ],
    )
    def scatter(rows_hbm, ia_hbm, ib_hbm, out_hbm, ia_v, ib_v, rows_v):
        wid = lax.axis_index("s") * nc + lax.axis_index("c")
        pltpu.sync_copy(ia_hbm.at[wid], ia_v)
        pltpu.sync_copy(ib_hbm.at[wid], ib_v)
        base = wid * per_w

        @pl.loop(0, n_win)
        def _(j):
            pltpu.sync_copy(rows_hbm.at[pl.ds(base + j * SC_WINDOW, SC_WINDOW)], rows_v)
            pltpu.sync_copy(rows_v, out_hbm.at[ia_v.at[j]])
            pltpu.sync_copy(rows_v, out_hbm.at[ib_v.at[j]])

    return scatter(rows, ia, ib)


def _sc_gather_rows(table, idx):
    n = idx.shape[0]
    d = table.shape[1]
    nc, nw = _sc_workers()
    per_w = n // nw
    n_win = per_w // SC_WINDOW
    idx3 = idx.reshape(nw, n_win, SC_WINDOW)
    mesh = plsc.VectorSubcoreMesh(core_axis_name="c", subcore_axis_name="s")

    @functools.partial(
        pl.kernel, mesh=mesh,
        out_type=jax.ShapeDtypeStruct((n, d), table.dtype),
        scratch_types=[
            pltpu.VMEM((n_win, SC_WINDOW), jnp.int32),
            pltpu.VMEM((SC_WINDOW, d), table.dtype),
        ],
    )
    def gather(table_hbm, idx_hbm, out_hbm, idx_v, rows_v):
        wid = lax.axis_index("s") * nc + lax.axis_index("c")
        pltpu.sync_copy(idx_hbm.at[wid], idx_v)
        base = wid * per_w

        @pl.loop(0, n_win)
        def _(j):
            pltpu.sync_copy(table_hbm.at[idx_v.at[j]], rows_v)
            pltpu.sync_copy(rows_v, out_hbm.at[pl.ds(base + j * SC_WINDOW, SC_WINDOW)])

    return gather(table, idx3)


def _expert_kernel(layer, be_ref, nused_ref, x_ref, w1_hbm, w3_hbm, w2_hbm, o_ref,
                   w1f, w3f, w2f, w1s, w3s, w2s, sems, group_ref):
    i = pl.program_id(0)
    n_used = nused_ref[0]
    e_now = be_ref[i]

    def fetch(e, slot):
        return [pltpu.make_async_copy(w_hbm.at[layer, e], w_f.at[slot], sems.at[k, slot])
                for k, (w_hbm, w_f) in enumerate(((w1_hbm, w1f), (w3_hbm, w3f), (w2_hbm, w2f)))]

    @pl.when(i == 0)
    def _():
        group_ref[0] = 0
        for copy in fetch(e_now, 0):
            copy.start()

    first_of_group = jnp.logical_and(
        i < n_used, jnp.logical_or(i == 0, e_now != be_ref[jnp.maximum(i - 1, 0)]))

    @pl.when(first_of_group)
    def _():
        slot = group_ref[0] % 2
        for copy in fetch(e_now, slot):
            copy.wait()
        w1s[...] = w1f[slot].astype(BF16)
        w3s[...] = w3f[slot].astype(BF16)
        w2s[...] = w2f[slot].astype(BF16)
        last = be_ref.shape[0] - 1
        j = lax.while_loop(
            lambda j: jnp.logical_and(j < n_used, be_ref[jnp.minimum(j, last)] == e_now),
            lambda j: j + 1, i + 1)
        next_e = be_ref[jnp.minimum(j, last)]

        @pl.when(j < n_used)
        def _():
            for copy in fetch(next_e, 1 - slot):
                copy.start()

        group_ref[0] = group_ref[0] + 1

    @pl.when(i < n_used)
    def _():
        lo, hi = _unpack_bf16_pairs(x_ref[...])
        xb = jnp.concatenate([lo, hi], axis=1).astype(BF16)
        a = _dot(xb, w1s[...])
        b = _dot(xb, w3s[...])
        for r0 in range(0, MOE_ROWS, MOE_SLAB):
            rs = slice(r0, r0 + MOE_SLAB)
            h = a[rs] * jax.nn.sigmoid(a[rs]) * b[rs]
            o_ref[rs, :] = _pack_bf16_pairs(_dot(h.astype(BF16), w2s[...]))


def _experts(layer, buf, block_e, n_used, w1, w3, w2):
    n_rows, dp = buf.shape
    nb = n_rows // MOE_ROWS
    d, fe = w1.shape[-2:]

    def row_map(i, be, nu):
        return (jnp.minimum(i, nu[0] - 1), 0)

    grid_spec = pltpu.PrefetchScalarGridSpec(
        num_scalar_prefetch=2,
        grid=(nb,),
        in_specs=[
            pl.BlockSpec((MOE_ROWS, dp), row_map),
            pl.BlockSpec(memory_space=pl.ANY),
            pl.BlockSpec(memory_space=pl.ANY),
            pl.BlockSpec(memory_space=pl.ANY),
        ],
        out_specs=pl.BlockSpec((MOE_ROWS, dp), row_map),
        scratch_shapes=[
            pltpu.VMEM((2, d, fe), F32), pltpu.VMEM((2, d, fe), F32), pltpu.VMEM((2, fe, d), F32),
            pltpu.VMEM((d, fe), BF16), pltpu.VMEM((d, fe), BF16), pltpu.VMEM((fe, d), BF16),
            pltpu.SemaphoreType.DMA((3, 2)),
            pltpu.SMEM((1,), jnp.int32),
        ],
    )
    return pl.pallas_call(
        functools.partial(_expert_kernel, layer),
        grid_spec=grid_spec,
        out_shape=jax.ShapeDtypeStruct((n_rows, dp), jnp.uint32),
        compiler_params=pltpu.CompilerParams(dimension_semantics=("arbitrary",),
                                             vmem_limit_bytes=EXPERT_VMEM_LIMIT),
    )(block_e, n_used, buf, w1, w3, w2)


def _combine_kernel(x1_ref, ya_ref, yb_ref, gates_ref, mod_ref, g_ref, b_ref, *rest):
    o_ref = rest[-1]
    d = D_MODEL
    o_ref[0] = _moe_output(x1_ref[0], ya_ref[0], yb_ref[0], gates_ref[...],
                           mod_ref[0][:, 5 * d:6 * d], g_ref[...], b_ref[...])


def _combine(x1, pairs, gates, mod3, mb0, g, b, out_bsz, ob0, prev):
    bsz, seq, d = x1.shape
    ts = COMB_ROWS
    ns = seq // ts
    in_specs = [
        pl.BlockSpec((1, ts, d), lambda i, j: (i, j, 0)),
        pl.BlockSpec((1, ts, d // 2), lambda i, j: (0, i * ns + j, 0)),
        pl.BlockSpec((1, ts, d // 2), lambda i, j: (1, i * ns + j, 0)),
        pl.BlockSpec((ts, TOP_K), lambda i, j: (i * ns + j, 0)),
        pl.BlockSpec((1, 1, 6 * d), lambda i, j: (i + mb0, 0, 0)),
        pl.BlockSpec((1, d), lambda i, j: (0, 0)),
        pl.BlockSpec((1, d), lambda i, j: (0, 0)),
    ]
    args = [x1, pairs, pairs, gates, mod3, g, b]
    aliases = {}
    if prev is not None:
        in_specs.append(pl.BlockSpec(memory_space=pl.ANY))
        aliases = {len(args): 0}
        args.append(prev)
    return pl.pallas_call(
        _combine_kernel,
        grid=(bsz, ns),
        in_specs=in_specs,
        out_specs=pl.BlockSpec((1, ts, d), lambda i, j: (i + ob0, j, 0)),
        out_shape=jax.ShapeDtypeStruct((out_bsz, seq, d), F32),
        input_output_aliases=aliases,
        compiler_params=pltpu.CompilerParams(dimension_semantics=("arbitrary", "arbitrary")),
    )(*args)


def _plan_kernel(ri_ref, cnt_ref, dest_ref, blk_ref):
    n_tok = ri_ref.shape[1]
    counts = cnt_ref[:, 0:1].astype(jnp.int32)
    shift = MOE_ROWS.bit_length() - 1
    padded = lax.shift_left(lax.shift_right_logical(counts + (MOE_ROWS - 1), shift), shift)
    e_out = lax.broadcasted_iota(jnp.int32, (N_EXPERTS, N_EXPERTS), 0)
    e_in = lax.broadcasted_iota(jnp.int32, (N_EXPERTS, N_EXPERTS), 1)
    upto = (e_in <= e_out).astype(BF16)
    padded_f = jnp.broadcast_to(padded.astype(F32), (N_EXPERTS, LANES))
    pad_end = _dot(upto, padded_f.astype(BF16))
    pad_start = pad_end[:, 0:1] - padded.astype(F32)
    erow = lax.broadcasted_iota(jnp.int32, (N_EXPERTS, n_tok), 0)
    rows = []
    for k in range(TOP_K):
        start = jnp.sum(jnp.where(erow == ri_ref[k:k + 1, :], pad_start, 0.0), axis=0, keepdims=True)
        rows.append(start.astype(jnp.int32) + ri_ref[TOP_K + k:TOP_K + k + 1, :])
    rows.append(jnp.zeros((SUBLANES - TOP_K, n_tok), jnp.int32))
    dest_ref[...] = jnp.concatenate(rows, axis=0)
    block_start = (lax.broadcasted_iota(jnp.int32, (N_EXPERTS, LANES), 1) * MOE_ROWS).astype(F32)
    block_e = jnp.sum((block_start >= pad_end).astype(jnp.int32), axis=0, keepdims=True)
    n_used = lax.shift_right_logical(pad_end[N_EXPERTS - 1:N_EXPERTS, :].astype(jnp.int32), shift)
    blk_ref[...] = jnp.concatenate(
        [jnp.minimum(block_e, N_EXPERTS - 1), n_used,
         jnp.zeros((SUBLANES - 2, LANES), jnp.int32)], axis=0)


def _dispatch_plan(ri, cnt):
    n_tok = ri.shape[1]
    n_blocks = (n_tok * TOP_K + N_EXPERTS * (MOE_ROWS - 1) + MOE_ROWS - 1) // MOE_ROWS
    assert n_blocks <= LANES and MOE_ROWS & (MOE_ROWS - 1) == 0
    dest, blk = pl.pallas_call(
        _plan_kernel,
        out_shape=[jax.ShapeDtypeStruct((SUBLANES, n_tok), jnp.int32),
                   jax.ShapeDtypeStruct((SUBLANES, LANES), jnp.int32)],
    )(ri, cnt)
    return dest, blk[0, 0:n_blocks], blk[1, 0:1], n_blocks * MOE_ROWS


def kernel(x, c, w_ada, b_ada, w_in, sc_conv, cv_conv, cv_conv_b, cv_ln_g, cv_ln_b, sg_ln_g, sg_ln_b, sg_w, sg_b, w_branch, w_gate, b_gate, w_o, ln1_g, ln1_b, w_router, b_router, w1, w3, w2, ln2_g, ln2_b):
    bsz, seq, d = x.shape
    mod3 = _ada(c, w_ada, b_ada).reshape(DEPTH, bsz, 1, 6 * d)
    wrt = w_router.T
    brc = b_router.reshape(N_EXPERTS, 1)
    w_in_p = _pack_weights(w_in)
    w_gate_p = _pack_weights(jnp.transpose(w_gate, (0, 2, 1, 3)).reshape(DEPTH * 3, d, d)).reshape(
        DEPTH, 3, d // 2, d)
    w_o_p = _pack_weights(w_o)
    w_branch_p = _pack_weights(w_branch.reshape(DEPTH * 3, W_BRANCH, d)).reshape(
        DEPTH, 3, W_BRANCH // 2, d)
    cb = bsz // N_CHAINS
    n_tok = cb * seq
    chains = [(x, h * cb) for h in range(N_CHAINS)]
    for l in range(DEPTH):
        last = l == DEPTH - 1
        result = None
        for h in range(N_CHAINS):
            x1, u2, ri, rf, cnt = _mixer(
                l, chains[h], mod3[l], h * cb, cb, w_in_p, sc_conv[l], cv_conv[l],
                cv_conv_b[l].reshape(1, -1), cv_ln_g[l].reshape(1, -1), cv_ln_b[l].reshape(1, -1),
                sg_ln_g[l].reshape(1, -1), sg_ln_b[l].reshape(1, -1), sg_w[l], sg_b[l].T,
                w_branch_p, w_gate_p, b_gate[l],
                w_o_p, ln1_g[l].reshape(1, -1), ln1_b[l].reshape(1, -1), wrt, brc)
            dest, block_e, n_used, n_rows = _dispatch_plan(ri, cnt)
            buf = _sc_scatter_rows(u2.reshape(n_tok, d // 2), dest[0], dest[1], n_rows)
            obuf = _experts(l, buf, block_e, n_used, w1, w3, w2)
            pair_idx = dest[0:TOP_K].reshape(TOP_K * n_tok)
            pairs = _sc_gather_rows(obuf, pair_idx).reshape(TOP_K, n_tok, d // 2)
            gates = rf[0:TOP_K].T
            g2, b2 = ln2_g[l].reshape(1, -1), ln2_b[l].reshape(1, -1)
            if last:
                result = _combine(x1, pairs, gates, mod3[l], h * cb, g2, b2, bsz, h * cb, result)
            else:
                chains[h] = (x1, pairs, gates, mod3[l], g2, b2)
    return result
```

```python
import functools

import jax
import jax.numpy as jnp
from jax import lax
from jax.experimental import pallas as pl
from jax.experimental.pallas import tpu as pltpu
from jax.experimental.pallas import tpu_sc as plsc

D_MODEL = 1024
DEPTH = 2
W_BRANCH = 1024
SC_KERNEL = 3
CV_KERNEL = 31
CHUNK = 128
SG_HEADS = 8
N_EXPERTS = 16
N_GROUPS = 4
EXPERTS_PER_GROUP = N_EXPERTS // N_GROUPS
TOP_K = 2
D_EXPERT = 512
ALPHA = (2.0 * DEPTH) ** 0.25
LN_EPS = 1e-5

F32 = jnp.float32
BF16 = jnp.bfloat16

V7X_VMEM_BYTES = 64 * 1024 * 1024
MIXER_VMEM_LIMIT = V7X_VMEM_BYTES - 6 * 1024 * 1024
EXPERT_VMEM_LIMIT = V7X_VMEM_BYTES // 2
SUBLANES = 8
LANES = 128

MIX_ROWS = 256
SC_HALO = SUBLANES
CV_HALO = 32
CONV_ROWS = 128
TIE_LAG = 2
TIE_FREE_JOBS = 4
CONV_COLS = 128
PROJ_COLS = 512
PACK_ROWS = 256
MOE_ROWS = 512
MOE_SLAB = 256
COMB_ROWS = 512
SC_WINDOW = 32
N_CHAINS = 2


def _dot(a, b):
    return jnp.dot(a, b, preferred_element_type=F32)


def _pack_bf16_pairs(v):
    m = v.shape[1] // 2
    lo = lax.bitcast_convert_type(v[:, 0:m].astype(BF16).astype(F32), jnp.uint32)
    hi = lax.bitcast_convert_type(v[:, m:2 * m].astype(BF16).astype(F32), jnp.uint32)
    return jnp.bitwise_or(jnp.bitwise_and(hi, jnp.uint32(0xFFFF0000)),
                          lax.shift_right_logical(lo, jnp.uint32(16)))


def _unpack_bf16_pairs(w):
    lo = lax.bitcast_convert_type(lax.shift_left(w, jnp.uint32(16)), F32)
    hi = lax.bitcast_convert_type(jnp.bitwise_and(w, jnp.uint32(0xFFFF0000)), F32)
    return lo, hi


def _layer_norm(v, g, b):
    mu = jnp.mean(v, axis=-1, keepdims=True)
    vc = v - mu
    var = jnp.mean(vc * vc, axis=-1, keepdims=True)
    return vc * lax.rsqrt(var + LN_EPS) * g + b


def _ada_kernel(c_ref, w_ref, b_ref, o_ref):
    c = c_ref[...]
    c_act = c * jax.nn.sigmoid(c)
    o_ref[0] = jnp.dot(c_act, w_ref[0], preferred_element_type=F32,
                       precision=lax.Precision.HIGHEST) + b_ref[0]


def _ada(c, w_ada, b_ada):
    bsz, d = c.shape
    n = w_ada.shape[-1]
    tn = 1536
    return pl.pallas_call(
        _ada_kernel,
        grid=(DEPTH, n // tn),
        in_specs=[
            pl.BlockSpec((bsz, d), lambda l, j: (0, 0)),
            pl.BlockSpec((1, d, tn), lambda l, j: (l, 0, j)),
            pl.BlockSpec((1, 1, tn), lambda l, j: (l, 0, j)),
        ],
        out_specs=pl.BlockSpec((1, bsz, tn), lambda l, j: (l, 0, j)),
        out_shape=jax.ShapeDtypeStruct((DEPTH, bsz, n), F32),
    )(c, w_ada, b_ada.reshape(DEPTH, 1, n))


def _top2_of4(rows):
    m1 = rows[0]
    i1 = jnp.zeros(rows[0].shape, jnp.int32)
    for k in range(1, 4):
        gt = rows[k] > m1
        m1 = jnp.where(gt, rows[k], m1)
        i1 = jnp.where(gt, k, i1)
    m2 = jnp.full(rows[0].shape, -jnp.inf, F32)
    i2 = jnp.zeros(rows[0].shape, jnp.int32)
    for k in range(4):
        cand = jnp.where(i1 == k, -jnp.inf, rows[k])
        gt = cand > m2
        m2 = jnp.where(gt, cand, m2)
        i2 = jnp.where(gt, k, i2)
    return m1, i1, m2, i2


def _zero_after(v):
    u = lax.bitcast_convert_type(v, jnp.uint32)
    u = lax.shift_right_logical(lax.shift_right_logical(u, jnp.uint32(16)), jnp.uint32(16))
    return lax.bitcast_convert_type(u, F32)


def _conv31_chunk(cvw_ref, cvbufs, cvout, r0, c0, tie):
    cs = slice(c0, c0 + CONV_COLS)
    cvbuf = cvbufs[c0 // PROJ_COLS]
    bs = slice(c0 % PROJ_COLS, c0 % PROJ_COLS + CONV_COLS)
    acc = None
    for r in range(SUBLANES):
        lead = SUBLANES if r else 0
        part = None
        for m in range((CV_KERNEL - 1 - r) // SUBLANES + 1):
            k = CV_KERNEL - 1 - (SUBLANES * m + r)
            start = CV_HALO + r0 - lead - SUBLANES * m
            w_row = cvw_ref[k:k + 1, cs]
            if tie is not None and acc is None and part is None:
                w_row = w_row + tie
            term = w_row * cvbuf[start:start + lead + CONV_ROWS, bs]
            part = term if part is None else part + term
        part = part[lead - r:lead - r + CONV_ROWS]
        acc = part if acc is None else acc + part
    cvout[r0:r0 + CONV_ROWS, cs] = acc


def _bf16_weights(packed):
    return pltpu.bitcast(packed, BF16)


def _pack_weights_kernel(w_ref, o_ref):
    o_ref[0] = pltpu.bitcast(w_ref[0].astype(BF16), jnp.uint32)


def _pack_weights(w):
    g, k, n = w.shape
    kb = PACK_ROWS
    return pl.pallas_call(
        _pack_weights_kernel,
        grid=(g, k // kb),
        in_specs=[pl.BlockSpec((1, kb, n), lambda i, j: (i, j, 0))],
        out_specs=pl.BlockSpec((1, kb // 2, n), lambda i, j: (i, j, 0)),
        out_shape=jax.ShapeDtypeStruct((g, k // 2, n), jnp.uint32),
        compiler_params=pltpu.CompilerParams(vmem_limit_bytes=EXPERT_VMEM_LIMIT),
    )(w)


def _moe_output(x1, ya_packed, yb_packed, gates, gate2, g, b):
    ya = jnp.concatenate(_unpack_bf16_pairs(ya_packed), axis=1)
    yb = jnp.concatenate(_unpack_bf16_pairs(yb_packed), axis=1)
    h = gates[:, 0:1] * ya + gates[:, 1:2] * yb
    return _layer_norm(ALPHA * x1 + gate2 * h, g, b)


def _mixer_kernel(tiles_per_seq, n_x_refs, *refs):
    last_step = pl.num_programs(0) - 1

    @pl.when(pl.program_id(0) < last_step)
    def _():
        _mixer_step(tiles_per_seq, n_x_refs, True, *refs)

    @pl.when(pl.program_id(0) == last_step)
    def _():
        _mixer_step(tiles_per_seq, n_x_refs, False, *refs)


def _mixer_step(tiles_per_seq, n_x_refs, run_first_half, *refs):
    x_refs = refs[:n_x_refs]
    (mod_ref, modt_ref, w_in_ref, scw_ref, cvw_ref, cvb_ref, cvg_ref, cvbeta_ref, sgg_ref,
     sgbeta_ref, sgw_ref, sgbt_ref, wbr_ref, wg_ref, bg_ref, wo_ref, ln1g_ref, ln1b_ref, wrt_ref,
     brc_ref,
     x1_ref, u2_ref, ri_ref, rf_ref, cnt_ref,
     qbuf, cvbuf0, cvbuf1, cvout, ybuf, pa_buf, pc_buf, gl_buf, mg_buf, xs_buf,
     base_ref) = refs[n_x_refs:]
    cvbufs = (cvbuf0, cvbuf1)
    ts = MIX_ROWS
    d = D_MODEL
    wb = W_BRANCH
    step = pl.program_id(0)
    tile = jnp.minimum(step, pl.num_programs(0) - 2)
    first_tile = tile % tiles_per_seq == 0

    @pl.when(step == 0)
    def _():
        base_ref[...] = jnp.zeros_like(base_ref)
        mg_buf[...] = jnp.zeros_like(mg_buf)
        xs_buf[...] = jnp.zeros_like(xs_buf)

    def w_in(c0):
        return _bf16_weights(w_in_ref[0, :, c0:c0 + PROJ_COLS])

    def w_gate(c0):
        n, col = divmod(c0, d)
        return _bf16_weights(wg_ref[0, n, :, col:col + PROJ_COLS])

    if not run_first_half:
        for _ in _mixer_tail(step, mg_buf, xs_buf, modt_ref, wo_ref, ln1g_ref, ln1b_ref, wrt_ref,
                             brc_ref, x1_ref, u2_ref, ri_ref, rf_ref, cnt_ref, base_ref):
            pass
        return

    @pl.when(first_tile)
    def _():
        qbuf[0:SC_HALO, :] = jnp.zeros((SC_HALO, wb), F32)
        for cvbuf in cvbufs:
            cvbuf[0:CV_HALO, :] = jnp.zeros((CV_HALO, PROJ_COLS), F32)

    if n_x_refs == 1:
        x = x_refs[0][0]
    else:
        x1p_ref, ya_ref, yb_ref, gates_ref, modp_ref, g2_ref, b2_ref = x_refs
        x = _moe_output(x1p_ref[0], ya_ref[0], yb_ref[0], gates_ref[...],
                        modp_ref[0][:, 5 * d:6 * d], g2_ref[...], b2_ref[...])
    mod = mod_ref[0]
    shift1, scale1 = mod[:, 0:d], mod[:, d:2 * d]
    ub = (x * (1.0 + scale1) + shift1).astype(BF16)

    def glu_block(c0):
        a = _dot(ub, w_in(3 * wb + c0))
        g = _dot(ub, w_in(4 * wb + c0))
        cvbufs[c0 // PROJ_COLS][CV_HALO:CV_HALO + ts, :] = a * jax.nn.sigmoid(g)

    tail = _mixer_tail(step, mg_buf, xs_buf, modt_ref, wo_ref, ln1g_ref, ln1b_ref, wrt_ref,
                       brc_ref, x1_ref, u2_ref, ri_ref, rf_ref, cnt_ref, base_ref)
    next(tail)
    glu_block(0)
    ties = {}

    def add_tie(chunk, tie):
        ties[chunk] = tie + ties[chunk] if chunk in ties else tie

    branch_z = {}

    def mixer_a():
        qbuf[SC_HALO:SC_HALO + ts, :] = pa_buf[:, wb:2 * wb] * pa_buf[:, 2 * wb:3 * wb]
        conv = scw_ref[SC_KERNEL - 1:SC_KERNEL, :] * qbuf[SC_HALO:SC_HALO + ts, :]
        for k in range(SC_KERNEL - 1):
            off = SC_HALO - (SC_KERNEL - 1) + k
            conv = conv + scw_ref[k:k + 1, :] * qbuf[off:off + ts, :]
        ybuf[0] = (pa_buf[:, 0:wb] * conv).astype(BF16)
        qbuf[0:SC_HALO, :] = qbuf[ts:ts + SC_HALO, :]
        branch_z[0] = _dot(ybuf[0], _bf16_weights(wbr_ref[0, 0]))

    def mixer_c():
        gu = jax.nn.gelu(pc_buf[:, 0:wb])
        gv = _layer_norm(jax.nn.gelu(pc_buf[:, wb:2 * wb]), sgg_ref[...],
                         sgbeta_ref[...]).astype(BF16)
        row = lax.broadcasted_iota(jnp.int32, (CHUNK, CHUNK), 0)
        col = lax.broadcasted_iota(jnp.int32, (CHUNK, CHUNK), 1)
        hd = wb // SG_HEADS
        for h in range(SG_HEADS):
            wm = jnp.where(row >= col, sgw_ref[h], 0.0).astype(BF16)
            bias = sgbt_ref[:, h:h + 1]
            for n in range(ts // CHUNK):
                rs = slice(n * CHUNK, (n + 1) * CHUNK)
                cs = slice(h * hd, (h + 1) * hd)
                mixed = _dot(wm, gv[rs, cs]) + bias
                ybuf[2, rs, cs] = (gu[rs, cs] * mixed).astype(BF16)
        branch_z[2] = _dot(ybuf[2], _bf16_weights(wbr_ref[0, 2]))

    def glu_rest():
        for c0 in range(PROJ_COLS, wb, PROJ_COLS):
            glu_block(c0)

    def tail_stage():
        next(tail, None)

    mxu_jobs = ([(pa_buf, w_in, c0, c0) for c0 in range(0, 3 * wb, PROJ_COLS)]
                + [(pc_buf, w_in, c0, 5 * wb + c0) for c0 in range(0, 2 * wb, PROJ_COLS)]
                + [(gl_buf, w_gate, c0, c0) for c0 in range(0, 3 * d, PROJ_COLS)])
    glu_rest()
    after_job = {7: tail_stage, 12: tail_stage}
    conv_jobs = [(r0, c0) for c0 in range(0, wb, CONV_COLS) for r0 in range(0, ts, CONV_ROWS)]
    chunks_per_job = len(conv_jobs) // len(mxu_jobs)
    for c, conv_job in enumerate(conv_jobs):
        _conv31_chunk(cvw_ref, cvbufs, cvout, *conv_job, ties.get(c))
        if (c + 1) % chunks_per_job:
            continue
        i = c // chunks_per_job
        dst, weights, dc, wc = mxu_jobs[i]
        res = _dot(ub, weights(wc))
        dst[:, dc:dc + PROJ_COLS] = res
        if i < len(mxu_jobs) - TIE_FREE_JOBS:
            add_tie(c + TIE_LAG, _zero_after(res[ts - 1:ts, PROJ_COLS - CONV_COLS:PROJ_COLS]))
        if i in after_job:
            after_job[i]()
    for cvbuf in cvbufs:
        cvbuf[0:CV_HALO, :] = cvbuf[ts:ts + CV_HALO, :]

    mixer_a()
    cv = _layer_norm(cvout[...] + cvb_ref[...], cvg_ref[...], cvbeta_ref[...])
    ybuf[1] = (cv * jax.nn.sigmoid(cv)).astype(BF16)
    branch_z[1] = _dot(ybuf[1], _bf16_weights(wbr_ref[0, 1]))
    mixer_c()

    merged = None
    for n in range(3):
        gated = jax.nn.sigmoid(gl_buf[:, n * d:(n + 1) * d] + bg_ref[n:n + 1, :]) * branch_z[n]
        merged = gated if merged is None else merged + gated
    mg_buf[...] = merged.astype(BF16)
    xs_buf[...] = x


def _mixer_tail(step, mg_buf, xs_buf, modt_ref, wo_ref, ln1g_ref, ln1b_ref, wrt_ref, brc_ref,
                x1_ref, u2_ref, ri_ref, rf_ref, cnt_ref, base_ref):
    ts = MIX_ROWS
    d = D_MODEL
    modt = modt_ref[0]
    gate1, shift2, scale2 = modt[:, 2 * d:3 * d], modt[:, 3 * d:4 * d], modt[:, 4 * d:5 * d]
    hmix = _dot(mg_buf[...], _bf16_weights(wo_ref[0]))
    x1 = _layer_norm(ALPHA * xs_buf[...] + gate1 * hmix, ln1g_ref[...], ln1b_ref[...])
    x1_ref[0] = x1
    u2 = x1 * (1.0 + scale2) + shift2
    u2_ref[0] = _pack_bf16_pairs(u2)
    yield

    logits = lax.dot_general(wrt_ref[...], u2, (((1,), (1,)), ((), ())),
                             preferred_element_type=F32,
                             precision=lax.Precision.HIGHEST)
    mx = jnp.max(logits, axis=0, keepdims=True)
    ex = jnp.exp(logits - mx)
    scores = ex / jnp.sum(ex, axis=0, keepdims=True)
    sel = scores + brc_ref[...]
    tops = []
    for g in range(N_GROUPS):
        rows = [sel[g * EXPERTS_PER_GROUP + k:g * EXPERTS_PER_GROUP + k + 1, :]
                for k in range(EXPERTS_PER_GROUP)]
        tops.append(_top2_of4(rows))
    best = tops[0][0] + tops[0][2]
    g_idx = jnp.zeros(best.shape, jnp.int32)
    loc1, loc2 = tops[0][1], tops[0][3]
    for g in range(1, N_GROUPS):
        gs = tops[g][0] + tops[g][2]
        gt = gs > best
        best = jnp.where(gt, gs, best)
        g_idx = jnp.where(gt, g, g_idx)
        loc1 = jnp.where(gt, tops[g][1], loc1)
        loc2 = jnp.where(gt, tops[g][3], loc2)
    e0 = g_idx * EXPERTS_PER_GROUP + loc1
    e1 = g_idx * EXPERTS_PER_GROUP + loc2
    erow = lax.broadcasted_iota(jnp.int32, (N_EXPERTS, ts), 0)
    is0 = erow == e0
    is1 = erow == e1
    s0 = jnp.sum(jnp.where(is0, scores, 0.0), axis=0, keepdims=True)
    s1 = jnp.sum(jnp.where(is1, scores, 0.0), axis=0, keepdims=True)
    ssum = s0 + s1
    yield

    onehot = jnp.logical_or(is0, is1).astype(BF16)
    src = lax.broadcasted_iota(jnp.int32, (ts, ts), 0)
    dst = lax.broadcasted_iota(jnp.int32, (ts, ts), 1)
    earlier = (src < dst).astype(BF16)
    prior = _dot(onehot, earlier) + base_ref[:, 0:1]
    r0 = jnp.sum(jnp.where(is0, prior, 0.0), axis=0, keepdims=True)
    r1 = jnp.sum(jnp.where(is1, prior, 0.0), axis=0, keepdims=True)
    counts = jnp.sum(onehot.astype(F32), axis=1, keepdims=True)
    base_ref[...] = base_ref[...] + jnp.where(step > 0, counts, 0.0)

    zi = jnp.zeros((SUBLANES - 4, ts), jnp.int32)
    ri_ref[...] = jnp.concatenate([e0, e1, r0.astype(jnp.int32), r1.astype(jnp.int32), zi], axis=0)
    zf = jnp.zeros((SUBLANES - 2, ts), F32)
    rf_ref[...] = jnp.concatenate([s0 / ssum, s1 / ssum, zf], axis=0)
    cnt_ref[...] = base_ref[...]


def _mixer(layer, xsrc, mod3, mb0, bsz, w_in, scw, cvw, cvb, cvg, cvbeta, sgg, sgbeta, sgw, sgbt,
           wbr, wg, bg, wo, ln1g, ln1b, wrt, brc):
    _, seq, d = xsrc[0].shape
    ts = MIX_ROWS
    ns = seq // ts
    n_tok = bsz * seq

    n_tiles = bsz * ns

    def const(shape):
        zeros = (0,) * len(shape)
        return pl.BlockSpec(shape, lambda s: zeros, pipeline_mode=pl.Buffered(1))

    def first_half(s):
        return jnp.minimum(s, n_tiles - 1)

    def second_half(s):
        return jnp.maximum(s - 1, 0)

    def layer_weights(shape):
        block = (1,) + tuple(shape[1:])
        index = (layer,) + (0,) * (len(shape) - 1)
        return pl.BlockSpec(block, lambda s: index, pipeline_mode=pl.Buffered(1))

    if len(xsrc) == 2:
        x, xb0 = xsrc
        x_args = [x]
        x_specs = [pl.BlockSpec((1, ts, d),
                                lambda s: (first_half(s) // ns + xb0, first_half(s) % ns, 0))]
    else:
        x1p, pairs, gates, mod3p, g2, b2 = xsrc
        x_args = [x1p, pairs, pairs, gates, mod3p, g2, b2]
        x_specs = [
            pl.BlockSpec((1, ts, d), lambda s: (first_half(s) // ns, first_half(s) % ns, 0)),
            pl.BlockSpec((1, ts, d // 2), lambda s: (0, first_half(s), 0)),
            pl.BlockSpec((1, ts, d // 2), lambda s: (1, first_half(s), 0)),
            pl.BlockSpec((ts, TOP_K), lambda s: (first_half(s), 0)),
            pl.BlockSpec((1, 1, 6 * d), lambda s: (first_half(s) // ns + mb0, 0, 0)),
            const(g2.shape), const(b2.shape),
        ]
    in_specs = x_specs + [
        pl.BlockSpec((1, 1, 6 * d), lambda s: (first_half(s) // ns + mb0, 0, 0)),
        pl.BlockSpec((1, 1, 6 * d), lambda s: (second_half(s) // ns + mb0, 0, 0)),
        layer_weights(w_in.shape), const(scw.shape), const(cvw.shape), const(cvb.shape),
        const(cvg.shape), const(cvbeta.shape), const(sgg.shape), const(sgbeta.shape),
        const(sgw.shape), const(sgbt.shape), layer_weights(wbr.shape), layer_weights(wg.shape),
        const(bg.shape), layer_weights(wo.shape), const(ln1g.shape), const(ln1b.shape),
        const(wrt.shape), const(brc.shape),
    ]
    out_specs = [
        pl.BlockSpec((1, ts, d), lambda s: (second_half(s) // ns, second_half(s) % ns, 0)),
        pl.BlockSpec((1, ts, d // 2), lambda s: (second_half(s) // ns, second_half(s) % ns, 0)),
        pl.BlockSpec((SUBLANES, ts), lambda s: (0, second_half(s))),
        pl.BlockSpec((SUBLANES, ts), lambda s: (0, second_half(s))),
        pl.BlockSpec((N_EXPERTS, LANES), lambda s: (0, 0)),
    ]
    out_shape = [
        jax.ShapeDtypeStruct((bsz, seq, d), F32),
        jax.ShapeDtypeStruct((bsz, seq, d // 2), jnp.uint32),
        jax.ShapeDtypeStruct((SUBLANES, n_tok), jnp.int32),
        jax.ShapeDtypeStruct((SUBLANES, n_tok), F32),
        jax.ShapeDtypeStruct((N_EXPERTS, LANES), F32),
    ]
    return pl.pallas_call(
        functools.partial(_mixer_kernel, ns, len(x_args)),
        grid=(n_tiles + 1,),
        in_specs=in_specs,
        out_specs=out_specs,
        out_shape=out_shape,
        scratch_shapes=[
            pltpu.VMEM((SC_HALO + ts, W_BRANCH), F32),
            pltpu.VMEM((CV_HALO + ts, PROJ_COLS), F32),
            pltpu.VMEM((CV_HALO + ts, PROJ_COLS), F32),
            pltpu.VMEM((ts, W_BRANCH), F32),
            pltpu.VMEM((3, ts, W_BRANCH), BF16),
            pltpu.VMEM((ts, 3 * W_BRANCH), F32),
            pltpu.VMEM((ts, 2 * W_BRANCH), F32),
            pltpu.VMEM((ts, 3 * D_MODEL), F32),
            pltpu.VMEM((ts, d), BF16),
            pltpu.VMEM((ts, d), F32),
            pltpu.VMEM((N_EXPERTS, LANES), F32),
        ],
        compiler_params=pltpu.CompilerParams(
            dimension_semantics=("arbitrary",),
            vmem_limit_bytes=MIXER_VMEM_LIMIT),
    )(*x_args, mod3, mod3, w_in, scw, cvw, cvb, cvg, cvbeta, sgg, sgbeta, sgw, sgbt, wbr, wg, bg,
      wo, ln1g, ln1b, wrt, brc)


def _sc_workers():
    info = plsc.get_sparse_core_info()
    return info.num_cores, info.num_cores * info.num_subcores


def _sc_scatter_rows(rows, dest_a, dest_b, n_out):
    n, d = rows.shape
    nc, nw = _sc_workers()
    per_w = n // nw
    n_win = per_w // SC_WINDOW
    ia = dest_a.reshape(nw, n_win, SC_WINDOW)
    ib = dest_b.reshape(nw, n_win, SC_WINDOW)
    mesh = plsc.VectorSubcoreMesh(core_axis_name="c", subcore_axis_name="s")

    @functools.partial(
        pl.kernel, mesh=mesh,
        out_type=jax.ShapeDtypeStruct((n_out, d), rows.dtype),
        scratch_types=[
            pltpu.VMEM((n_win, SC_WINDOW), jnp.int32),
            pltpu.VMEM((n_win, SC_WINDOW), jnp.int32),
            pltpu.VMEM((SC_WINDOW, d), rows.dtype),
        ],
    )
    def scatter(rows_hbm, ia_hbm, ib_hbm, out_hbm, ia_v, ib_v, rows_v):
        wid = lax.axis_index("s") * nc + lax.axis_index("c")
        pltpu.sync_copy(ia_hbm.at[wid], ia_v)
        pltpu.sync_copy(ib_hbm.at[wid], ib_v)
        base = wid * per_w

        @pl.loop(0, n_win)
        def _(j):
            pltpu.sync_copy(rows_hbm.at[pl.ds(base + j * SC_WINDOW, SC_WINDOW)], rows_v)
            pltpu.sync_copy(rows_v, out_hbm.at[ia_v.at[j]])
            pltpu.sync_copy(rows_v, out_hbm.at[ib_v.at[j]])

    return scatter(rows, ia, ib)


def _sc_gather_rows(table, idx):
    n = idx.shape[0]
    d = table.shape[1]
    nc, nw = _sc_workers()
    per_w = n // nw
    n_win = per_w // SC_WINDOW
    idx3 = idx.reshape(nw, n_win, SC_WINDOW)
    mesh = plsc.VectorSubcoreMesh(core_axis_name="c", subcore_axis_name="s")

    @functools.partial(
        pl.kernel, mesh=mesh,
        out_type=jax.ShapeDtypeStruct((n, d), table.dtype),
        scratch_types=[
            pltpu.VMEM((n_win, SC_WINDOW), jnp.int32),
            pltpu.VMEM((SC_WINDOW, d), table.dtype),
        ],
    )
    def gather(table_hbm, idx_hbm, out_hbm, idx_v, rows_v):
        wid = lax.axis_index("s") * nc + lax.axis_index("c")
        pltpu.sync_copy(idx_hbm.at[wid], idx_v)
        base = wid * per_w

        @pl.loop(0, n_win)
        def _(j):
            pltpu.sync_copy(table_hbm.at[idx_v.at[j]], rows_v)
            pltpu.sync_copy(rows_v, out_hbm.at[pl.ds(base + j * SC_WINDOW, SC_WINDOW)])

    return gather(table, idx3)


def _expert_kernel(layer, be_ref, nused_ref, x_ref, w1_hbm, w3_hbm, w2_hbm, o_ref,
                   w1f, w3f, w2f, w1s, w3s, w2s, sems, group_ref):
    i = pl.program_id(0)
    n_used = nused_ref[0]
    e_now = be_ref[i]

    def fetch(e, slot):
        return [pltpu.make_async_copy(w_hbm.at[layer, e], w_f.at[slot], sems.at[k, slot])
                for k, (w_hbm, w_f) in enumerate(((w1_hbm, w1f), (w3_hbm, w3f), (w2_hbm, w2f)))]

    @pl.when(i == 0)
    def _():
        group_ref[0] = 0
        for copy in fetch(e_now, 0):
            copy.start()

    first_of_group = jnp.logical_and(
        i < n_used, jnp.logical_or(i == 0, e_now != be_ref[jnp.maximum(i - 1, 0)]))

    @pl.when(first_of_group)
    def _():
        slot = group_ref[0] % 2
        for copy in fetch(e_now, slot):
            copy.wait()
        w1s[...] = w1f[slot].astype(BF16)
        w3s[...] = w3f[slot].astype(BF16)
        w2s[...] = w2f[slot].astype(BF16)
        last = be_ref.shape[0] - 1
        j = lax.while_loop(
            lambda j: jnp.logical_and(j < n_used, be_ref[jnp.minimum(j, last)] == e_now),
            lambda j: j + 1, i + 1)
        next_e = be_ref[jnp.minimum(j, last)]

        @pl.when(j < n_used)
        def _():
            for copy in fetch(next_e, 1 - slot):
                copy.start()

        group_ref[0] = group_ref[0] + 1

    @pl.when(i < n_used)
    def _():
        lo, hi = _unpack_bf16_pairs(x_ref[...])
        xb = jnp.concatenate([lo, hi], axis=1).astype(BF16)
        a = _dot(xb, w1s[...])
        b = _dot(xb, w3s[...])
        for r0 in range(0, MOE_ROWS, MOE_SLAB):
            rs = slice(r0, r0 + MOE_SLAB)
            h = a[rs] * jax.nn.sigmoid(a[rs]) * b[rs]
            o_ref[rs, :] = _pack_bf16_pairs(_dot(h.astype(BF16), w2s[...]))


def _experts(layer, buf, block_e, n_used, w1, w3, w2):
    n_rows, dp = buf.shape
    nb = n_rows // MOE_ROWS
    d, fe = w1.shape[-2:]

    def row_map(i, be, nu):
        return (jnp.minimum(i, nu[0] - 1), 0)

    grid_spec = pltpu.PrefetchScalarGridSpec(
        num_scalar_prefetch=2,
        grid=(nb,),
        in_specs=[
            pl.BlockSpec((MOE_ROWS, dp), row_map),
            pl.BlockSpec(memory_space=pl.ANY),
            pl.BlockSpec(memory_space=pl.ANY),
            pl.BlockSpec(memory_space=pl.ANY),
        ],
        out_specs=pl.BlockSpec((MOE_ROWS, dp), row_map),
        scratch_shapes=[
            pltpu.VMEM((2, d, fe), F32), pltpu.VMEM((2, d, fe), F32), pltpu.VMEM((2, fe, d), F32),
            pltpu.VMEM((d, fe), BF16), pltpu.VMEM((d, fe), BF16), pltpu.VMEM((fe, d), BF16),
            pltpu.SemaphoreType.DMA((3, 2)),
            pltpu.SMEM((1,), jnp.int32),
        ],
    )
    return pl.pallas_call(
        functools.partial(_expert_kernel, layer),
        grid_spec=grid_spec,
        out_shape=jax.ShapeDtypeStruct((n_rows, dp), jnp.uint32),
        compiler_params=pltpu.CompilerParams(dimension_semantics=("arbitrary",),
                                             vmem_limit_bytes=EXPERT_VMEM_LIMIT),
    )(block_e, n_used, buf, w1, w3, w2)


def _combine_kernel(x1_ref, ya_ref, yb_ref, gates_ref, mod_ref, g_ref, b_ref, *rest):
    o_ref = rest[-1]
    d = D_MODEL
    o_ref[0] = _moe_output(x1_ref[0], ya_ref[0], yb_ref[0], gates_ref[...],
                           mod_ref[0][:, 5 * d:6 * d], g_ref[...], b_ref[...])


def _combine(x1, pairs, gates, mod3, mb0, g, b, out_bsz, ob0, prev):
    bsz, seq, d = x1.shape
    ts = COMB_ROWS
    ns = seq // ts
    in_specs = [
        pl.BlockSpec((1, ts, d), lambda i, j: (i, j, 0)),
        pl.BlockSpec((1, ts, d // 2), lambda i, j: (0, i * ns + j, 0)),
        pl.BlockSpec((1, ts, d // 2), lambda i, j: (1, i * ns + j, 0)),
        pl.BlockSpec((ts, TOP_K), lambda i, j: (i * ns + j, 0)),
        pl.BlockSpec((1, 1, 6 * d), lambda i, j: (i + mb0, 0, 0)),
        pl.BlockSpec((1, d), lambda i, j: (0, 0)),
        pl.BlockSpec((1, d), lambda i, j: (0, 0)),
    ]
    args = [x1, pairs, pairs, gates, mod3, g, b]
    aliases = {}
    if prev is not None:
        in_specs.append(pl.BlockSpec(memory_space=pl.ANY))
        aliases = {len(args): 0}
        args.append(prev)
    return pl.pallas_call(
        _combine_kernel,
        grid=(bsz, ns),
        in_specs=in_specs,
        out_specs=pl.BlockSpec((1, ts, d), lambda i, j: (i + ob0, j, 0)),
        out_shape=jax.ShapeDtypeStruct((out_bsz, seq, d), F32),
        input_output_aliases=aliases,
        compiler_params=pltpu.CompilerParams(dimension_semantics=("arbitrary", "arbitrary")),
    )(*args)


def _plan_kernel(ri_ref, cnt_ref, dest_ref, blk_ref):
    n_tok = ri_ref.shape[1]
    counts = cnt_ref[:, 0:1].astype(jnp.int32)
    shift = MOE_ROWS.bit_length() - 1
    padded = lax.shift_left(lax.shift_right_logical(counts + (MOE_ROWS - 1), shift), shift)
    e_out = lax.broadcasted_iota(jnp.int32, (N_EXPERTS, N_EXPERTS), 0)
    e_in = lax.broadcasted_iota(jnp.int32, (N_EXPERTS, N_EXPERTS), 1)
    upto = (e_in <= e_out).astype(BF16)
    padded_f = jnp.broadcast_to(padded.astype(F32), (N_EXPERTS, LANES))
    pad_end = _dot(upto, padded_f.astype(BF16))
    pad_start = pad_end[:, 0:1] - padded.astype(F32)
    erow = lax.broadcasted_iota(jnp.int32, (N_EXPERTS, n_tok), 0)
    rows = []
    for k in range(TOP_K):
        start = jnp.sum(jnp.where(erow == ri_ref[k:k + 1, :], pad_start, 0.0), axis=0, keepdims=True)
        rows.append(start.astype(jnp.int32) + ri_ref[TOP_K + k:TOP_K + k + 1, :])
    rows.append(jnp.zeros((SUBLANES - TOP_K, n_tok), jnp.int32))
    dest_ref[...] = jnp.concatenate(rows, axis=0)
    block_start = (lax.broadcasted_iota(jnp.int32, (N_EXPERTS, LANES), 1) * MOE_ROWS).astype(F32)
    block_e = jnp.sum((block_start >= pad_end).astype(jnp.int32), axis=0, keepdims=True)
    n_used = lax.shift_right_logical(pad_end[N_EXPERTS - 1:N_EXPERTS, :].astype(jnp.int32), shift)
    blk_ref[...] = jnp.concatenate(
        [jnp.minimum(block_e, N_EXPERTS - 1), n_used,
         jnp.zeros((SUBLANES - 2, LANES), jnp.int32)], axis=0)


def _dispatch_plan(ri, cnt):
    n_tok = ri.shape[1]
    n_blocks = (n_tok * TOP_K + N_EXPERTS * (MOE_ROWS - 1) + MOE_ROWS - 1) // MOE_ROWS
    assert n_blocks <= LANES and MOE_ROWS & (MOE_ROWS - 1) == 0
    dest, blk = pl.pallas_call(
        _plan_kernel,
        out_shape=[jax.ShapeDtypeStruct((SUBLANES, n_tok), jnp.int32),
                   jax.ShapeDtypeStruct((SUBLANES, LANES), jnp.int32)],
    )(ri, cnt)
    return dest, blk[0, 0:n_blocks], blk[1, 0:1], n_blocks * MOE_ROWS


def kernel(x, c, w_ada, b_ada, w_in, sc_conv, cv_conv, cv_conv_b, cv_ln_g, cv_ln_b, sg_ln_g, sg_ln_b, sg_w, sg_b, w_branch, w_gate, b_gate, w_o, ln1_g, ln1_b, w_router, b_router, w1, w3, w2, ln2_g, ln2_b):
    bsz, seq, d = x.shape
    mod3 = _ada(c, w_ada, b_ada).reshape(DEPTH, bsz, 1, 6 * d)
    wrt = w_router.T
    brc = b_router.reshape(N_EXPERTS, 1)
    w_in_p = _pack_weights(w_in)
    w_gate_p = _pack_weights(jnp.transpose(w_gate, (0, 2, 1, 3)).reshape(DEPTH * 3, d, d)).reshape(
        DEPTH, 3, d // 2, d)
    w_o_p = _pack_weights(w_o)
    w_branch_p = _pack_weights(w_branch.reshape(DEPTH * 3, W_BRANCH, d)).reshape(
        DEPTH, 3, W_BRANCH // 2, d)
    cb = bsz // N_CHAINS
    n_tok = cb * seq
    chains = [(x, h * cb) for h in range(N_CHAINS)]
    for l in range(DEPTH):
        last = l == DEPTH - 1
        result = None
        for h in range(N_CHAINS):
            x1, u2, ri, rf, cnt = _mixer(
                l, chains[h], mod3[l], h * cb, cb, w_in_p, sc_conv[l], cv_conv[l],
                cv_conv_b[l].reshape(1, -1), cv_ln_g[l].reshape(1, -1), cv_ln_b[l].reshape(1, -1),
                sg_ln_g[l].reshape(1, -1), sg_ln_b[l].reshape(1, -1), sg_w[l], sg_b[l].T,
                w_branch_p, w_gate_p, b_gate[l],
                w_o_p, ln1_g[l].reshape(1, -1), ln1_b[l].reshape(1, -1), wrt, brc)
            dest, block_e, n_used, n_rows = _dispatch_plan(ri, cnt)
            buf = _sc_scatter_rows(u2.reshape(n_tok, d // 2), dest[0], dest[1], n_rows)
            obuf = _experts(l, buf, block_e, n_used, w1, w3, w2)
            pair_idx = dest[0:TOP_K].reshape(TOP_K * n_tok)
            pairs = _sc_gather_rows(obuf, pair_idx).reshape(TOP_K, n_tok, d // 2)
            gates = rf[0:TOP_K].T
            g2, b2 = ln2_g[l].reshape(1, -1), ln2_b[l].reshape(1, -1)
            if last:
                result = _combine(x1, pairs, gates, mod3[l], h * cb, g2, b2, bsz, h * cb, result)
            else:
                chains[h] = (x1, pairs, gates, mod3[l], g2, b2)
    return result
```

```python
import functools

import jax
import jax.numpy as jnp
from jax import lax
from jax.experimental import pallas as pl
from jax.experimental.pallas import tpu as pltpu
from jax.experimental.pallas import tpu_sc as plsc

D_MODEL = 1024
DEPTH = 2
W_BRANCH = 1024
SC_KERNEL = 3
CV_KERNEL = 31
CHUNK = 128
SG_HEADS = 8
N_EXPERTS = 16
N_GROUPS = 4
EXPERTS_PER_GROUP = N_EXPERTS // N_GROUPS
TOP_K = 2
D_EXPERT = 512
ALPHA = (2.0 * DEPTH) ** 0.25
LN_EPS = 1e-5

F32 = jnp.float32
BF16 = jnp.bfloat16

V7X_VMEM_BYTES = 64 * 1024 * 1024
MIXER_VMEM_LIMIT = V7X_VMEM_BYTES - 6 * 1024 * 1024
EXPERT_VMEM_LIMIT = V7X_VMEM_BYTES // 2
SUBLANES = 8
LANES = 128

MIX_ROWS = 256
SC_HALO = SUBLANES
CV_HALO = 32
CONV_ROWS = 128
TIE_LAG = 2
TIE_FREE_JOBS = 4
CONV_COLS = 128
PROJ_COLS = 512
PACK_ROWS = 256
MOE_ROWS = 512
MOE_SLAB = 256
COMB_ROWS = 512
SC_WINDOW = 32
N_CHAINS = 2


def _dot(a, b):
    return jnp.dot(a, b, preferred_element_type=F32)


def _pack_bf16_pairs(v):
    m = v.shape[1] // 2
    lo = lax.bitcast_convert_type(v[:, 0:m].astype(BF16).astype(F32), jnp.uint32)
    hi = lax.bitcast_convert_type(v[:, m:2 * m].astype(BF16).astype(F32), jnp.uint32)
    return jnp.bitwise_or(jnp.bitwise_and(hi, jnp.uint32(0xFFFF0000)),
                          lax.shift_right_logical(lo, jnp.uint32(16)))


def _unpack_bf16_pairs(w):
    lo = lax.bitcast_convert_type(lax.shift_left(w, jnp.uint32(16)), F32)
    hi = lax.bitcast_convert_type(jnp.bitwise_and(w, jnp.uint32(0xFFFF0000)), F32)
    return lo, hi


def _layer_norm(v, g, b):
    mu = jnp.mean(v, axis=-1, keepdims=True)
    vc = v - mu
    var = jnp.mean(vc * vc, axis=-1, keepdims=True)
    return vc * lax.rsqrt(var + LN_EPS) * g + b


def _ada_kernel(c_ref, w_ref, b_ref, o_ref):
    c = c_ref[...]
    c_act = c * jax.nn.sigmoid(c)
    o_ref[0] = jnp.dot(c_act, w_ref[0], preferred_element_type=F32,
                       precision=lax.Precision.HIGHEST) + b_ref[0]


def _ada(c, w_ada, b_ada):
    bsz, d = c.shape
    n = w_ada.shape[-1]
    tn = 1536
    return pl.pallas_call(
        _ada_kernel,
        grid=(DEPTH, n // tn),
        in_specs=[
            pl.BlockSpec((bsz, d), lambda l, j: (0, 0)),
            pl.BlockSpec((1, d, tn), lambda l, j: (l, 0, j)),
            pl.BlockSpec((1, 1, tn), lambda l, j: (l, 0, j)),
        ],
        out_specs=pl.BlockSpec((1, bsz, tn), lambda l, j: (l, 0, j)),
        out_shape=jax.ShapeDtypeStruct((DEPTH, bsz, n), F32),
    )(c, w_ada, b_ada.reshape(DEPTH, 1, n))


def _top2_of4(rows):
    m1 = rows[0]
    i1 = jnp.zeros(rows[0].shape, jnp.int32)
    for k in range(1, 4):
        gt = rows[k] > m1
        m1 = jnp.where(gt, rows[k], m1)
        i1 = jnp.where(gt, k, i1)
    m2 = jnp.full(rows[0].shape, -jnp.inf, F32)
    i2 = jnp.zeros(rows[0].shape, jnp.int32)
    for k in range(4):
        cand = jnp.where(i1 == k, -jnp.inf, rows[k])
        gt = cand > m2
        m2 = jnp.where(gt, cand, m2)
        i2 = jnp.where(gt, k, i2)
    return m1, i1, m2, i2


def _zero_after(v):
    u = lax.bitcast_convert_type(v, jnp.uint32)
    u = lax.shift_right_logical(lax.shift_right_logical(u, jnp.uint32(16)), jnp.uint32(16))
    return lax.bitcast_convert_type(u, F32)


def _conv31_chunk(cvw_ref, cvbufs, cvout, r0, c0, tie):
    cs = slice(c0, c0 + CONV_COLS)
    cvbuf = cvbufs[c0 // PROJ_COLS]
    bs = slice(c0 % PROJ_COLS, c0 % PROJ_COLS + CONV_COLS)
    acc = None
    for r in range(SUBLANES):
        lead = SUBLANES if r else 0
        part = None
        for m in range((CV_KERNEL - 1 - r) // SUBLANES + 1):
            k = CV_KERNEL - 1 - (SUBLANES * m + r)
            start = CV_HALO + r0 - lead - SUBLANES * m
            w_row = cvw_ref[k:k + 1, cs]
            if tie is not None and acc is None and part is None:
                w_row = w_row + tie
            term = w_row * cvbuf[start:start + lead + CONV_ROWS, bs]
            part = term if part is None else part + term
        part = part[lead - r:lead - r + CONV_ROWS]
        acc = part if acc is None else acc + part
    cvout[r0:r0 + CONV_ROWS, cs] = acc


def _bf16_weights(packed):
    return pltpu.bitcast(packed, BF16)


def _pack_weights_kernel(w_ref, o_ref):
    o_ref[0] = pltpu.bitcast(w_ref[0].astype(BF16), jnp.uint32)


def _pack_weights(w):
    g, k, n = w.shape
    kb = PACK_ROWS
    return pl.pallas_call(
        _pack_weights_kernel,
        grid=(g, k // kb),
        in_specs=[pl.BlockSpec((1, kb, n), lambda i, j: (i, j, 0))],
        out_specs=pl.BlockSpec((1, kb // 2, n), lambda i, j: (i, j, 0)),
        out_shape=jax.ShapeDtypeStruct((g, k // 2, n), jnp.uint32),
        compiler_params=pltpu.CompilerParams(vmem_limit_bytes=EXPERT_VMEM_LIMIT),
    )(w)


def _moe_output(x1, ya_packed, yb_packed, gates, gate2, g, b):
    ya = jnp.concatenate(_unpack_bf16_pairs(ya_packed), axis=1)
    yb = jnp.concatenate(_unpack_bf16_pairs(yb_packed), axis=1)
    h = gates[:, 0:1] * ya + gates[:, 1:2] * yb
    return _layer_norm(ALPHA * x1 + gate2 * h, g, b)


def _mixer_kernel(tiles_per_seq, n_x_refs, *refs):
    last_step = pl.num_programs(0) - 1

    @pl.when(pl.program_id(0) < last_step)
    def _():
        _mixer_step(tiles_per_seq, n_x_refs, True, *refs)

    @pl.when(pl.program_id(0) == last_step)
    def _():
        _mixer_step(tiles_per_seq, n_x_refs, False, *refs)


def _mixer_step(tiles_per_seq, n_x_refs, run_first_half, *refs):
    x_refs = refs[:n_x_refs]
    (mod_ref, modt_ref, w_in_ref, scw_ref, cvw_ref, cvb_ref, cvg_ref, cvbeta_ref, sgg_ref,
     sgbeta_ref, sgw_ref, sgbt_ref, wbr_ref, wg_ref, bg_ref, wo_ref, ln1g_ref, ln1b_ref, wrt_ref,
     brc_ref,
     x1_ref, u2_ref, ri_ref, rf_ref, cnt_ref,
     qbuf, cvbuf0, cvbuf1, cvout, ybuf, pa_buf, pc_buf, gl_buf, mg_buf, xs_buf,
     base_ref) = refs[n_x_refs:]
    cvbufs = (cvbuf0, cvbuf1)
    ts = MIX_ROWS
    d = D_MODEL
    wb = W_BRANCH
    step = pl.program_id(0)
    tile = jnp.minimum(step, pl.num_programs(0) - 2)
    first_tile = tile % tiles_per_seq == 0

    @pl.when(step == 0)
    def _():
        base_ref[...] = jnp.zeros_like(base_ref)
        mg_buf[...] = jnp.zeros_like(mg_buf)
        xs_buf[...] = jnp.zeros_like(xs_buf)

    def w_in(c0):
        return _bf16_weights(w_in_ref[0, :, c0:c0 + PROJ_COLS])

    def w_gate(c0):
        n, col = divmod(c0, d)
        return _bf16_weights(wg_ref[0, n, :, col:col + PROJ_COLS])

    if not run_first_half:
        for _ in _mixer_tail(step, mg_buf, xs_buf, modt_ref, wo_ref, ln1g_ref, ln1b_ref, wrt_ref,
                             brc_ref, x1_ref, u2_ref, ri_ref, rf_ref, cnt_ref, base_ref):
            pass
        return

    @pl.when(first_tile)
    def _():
        qbuf[0:SC_HALO, :] = jnp.zeros((SC_HALO, wb), F32)
        for cvbuf in cvbufs:
            cvbuf[0:CV_HALO, :] = jnp.zeros((CV_HALO, PROJ_COLS), F32)

    if n_x_refs == 1:
        x = x_refs[0][0]
    else:
        x1p_ref, ya_ref, yb_ref, gates_ref, modp_ref, g2_ref, b2_ref = x_refs
        x = _moe_output(x1p_ref[0], ya_ref[0], yb_ref[0], gates_ref[...],
                        modp_ref[0][:, 5 * d:6 * d], g2_ref[...], b2_ref[...])
    mod = mod_ref[0]
    shift1, scale1 = mod[:, 0:d], mod[:, d:2 * d]
    ub = (x * (1.0 + scale1) + shift1).astype(BF16)

    def glu_block(c0):
        a = _dot(ub, w_in(3 * wb + c0))
        g = _dot(ub, w_in(4 * wb + c0))
        cvbufs[c0 // PROJ_COLS][CV_HALO:CV_HALO + ts, :] = a * jax.nn.sigmoid(g)

    tail = _mixer_tail(step, mg_buf, xs_buf, modt_ref, wo_ref, ln1g_ref, ln1b_ref, wrt_ref,
                       brc_ref, x1_ref, u2_ref, ri_ref, rf_ref, cnt_ref, base_ref)
    next(tail)
    glu_block(0)
    ties = {}

    def add_tie(chunk, tie):
        ties[chunk] = tie + ties[chunk] if chunk in ties else tie

    branch_z = {}

    def mixer_a():
        qbuf[SC_HALO:SC_HALO + ts, :] = pa_buf[:, wb:2 * wb] * pa_buf[:, 2 * wb:3 * wb]
        conv = scw_ref[SC_KERNEL - 1:SC_KERNEL, :] * qbuf[SC_HALO:SC_HALO + ts, :]
        for k in range(SC_KERNEL - 1):
            off = SC_HALO - (SC_KERNEL - 1) + k
            conv = conv + scw_ref[k:k + 1, :] * qbuf[off:off + ts, :]
        ybuf[0] = (pa_buf[:, 0:wb] * conv).astype(BF16)
        qbuf[0:SC_HALO, :] = qbuf[ts:ts + SC_HALO, :]
        branch_z[0] = _dot(ybuf[0], _bf16_weights(wbr_ref[0, 0]))

    def mixer_c():
        gu = jax.nn.gelu(pc_buf[:, 0:wb])
        gv = _layer_norm(jax.nn.gelu(pc_buf[:, wb:2 * wb]), sgg_ref[...],
                         sgbeta_ref[...]).astype(BF16)
        row = lax.broadcasted_iota(jnp.int32, (CHUNK, CHUNK), 0)
        col = lax.broadcasted_iota(jnp.int32, (CHUNK, CHUNK), 1)
        hd = wb // SG_HEADS
        for h in range(SG_HEADS):
            wm = jnp.where(row >= col, sgw_ref[h], 0.0).astype(BF16)
            bias = sgbt_ref[:, h:h + 1]
            for n in range(ts // CHUNK):
                rs = slice(n * CHUNK, (n + 1) * CHUNK)
                cs = slice(h * hd, (h + 1) * hd)
                mixed = _dot(wm, gv[rs, cs]) + bias
                ybuf[2, rs, cs] = (gu[rs, cs] * mixed).astype(BF16)
        branch_z[2] = _dot(ybuf[2], _bf16_weights(wbr_ref[0, 2]))

    def glu_rest():
        for c0 in range(PROJ_COLS, wb, PROJ_COLS):
            glu_block(c0)

    def tail_stage():
        next(tail, None)

    mxu_jobs = ([(pa_buf, w_in, c0, c0) for c0 in range(0, 3 * wb, PROJ_COLS)]
                + [(pc_buf, w_in, c0, 5 * wb + c0) for c0 in range(0, 2 * wb, PROJ_COLS)]
                + [(gl_buf, w_gate, c0, c0) for c0 in range(0, 3 * d, PROJ_COLS)])
    glu_rest()
    after_job = {7: tail_stage, 12: tail_stage}
    conv_jobs = [(r0, c0) for c0 in range(0, wb, CONV_COLS) for r0 in range(0, ts, CONV_ROWS)]
    chunks_per_job = len(conv_jobs) // len(mxu_jobs)
    for c, conv_job in enumerate(conv_jobs):
        _conv31_chunk(cvw_ref, cvbufs, cvout, *conv_job, ties.get(c))
        if (c + 1) % chunks_per_job:
            continue
        i = c // chunks_per_job
        dst, weights, dc, wc = mxu_jobs[i]
        res = _dot(ub, weights(wc))
        dst[:, dc:dc + PROJ_COLS] = res
        if i < len(mxu_jobs) - TIE_FREE_JOBS:
            add_tie(c + TIE_LAG, _zero_after(res[ts - 1:ts, PROJ_COLS - CONV_COLS:PROJ_COLS]))
        if i in after_job:
            after_job[i]()
    for cvbuf in cvbufs:
        cvbuf[0:CV_HALO, :] = cvbuf[ts:ts + CV_HALO, :]

    mixer_a()
    cv = _layer_norm(cvout[...] + cvb_ref[...], cvg_ref[...], cvbeta_ref[...])
    ybuf[1] = (cv * jax.nn.sigmoid(cv)).astype(BF16)
    branch_z[1] = _dot(ybuf[1], _bf16_weights(wbr_ref[0, 1]))
    mixer_c()

    merged = None
    for n in range(3):
        gated = jax.nn.sigmoid(gl_buf[:, n * d:(n + 1) * d] + bg_ref[n:n + 1, :]) * branch_z[n]
        merged = gated if merged is None else merged + gated
    mg_buf[...] = merged.astype(BF16)
    xs_buf[...] = x


def _mixer_tail(step, mg_buf, xs_buf, modt_ref, wo_ref, ln1g_ref, ln1b_ref, wrt_ref, brc_ref,
                x1_ref, u2_ref, ri_ref, rf_ref, cnt_ref, base_ref):
    ts = MIX_ROWS
    d = D_MODEL
    modt = modt_ref[0]
    gate1, shift2, scale2 = modt[:, 2 * d:3 * d], modt[:, 3 * d:4 * d], modt[:, 4 * d:5 * d]
    hmix = _dot(mg_buf[...], _bf16_weights(wo_ref[0]))
    x1 = _layer_norm(ALPHA * xs_buf[...] + gate1 * hmix, ln1g_ref[...], ln1b_ref[...])
    x1_ref[0] = x1
    u2 = x1 * (1.0 + scale2) + shift2
    u2_ref[0] = _pack_bf16_pairs(u2)
    yield

    logits = lax.dot_general(wrt_ref[...], u2, (((1,), (1,)), ((), ())),
                             preferred_element_type=F32,
                             precision=lax.Precision.HIGHEST)
    mx = jnp.max(logits, axis=0, keepdims=True)
    ex = jnp.exp(logits - mx)
    scores = ex / jnp.sum(ex, axis=0, keepdims=True)
    sel = scores + brc_ref[...]
    tops = []
    for g in range(N_GROUPS):
        rows = [sel[g * EXPERTS_PER_GROUP + k:g * EXPERTS_PER_GROUP + k + 1, :]
                for k in range(EXPERTS_PER_GROUP)]
        tops.append(_top2_of4(rows))
    best = tops[0][0] + tops[0][2]
    g_idx = jnp.zeros(best.shape, jnp.int32)
    loc1, loc2 = tops[0][1], tops[0][3]
    for g in range(1, N_GROUPS):
        gs = tops[g][0] + tops[g][2]
        gt = gs > best
        best = jnp.where(gt, gs, best)
        g_idx = jnp.where(gt, g, g_idx)
        loc1 = jnp.where(gt, tops[g][1], loc1)
        loc2 = jnp.where(gt, tops[g][3], loc2)
    e0 = g_idx * EXPERTS_PER_GROUP + loc1
    e1 = g_idx * EXPERTS_PER_GROUP + loc2
    erow = lax.broadcasted_iota(jnp.int32, (N_EXPERTS, ts), 0)
    is0 = erow == e0
    is1 = erow == e1
    s0 = jnp.sum(jnp.where(is0, scores, 0.0), axis=0, keepdims=True)
    s1 = jnp.sum(jnp.where(is1, scores, 0.0), axis=0, keepdims=True)
    ssum = s0 + s1
    yield

    onehot = jnp.logical_or(is0, is1).astype(BF16)
    src = lax.broadcasted_iota(jnp.int32, (ts, ts), 0)
    dst = lax.broadcasted_iota(jnp.int32, (ts, ts), 1)
    earlier = (src < dst).astype(BF16)
    prior = _dot(onehot, earlier) + base_ref[:, 0:1]
    r0 = jnp.sum(jnp.where(is0, prior, 0.0), axis=0, keepdims=True)
    r1 = jnp.sum(jnp.where(is1, prior, 0.0), axis=0, keepdims=True)
    counts = jnp.sum(onehot.astype(F32), axis=1, keepdims=True)
    base_ref[...] = base_ref[...] + jnp.where(step > 0, counts, 0.0)

    zi = jnp.zeros((SUBLANES - 4, ts), jnp.int32)
    ri_ref[...] = jnp.concatenate([e0, e1, r0.astype(jnp.int32), r1.astype(jnp.int32), zi], axis=0)
    zf = jnp.zeros((SUBLANES - 2, ts), F32)
    rf_ref[...] = jnp.concatenate([s0 / ssum, s1 / ssum, zf], axis=0).T
    cnt_ref[...] = base_ref[...]


def _mixer(layer, xsrc, mod3, mb0, bsz, w_in, scw, cvw, cvb, cvg, cvbeta, sgg, sgbeta, sgw, sgbt,
           wbr, wg, bg, wo, ln1g, ln1b, wrt, brc):
    _, seq, d = xsrc[0].shape
    ts = MIX_ROWS
    ns = seq // ts
    n_tok = bsz * seq

    n_tiles = bsz * ns

    def const(shape):
        zeros = (0,) * len(shape)
        return pl.BlockSpec(shape, lambda s: zeros, pipeline_mode=pl.Buffered(1))

    def first_half(s):
        return jnp.minimum(s, n_tiles - 1)

    def second_half(s):
        return jnp.maximum(s - 1, 0)

    def layer_weights(shape):
        block = (1,) + tuple(shape[1:])
        index = (layer,) + (0,) * (len(shape) - 1)
        return pl.BlockSpec(block, lambda s: index, pipeline_mode=pl.Buffered(1))

    if len(xsrc) == 2:
        x, xb0 = xsrc
        x_args = [x]
        x_specs = [pl.BlockSpec((1, ts, d),
                                lambda s: (first_half(s) // ns + xb0, first_half(s) % ns, 0))]
    else:
        x1p, pairs, gates, mod3p, g2, b2 = xsrc
        x_args = [x1p, pairs, pairs, gates, mod3p, g2, b2]
        x_specs = [
            pl.BlockSpec((1, ts, d), lambda s: (first_half(s) // ns, first_half(s) % ns, 0)),
            pl.BlockSpec((1, ts, d // 2), lambda s: (0, first_half(s), 0)),
            pl.BlockSpec((1, ts, d // 2), lambda s: (1, first_half(s), 0)),
            pl.BlockSpec((ts, SUBLANES), lambda s: (first_half(s), 0)),
            pl.BlockSpec((1, 1, 6 * d), lambda s: (first_half(s) // ns + mb0, 0, 0)),
            const(g2.shape), const(b2.shape),
        ]
    in_specs = x_specs + [
        pl.BlockSpec((1, 1, 6 * d), lambda s: (first_half(s) // ns + mb0, 0, 0)),
        pl.BlockSpec((1, 1, 6 * d), lambda s: (second_half(s) // ns + mb0, 0, 0)),
        layer_weights(w_in.shape), const(scw.shape), const(cvw.shape), const(cvb.shape),
        const(cvg.shape), const(cvbeta.shape), const(sgg.shape), const(sgbeta.shape),
        const(sgw.shape), const(sgbt.shape), layer_weights(wbr.shape), layer_weights(wg.shape),
        const(bg.shape), layer_weights(wo.shape), const(ln1g.shape), const(ln1b.shape),
        const(wrt.shape), const(brc.shape),
    ]
    out_specs = [
        pl.BlockSpec((1, ts, d), lambda s: (second_half(s) // ns, second_half(s) % ns, 0)),
        pl.BlockSpec((1, ts, d // 2), lambda s: (second_half(s) // ns, second_half(s) % ns, 0)),
        pl.BlockSpec((SUBLANES, ts), lambda s: (0, second_half(s))),
        pl.BlockSpec((ts, SUBLANES), lambda s: (second_half(s), 0)),
        pl.BlockSpec((N_EXPERTS, LANES), lambda s: (0, 0)),
    ]
    out_shape = [
        jax.ShapeDtypeStruct((bsz, seq, d), F32),
        jax.ShapeDtypeStruct((bsz, seq, d // 2), jnp.uint32),
        jax.ShapeDtypeStruct((SUBLANES, n_tok), jnp.int32),
        jax.ShapeDtypeStruct((n_tok, SUBLANES), F32),
        jax.ShapeDtypeStruct((N_EXPERTS, LANES), F32),
    ]
    return pl.pallas_call(
        functools.partial(_mixer_kernel, ns, len(x_args)),
        grid=(n_tiles + 1,),
        in_specs=in_specs,
        out_specs=out_specs,
        out_shape=out_shape,
        scratch_shapes=[
            pltpu.VMEM((SC_HALO + ts, W_BRANCH), F32),
            pltpu.VMEM((CV_HALO + ts, PROJ_COLS), F32),
            pltpu.VMEM((CV_HALO + ts, PROJ_COLS), F32),
            pltpu.VMEM((ts, W_BRANCH), F32),
            pltpu.VMEM((3, ts, W_BRANCH), BF16),
            pltpu.VMEM((ts, 3 * W_BRANCH), F32),
            pltpu.VMEM((ts, 2 * W_BRANCH), F32),
            pltpu.VMEM((ts, 3 * D_MODEL), F32),
            pltpu.VMEM((ts, d), BF16),
            pltpu.VMEM((ts, d), F32),
            pltpu.VMEM((N_EXPERTS, LANES), F32),
        ],
        compiler_params=pltpu.CompilerParams(
            dimension_semantics=("arbitrary",),
            vmem_limit_bytes=MIXER_VMEM_LIMIT),
    )(*x_args, mod3, mod3, w_in, scw, cvw, cvb, cvg, cvbeta, sgg, sgbeta, sgw, sgbt, wbr, wg, bg,
      wo, ln1g, ln1b, wrt, brc)


def _sc_workers():
    info = plsc.get_sparse_core_info()
    return info.num_cores, info.num_cores * info.num_subcores


def _sc_scatter_rows(rows, dest_a, dest_b, n_out):
    n, d = rows.shape
    nc, nw = _sc_workers()
    per_w = n // nw
    n_win = per_w // SC_WINDOW
    ia = dest_a.reshape(nw, n_win, SC_WINDOW)
    ib = dest_b.reshape(nw, n_win, SC_WINDOW)
    mesh = plsc.VectorSubcoreMesh(core_axis_name="c", subcore_axis_name="s")

    @functools.partial(
        pl.kernel, mesh=mesh,
        out_type=jax.ShapeDtypeStruct((n_out, d), rows.dtype),
        scratch_types=[
            pltpu.VMEM((n_win, SC_WINDOW), jnp.int32),
            pltpu.VMEM((n_win, SC_WINDOW), jnp.int32),
            pltpu.VMEM((SC_WINDOW, d), rows.dtype),
        ],
    )
    def scatter(rows_hbm, ia_hbm, ib_hbm, out_hbm, ia_v, ib_v, rows_v):
        wid = lax.axis_index("s") * nc + lax.axis_index("c")
        pltpu.sync_copy(ia_hbm.at[wid], ia_v)
        pltpu.sync_copy(ib_hbm.at[wid], ib_v)
        base = wid * per_w

        @pl.loop(0, n_win)
        def _(j):
            pltpu.sync_copy(rows_hbm.at[pl.ds(base + j * SC_WINDOW, SC_WINDOW)], rows_v)
            pltpu.sync_copy(rows_v, out_hbm.at[ia_v.at[j]])
            pltpu.sync_copy(rows_v, out_hbm.at[ib_v.at[j]])

    return scatter(rows, ia, ib)


def _sc_gather_rows(table, idx):
    n = idx.shape[0]
    d = table.shape[1]
    nc, nw = _sc_workers()
    per_w = n // nw
    n_win = per_w // SC_WINDOW
    idx3 = idx.reshape(nw, n_win, SC_WINDOW)
    mesh = plsc.VectorSubcoreMesh(core_axis_name="c", subcore_axis_name="s")

    @functools.partial(
        pl.kernel, mesh=mesh,
        out_type=jax.ShapeDtypeStruct((n, d), table.dtype),
        scratch_types=[
            pltpu.VMEM((n_win, SC_WINDOW), jnp.int32),
            pltpu.VMEM((SC_WINDOW, d), table.dtype),
        ],
    )
    def gather(table_hbm, idx_hbm, out_hbm, idx_v, rows_v):
        wid = lax.axis_index("s") * nc + lax.axis_index("c")
        pltpu.sync_copy(idx_hbm.at[wid], idx_v)
        base = wid * per_w

        @pl.loop(0, n_win)
        def _(j):
            pltpu.sync_copy(table_hbm.at[idx_v.at[j]], rows_v)
            pltpu.sync_copy(rows_v, out_hbm.at[pl.ds(base + j * SC_WINDOW, SC_WINDOW)])

    return gather(table, idx3)


def _expert_kernel(layer, be_ref, nused_ref, x_ref, w1_hbm, w3_hbm, w2_hbm, o_ref,
                   w1f, w3f, w2f, w1s, w3s, w2s, sems, group_ref):
    i = pl.program_id(0)
    n_used = nused_ref[0]
    e_now = be_ref[i]

    def fetch(e, slot):
        return [pltpu.make_async_copy(w_hbm.at[layer, e], w_f.at[slot], sems.at[k, slot])
                for k, (w_hbm, w_f) in enumerate(((w1_hbm, w1f), (w3_hbm, w3f), (w2_hbm, w2f)))]

    @pl.when(i == 0)
    def _():
        group_ref[0] = 0
        for copy in fetch(e_now, 0):
            copy.start()

    first_of_group = jnp.logical_and(
        i < n_used, jnp.logical_or(i == 0, e_now != be_ref[jnp.maximum(i - 1, 0)]))

    @pl.when(first_of_group)
    def _():
        slot = group_ref[0] % 2
        for copy in fetch(e_now, slot):
            copy.wait()
        w1s[...] = w1f[slot].astype(BF16)
        w3s[...] = w3f[slot].astype(BF16)
        w2s[...] = w2f[slot].astype(BF16)
        last = be_ref.shape[0] - 1
        j = lax.while_loop(
            lambda j: jnp.logical_and(j < n_used, be_ref[jnp.minimum(j, last)] == e_now),
            lambda j: j + 1, i + 1)
        next_e = be_ref[jnp.minimum(j, last)]

        @pl.when(j < n_used)
        def _():
            for copy in fetch(next_e, 1 - slot):
                copy.start()

        group_ref[0] = group_ref[0] + 1

    @pl.when(i < n_used)
    def _():
        lo, hi = _unpack_bf16_pairs(x_ref[...])
        xb = jnp.concatenate([lo, hi], axis=1).astype(BF16)
        a = _dot(xb, w1s[...])
        b = _dot(xb, w3s[...])
        for r0 in range(0, MOE_ROWS, MOE_SLAB):
            rs = slice(r0, r0 + MOE_SLAB)
            h = a[rs] * jax.nn.sigmoid(a[rs]) * b[rs]
            o_ref[rs, :] = _pack_bf16_pairs(_dot(h.astype(BF16), w2s[...]))


def _experts(layer, buf, block_e, n_used, w1, w3, w2):
    n_rows, dp = buf.shape
    nb = n_rows // MOE_ROWS
    d, fe = w1.shape[-2:]

    def row_map(i, be, nu):
        return (jnp.minimum(i, nu[0] - 1), 0)

    grid_spec = pltpu.PrefetchScalarGridSpec(
        num_scalar_prefetch=2,
        grid=(nb,),
        in_specs=[
            pl.BlockSpec((MOE_ROWS, dp), row_map),
            pl.BlockSpec(memory_space=pl.ANY),
            pl.BlockSpec(memory_space=pl.ANY),
            pl.BlockSpec(memory_space=pl.ANY),
        ],
        out_specs=pl.BlockSpec((MOE_ROWS, dp), row_map),
        scratch_shapes=[
            pltpu.VMEM((2, d, fe), F32), pltpu.VMEM((2, d, fe), F32), pltpu.VMEM((2, fe, d), F32),
            pltpu.VMEM((d, fe), BF16), pltpu.VMEM((d, fe), BF16), pltpu.VMEM((fe, d), BF16),
            pltpu.SemaphoreType.DMA((3, 2)),
            pltpu.SMEM((1,), jnp.int32),
        ],
    )
    return pl.pallas_call(
        functools.partial(_expert_kernel, layer),
        grid_spec=grid_spec,
        out_shape=jax.ShapeDtypeStruct((n_rows, dp), jnp.uint32),
        compiler_params=pltpu.CompilerParams(dimension_semantics=("arbitrary",),
                                             vmem_limit_bytes=EXPERT_VMEM_LIMIT),
    )(block_e, n_used, buf, w1, w3, w2)


def _combine_kernel(x1_ref, ya_ref, yb_ref, gates_ref, mod_ref, g_ref, b_ref, *rest):
    o_ref = rest[-1]
    d = D_MODEL
    o_ref[0] = _moe_output(x1_ref[0], ya_ref[0], yb_ref[0], gates_ref[...],
                           mod_ref[0][:, 5 * d:6 * d], g_ref[...], b_ref[...])


def _combine(x1, pairs, gates, mod3, mb0, g, b, out_bsz, ob0, prev):
    bsz, seq, d = x1.shape
    ts = COMB_ROWS
    ns = seq // ts
    in_specs = [
        pl.BlockSpec((1, ts, d), lambda i, j: (i, j, 0)),
        pl.BlockSpec((1, ts, d // 2), lambda i, j: (0, i * ns + j, 0)),
        pl.BlockSpec((1, ts, d // 2), lambda i, j: (1, i * ns + j, 0)),
        pl.BlockSpec((ts, SUBLANES), lambda i, j: (i * ns + j, 0)),
        pl.BlockSpec((1, 1, 6 * d), lambda i, j: (i + mb0, 0, 0)),
        pl.BlockSpec((1, d), lambda i, j: (0, 0)),
        pl.BlockSpec((1, d), lambda i, j: (0, 0)),
    ]
    args = [x1, pairs, pairs, gates, mod3, g, b]
    aliases = {}
    if prev is not None:
        in_specs.append(pl.BlockSpec(memory_space=pl.ANY))
        aliases = {len(args): 0}
        args.append(prev)
    return pl.pallas_call(
        _combine_kernel,
        grid=(bsz, ns),
        in_specs=in_specs,
        out_specs=pl.BlockSpec((1, ts, d), lambda i, j: (i + ob0, j, 0)),
        out_shape=jax.ShapeDtypeStruct((out_bsz, seq, d), F32),
        input_output_aliases=aliases,
        compiler_params=pltpu.CompilerParams(dimension_semantics=("arbitrary", "arbitrary")),
    )(*args)


def _plan_kernel(ri_ref, cnt_ref, dest_ref, blk_ref):
    n_tok = ri_ref.shape[1]
    counts = cnt_ref[:, 0:1].astype(jnp.int32)
    shift = MOE_ROWS.bit_length() - 1
    padded = lax.shift_left(lax.shift_right_logical(counts + (MOE_ROWS - 1), shift), shift)
    e_out = lax.broadcasted_iota(jnp.int32, (N_EXPERTS, N_EXPERTS), 0)
    e_in = lax.broadcasted_iota(jnp.int32, (N_EXPERTS, N_EXPERTS), 1)
    upto = (e_in <= e_out).astype(BF16)
    padded_f = jnp.broadcast_to(padded.astype(F32), (N_EXPERTS, LANES))
    pad_end = _dot(upto, padded_f.astype(BF16))
    pad_start = pad_end[:, 0:1] - padded.astype(F32)
    erow = lax.broadcasted_iota(jnp.int32, (N_EXPERTS, n_tok), 0)
    rows = []
    for k in range(TOP_K):
        start = jnp.sum(jnp.where(erow == ri_ref[k:k + 1, :], pad_start, 0.0), axis=0, keepdims=True)
        rows.append(start.astype(jnp.int32) + ri_ref[TOP_K + k:TOP_K + k + 1, :])
    rows.append(jnp.zeros((SUBLANES - TOP_K, n_tok), jnp.int32))
    dest_ref[...] = jnp.concatenate(rows, axis=0)
    block_start = (lax.broadcasted_iota(jnp.int32, (N_EXPERTS, LANES), 1) * MOE_ROWS).astype(F32)
    block_e = jnp.sum((block_start >= pad_end).astype(jnp.int32), axis=0, keepdims=True)
    n_used = lax.shift_right_logical(pad_end[N_EXPERTS - 1:N_EXPERTS, :].astype(jnp.int32), shift)
    blk_ref[...] = jnp.concatenate(
        [jnp.minimum(block_e, N_EXPERTS - 1), n_used,
         jnp.zeros((SUBLANES - 2, LANES), jnp.int32)], axis=0)


def _dispatch_plan(ri, cnt):
    n_tok = ri.shape[1]
    n_blocks = (n_tok * TOP_K + N_EXPERTS * (MOE_ROWS - 1) + MOE_ROWS - 1) // MOE_ROWS
    assert n_blocks <= LANES and MOE_ROWS & (MOE_ROWS - 1) == 0
    dest, blk = pl.pallas_call(
        _plan_kernel,
        out_shape=[jax.ShapeDtypeStruct((SUBLANES, n_tok), jnp.int32),
                   jax.ShapeDtypeStruct((SUBLANES, LANES), jnp.int32)],
    )(ri, cnt)
    return dest, blk[0, 0:n_blocks], blk[1, 0:1], n_blocks * MOE_ROWS


def kernel(x, c, w_ada, b_ada, w_in, sc_conv, cv_conv, cv_conv_b, cv_ln_g, cv_ln_b, sg_ln_g, sg_ln_b, sg_w, sg_b, w_branch, w_gate, b_gate, w_o, ln1_g, ln1_b, w_router, b_router, w1, w3, w2, ln2_g, ln2_b):
    bsz, seq, d = x.shape
    mod3 = _ada(c, w_ada, b_ada).reshape(DEPTH, bsz, 1, 6 * d)
    wrt = w_router.T
    brc = b_router.reshape(N_EXPERTS, 1)
    w_in_p = _pack_weights(w_in)
    w_gate_p = _pack_weights(jnp.transpose(w_gate, (0, 2, 1, 3)).reshape(DEPTH * 3, d, d)).reshape(
        DEPTH, 3, d // 2, d)
    w_o_p = _pack_weights(w_o)
    w_branch_p = _pack_weights(w_branch.reshape(DEPTH * 3, W_BRANCH, d)).reshape(
        DEPTH, 3, W_BRANCH // 2, d)
    cb = bsz // N_CHAINS
    n_tok = cb * seq
    chains = [(x, h * cb) for h in range(N_CHAINS)]
    for l in range(DEPTH):
        last = l == DEPTH - 1
        result = None
        for h in range(N_CHAINS):
            x1, u2, ri, rf, cnt = _mixer(
                l, chains[h], mod3[l], h * cb, cb, w_in_p, sc_conv[l], cv_conv[l],
                cv_conv_b[l].reshape(1, -1), cv_ln_g[l].reshape(1, -1), cv_ln_b[l].reshape(1, -1),
                sg_ln_g[l].reshape(1, -1), sg_ln_b[l].reshape(1, -1), sg_w[l], sg_b[l].T,
                w_branch_p, w_gate_p, b_gate[l],
                w_o_p, ln1_g[l].reshape(1, -1), ln1_b[l].reshape(1, -1), wrt, brc)
            dest, block_e, n_used, n_rows = _dispatch_plan(ri, cnt)
            buf = _sc_scatter_rows(u2.reshape(n_tok, d // 2), dest[0], dest[1], n_rows)
            obuf = _experts(l, buf, block_e, n_used, w1, w3, w2)
            pair_idx = dest[0:TOP_K].reshape(TOP_K * n_tok)
            pairs = _sc_gather_rows(obuf, pair_idx).reshape(TOP_K, n_tok, d // 2)
            gates = rf
            g2, b2 = ln2_g[l].reshape(1, -1), ln2_b[l].reshape(1, -1)
            if last:
                result = _combine(x1, pairs, gates, mod3[l], h * cb, g2, b2, bsz, h * cb, result)
            else:
                chains[h] = (x1, pairs, gates, mod3[l], g2, b2)
    return result
```

```python
import functools

import jax
import jax.numpy as jnp
from jax import lax
from jax.experimental import pallas as pl
from jax.experimental.pallas import tpu as pltpu
from jax.experimental.pallas import tpu_sc as plsc

D_MODEL = 1024
DEPTH = 2
W_BRANCH = 1024
SC_KERNEL = 3
CV_KERNEL = 31
CHUNK = 128
SG_HEADS = 8
N_EXPERTS = 16
N_GROUPS = 4
EXPERTS_PER_GROUP = N_EXPERTS // N_GROUPS
TOP_K = 2
D_EXPERT = 512
ALPHA = (2.0 * DEPTH) ** 0.25
LN_EPS = 1e-5

F32 = jnp.float32
BF16 = jnp.bfloat16

V7X_VMEM_BYTES = 64 * 1024 * 1024
MIXER_VMEM_LIMIT = V7X_VMEM_BYTES - 6 * 1024 * 1024
EXPERT_VMEM_LIMIT = V7X_VMEM_BYTES // 2
SUBLANES = 8
LANES = 128

MIX_ROWS = 256
SC_HALO = SUBLANES
CV_HALO = 32
CONV_ROWS = 128
TIE_LAG = 2
TIE_FREE_JOBS = 4
CONV_COLS = 128
PROJ_COLS = 512
PACK_ROWS = 256
MOE_ROWS = 512
MOE_SLAB = 256
COMB_ROWS = 512
SC_WINDOW = 32
N_CHAINS = 2


def _dot(a, b):
    return jnp.dot(a, b, preferred_element_type=F32)


def _pack_bf16_pairs(v):
    m = v.shape[1] // 2
    lo = lax.bitcast_convert_type(v[:, 0:m].astype(BF16).astype(F32), jnp.uint32)
    hi = lax.bitcast_convert_type(v[:, m:2 * m].astype(BF16).astype(F32), jnp.uint32)
    return jnp.bitwise_or(jnp.bitwise_and(hi, jnp.uint32(0xFFFF0000)),
                          lax.shift_right_logical(lo, jnp.uint32(16)))


def _unpack_bf16_pairs(w):
    lo = lax.bitcast_convert_type(lax.shift_left(w, jnp.uint32(16)), F32)
    hi = lax.bitcast_convert_type(jnp.bitwise_and(w, jnp.uint32(0xFFFF0000)), F32)
    return lo, hi


def _layer_norm(v, g, b):
    mu = jnp.mean(v, axis=-1, keepdims=True)
    vc = v - mu
    var = jnp.mean(vc * vc, axis=-1, keepdims=True)
    return vc * lax.rsqrt(var + LN_EPS) * g + b


def _ada_kernel(c_ref, w_ref, b_ref, o_ref):
    c = c_ref[...]
    c_act = c * jax.nn.sigmoid(c)
    o_ref[0] = jnp.dot(c_act, w_ref[0], preferred_element_type=F32,
                       precision=lax.Precision.HIGHEST) + b_ref[0]


def _ada(c, w_ada, b_ada):
    bsz, d = c.shape
    n = w_ada.shape[-1]
    tn = 1536
    return pl.pallas_call(
        _ada_kernel,
        grid=(DEPTH, n // tn),
        in_specs=[
            pl.BlockSpec((bsz, d), lambda l, j: (0, 0)),
            pl.BlockSpec((1, d, tn), lambda l, j: (l, 0, j)),
            pl.BlockSpec((1, 1, tn), lambda l, j: (l, 0, j)),
        ],
        out_specs=pl.BlockSpec((1, bsz, tn), lambda l, j: (l, 0, j)),
        out_shape=jax.ShapeDtypeStruct((DEPTH, bsz, n), F32),
    )(c, w_ada, b_ada.reshape(DEPTH, 1, n))


def _top2_of4(rows):
    m1 = rows[0]
    i1 = jnp.zeros(rows[0].shape, jnp.int32)
    for k in range(1, 4):
        gt = rows[k] > m1
        m1 = jnp.where(gt, rows[k], m1)
        i1 = jnp.where(gt, k, i1)
    m2 = jnp.full(rows[0].shape, -jnp.inf, F32)
    i2 = jnp.zeros(rows[0].shape, jnp.int32)
    for k in range(4):
        cand = jnp.where(i1 == k, -jnp.inf, rows[k])
        gt = cand > m2
        m2 = jnp.where(gt, cand, m2)
        i2 = jnp.where(gt, k, i2)
    return m1, i1, m2, i2


def _zero_after(v):
    u = lax.bitcast_convert_type(v, jnp.uint32)
    u = lax.shift_right_logical(lax.shift_right_logical(u, jnp.uint32(16)), jnp.uint32(16))
    return lax.bitcast_convert_type(u, F32)


def _conv31_chunk(cvw_ref, cvbufs, cvout, r0, c0, tie):
    cs = slice(c0, c0 + CONV_COLS)
    cvbuf = cvbufs[c0 // PROJ_COLS]
    bs = slice(c0 % PROJ_COLS, c0 % PROJ_COLS + CONV_COLS)
    acc = None
    for r in range(SUBLANES):
        lead = SUBLANES if r else 0
        part = None
        for m in range((CV_KERNEL - 1 - r) // SUBLANES + 1):
            k = CV_KERNEL - 1 - (SUBLANES * m + r)
            start = CV_HALO + r0 - lead - SUBLANES * m
            w_row = cvw_ref[k:k + 1, cs]
            if tie is not None and acc is None and part is None:
                w_row = w_row + tie
            term = w_row * cvbuf[start:start + lead + CONV_ROWS, bs]
            part = term if part is None else part + term
        part = part[lead - r:lead - r + CONV_ROWS]
        acc = part if acc is None else acc + part
    cvout[r0:r0 + CONV_ROWS, cs] = acc


def _bf16_weights(packed):
    return pltpu.bitcast(packed, BF16)


def _pack_weights_kernel(w_ref, o_ref):
    o_ref[0] = pltpu.bitcast(w_ref[0].astype(BF16), jnp.uint32)


def _pack_weights(w):
    g, k, n = w.shape
    kb = PACK_ROWS
    return pl.pallas_call(
        _pack_weights_kernel,
        grid=(g, k // kb),
        in_specs=[pl.BlockSpec((1, kb, n), lambda i, j: (i, j, 0))],
        out_specs=pl.BlockSpec((1, kb // 2, n), lambda i, j: (i, j, 0)),
        out_shape=jax.ShapeDtypeStruct((g, k // 2, n), jnp.uint32),
        compiler_params=pltpu.CompilerParams(vmem_limit_bytes=EXPERT_VMEM_LIMIT),
    )(w)


def _moe_output(x1, ya_packed, yb_packed, gates, gate2, g, b):
    ya = jnp.concatenate(_unpack_bf16_pairs(ya_packed), axis=1)
    yb = jnp.concatenate(_unpack_bf16_pairs(yb_packed), axis=1)
    h = gates[:, 0:1] * ya + gates[:, 1:2] * yb
    return _layer_norm(ALPHA * x1 + gate2 * h, g, b)


def _mixer_kernel(tiles_per_seq, n_x_refs, *refs):
    last_step = pl.num_programs(0) - 1

    @pl.when(pl.program_id(0) < last_step)
    def _():
        _mixer_step(tiles_per_seq, n_x_refs, True, *refs)

    @pl.when(pl.program_id(0) == last_step)
    def _():
        _mixer_step(tiles_per_seq, n_x_refs, False, *refs)


def _mixer_step(tiles_per_seq, n_x_refs, run_first_half, *refs):
    x_refs = refs[:n_x_refs]
    (mod_ref, modt_ref, w_in_ref, scw_ref, cvw_ref, cvb_ref, cvg_ref, cvbeta_ref, sgg_ref,
     sgbeta_ref, sgw_ref, sgbt_ref, wbr_ref, wg_ref, bg_ref, wo_ref, ln1g_ref, ln1b_ref, wrt_ref,
     brc_ref,
     x1_ref, u2_ref, dest_ref, rf_ref, blk_ref,
     qbuf, cvbuf0, cvbuf1, cvout, ybuf, pa_buf, pc_buf, gl_buf, mg_buf, xs_buf, ri_all,
     base_ref) = refs[n_x_refs:]
    cvbufs = (cvbuf0, cvbuf1)
    ts = MIX_ROWS
    d = D_MODEL
    wb = W_BRANCH
    step = pl.program_id(0)
    tile = jnp.minimum(step, pl.num_programs(0) - 2)
    first_tile = tile % tiles_per_seq == 0

    @pl.when(step == 0)
    def _():
        base_ref[...] = jnp.zeros_like(base_ref)
        mg_buf[...] = jnp.zeros_like(mg_buf)
        xs_buf[...] = jnp.zeros_like(xs_buf)

    def w_in(c0):
        return _bf16_weights(w_in_ref[0, :, c0:c0 + PROJ_COLS])

    def w_gate(c0):
        n, col = divmod(c0, d)
        return _bf16_weights(wg_ref[0, n, :, col:col + PROJ_COLS])

    if not run_first_half:
        for _ in _mixer_tail(step, mg_buf, xs_buf, modt_ref, wo_ref, ln1g_ref, ln1b_ref, wrt_ref,
                             brc_ref, x1_ref, u2_ref, ri_all, rf_ref, base_ref):
            pass
        pad_start, blk_ref[...] = _expert_layout(base_ref[:, 0:1])
        for t in range(ri_all.shape[0]):
            dest_ref[:, t * ts:(t + 1) * ts] = _slots(ri_all[t], pad_start)
        return

    @pl.when(first_tile)
    def _():
        qbuf[0:SC_HALO, :] = jnp.zeros((SC_HALO, wb), F32)
        for cvbuf in cvbufs:
            cvbuf[0:CV_HALO, :] = jnp.zeros((CV_HALO, PROJ_COLS), F32)

    if n_x_refs == 1:
        x = x_refs[0][0]
    else:
        x1p_ref, ya_ref, yb_ref, gates_ref, modp_ref, g2_ref, b2_ref = x_refs
        x = _moe_output(x1p_ref[0], ya_ref[0], yb_ref[0], gates_ref[...],
                        modp_ref[0][:, 5 * d:6 * d], g2_ref[...], b2_ref[...])
    mod = mod_ref[0]
    shift1, scale1 = mod[:, 0:d], mod[:, d:2 * d]
    ub = (x * (1.0 + scale1) + shift1).astype(BF16)

    def glu_block(c0):
        a = _dot(ub, w_in(3 * wb + c0))
        g = _dot(ub, w_in(4 * wb + c0))
        cvbufs[c0 // PROJ_COLS][CV_HALO:CV_HALO + ts, :] = a * jax.nn.sigmoid(g)

    tail = _mixer_tail(step, mg_buf, xs_buf, modt_ref, wo_ref, ln1g_ref, ln1b_ref, wrt_ref,
                       brc_ref, x1_ref, u2_ref, ri_all, rf_ref, base_ref)
    next(tail)
    glu_block(0)
    ties = {}

    def add_tie(chunk, tie):
        ties[chunk] = tie + ties[chunk] if chunk in ties else tie

    branch_z = {}

    def mixer_a():
        qbuf[SC_HALO:SC_HALO + ts, :] = pa_buf[:, wb:2 * wb] * pa_buf[:, 2 * wb:3 * wb]
        conv = scw_ref[SC_KERNEL - 1:SC_KERNEL, :] * qbuf[SC_HALO:SC_HALO + ts, :]
        for k in range(SC_KERNEL - 1):
            off = SC_HALO - (SC_KERNEL - 1) + k
            conv = conv + scw_ref[k:k + 1, :] * qbuf[off:off + ts, :]
        ybuf[0] = (pa_buf[:, 0:wb] * conv).astype(BF16)
        qbuf[0:SC_HALO, :] = qbuf[ts:ts + SC_HALO, :]
        branch_z[0] = _dot(ybuf[0], _bf16_weights(wbr_ref[0, 0]))

    def mixer_c():
        gu = jax.nn.gelu(pc_buf[:, 0:wb])
        gv = _layer_norm(jax.nn.gelu(pc_buf[:, wb:2 * wb]), sgg_ref[...],
                         sgbeta_ref[...]).astype(BF16)
        row = lax.broadcasted_iota(jnp.int32, (CHUNK, CHUNK), 0)
        col = lax.broadcasted_iota(jnp.int32, (CHUNK, CHUNK), 1)
        hd = wb // SG_HEADS
        for h in range(SG_HEADS):
            wm = jnp.where(row >= col, sgw_ref[h], 0.0).astype(BF16)
            bias = sgbt_ref[:, h:h + 1]
            for n in range(ts // CHUNK):
                rs = slice(n * CHUNK, (n + 1) * CHUNK)
                cs = slice(h * hd, (h + 1) * hd)
                mixed = _dot(wm, gv[rs, cs]) + bias
                ybuf[2, rs, cs] = (gu[rs, cs] * mixed).astype(BF16)
        branch_z[2] = _dot(ybuf[2], _bf16_weights(wbr_ref[0, 2]))

    def glu_rest():
        for c0 in range(PROJ_COLS, wb, PROJ_COLS):
            glu_block(c0)

    def tail_stage():
        next(tail, None)

    mxu_jobs = ([(pa_buf, w_in, c0, c0) for c0 in range(0, 3 * wb, PROJ_COLS)]
                + [(pc_buf, w_in, c0, 5 * wb + c0) for c0 in range(0, 2 * wb, PROJ_COLS)]
                + [(gl_buf, w_gate, c0, c0) for c0 in range(0, 3 * d, PROJ_COLS)])
    glu_rest()
    after_job = {7: tail_stage, 12: tail_stage}
    conv_jobs = [(r0, c0) for c0 in range(0, wb, CONV_COLS) for r0 in range(0, ts, CONV_ROWS)]
    chunks_per_job = len(conv_jobs) // len(mxu_jobs)
    for c, conv_job in enumerate(conv_jobs):
        _conv31_chunk(cvw_ref, cvbufs, cvout, *conv_job, ties.get(c))
        if (c + 1) % chunks_per_job:
            continue
        i = c // chunks_per_job
        dst, weights, dc, wc = mxu_jobs[i]
        res = _dot(ub, weights(wc))
        dst[:, dc:dc + PROJ_COLS] = res
        if i < len(mxu_jobs) - TIE_FREE_JOBS:
            add_tie(c + TIE_LAG, _zero_after(res[ts - 1:ts, PROJ_COLS - CONV_COLS:PROJ_COLS]))
        if i in after_job:
            after_job[i]()
    for cvbuf in cvbufs:
        cvbuf[0:CV_HALO, :] = cvbuf[ts:ts + CV_HALO, :]

    mixer_a()
    cv = _layer_norm(cvout[...] + cvb_ref[...], cvg_ref[...], cvbeta_ref[...])
    ybuf[1] = (cv * jax.nn.sigmoid(cv)).astype(BF16)
    branch_z[1] = _dot(ybuf[1], _bf16_weights(wbr_ref[0, 1]))
    mixer_c()

    merged = None
    for n in range(3):
        gated = jax.nn.sigmoid(gl_buf[:, n * d:(n + 1) * d] + bg_ref[n:n + 1, :]) * branch_z[n]
        merged = gated if merged is None else merged + gated
    mg_buf[...] = merged.astype(BF16)
    xs_buf[...] = x


def _mixer_tail(step, mg_buf, xs_buf, modt_ref, wo_ref, ln1g_ref, ln1b_ref, wrt_ref, brc_ref,
                x1_ref, u2_ref, ri_all, rf_ref, base_ref):
    ts = MIX_ROWS
    d = D_MODEL
    modt = modt_ref[0]
    gate1, shift2, scale2 = modt[:, 2 * d:3 * d], modt[:, 3 * d:4 * d], modt[:, 4 * d:5 * d]
    hmix = _dot(mg_buf[...], _bf16_weights(wo_ref[0]))
    x1 = _layer_norm(ALPHA * xs_buf[...] + gate1 * hmix, ln1g_ref[...], ln1b_ref[...])
    x1_ref[0] = x1
    u2 = x1 * (1.0 + scale2) + shift2
    u2_ref[0] = _pack_bf16_pairs(u2)
    yield

    logits = lax.dot_general(wrt_ref[...], u2, (((1,), (1,)), ((), ())),
                             preferred_element_type=F32,
                             precision=lax.Precision.HIGHEST)
    mx = jnp.max(logits, axis=0, keepdims=True)
    ex = jnp.exp(logits - mx)
    scores = ex / jnp.sum(ex, axis=0, keepdims=True)
    sel = scores + brc_ref[...]
    tops = []
    for g in range(N_GROUPS):
        rows = [sel[g * EXPERTS_PER_GROUP + k:g * EXPERTS_PER_GROUP + k + 1, :]
                for k in range(EXPERTS_PER_GROUP)]
        tops.append(_top2_of4(rows))
    best = tops[0][0] + tops[0][2]
    g_idx = jnp.zeros(best.shape, jnp.int32)
    loc1, loc2 = tops[0][1], tops[0][3]
    for g in range(1, N_GROUPS):
        gs = tops[g][0] + tops[g][2]
        gt = gs > best
        best = jnp.where(gt, gs, best)
        g_idx = jnp.where(gt, g, g_idx)
        loc1 = jnp.where(gt, tops[g][1], loc1)
        loc2 = jnp.where(gt, tops[g][3], loc2)
    e0 = g_idx * EXPERTS_PER_GROUP + loc1
    e1 = g_idx * EXPERTS_PER_GROUP + loc2
    erow = lax.broadcasted_iota(jnp.int32, (N_EXPERTS, ts), 0)
    is0 = erow == e0
    is1 = erow == e1
    s0 = jnp.sum(jnp.where(is0, scores, 0.0), axis=0, keepdims=True)
    s1 = jnp.sum(jnp.where(is1, scores, 0.0), axis=0, keepdims=True)
    ssum = s0 + s1
    yield

    onehot = jnp.logical_or(is0, is1).astype(BF16)
    src = lax.broadcasted_iota(jnp.int32, (ts, ts), 0)
    dst = lax.broadcasted_iota(jnp.int32, (ts, ts), 1)
    earlier = (src < dst).astype(BF16)
    prior = _dot(onehot, earlier) + base_ref[:, 0:1]
    r0 = jnp.sum(jnp.where(is0, prior, 0.0), axis=0, keepdims=True)
    r1 = jnp.sum(jnp.where(is1, prior, 0.0), axis=0, keepdims=True)
    counts = jnp.sum(onehot.astype(F32), axis=1, keepdims=True)
    base_ref[...] = base_ref[...] + jnp.where(step > 0, counts, 0.0)

    zi = jnp.zeros((SUBLANES - 4, ts), jnp.int32)
    ri_all[jnp.maximum(step - 1, 0)] = jnp.concatenate(
        [e0, e1, r0.astype(jnp.int32), r1.astype(jnp.int32), zi], axis=0)
    zf = jnp.zeros((SUBLANES - 2, ts), F32)
    rf_ref[...] = jnp.concatenate([s0 / ssum, s1 / ssum, zf], axis=0).T


def _mixer(layer, xsrc, mod3, mb0, bsz, w_in, scw, cvw, cvb, cvg, cvbeta, sgg, sgbeta, sgw, sgbt,
           wbr, wg, bg, wo, ln1g, ln1b, wrt, brc):
    _, seq, d = xsrc[0].shape
    ts = MIX_ROWS
    ns = seq // ts
    n_tok = bsz * seq

    n_tiles = bsz * ns

    def const(shape):
        zeros = (0,) * len(shape)
        return pl.BlockSpec(shape, lambda s: zeros, pipeline_mode=pl.Buffered(1))

    def first_half(s):
        return jnp.minimum(s, n_tiles - 1)

    def second_half(s):
        return jnp.maximum(s - 1, 0)

    def layer_weights(shape):
        block = (1,) + tuple(shape[1:])
        index = (layer,) + (0,) * (len(shape) - 1)
        return pl.BlockSpec(block, lambda s: index, pipeline_mode=pl.Buffered(1))

    if len(xsrc) == 2:
        x, xb0 = xsrc
        x_args = [x]
        x_specs = [pl.BlockSpec((1, ts, d),
                                lambda s: (first_half(s) // ns + xb0, first_half(s) % ns, 0))]
    else:
        x1p, pairs, gates, mod3p, g2, b2 = xsrc
        x_args = [x1p, pairs, pairs, gates, mod3p, g2, b2]
        x_specs = [
            pl.BlockSpec((1, ts, d), lambda s: (first_half(s) // ns, first_half(s) % ns, 0)),
            pl.BlockSpec((1, ts, d // 2), lambda s: (0, first_half(s), 0)),
            pl.BlockSpec((1, ts, d // 2), lambda s: (1, first_half(s), 0)),
            pl.BlockSpec((ts, SUBLANES), lambda s: (first_half(s), 0)),
            pl.BlockSpec((1, 1, 6 * d), lambda s: (first_half(s) // ns + mb0, 0, 0)),
            const(g2.shape), const(b2.shape),
        ]
    in_specs = x_specs + [
        pl.BlockSpec((1, 1, 6 * d), lambda s: (first_half(s) // ns + mb0, 0, 0)),
        pl.BlockSpec((1, 1, 6 * d), lambda s: (second_half(s) // ns + mb0, 0, 0)),
        layer_weights(w_in.shape), const(scw.shape), const(cvw.shape), const(cvb.shape),
        const(cvg.shape), const(cvbeta.shape), const(sgg.shape), const(sgbeta.shape),
        const(sgw.shape), const(sgbt.shape), layer_weights(wbr.shape), layer_weights(wg.shape),
        const(bg.shape), layer_weights(wo.shape), const(ln1g.shape), const(ln1b.shape),
        const(wrt.shape), const(brc.shape),
    ]
    out_specs = [
        pl.BlockSpec((1, ts, d), lambda s: (second_half(s) // ns, second_half(s) % ns, 0)),
        pl.BlockSpec((1, ts, d // 2), lambda s: (second_half(s) // ns, second_half(s) % ns, 0)),
        pl.BlockSpec((SUBLANES, n_tok), lambda s: (0, 0)),
        pl.BlockSpec((ts, SUBLANES), lambda s: (second_half(s), 0)),
        pl.BlockSpec((SUBLANES, LANES), lambda s: (0, 0)),
    ]
    out_shape = [
        jax.ShapeDtypeStruct((bsz, seq, d), F32),
        jax.ShapeDtypeStruct((bsz, seq, d // 2), jnp.uint32),
        jax.ShapeDtypeStruct((SUBLANES, n_tok), jnp.int32),
        jax.ShapeDtypeStruct((n_tok, SUBLANES), F32),
        jax.ShapeDtypeStruct((SUBLANES, LANES), jnp.int32),
    ]
    return pl.pallas_call(
        functools.partial(_mixer_kernel, ns, len(x_args)),
        grid=(n_tiles + 1,),
        in_specs=in_specs,
        out_specs=out_specs,
        out_shape=out_shape,
        scratch_shapes=[
            pltpu.VMEM((SC_HALO + ts, W_BRANCH), F32),
            pltpu.VMEM((CV_HALO + ts, PROJ_COLS), F32),
            pltpu.VMEM((CV_HALO + ts, PROJ_COLS), F32),
            pltpu.VMEM((ts, W_BRANCH), F32),
            pltpu.VMEM((3, ts, W_BRANCH), BF16),
            pltpu.VMEM((ts, 3 * W_BRANCH), F32),
            pltpu.VMEM((ts, 2 * W_BRANCH), F32),
            pltpu.VMEM((ts, 3 * D_MODEL), F32),
            pltpu.VMEM((ts, d), BF16),
            pltpu.VMEM((ts, d), F32),
            pltpu.VMEM((n_tiles, SUBLANES, ts), jnp.int32),
            pltpu.VMEM((N_EXPERTS, LANES), F32),
        ],
        compiler_params=pltpu.CompilerParams(
            dimension_semantics=("arbitrary",),
            vmem_limit_bytes=MIXER_VMEM_LIMIT),
    )(*x_args, mod3, mod3, w_in, scw, cvw, cvb, cvg, cvbeta, sgg, sgbeta, sgw, sgbt, wbr, wg, bg,
      wo, ln1g, ln1b, wrt, brc)


def _sc_workers():
    info = plsc.get_sparse_core_info()
    return info.num_cores, info.num_cores * info.num_subcores


def _sc_scatter_rows(rows, dest_a, dest_b, n_out):
    n, d = rows.shape
    nc, nw = _sc_workers()
    per_w = n // nw
    n_win = per_w // SC_WINDOW
    ia = dest_a.reshape(nw, n_win, SC_WINDOW)
    ib = dest_b.reshape(nw, n_win, SC_WINDOW)
    mesh = plsc.VectorSubcoreMesh(core_axis_name="c", subcore_axis_name="s")

    @functools.partial(
        pl.kernel, mesh=mesh,
        out_type=jax.ShapeDtypeStruct((n_out, d), rows.dtype),
        scratch_types=[
            pltpu.VMEM((n_win, SC_WINDOW), jnp.int32),
            pltpu.VMEM((n_win, SC_WINDOW), jnp.int32),
            pltpu.VMEM((SC_WINDOW, d), rows.dtype),
        ],
    )
    def scatter(rows_hbm, ia_hbm, ib_hbm, out_hbm, ia_v, ib_v, rows_v):
        wid = lax.axis_index("s") * nc + lax.axis_index("c")
        pltpu.sync_copy(ia_hbm.at[wid], ia_v)
        pltpu.sync_copy(ib_hbm.at[wid], ib_v)
        base = wid * per_w

        @pl.loop(0, n_win)
        def _(j):
            pltpu.sync_copy(rows_hbm.at[pl.ds(base + j * SC_WINDOW, SC_WINDOW)], rows_v)
            pltpu.sync_copy(rows_v, out_hbm.at[ia_v.at[j]])
            pltpu.sync_copy(rows_v, out_hbm.at[ib_v.at[j]])

    return scatter(rows, ia, ib)


def _sc_gather_rows(table, idx):
    n = idx.shape[0]
    d = table.shape[1]
    nc, nw = _sc_workers()
    per_w = n // nw
    n_win = per_w // SC_WINDOW
    idx3 = idx.reshape(nw, n_win, SC_WINDOW)
    mesh = plsc.VectorSubcoreMesh(core_axis_name="c", subcore_axis_name="s")

    @functools.partial(
        pl.kernel, mesh=mesh,
        out_type=jax.ShapeDtypeStruct((n, d), table.dtype),
        scratch_types=[
            pltpu.VMEM((n_win, SC_WINDOW), jnp.int32),
            pltpu.VMEM((SC_WINDOW, d), table.dtype),
        ],
    )
    def gather(table_hbm, idx_hbm, out_hbm, idx_v, rows_v):
        wid = lax.axis_index("s") * nc + lax.axis_index("c")
        pltpu.sync_copy(idx_hbm.at[wid], idx_v)
        base = wid * per_w

        @pl.loop(0, n_win)
        def _(j):
            pltpu.sync_copy(table_hbm.at[idx_v.at[j]], rows_v)
            pltpu.sync_copy(rows_v, out_hbm.at[pl.ds(base + j * SC_WINDOW, SC_WINDOW)])

    return gather(table, idx3)


def _expert_kernel(layer, be_ref, nused_ref, x_ref, w1_hbm, w3_hbm, w2_hbm, o_ref,
                   w1f, w3f, w2f, w1s, w3s, w2s, sems, group_ref):
    i = pl.program_id(0)
    n_used = nused_ref[0]
    e_now = be_ref[i]

    def fetch(e, slot):
        return [pltpu.make_async_copy(w_hbm.at[layer, e], w_f.at[slot], sems.at[k, slot])
                for k, (w_hbm, w_f) in enumerate(((w1_hbm, w1f), (w3_hbm, w3f), (w2_hbm, w2f)))]

    @pl.when(i == 0)
    def _():
        group_ref[0] = 0
        for copy in fetch(e_now, 0):
            copy.start()

    first_of_group = jnp.logical_and(
        i < n_used, jnp.logical_or(i == 0, e_now != be_ref[jnp.maximum(i - 1, 0)]))

    @pl.when(first_of_group)
    def _():
        slot = group_ref[0] % 2
        for copy in fetch(e_now, slot):
            copy.wait()
        w1s[...] = w1f[slot].astype(BF16)
        w3s[...] = w3f[slot].astype(BF16)
        w2s[...] = w2f[slot].astype(BF16)
        last = be_ref.shape[0] - 1
        j = lax.while_loop(
            lambda j: jnp.logical_and(j < n_used, be_ref[jnp.minimum(j, last)] == e_now),
            lambda j: j + 1, i + 1)
        next_e = be_ref[jnp.minimum(j, last)]

        @pl.when(j < n_used)
        def _():
            for copy in fetch(next_e, 1 - slot):
                copy.start()

        group_ref[0] = group_ref[0] + 1

    @pl.when(i < n_used)
    def _():
        lo, hi = _unpack_bf16_pairs(x_ref[...])
        xb = jnp.concatenate([lo, hi], axis=1).astype(BF16)
        a = _dot(xb, w1s[...])
        b = _dot(xb, w3s[...])
        for r0 in range(0, MOE_ROWS, MOE_SLAB):
            rs = slice(r0, r0 + MOE_SLAB)
            h = a[rs] * jax.nn.sigmoid(a[rs]) * b[rs]
            o_ref[rs, :] = _pack_bf16_pairs(_dot(h.astype(BF16), w2s[...]))


def _experts(layer, buf, block_e, n_used, w1, w3, w2):
    n_rows, dp = buf.shape
    nb = n_rows // MOE_ROWS
    d, fe = w1.shape[-2:]

    def row_map(i, be, nu):
        return (jnp.minimum(i, nu[0] - 1), 0)

    grid_spec = pltpu.PrefetchScalarGridSpec(
        num_scalar_prefetch=2,
        grid=(nb,),
        in_specs=[
            pl.BlockSpec((MOE_ROWS, dp), row_map),
            pl.BlockSpec(memory_space=pl.ANY),
            pl.BlockSpec(memory_space=pl.ANY),
            pl.BlockSpec(memory_space=pl.ANY),
        ],
        out_specs=pl.BlockSpec((MOE_ROWS, dp), row_map),
        scratch_shapes=[
            pltpu.VMEM((2, d, fe), F32), pltpu.VMEM((2, d, fe), F32), pltpu.VMEM((2, fe, d), F32),
            pltpu.VMEM((d, fe), BF16), pltpu.VMEM((d, fe), BF16), pltpu.VMEM((fe, d), BF16),
            pltpu.SemaphoreType.DMA((3, 2)),
            pltpu.SMEM((1,), jnp.int32),
        ],
    )
    return pl.pallas_call(
        functools.partial(_expert_kernel, layer),
        grid_spec=grid_spec,
        out_shape=jax.ShapeDtypeStruct((n_rows, dp), jnp.uint32),
        compiler_params=pltpu.CompilerParams(dimension_semantics=("arbitrary",),
                                             vmem_limit_bytes=EXPERT_VMEM_LIMIT),
    )(block_e, n_used, buf, w1, w3, w2)


def _combine_kernel(x1_ref, ya_ref, yb_ref, gates_ref, mod_ref, g_ref, b_ref, *rest):
    o_ref = rest[-1]
    d = D_MODEL
    o_ref[0] = _moe_output(x1_ref[0], ya_ref[0], yb_ref[0], gates_ref[...],
                           mod_ref[0][:, 5 * d:6 * d], g_ref[...], b_ref[...])


def _combine(x1, pairs, gates, mod3, mb0, g, b, out_bsz, ob0, prev):
    bsz, seq, d = x1.shape
    ts = COMB_ROWS
    ns = seq // ts
    in_specs = [
        pl.BlockSpec((1, ts, d), lambda i, j: (i, j, 0)),
        pl.BlockSpec((1, ts, d // 2), lambda i, j: (0, i * ns + j, 0)),
        pl.BlockSpec((1, ts, d // 2), lambda i, j: (1, i * ns + j, 0)),
        pl.BlockSpec((ts, SUBLANES), lambda i, j: (i * ns + j, 0)),
        pl.BlockSpec((1, 1, 6 * d), lambda i, j: (i + mb0, 0, 0)),
        pl.BlockSpec((1, d), lambda i, j: (0, 0)),
        pl.BlockSpec((1, d), lambda i, j: (0, 0)),
    ]
    args = [x1, pairs, pairs, gates, mod3, g, b]
    aliases = {}
    if prev is not None:
        in_specs.append(pl.BlockSpec(memory_space=pl.ANY))
        aliases = {len(args): 0}
        args.append(prev)
    return pl.pallas_call(
        _combine_kernel,
        grid=(bsz, ns),
        in_specs=in_specs,
        out_specs=pl.BlockSpec((1, ts, d), lambda i, j: (i + ob0, j, 0)),
        out_shape=jax.ShapeDtypeStruct((out_bsz, seq, d), F32),
        input_output_aliases=aliases,
        compiler_params=pltpu.CompilerParams(dimension_semantics=("arbitrary", "arbitrary")),
    )(*args)


def _expert_layout(counts):
    shift = MOE_ROWS.bit_length() - 1
    padded = lax.shift_left(
        lax.shift_right_logical(counts.astype(jnp.int32) + (MOE_ROWS - 1), shift), shift)
    e_out = lax.broadcasted_iota(jnp.int32, (N_EXPERTS, N_EXPERTS), 0)
    e_in = lax.broadcasted_iota(jnp.int32, (N_EXPERTS, N_EXPERTS), 1)
    upto = (e_in <= e_out).astype(BF16)
    padded_f = jnp.broadcast_to(padded.astype(F32), (N_EXPERTS, LANES))
    pad_end = _dot(upto, padded_f.astype(BF16))
    pad_start = pad_end[:, 0:1] - padded.astype(F32)
    block_start = (lax.broadcasted_iota(jnp.int32, (N_EXPERTS, LANES), 1) * MOE_ROWS).astype(F32)
    block_e = jnp.sum((block_start >= pad_end).astype(jnp.int32), axis=0, keepdims=True)
    n_used = lax.shift_right_logical(pad_end[N_EXPERTS - 1:N_EXPERTS, :].astype(jnp.int32), shift)
    blk = jnp.concatenate([jnp.minimum(block_e, N_EXPERTS - 1), n_used,
                           jnp.zeros((SUBLANES - 2, LANES), jnp.int32)], axis=0)
    return pad_start, blk


def _slots(ri, pad_start):
    n = ri.shape[1]
    erow = lax.broadcasted_iota(jnp.int32, (N_EXPERTS, n), 0)
    rows = []
    for k in range(TOP_K):
        start = jnp.sum(jnp.where(erow == ri[k:k + 1, :], pad_start, 0.0), axis=0, keepdims=True)
        rows.append(start.astype(jnp.int32) + ri[TOP_K + k:TOP_K + k + 1, :])
    rows.append(jnp.zeros((SUBLANES - TOP_K, n), jnp.int32))
    return jnp.concatenate(rows, axis=0)


def kernel(x, c, w_ada, b_ada, w_in, sc_conv, cv_conv, cv_conv_b, cv_ln_g, cv_ln_b, sg_ln_g, sg_ln_b, sg_w, sg_b, w_branch, w_gate, b_gate, w_o, ln1_g, ln1_b, w_router, b_router, w1, w3, w2, ln2_g, ln2_b):
    bsz, seq, d = x.shape
    mod3 = _ada(c, w_ada, b_ada).reshape(DEPTH, bsz, 1, 6 * d)
    wrt = w_router.T
    brc = b_router.reshape(N_EXPERTS, 1)
    w_in_p = _pack_weights(w_in)
    w_gate_p = _pack_weights(jnp.transpose(w_gate, (0, 2, 1, 3)).reshape(DEPTH * 3, d, d)).reshape(
        DEPTH, 3, d // 2, d)
    w_o_p = _pack_weights(w_o)
    w_branch_p = _pack_weights(w_branch.reshape(DEPTH * 3, W_BRANCH, d)).reshape(
        DEPTH, 3, W_BRANCH // 2, d)
    cb = bsz // N_CHAINS
    n_tok = cb * seq
    n_blocks = (n_tok * TOP_K + N_EXPERTS * (MOE_ROWS - 1) + MOE_ROWS - 1) // MOE_ROWS
    assert n_blocks <= LANES and MOE_ROWS & (MOE_ROWS - 1) == 0
    chains = [(x, h * cb) for h in range(N_CHAINS)]
    for l in range(DEPTH):
        last = l == DEPTH - 1
        result = None
        for h in range(N_CHAINS):
            x1, u2, dest, rf, blk = _mixer(
                l, chains[h], mod3[l], h * cb, cb, w_in_p, sc_conv[l], cv_conv[l],
                cv_conv_b[l].reshape(1, -1), cv_ln_g[l].reshape(1, -1), cv_ln_b[l].reshape(1, -1),
                sg_ln_g[l].reshape(1, -1), sg_ln_b[l].reshape(1, -1), sg_w[l], sg_b[l].T,
                w_branch_p, w_gate_p, b_gate[l],
                w_o_p, ln1_g[l].reshape(1, -1), ln1_b[l].reshape(1, -1), wrt, brc)
            buf = _sc_scatter_rows(u2.reshape(n_tok, d // 2), dest[0], dest[1], n_blocks * MOE_ROWS)
            obuf = _experts(l, buf, blk[0, 0:n_blocks], blk[1, 0:1], w1, w3, w2)
            pair_idx = dest[0:TOP_K].reshape(TOP_K * n_tok)
            pairs = _sc_gather_rows(obuf, pair_idx).reshape(TOP_K, n_tok, d // 2)
            gates = rf
            g2, b2 = ln2_g[l].reshape(1, -1), ln2_b[l].reshape(1, -1)
            if last:
                result = _combine(x1, pairs, gates, mod3[l], h * cb, g2, b2, bsz, h * cb, result)
            else:
                chains[h] = (x1, pairs, gates, mod3[l], g2, b2)
    return result
```

```python
import functools

import jax
import jax.numpy as jnp
from jax import lax
from jax.experimental import pallas as pl
from jax.experimental.pallas import tpu as pltpu
from jax.experimental.pallas import tpu_sc as plsc

D_MODEL = 1024
DEPTH = 2
W_BRANCH = 1024
SC_KERNEL = 3
CV_KERNEL = 31
CHUNK = 128
SG_HEADS = 8
N_EXPERTS = 16
N_GROUPS = 4
EXPERTS_PER_GROUP = N_EXPERTS // N_GROUPS
TOP_K = 2
D_EXPERT = 512
ALPHA = (2.0 * DEPTH) ** 0.25
LN_EPS = 1e-5

F32 = jnp.float32
BF16 = jnp.bfloat16

V7X_VMEM_BYTES = 64 * 1024 * 1024
MIXER_VMEM_LIMIT = V7X_VMEM_BYTES - 6 * 1024 * 1024
EXPERT_VMEM_LIMIT = V7X_VMEM_BYTES // 2
SUBLANES = 8
LANES = 128

MIX_ROWS = 256
SC_HALO = SUBLANES
CV_HALO = 32
CONV_ROWS = 128
TIE_LAG = 2
TIE_FREE_JOBS = 4
CONV_COLS = 128
PROJ_COLS = 512
PACK_ELEMS = 2 * 1024 * 1024
MOE_ROWS = 512
MOE_SLAB = 256
COMB_ROWS = 1024
SC_WINDOW = 32
N_CHAINS = 2


def _dot(a, b):
    return jnp.dot(a, b, preferred_element_type=F32)


def _pack_bf16_pairs(v):
    m = v.shape[1] // 2
    lo = lax.bitcast_convert_type(v[:, 0:m].astype(BF16).astype(F32), jnp.uint32)
    hi = lax.bitcast_convert_type(v[:, m:2 * m].astype(BF16).astype(F32), jnp.uint32)
    return jnp.bitwise_or(jnp.bitwise_and(hi, jnp.uint32(0xFFFF0000)),
                          lax.shift_right_logical(lo, jnp.uint32(16)))


def _unpack_bf16_pairs(w):
    lo = lax.bitcast_convert_type(lax.shift_left(w, jnp.uint32(16)), F32)
    hi = lax.bitcast_convert_type(jnp.bitwise_and(w, jnp.uint32(0xFFFF0000)), F32)
    return lo, hi


def _layer_norm(v, g, b):
    mu = jnp.mean(v, axis=-1, keepdims=True)
    vc = v - mu
    var = jnp.mean(vc * vc, axis=-1, keepdims=True)
    return vc * lax.rsqrt(var + LN_EPS) * g + b


def _ada_kernel(c_ref, w_ref, b_ref, o_ref):
    c = c_ref[...]
    c_act = c * jax.nn.sigmoid(c)
    o_ref[0] = jnp.dot(c_act, w_ref[0], preferred_element_type=F32,
                       precision=lax.Precision.HIGHEST) + b_ref[0]


def _ada(c, w_ada, b_ada):
    bsz, d = c.shape
    n = w_ada.shape[-1]
    tn = 1536
    return pl.pallas_call(
        _ada_kernel,
        grid=(DEPTH, n // tn),
        in_specs=[
            pl.BlockSpec((bsz, d), lambda l, j: (0, 0)),
            pl.BlockSpec((1, d, tn), lambda l, j: (l, 0, j)),
            pl.BlockSpec((1, 1, tn), lambda l, j: (l, 0, j)),
        ],
        out_specs=pl.BlockSpec((1, bsz, tn), lambda l, j: (l, 0, j)),
        out_shape=jax.ShapeDtypeStruct((DEPTH, bsz, n), F32),
    )(c, w_ada, b_ada.reshape(DEPTH, 1, n))


def _top2_of4(rows):
    m1 = rows[0]
    i1 = jnp.zeros(rows[0].shape, jnp.int32)
    for k in range(1, 4):
        gt = rows[k] > m1
        m1 = jnp.where(gt, rows[k], m1)
        i1 = jnp.where(gt, k, i1)
    m2 = jnp.full(rows[0].shape, -jnp.inf, F32)
    i2 = jnp.zeros(rows[0].shape, jnp.int32)
    for k in range(4):
        cand = jnp.where(i1 == k, -jnp.inf, rows[k])
        gt = cand > m2
        m2 = jnp.where(gt, cand, m2)
        i2 = jnp.where(gt, k, i2)
    return m1, i1, m2, i2


def _zero_after(v):
    u = lax.bitcast_convert_type(v, jnp.uint32)
    u = lax.shift_right_logical(lax.shift_right_logical(u, jnp.uint32(16)), jnp.uint32(16))
    return lax.bitcast_convert_type(u, F32)


def _conv31_chunk(cvw_ref, cvbufs, cvout, r0, c0, tie):
    cs = slice(c0, c0 + CONV_COLS)
    cvbuf = cvbufs[c0 // PROJ_COLS]
    bs = slice(c0 % PROJ_COLS, c0 % PROJ_COLS + CONV_COLS)
    acc = None
    for r in range(SUBLANES):
        lead = SUBLANES if r else 0
        part = None
        for m in range((CV_KERNEL - 1 - r) // SUBLANES + 1):
            k = CV_KERNEL - 1 - (SUBLANES * m + r)
            start = CV_HALO + r0 - lead - SUBLANES * m
            w_row = cvw_ref[k:k + 1, cs]
            if tie is not None and acc is None and part is None:
                w_row = w_row + tie
            term = w_row * cvbuf[start:start + lead + CONV_ROWS, bs]
            part = term if part is None else part + term
        part = part[lead - r:lead - r + CONV_ROWS]
        acc = part if acc is None else acc + part
    cvout[r0:r0 + CONV_ROWS, cs] = acc


def _bf16_weights(packed):
    return pltpu.bitcast(packed, BF16)


def _pack_weights_kernel(w_ref, o_ref):
    o_ref[0] = pltpu.bitcast(w_ref[0].astype(BF16), jnp.uint32)


def _pack_weights(w):
    g, k, n = w.shape
    kb = k
    while kb * n > PACK_ELEMS and kb % (4 * SUBLANES) == 0:
        kb //= 2
    return pl.pallas_call(
        _pack_weights_kernel,
        grid=(g, k // kb),
        in_specs=[pl.BlockSpec((1, kb, n), lambda i, j: (i, j, 0))],
        out_specs=pl.BlockSpec((1, kb // 2, n), lambda i, j: (i, j, 0)),
        out_shape=jax.ShapeDtypeStruct((g, k // 2, n), jnp.uint32),
        compiler_params=pltpu.CompilerParams(vmem_limit_bytes=MIXER_VMEM_LIMIT),
    )(w)


def _moe_output(x1, ya_packed, yb_packed, gates, gate2, g, b):
    ya = jnp.concatenate(_unpack_bf16_pairs(ya_packed), axis=1)
    yb = jnp.concatenate(_unpack_bf16_pairs(yb_packed), axis=1)
    h = gates[:, 0:1] * ya + gates[:, 1:2] * yb
    return _layer_norm(ALPHA * x1 + gate2 * h, g, b)


def _mixer_kernel(tiles_per_seq, n_x_refs, *refs):
    last_step = pl.num_programs(0) - 1

    @pl.when(pl.program_id(0) < last_step)
    def _():
        _mixer_step(tiles_per_seq, n_x_refs, True, *refs)

    @pl.when(pl.program_id(0) == last_step)
    def _():
        _mixer_step(tiles_per_seq, n_x_refs, False, *refs)


def _mixer_step(tiles_per_seq, n_x_refs, run_first_half, *refs):
    x_refs = refs[:n_x_refs]
    (mod_ref, modt_ref, w_in_ref, scw_ref, cvw_ref, cvb_ref, cvg_ref, cvbeta_ref, sgg_ref,
     sgbeta_ref, sgw_ref, sgbt_ref, wbr_ref, wg_ref, bg_ref, wo_ref, ln1g_ref, ln1b_ref, wrt_ref,
     brc_ref,
     x1_ref, u2_ref, dest_ref, rf_ref, blk_ref,
     qbuf, cvbuf0, cvbuf1, cvout, ybuf, pa_buf, pc_buf, gl_buf, mg_buf, xs_buf, ri_all,
     base_ref) = refs[n_x_refs:]
    cvbufs = (cvbuf0, cvbuf1)
    ts = MIX_ROWS
    d = D_MODEL
    wb = W_BRANCH
    step = pl.program_id(0)
    tile = jnp.minimum(step, pl.num_programs(0) - 2)
    first_tile = tile % tiles_per_seq == 0

    @pl.when(step == 0)
    def _():
        base_ref[...] = jnp.zeros_like(base_ref)
        mg_buf[...] = jnp.zeros_like(mg_buf)
        xs_buf[...] = jnp.zeros_like(xs_buf)

    def w_in(c0):
        return _bf16_weights(w_in_ref[0, :, c0:c0 + PROJ_COLS])

    def w_gate(c0):
        n, col = divmod(c0, d)
        return _bf16_weights(wg_ref[0, n, :, col:col + PROJ_COLS])

    if not run_first_half:
        for _ in _mixer_tail(step, mg_buf, xs_buf, modt_ref, wo_ref, ln1g_ref, ln1b_ref, wrt_ref,
                             brc_ref, x1_ref, u2_ref, ri_all, rf_ref, base_ref):
            pass
        pad_start, blk_ref[...] = _expert_layout(base_ref[:, 0:1])
        for t in range(ri_all.shape[0]):
            dest_ref[:, t * ts:(t + 1) * ts] = _slots(ri_all[t], pad_start)
        return

    @pl.when(first_tile)
    def _():
        qbuf[0:SC_HALO, :] = jnp.zeros((SC_HALO, wb), F32)
        for cvbuf in cvbufs:
            cvbuf[0:CV_HALO, :] = jnp.zeros((CV_HALO, PROJ_COLS), F32)

    if n_x_refs == 1:
        x = x_refs[0][0]
    else:
        x1p_ref, ya_ref, yb_ref, gates_ref, modp_ref, g2_ref, b2_ref = x_refs
        x = _moe_output(x1p_ref[0], ya_ref[0], yb_ref[0], gates_ref[...],
                        modp_ref[0][:, 5 * d:6 * d], g2_ref[...], b2_ref[...])
    mod = mod_ref[0]
    shift1, scale1 = mod[:, 0:d], mod[:, d:2 * d]
    ub = (x * (1.0 + scale1) + shift1).astype(BF16)

    def glu_block(c0):
        a = _dot(ub, w_in(3 * wb + c0))
        g = _dot(ub, w_in(4 * wb + c0))
        cvbufs[c0 // PROJ_COLS][CV_HALO:CV_HALO + ts, :] = a * jax.nn.sigmoid(g)

    tail = _mixer_tail(step, mg_buf, xs_buf, modt_ref, wo_ref, ln1g_ref, ln1b_ref, wrt_ref,
                       brc_ref, x1_ref, u2_ref, ri_all, rf_ref, base_ref)
    next(tail)
    glu_block(0)
    ties = {}

    def add_tie(chunk, tie):
        ties[chunk] = tie + ties[chunk] if chunk in ties else tie

    branch_z = {}

    def mixer_a():
        qbuf[SC_HALO:SC_HALO + ts, :] = pa_buf[:, wb:2 * wb] * pa_buf[:, 2 * wb:3 * wb]
        conv = scw_ref[SC_KERNEL - 1:SC_KERNEL, :] * qbuf[SC_HALO:SC_HALO + ts, :]
        for k in range(SC_KERNEL - 1):
            off = SC_HALO - (SC_KERNEL - 1) + k
            conv = conv + scw_ref[k:k + 1, :] * qbuf[off:off + ts, :]
        ybuf[0] = (pa_buf[:, 0:wb] * conv).astype(BF16)
        qbuf[0:SC_HALO, :] = qbuf[ts:ts + SC_HALO, :]
        branch_z[0] = _dot(ybuf[0], _bf16_weights(wbr_ref[0, 0]))

    def mixer_c():
        gu = jax.nn.gelu(pc_buf[:, 0:wb])
        gv = _layer_norm(jax.nn.gelu(pc_buf[:, wb:2 * wb]), sgg_ref[...],
                         sgbeta_ref[...]).astype(BF16)
        row = lax.broadcasted_iota(jnp.int32, (CHUNK, CHUNK), 0)
        col = lax.broadcasted_iota(jnp.int32, (CHUNK, CHUNK), 1)
        hd = wb // SG_HEADS
        for h in range(SG_HEADS):
            wm = jnp.where(row >= col, sgw_ref[h], 0.0).astype(BF16)
            bias = sgbt_ref[:, h:h + 1]
            for n in range(ts // CHUNK):
                rs = slice(n * CHUNK, (n + 1) * CHUNK)
                cs = slice(h * hd, (h + 1) * hd)
                mixed = _dot(wm, gv[rs, cs]) + bias
                ybuf[2, rs, cs] = (gu[rs, cs] * mixed).astype(BF16)
        branch_z[2] = _dot(ybuf[2], _bf16_weights(wbr_ref[0, 2]))

    def glu_rest():
        for c0 in range(PROJ_COLS, wb, PROJ_COLS):
            glu_block(c0)

    def tail_stage():
        next(tail, None)

    mxu_jobs = ([(pa_buf, w_in, c0, c0) for c0 in range(0, 3 * wb, PROJ_COLS)]
                + [(pc_buf, w_in, c0, 5 * wb + c0) for c0 in range(0, 2 * wb, PROJ_COLS)]
                + [(gl_buf, w_gate, c0, c0) for c0 in range(0, 3 * d, PROJ_COLS)])
    glu_rest()
    after_job = {7: tail_stage, 12: tail_stage}
    conv_jobs = [(r0, c0) for c0 in range(0, wb, CONV_COLS) for r0 in range(0, ts, CONV_ROWS)]
    chunks_per_job = len(conv_jobs) // len(mxu_jobs)
    for c, conv_job in enumerate(conv_jobs):
        _conv31_chunk(cvw_ref, cvbufs, cvout, *conv_job, ties.get(c))
        if (c + 1) % chunks_per_job:
            continue
        i = c // chunks_per_job
        dst, weights, dc, wc = mxu_jobs[i]
        res = _dot(ub, weights(wc))
        dst[:, dc:dc + PROJ_COLS] = res
        if i < len(mxu_jobs) - TIE_FREE_JOBS:
            add_tie(c + TIE_LAG, _zero_after(res[ts - 1:ts, PROJ_COLS - CONV_COLS:PROJ_COLS]))
        if i in after_job:
            after_job[i]()
    for cvbuf in cvbufs:
        cvbuf[0:CV_HALO, :] = cvbuf[ts:ts + CV_HALO, :]

    mixer_a()
    cv = _layer_norm(cvout[...] + cvb_ref[...], cvg_ref[...], cvbeta_ref[...])
    ybuf[1] = (cv * jax.nn.sigmoid(cv)).astype(BF16)
    branch_z[1] = _dot(ybuf[1], _bf16_weights(wbr_ref[0, 1]))
    mixer_c()

    merged = None
    for n in range(3):
        gated = jax.nn.sigmoid(gl_buf[:, n * d:(n + 1) * d] + bg_ref[n:n + 1, :]) * branch_z[n]
        merged = gated if merged is None else merged + gated
    mg_buf[...] = merged.astype(BF16)
    xs_buf[...] = x


def _mixer_tail(step, mg_buf, xs_buf, modt_ref, wo_ref, ln1g_ref, ln1b_ref, wrt_ref, brc_ref,
                x1_ref, u2_ref, ri_all, rf_ref, base_ref):
    ts = MIX_ROWS
    d = D_MODEL
    modt = modt_ref[0]
    gate1, shift2, scale2 = modt[:, 2 * d:3 * d], modt[:, 3 * d:4 * d], modt[:, 4 * d:5 * d]
    hmix = _dot(mg_buf[...], _bf16_weights(wo_ref[0]))
    x1 = _layer_norm(ALPHA * xs_buf[...] + gate1 * hmix, ln1g_ref[...], ln1b_ref[...])
    x1_ref[0] = x1
    u2 = x1 * (1.0 + scale2) + shift2
    u2_ref[0] = _pack_bf16_pairs(u2)
    yield

    logits = lax.dot_general(wrt_ref[...], u2, (((1,), (1,)), ((), ())),
                             preferred_element_type=F32,
                             precision=lax.Precision.HIGHEST)
    mx = jnp.max(logits, axis=0, keepdims=True)
    ex = jnp.exp(logits - mx)
    scores = ex / jnp.sum(ex, axis=0, keepdims=True)
    sel = scores + brc_ref[...]
    tops = []
    for g in range(N_GROUPS):
        rows = [sel[g * EXPERTS_PER_GROUP + k:g * EXPERTS_PER_GROUP + k + 1, :]
                for k in range(EXPERTS_PER_GROUP)]
        tops.append(_top2_of4(rows))
    best = tops[0][0] + tops[0][2]
    g_idx = jnp.zeros(best.shape, jnp.int32)
    loc1, loc2 = tops[0][1], tops[0][3]
    for g in range(1, N_GROUPS):
        gs = tops[g][0] + tops[g][2]
        gt = gs > best
        best = jnp.where(gt, gs, best)
        g_idx = jnp.where(gt, g, g_idx)
        loc1 = jnp.where(gt, tops[g][1], loc1)
        loc2 = jnp.where(gt, tops[g][3], loc2)
    e0 = g_idx * EXPERTS_PER_GROUP + loc1
    e1 = g_idx * EXPERTS_PER_GROUP + loc2
    erow = lax.broadcasted_iota(jnp.int32, (N_EXPERTS, ts), 0)
    is0 = erow == e0
    is1 = erow == e1
    s0 = jnp.sum(jnp.where(is0, scores, 0.0), axis=0, keepdims=True)
    s1 = jnp.sum(jnp.where(is1, scores, 0.0), axis=0, keepdims=True)
    ssum = s0 + s1
    yield

    onehot = jnp.logical_or(is0, is1).astype(BF16)
    src = lax.broadcasted_iota(jnp.int32, (ts, ts), 0)
    dst = lax.broadcasted_iota(jnp.int32, (ts, ts), 1)
    earlier = (src < dst).astype(BF16)
    prior = _dot(onehot, earlier) + base_ref[:, 0:1]
    r0 = jnp.sum(jnp.where(is0, prior, 0.0), axis=0, keepdims=True)
    r1 = jnp.sum(jnp.where(is1, prior, 0.0), axis=0, keepdims=True)
    counts = jnp.sum(onehot.astype(F32), axis=1, keepdims=True)
    base_ref[...] = base_ref[...] + jnp.where(step > 0, counts, 0.0)

    zi = jnp.zeros((SUBLANES - 4, ts), jnp.int32)
    ri_all[jnp.maximum(step - 1, 0)] = jnp.concatenate(
        [e0, e1, r0.astype(jnp.int32), r1.astype(jnp.int32), zi], axis=0)
    zf = jnp.zeros((SUBLANES - 2, ts), F32)
    rf_ref[...] = jnp.concatenate([s0 / ssum, s1 / ssum, zf], axis=0).T


def _mixer(layer, xsrc, mod3, mb0, bsz, w_in, scw, cvw, cvb, cvg, cvbeta, sgg, sgbeta, sgw, sgbt,
           wbr, wg, bg, wo, ln1g, ln1b, wrt, brc):
    _, seq, d = xsrc[0].shape
    ts = MIX_ROWS
    ns = seq // ts
    n_tok = bsz * seq

    n_tiles = bsz * ns

    def const(shape):
        zeros = (0,) * len(shape)
        return pl.BlockSpec(shape, lambda s: zeros, pipeline_mode=pl.Buffered(1))

    def first_half(s):
        return jnp.minimum(s, n_tiles - 1)

    def second_half(s):
        return jnp.maximum(s - 1, 0)

    def layer_weights(shape):
        block = (1,) + tuple(shape[1:])
        index = (layer,) + (0,) * (len(shape) - 1)
        return pl.BlockSpec(block, lambda s: index, pipeline_mode=pl.Buffered(1))

    if len(xsrc) == 2:
        x, xb0 = xsrc
        x_args = [x]
        x_specs = [pl.BlockSpec((1, ts, d),
                                lambda s: (first_half(s) // ns + xb0, first_half(s) % ns, 0))]
    else:
        x1p, pairs, gates, mod3p, g2, b2 = xsrc
        x_args = [x1p, pairs, pairs, gates, mod3p, g2, b2]
        x_specs = [
            pl.BlockSpec((1, ts, d), lambda s: (first_half(s) // ns, first_half(s) % ns, 0)),
            pl.BlockSpec((1, ts, d // 2), lambda s: (0, first_half(s), 0)),
            pl.BlockSpec((1, ts, d // 2), lambda s: (1, first_half(s), 0)),
            pl.BlockSpec((ts, SUBLANES), lambda s: (first_half(s), 0)),
            pl.BlockSpec((1, 1, 6 * d), lambda s: (first_half(s) // ns + mb0, 0, 0)),
            const(g2.shape), const(b2.shape),
        ]
    in_specs = x_specs + [
        pl.BlockSpec((1, 1, 6 * d), lambda s: (first_half(s) // ns + mb0, 0, 0)),
        pl.BlockSpec((1, 1, 6 * d), lambda s: (second_half(s) // ns + mb0, 0, 0)),
        layer_weights(w_in.shape), const(scw.shape), const(cvw.shape), const(cvb.shape),
        const(cvg.shape), const(cvbeta.shape), const(sgg.shape), const(sgbeta.shape),
        const(sgw.shape), const(sgbt.shape), layer_weights(wbr.shape), layer_weights(wg.shape),
        const(bg.shape), layer_weights(wo.shape), const(ln1g.shape), const(ln1b.shape),
        const(wrt.shape), const(brc.shape),
    ]
    out_specs = [
        pl.BlockSpec((1, ts, d), lambda s: (second_half(s) // ns, second_half(s) % ns, 0)),
        pl.BlockSpec((1, ts, d // 2), lambda s: (second_half(s) // ns, second_half(s) % ns, 0)),
        pl.BlockSpec((SUBLANES, n_tok), lambda s: (0, 0)),
        pl.BlockSpec((ts, SUBLANES), lambda s: (second_half(s), 0)),
        pl.BlockSpec((SUBLANES, LANES), lambda s: (0, 0)),
    ]
    out_shape = [
        jax.ShapeDtypeStruct((bsz, seq, d), F32),
        jax.ShapeDtypeStruct((bsz, seq, d // 2), jnp.uint32),
        jax.ShapeDtypeStruct((SUBLANES, n_tok), jnp.int32),
        jax.ShapeDtypeStruct((n_tok, SUBLANES), F32),
        jax.ShapeDtypeStruct((SUBLANES, LANES), jnp.int32),
    ]
    return pl.pallas_call(
        functools.partial(_mixer_kernel, ns, len(x_args)),
        grid=(n_tiles + 1,),
        in_specs=in_specs,
        out_specs=out_specs,
        out_shape=out_shape,
        scratch_shapes=[
            pltpu.VMEM((SC_HALO + ts, W_BRANCH), F32),
            pltpu.VMEM((CV_HALO + ts, PROJ_COLS), F32),
            pltpu.VMEM((CV_HALO + ts, PROJ_COLS), F32),
            pltpu.VMEM((ts, W_BRANCH), F32),
            pltpu.VMEM((3, ts, W_BRANCH), BF16),
            pltpu.VMEM((ts, 3 * W_BRANCH), F32),
            pltpu.VMEM((ts, 2 * W_BRANCH), F32),
            pltpu.VMEM((ts, 3 * D_MODEL), F32),
            pltpu.VMEM((ts, d), BF16),
            pltpu.VMEM((ts, d), F32),
            pltpu.VMEM((n_tiles, SUBLANES, ts), jnp.int32),
            pltpu.VMEM((N_EXPERTS, LANES), F32),
        ],
        compiler_params=pltpu.CompilerParams(
            dimension_semantics=("arbitrary",),
            vmem_limit_bytes=MIXER_VMEM_LIMIT),
    )(*x_args, mod3, mod3, w_in, scw, cvw, cvb, cvg, cvbeta, sgg, sgbeta, sgw, sgbt, wbr, wg, bg,
      wo, ln1g, ln1b, wrt, brc)


def _sc_workers():
    info = plsc.get_sparse_core_info()
    return info.num_cores, info.num_cores * info.num_subcores


def _sc_scatter_rows(rows, dest_a, dest_b, n_out):
    n, d = rows.shape
    nc, nw = _sc_workers()
    per_w = n // nw
    n_win = per_w // SC_WINDOW
    ia = dest_a.reshape(nw, n_win, SC_WINDOW)
    ib = dest_b.reshape(nw, n_win, SC_WINDOW)
    mesh = plsc.VectorSubcoreMesh(core_axis_name="c", subcore_axis_name="s")

    @functools.partial(
        pl.kernel, mesh=mesh,
        out_type=jax.ShapeDtypeStruct((n_out, d), rows.dtype),
        scratch_types=[
            pltpu.VMEM((n_win, SC_WINDOW), jnp.int32),
            pltpu.VMEM((n_win, SC_WINDOW), jnp.int32),
            pltpu.VMEM((SC_WINDOW, d), rows.dtype),
        ],
    )
    def scatter(rows_hbm, ia_hbm, ib_hbm, out_hbm, ia_v, ib_v, rows_v):
        wid = lax.axis_index("s") * nc + lax.axis_index("c")
        pltpu.sync_copy(ia_hbm.at[wid], ia_v)
        pltpu.sync_copy(ib_hbm.at[wid], ib_v)
        base = wid * per_w

        @pl.loop(0, n_win)
        def _(j):
            pltpu.sync_copy(rows_hbm.at[pl.ds(base + j * SC_WINDOW, SC_WINDOW)], rows_v)
            pltpu.sync_copy(rows_v, out_hbm.at[ia_v.at[j]])
            pltpu.sync_copy(rows_v, out_hbm.at[ib_v.at[j]])

    return scatter(rows, ia, ib)


def _sc_gather_rows(table, idx):
    n = idx.shape[0]
    d = table.shape[1]
    nc, nw = _sc_workers()
    per_w = n // nw
    n_win = per_w // SC_WINDOW
    idx3 = idx.reshape(nw, n_win, SC_WINDOW)
    mesh = plsc.VectorSubcoreMesh(core_axis_name="c", subcore_axis_name="s")

    @functools.partial(
        pl.kernel, mesh=mesh,
        out_type=jax.ShapeDtypeStruct((n, d), table.dtype),
        scratch_types=[
            pltpu.VMEM((n_win, SC_WINDOW), jnp.int32),
            pltpu.VMEM((SC_WINDOW, d), table.dtype),
        ],
    )
    def gather(table_hbm, idx_hbm, out_hbm, idx_v, rows_v):
        wid = lax.axis_index("s") * nc + lax.axis_index("c")
        pltpu.sync_copy(idx_hbm.at[wid], idx_v)
        base = wid * per_w

        @pl.loop(0, n_win)
        def _(j):
            pltpu.sync_copy(table_hbm.at[idx_v.at[j]], rows_v)
            pltpu.sync_copy(rows_v, out_hbm.at[pl.ds(base + j * SC_WINDOW, SC_WINDOW)])

    return gather(table, idx3)


def _expert_kernel(layer, be_ref, nused_ref, x_ref, w1_hbm, w3_hbm, w2_hbm, o_ref,
                   w1f, w3f, w2f, w1s, w3s, w2s, sems, group_ref):
    i = pl.program_id(0)
    n_used = nused_ref[0]
    e_now = be_ref[i]

    def fetch(e, slot):
        return [pltpu.make_async_copy(w_hbm.at[layer, e], w_f.at[slot], sems.at[k, slot])
                for k, (w_hbm, w_f) in enumerate(((w1_hbm, w1f), (w3_hbm, w3f), (w2_hbm, w2f)))]

    @pl.when(i == 0)
    def _():
        group_ref[0] = 0
        for copy in fetch(e_now, 0):
            copy.start()

    first_of_group = jnp.logical_and(
        i < n_used, jnp.logical_or(i == 0, e_now != be_ref[jnp.maximum(i - 1, 0)]))

    @pl.when(first_of_group)
    def _():
        slot = group_ref[0] % 2
        for copy in fetch(e_now, slot):
            copy.wait()
        w1s[...] = w1f[slot].astype(BF16)
        w3s[...] = w3f[slot].astype(BF16)
        w2s[...] = w2f[slot].astype(BF16)
        last = be_ref.shape[0] - 1
        j = lax.while_loop(
            lambda j: jnp.logical_and(j < n_used, be_ref[jnp.minimum(j, last)] == e_now),
            lambda j: j + 1, i + 1)
        next_e = be_ref[jnp.minimum(j, last)]

        @pl.when(j < n_used)
        def _():
            for copy in fetch(next_e, 1 - slot):
                copy.start()

        group_ref[0] = group_ref[0] + 1

    @pl.when(i < n_used)
    def _():
        lo, hi = _unpack_bf16_pairs(x_ref[...])
        xb = jnp.concatenate([lo, hi], axis=1).astype(BF16)
        a = _dot(xb, w1s[...])
        b = _dot(xb, w3s[...])
        for r0 in range(0, MOE_ROWS, MOE_SLAB):
            rs = slice(r0, r0 + MOE_SLAB)
            h = a[rs] * jax.nn.sigmoid(a[rs]) * b[rs]
            o_ref[rs, :] = _pack_bf16_pairs(_dot(h.astype(BF16), w2s[...]))


def _experts(layer, buf, block_e, n_used, w1, w3, w2):
    n_rows, dp = buf.shape
    nb = n_rows // MOE_ROWS
    d, fe = w1.shape[-2:]

    def row_map(i, be, nu):
        return (jnp.minimum(i, nu[0] - 1), 0)

    grid_spec = pltpu.PrefetchScalarGridSpec(
        num_scalar_prefetch=2,
        grid=(nb,),
        in_specs=[
            pl.BlockSpec((MOE_ROWS, dp), row_map),
            pl.BlockSpec(memory_space=pl.ANY),
            pl.BlockSpec(memory_space=pl.ANY),
            pl.BlockSpec(memory_space=pl.ANY),
        ],
        out_specs=pl.BlockSpec((MOE_ROWS, dp), row_map),
        scratch_shapes=[
            pltpu.VMEM((2, d, fe), F32), pltpu.VMEM((2, d, fe), F32), pltpu.VMEM((2, fe, d), F32),
            pltpu.VMEM((d, fe), BF16), pltpu.VMEM((d, fe), BF16), pltpu.VMEM((fe, d), BF16),
            pltpu.SemaphoreType.DMA((3, 2)),
            pltpu.SMEM((1,), jnp.int32),
        ],
    )
    return pl.pallas_call(
        functools.partial(_expert_kernel, layer),
        grid_spec=grid_spec,
        out_shape=jax.ShapeDtypeStruct((n_rows, dp), jnp.uint32),
        compiler_params=pltpu.CompilerParams(dimension_semantics=("arbitrary",),
                                             vmem_limit_bytes=EXPERT_VMEM_LIMIT),
    )(block_e, n_used, buf, w1, w3, w2)


def _combine_kernel(x1_ref, ya_ref, yb_ref, gates_ref, mod_ref, g_ref, b_ref, *rest):
    o_ref = rest[-1]
    d = D_MODEL
    o_ref[0] = _moe_output(x1_ref[0], ya_ref[0], yb_ref[0], gates_ref[...],
                           mod_ref[0][:, 5 * d:6 * d], g_ref[...], b_ref[...])


def _combine(x1, pairs, gates, mod3, mb0, g, b, out_bsz, ob0, prev):
    bsz, seq, d = x1.shape
    ts = min(COMB_ROWS, seq)
    ns = seq // ts
    in_specs = [
        pl.BlockSpec((1, ts, d), lambda i, j: (i, j, 0)),
        pl.BlockSpec((1, ts, d // 2), lambda i, j: (0, i * ns + j, 0)),
        pl.BlockSpec((1, ts, d // 2), lambda i, j: (1, i * ns + j, 0)),
        pl.BlockSpec((ts, SUBLANES), lambda i, j: (i * ns + j, 0)),
        pl.BlockSpec((1, 1, 6 * d), lambda i, j: (i + mb0, 0, 0)),
        pl.BlockSpec((1, d), lambda i, j: (0, 0)),
        pl.BlockSpec((1, d), lambda i, j: (0, 0)),
    ]
    args = [x1, pairs, pairs, gates, mod3, g, b]
    aliases = {}
    if prev is not None:
        in_specs.append(pl.BlockSpec(memory_space=pl.ANY))
        aliases = {len(args): 0}
        args.append(prev)
    return pl.pallas_call(
        _combine_kernel,
        grid=(bsz, ns),
        in_specs=in_specs,
        out_specs=pl.BlockSpec((1, ts, d), lambda i, j: (i + ob0, j, 0)),
        out_shape=jax.ShapeDtypeStruct((out_bsz, seq, d), F32),
        input_output_aliases=aliases,
        compiler_params=pltpu.CompilerParams(dimension_semantics=("arbitrary", "arbitrary"),
                                             vmem_limit_bytes=MIXER_VMEM_LIMIT),
    )(*args)


def _expert_layout(counts):
    shift = MOE_ROWS.bit_length() - 1
    padded = lax.shift_left(
        lax.shift_right_logical(counts.astype(jnp.int32) + (MOE_ROWS - 1), shift), shift)
    e_out = lax.broadcasted_iota(jnp.int32, (N_EXPERTS, N_EXPERTS), 0)
    e_in = lax.broadcasted_iota(jnp.int32, (N_EXPERTS, N_EXPERTS), 1)
    upto = (e_in <= e_out).astype(BF16)
    padded_f = jnp.broadcast_to(padded.astype(F32), (N_EXPERTS, LANES))
    pad_end = _dot(upto, padded_f.astype(BF16))
    pad_start = pad_end[:, 0:1] - padded.astype(F32)
    block_start = (lax.broadcasted_iota(jnp.int32, (N_EXPERTS, LANES), 1) * MOE_ROWS).astype(F32)
    block_e = jnp.sum((block_start >= pad_end).astype(jnp.int32), axis=0, keepdims=True)
    n_used = lax.shift_right_logical(pad_end[N_EXPERTS - 1:N_EXPERTS, :].astype(jnp.int32), shift)
    blk = jnp.concatenate([jnp.minimum(block_e, N_EXPERTS - 1), n_used,
                           jnp.zeros((SUBLANES - 2, LANES), jnp.int32)], axis=0)
    return pad_start, blk


def _slots(ri, pad_start):
    n = ri.shape[1]
    erow = lax.broadcasted_iota(jnp.int32, (N_EXPERTS, n), 0)
    rows = []
    for k in range(TOP_K):
        start = jnp.sum(jnp.where(erow == ri[k:k + 1, :], pad_start, 0.0), axis=0, keepdims=True)
        rows.append(start.astype(jnp.int32) + ri[TOP_K + k:TOP_K + k + 1, :])
    rows.append(jnp.zeros((SUBLANES - TOP_K, n), jnp.int32))
    return jnp.concatenate(rows, axis=0)


def kernel(x, c, w_ada, b_ada, w_in, sc_conv, cv_conv, cv_conv_b, cv_ln_g, cv_ln_b, sg_ln_g, sg_ln_b, sg_w, sg_b, w_branch, w_gate, b_gate, w_o, ln1_g, ln1_b, w_router, b_router, w1, w3, w2, ln2_g, ln2_b):
    bsz, seq, d = x.shape
    mod3 = _ada(c, w_ada, b_ada).reshape(DEPTH, bsz, 1, 6 * d)
    wrt = w_router.T
    brc = b_router.reshape(N_EXPERTS, 1)
    w_in_p = _pack_weights(w_in)
    w_gate_p = _pack_weights(jnp.transpose(w_gate, (0, 2, 1, 3)).reshape(DEPTH * 3, d, d)).reshape(
        DEPTH, 3, d // 2, d)
    w_o_p = _pack_weights(w_o)
    w_branch_p = _pack_weights(w_branch.reshape(DEPTH * 3, W_BRANCH, d)).reshape(
        DEPTH, 3, W_BRANCH // 2, d)
    cb = bsz // N_CHAINS
    n_tok = cb * seq
    n_blocks = (n_tok * TOP_K + N_EXPERTS * (MOE_ROWS - 1) + MOE_ROWS - 1) // MOE_ROWS
    assert n_blocks <= LANES and MOE_ROWS & (MOE_ROWS - 1) == 0
    chains = [(x, h * cb) for h in range(N_CHAINS)]
    for l in range(DEPTH):
        last = l == DEPTH - 1
        result = None
        for h in range(N_CHAINS):
            x1, u2, dest, rf, blk = _mixer(
                l, chains[h], mod3[l], h * cb, cb, w_in_p, sc_conv[l], cv_conv[l],
                cv_conv_b[l].reshape(1, -1), cv_ln_g[l].reshape(1, -1), cv_ln_b[l].reshape(1, -1),
                sg_ln_g[l].reshape(1, -1), sg_ln_b[l].reshape(1, -1), sg_w[l], sg_b[l].T,
                w_branch_p, w_gate_p, b_gate[l],
                w_o_p, ln1_g[l].reshape(1, -1), ln1_b[l].reshape(1, -1), wrt, brc)
            buf = _sc_scatter_rows(u2.reshape(n_tok, d // 2), dest[0], dest[1], n_blocks * MOE_ROWS)
            obuf = _experts(l, buf, blk[0, 0:n_blocks], blk[1, 0:1], w1, w3, w2)
            pair_idx = dest[0:TOP_K].reshape(TOP_K * n_tok)
            pairs = _sc_gather_rows(obuf, pair_idx).reshape(TOP_K, n_tok, d // 2)
            gates = rf
            g2, b2 = ln2_g[l].reshape(1, -1), ln2_b[l].reshape(1, -1)
            if last:
                result = _combine(x1, pairs, gates, mod3[l], h * cb, g2, b2, bsz, h * cb, result)
            else:
                chains[h] = (x1, pairs, gates, mod3[l], g2, b2)
    return result
```

```python
import functools

import jax
import jax.numpy as jnp
from jax import lax
from jax.experimental import pallas as pl
from jax.experimental.pallas import tpu as pltpu
from jax.experimental.pallas import tpu_sc as plsc

D_MODEL = 1024
DEPTH = 2
W_BRANCH = 1024
SC_KERNEL = 3
CV_KERNEL = 31
CHUNK = 128
SG_HEADS = 8
N_EXPERTS = 16
N_GROUPS = 4
EXPERTS_PER_GROUP = N_EXPERTS // N_GROUPS
TOP_K = 2
D_EXPERT = 512
ALPHA = (2.0 * DEPTH) ** 0.25
LN_EPS = 1e-5

F32 = jnp.float32
BF16 = jnp.bfloat16

V7X_VMEM_BYTES = 64 * 1024 * 1024
MIXER_VMEM_LIMIT = V7X_VMEM_BYTES - 6 * 1024 * 1024
EXPERT_VMEM_LIMIT = V7X_VMEM_BYTES // 2
SUBLANES = 8
LANES = 128

MIX_ROWS = 256
SC_HALO = SUBLANES
CV_HALO = 32
CONV_ROWS = 128
TIE_LAG = 2
TIE_FREE_JOBS = 4
CONV_COLS = 128
PROJ_COLS = 512
PACK_ROWS = 128
WCOL_GATE = 7 * W_BRANCH
WCOL_BRANCH = WCOL_GATE + 3 * D_MODEL
WCOL_OUT = WCOL_BRANCH + 3 * D_MODEL
MOE_ROWS = 512
MOE_SLAB = 256
COMB_ROWS = 1024
SC_WINDOW = 32
N_CHAINS = 2


def _dot(a, b):
    return jnp.dot(a, b, preferred_element_type=F32)


def _pack_bf16_pairs(v):
    m = v.shape[1] // 2
    lo = lax.bitcast_convert_type(v[:, 0:m].astype(BF16).astype(F32), jnp.uint32)
    hi = lax.bitcast_convert_type(v[:, m:2 * m].astype(BF16).astype(F32), jnp.uint32)
    return jnp.bitwise_or(jnp.bitwise_and(hi, jnp.uint32(0xFFFF0000)),
                          lax.shift_right_logical(lo, jnp.uint32(16)))


def _unpack_bf16_pairs(w):
    lo = lax.bitcast_convert_type(lax.shift_left(w, jnp.uint32(16)), F32)
    hi = lax.bitcast_convert_type(jnp.bitwise_and(w, jnp.uint32(0xFFFF0000)), F32)
    return lo, hi


def _layer_norm(v, g, b):
    mu = jnp.mean(v, axis=-1, keepdims=True)
    vc = v - mu
    var = jnp.mean(vc * vc, axis=-1, keepdims=True)
    return vc * lax.rsqrt(var + LN_EPS) * g + b


def _ada_kernel(c_ref, w_ref, b_ref, o_ref):
    c = c_ref[...]
    c_act = c * jax.nn.sigmoid(c)
    o_ref[0] = jnp.dot(c_act, w_ref[0], preferred_element_type=F32,
                       precision=lax.Precision.HIGHEST) + b_ref[0]


def _ada(c, w_ada, b_ada):
    bsz, d = c.shape
    n = w_ada.shape[-1]
    tn = 1536
    return pl.pallas_call(
        _ada_kernel,
        grid=(DEPTH, n // tn),
        in_specs=[
            pl.BlockSpec((bsz, d), lambda l, j: (0, 0)),
            pl.BlockSpec((1, d, tn), lambda l, j: (l, 0, j)),
            pl.BlockSpec((1, 1, tn), lambda l, j: (l, 0, j)),
        ],
        out_specs=pl.BlockSpec((1, bsz, tn), lambda l, j: (l, 0, j)),
        out_shape=jax.ShapeDtypeStruct((DEPTH, bsz, n), F32),
    )(c, w_ada, b_ada.reshape(DEPTH, 1, n))


def _top2_of4(rows):
    m1 = rows[0]
    i1 = jnp.zeros(rows[0].shape, jnp.int32)
    for k in range(1, 4):
        gt = rows[k] > m1
        m1 = jnp.where(gt, rows[k], m1)
        i1 = jnp.where(gt, k, i1)
    m2 = jnp.full(rows[0].shape, -jnp.inf, F32)
    i2 = jnp.zeros(rows[0].shape, jnp.int32)
    for k in range(4):
        cand = jnp.where(i1 == k, -jnp.inf, rows[k])
        gt = cand > m2
        m2 = jnp.where(gt, cand, m2)
        i2 = jnp.where(gt, k, i2)
    return m1, i1, m2, i2


def _zero_after(v):
    u = lax.bitcast_convert_type(v, jnp.uint32)
    u = lax.shift_right_logical(lax.shift_right_logical(u, jnp.uint32(16)), jnp.uint32(16))
    return lax.bitcast_convert_type(u, F32)


def _conv31_chunk(cvw_ref, cvbufs, cvout, r0, c0, tie):
    cs = slice(c0, c0 + CONV_COLS)
    cvbuf = cvbufs[c0 // PROJ_COLS]
    bs = slice(c0 % PROJ_COLS, c0 % PROJ_COLS + CONV_COLS)
    acc = None
    for r in range(SUBLANES):
        lead = SUBLANES if r else 0
        part = None
        for m in range((CV_KERNEL - 1 - r) // SUBLANES + 1):
            k = CV_KERNEL - 1 - (SUBLANES * m + r)
            start = CV_HALO + r0 - lead - SUBLANES * m
            w_row = cvw_ref[k:k + 1, cs]
            if tie is not None and acc is None and part is None:
                w_row = w_row + tie
            term = w_row * cvbuf[start:start + lead + CONV_ROWS, bs]
            part = term if part is None else part + term
        part = part[lead - r:lead - r + CONV_ROWS]
        acc = part if acc is None else acc + part
    cvout[r0:r0 + CONV_ROWS, cs] = acc


def _bf16_weights(packed):
    return pltpu.bitcast(packed, BF16)


def _pack_weights_kernel(*refs):
    o_ref = refs[-1]
    c0 = 0
    for w_ref in refs[:-1]:
        w = w_ref[0] if len(w_ref.shape) == 3 else w_ref[0, 0]
        o_ref[0, :, c0:c0 + w.shape[1]] = pltpu.bitcast(w.astype(BF16), jnp.uint32)
        c0 += w.shape[1]


def _pack_weights(w_in, w_gate_t, w_branch, w_o):
    depth, k, n_in = w_in.shape
    d = w_o.shape[-1]
    assert w_branch.shape[2] == k and w_gate_t.shape[2] == k and w_o.shape[1] == k
    kb = PACK_ROWS
    per_branch = [pl.BlockSpec((1, 1, kb, d), functools.partial(lambda n, l, j: (l, n, j, 0), n))
                  for n in range(3)]
    return pl.pallas_call(
        _pack_weights_kernel,
        grid=(depth, k // kb),
        in_specs=([pl.BlockSpec((1, kb, n_in), lambda l, j: (l, j, 0))] + per_branch + per_branch
                  + [pl.BlockSpec((1, kb, d), lambda l, j: (l, j, 0))]),
        out_specs=pl.BlockSpec((1, kb // 2, n_in + 7 * d), lambda l, j: (l, j, 0)),
        out_shape=jax.ShapeDtypeStruct((depth, k // 2, n_in + 7 * d), jnp.uint32),
        compiler_params=pltpu.CompilerParams(vmem_limit_bytes=MIXER_VMEM_LIMIT),
    )(w_in, w_gate_t, w_gate_t, w_gate_t, w_branch, w_branch, w_branch, w_o)


def _moe_output(x1, ya_packed, yb_packed, gates, gate2, g, b):
    ya = jnp.concatenate(_unpack_bf16_pairs(ya_packed), axis=1)
    yb = jnp.concatenate(_unpack_bf16_pairs(yb_packed), axis=1)
    h = gates[:, 0:1] * ya + gates[:, 1:2] * yb
    return _layer_norm(ALPHA * x1 + gate2 * h, g, b)


def _mixer_kernel(tiles_per_seq, n_x_refs, *refs):
    last_step = pl.num_programs(0) - 1

    @pl.when(pl.program_id(0) < last_step)
    def _():
        _mixer_step(tiles_per_seq, n_x_refs, True, *refs)

    @pl.when(pl.program_id(0) == last_step)
    def _():
        _mixer_step(tiles_per_seq, n_x_refs, False, *refs)


def _mixer_step(tiles_per_seq, n_x_refs, run_first_half, *refs):
    x_refs = refs[:n_x_refs]
    (mod_ref, modt_ref, w_ref, scw_ref, cvw_ref, cvb_ref, cvg_ref, cvbeta_ref, sgg_ref,
     sgbeta_ref, sgw_ref, sgbt_ref, bg_ref, ln1g_ref, ln1b_ref, wrt_ref, brc_ref,
     x1_ref, u2_ref, dest_ref, rf_ref, blk_ref,
     qbuf, cvbuf0, cvbuf1, cvout, ybuf, pa_buf, pc_buf, gl_buf, mg_buf, xs_buf, ri_all,
     base_ref) = refs[n_x_refs:]
    cvbufs = (cvbuf0, cvbuf1)
    ts = MIX_ROWS
    d = D_MODEL
    wb = W_BRANCH
    step = pl.program_id(0)
    tile = jnp.minimum(step, pl.num_programs(0) - 2)
    first_tile = tile % tiles_per_seq == 0

    @pl.when(step == 0)
    def _():
        base_ref[...] = jnp.zeros_like(base_ref)
        mg_buf[...] = jnp.zeros_like(mg_buf)
        xs_buf[...] = jnp.zeros_like(xs_buf)

    def w_in(c0):
        return _bf16_weights(w_ref[0, :, c0:c0 + PROJ_COLS])

    def w_gate(c0):
        return w_in(WCOL_GATE + c0)

    def w_square(c0):
        return _bf16_weights(w_ref[0, :, c0:c0 + d])

    if not run_first_half:
        for _ in _mixer_tail(step, mg_buf, xs_buf, modt_ref, w_square(WCOL_OUT), ln1g_ref, ln1b_ref, wrt_ref,
                             brc_ref, x1_ref, u2_ref, ri_all, rf_ref, base_ref):
            pass
        pad_start, blk_ref[...] = _expert_layout(base_ref[:, 0:1])
        for t in range(ri_all.shape[0]):
            dest_ref[:, t * ts:(t + 1) * ts] = _slots(ri_all[t], pad_start)
        return

    @pl.when(first_tile)
    def _():
        qbuf[0:SC_HALO, :] = jnp.zeros((SC_HALO, wb), F32)
        for cvbuf in cvbufs:
            cvbuf[0:CV_HALO, :] = jnp.zeros((CV_HALO, PROJ_COLS), F32)

    if n_x_refs == 1:
        x = x_refs[0][0]
    else:
        x1p_ref, ya_ref, yb_ref, gates_ref, modp_ref, g2_ref, b2_ref = x_refs
        x = _moe_output(x1p_ref[0], ya_ref[0], yb_ref[0], gates_ref[...],
                        modp_ref[0][:, 5 * d:6 * d], g2_ref[...], b2_ref[...])
    mod = mod_ref[0]
    shift1, scale1 = mod[:, 0:d], mod[:, d:2 * d]
    ub = (x * (1.0 + scale1) + shift1).astype(BF16)

    def glu_block(c0):
        a = _dot(ub, w_in(3 * wb + c0))
        g = _dot(ub, w_in(4 * wb + c0))
        cvbufs[c0 // PROJ_COLS][CV_HALO:CV_HALO + ts, :] = a * jax.nn.sigmoid(g)

    tail = _mixer_tail(step, mg_buf, xs_buf, modt_ref, w_square(WCOL_OUT), ln1g_ref, ln1b_ref, wrt_ref,
                       brc_ref, x1_ref, u2_ref, ri_all, rf_ref, base_ref)
    next(tail)
    glu_block(0)
    ties = {}

    def add_tie(chunk, tie):
        ties[chunk] = tie + ties[chunk] if chunk in ties else tie

    branch_z = {}

    def mixer_a():
        qbuf[SC_HALO:SC_HALO + ts, :] = pa_buf[:, wb:2 * wb] * pa_buf[:, 2 * wb:3 * wb]
        conv = scw_ref[SC_KERNEL - 1:SC_KERNEL, :] * qbuf[SC_HALO:SC_HALO + ts, :]
        for k in range(SC_KERNEL - 1):
            off = SC_HALO - (SC_KERNEL - 1) + k
            conv = conv + scw_ref[k:k + 1, :] * qbuf[off:off + ts, :]
        ybuf[0] = (pa_buf[:, 0:wb] * conv).astype(BF16)
        qbuf[0:SC_HALO, :] = qbuf[ts:ts + SC_HALO, :]
        branch_z[0] = _dot(ybuf[0], w_square(WCOL_BRANCH))

    def mixer_c():
        gu = jax.nn.gelu(pc_buf[:, 0:wb])
        gv = _layer_norm(jax.nn.gelu(pc_buf[:, wb:2 * wb]), sgg_ref[...],
                         sgbeta_ref[...]).astype(BF16)
        row = lax.broadcasted_iota(jnp.int32, (CHUNK, CHUNK), 0)
        col = lax.broadcasted_iota(jnp.int32, (CHUNK, CHUNK), 1)
        hd = wb // SG_HEADS
        for h in range(SG_HEADS):
            wm = jnp.where(row >= col, sgw_ref[h], 0.0).astype(BF16)
            bias = sgbt_ref[:, h:h + 1]
            for n in range(ts // CHUNK):
                rs = slice(n * CHUNK, (n + 1) * CHUNK)
                cs = slice(h * hd, (h + 1) * hd)
                mixed = _dot(wm, gv[rs, cs]) + bias
                ybuf[2, rs, cs] = (gu[rs, cs] * mixed).astype(BF16)
        branch_z[2] = _dot(ybuf[2], w_square(WCOL_BRANCH + 2 * d))

    def glu_rest():
        for c0 in range(PROJ_COLS, wb, PROJ_COLS):
            glu_block(c0)

    def tail_stage():
        next(tail, None)

    mxu_jobs = ([(pa_buf, w_in, c0, c0) for c0 in range(0, 3 * wb, PROJ_COLS)]
                + [(pc_buf, w_in, c0, 5 * wb + c0) for c0 in range(0, 2 * wb, PROJ_COLS)]
                + [(gl_buf, w_gate, c0, c0) for c0 in range(0, 3 * d, PROJ_COLS)])
    glu_rest()
    after_job = {7: tail_stage, 12: tail_stage}
    conv_jobs = [(r0, c0) for c0 in range(0, wb, CONV_COLS) for r0 in range(0, ts, CONV_ROWS)]
    chunks_per_job = len(conv_jobs) // len(mxu_jobs)
    for c, conv_job in enumerate(conv_jobs):
        _conv31_chunk(cvw_ref, cvbufs, cvout, *conv_job, ties.get(c))
        if (c + 1) % chunks_per_job:
            continue
        i = c // chunks_per_job
        dst, weights, dc, wc = mxu_jobs[i]
        res = _dot(ub, weights(wc))
        dst[:, dc:dc + PROJ_COLS] = res
        if i < len(mxu_jobs) - TIE_FREE_JOBS:
            add_tie(c + TIE_LAG, _zero_after(res[ts - 1:ts, PROJ_COLS - CONV_COLS:PROJ_COLS]))
        if i in after_job:
            after_job[i]()
    for cvbuf in cvbufs:
        cvbuf[0:CV_HALO, :] = cvbuf[ts:ts + CV_HALO, :]

    mixer_a()
    cv = _layer_norm(cvout[...] + cvb_ref[...], cvg_ref[...], cvbeta_ref[...])
    ybuf[1] = (cv * jax.nn.sigmoid(cv)).astype(BF16)
    branch_z[1] = _dot(ybuf[1], w_square(WCOL_BRANCH + d))
    mixer_c()

    merged = None
    for n in range(3):
        gated = jax.nn.sigmoid(gl_buf[:, n * d:(n + 1) * d] + bg_ref[n:n + 1, :]) * branch_z[n]
        merged = gated if merged is None else merged + gated
    mg_buf[...] = merged.astype(BF16)
    xs_buf[...] = x


def _mixer_tail(step, mg_buf, xs_buf, modt_ref, w_out, ln1g_ref, ln1b_ref, wrt_ref, brc_ref,
                x1_ref, u2_ref, ri_all, rf_ref, base_ref):
    ts = MIX_ROWS
    d = D_MODEL
    modt = modt_ref[0]
    gate1, shift2, scale2 = modt[:, 2 * d:3 * d], modt[:, 3 * d:4 * d], modt[:, 4 * d:5 * d]
    hmix = _dot(mg_buf[...], w_out)
    x1 = _layer_norm(ALPHA * xs_buf[...] + gate1 * hmix, ln1g_ref[...], ln1b_ref[...])
    x1_ref[0] = x1
    u2 = x1 * (1.0 + scale2) + shift2
    u2_ref[0] = _pack_bf16_pairs(u2)
    yield

    logits = lax.dot_general(wrt_ref[...], u2, (((1,), (1,)), ((), ())),
                             preferred_element_type=F32,
                             precision=lax.Precision.HIGHEST)
    mx = jnp.max(logits, axis=0, keepdims=True)
    ex = jnp.exp(logits - mx)
    scores = ex / jnp.sum(ex, axis=0, keepdims=True)
    sel = scores + brc_ref[...]
    tops = []
    for g in range(N_GROUPS):
        rows = [sel[g * EXPERTS_PER_GROUP + k:g * EXPERTS_PER_GROUP + k + 1, :]
                for k in range(EXPERTS_PER_GROUP)]
        tops.append(_top2_of4(rows))
    best = tops[0][0] + tops[0][2]
    g_idx = jnp.zeros(best.shape, jnp.int32)
    loc1, loc2 = tops[0][1], tops[0][3]
    for g in range(1, N_GROUPS):
        gs = tops[g][0] + tops[g][2]
        gt = gs > best
        best = jnp.where(gt, gs, best)
        g_idx = jnp.where(gt, g, g_idx)
        loc1 = jnp.where(gt, tops[g][1], loc1)
        loc2 = jnp.where(gt, tops[g][3], loc2)
    e0 = g_idx * EXPERTS_PER_GROUP + loc1
    e1 = g_idx * EXPERTS_PER_GROUP + loc2
    erow = lax.broadcasted_iota(jnp.int32, (N_EXPERTS, ts), 0)
    is0 = erow == e0
    is1 = erow == e1
    s0 = jnp.sum(jnp.where(is0, scores, 0.0), axis=0, keepdims=True)
    s1 = jnp.sum(jnp.where(is1, scores, 0.0), axis=0, keepdims=True)
    ssum = s0 + s1
    yield

    onehot = jnp.logical_or(is0, is1).astype(BF16)
    src = lax.broadcasted_iota(jnp.int32, (ts, ts), 0)
    dst = lax.broadcasted_iota(jnp.int32, (ts, ts), 1)
    earlier = (src < dst).astype(BF16)
    prior = _dot(onehot, earlier) + base_ref[:, 0:1]
    r0 = jnp.sum(jnp.where(is0, prior, 0.0), axis=0, keepdims=True)
    r1 = jnp.sum(jnp.where(is1, prior, 0.0), axis=0, keepdims=True)
    counts = jnp.sum(onehot.astype(F32), axis=1, keepdims=True)
    base_ref[...] = base_ref[...] + jnp.where(step > 0, counts, 0.0)

    zi = jnp.zeros((SUBLANES - 4, ts), jnp.int32)
    ri_all[jnp.maximum(step - 1, 0)] = jnp.concatenate(
        [e0, e1, r0.astype(jnp.int32), r1.astype(jnp.int32), zi], axis=0)
    zf = jnp.zeros((SUBLANES - 2, ts), F32)
    rf_ref[...] = jnp.concatenate([s0 / ssum, s1 / ssum, zf], axis=0).T


def _mixer(layer, xsrc, mod3, mb0, bsz, w_all, scw, cvw, cvb, cvg, cvbeta, sgg, sgbeta, sgw, sgbt,
           bg, ln1g, ln1b, wrt, brc):
    _, seq, d = xsrc[0].shape
    ts = MIX_ROWS
    ns = seq // ts
    n_tok = bsz * seq

    n_tiles = bsz * ns

    def const(shape):
        zeros = (0,) * len(shape)
        return pl.BlockSpec(shape, lambda s: zeros, pipeline_mode=pl.Buffered(1))

    def first_half(s):
        return jnp.minimum(s, n_tiles - 1)

    def second_half(s):
        return jnp.maximum(s - 1, 0)

    def layer_weights(shape):
        block = (1,) + tuple(shape[1:])
        index = (layer,) + (0,) * (len(shape) - 1)
        return pl.BlockSpec(block, lambda s: index, pipeline_mode=pl.Buffered(1))

    if len(xsrc) == 2:
        x, xb0 = xsrc
        x_args = [x]
        x_specs = [pl.BlockSpec((1, ts, d),
                                lambda s: (first_half(s) // ns + xb0, first_half(s) % ns, 0))]
    else:
        x1p, pairs, gates, mod3p, g2, b2 = xsrc
        x_args = [x1p, pairs, pairs, gates, mod3p, g2, b2]
        x_specs = [
            pl.BlockSpec((1, ts, d), lambda s: (first_half(s) // ns, first_half(s) % ns, 0)),
            pl.BlockSpec((1, ts, d // 2), lambda s: (0, first_half(s), 0)),
            pl.BlockSpec((1, ts, d // 2), lambda s: (1, first_half(s), 0)),
            pl.BlockSpec((ts, SUBLANES), lambda s: (first_half(s), 0)),
            pl.BlockSpec((1, 1, 6 * d), lambda s: (first_half(s) // ns + mb0, 0, 0)),
            const(g2.shape), const(b2.shape),
        ]
    in_specs = x_specs + [
        pl.BlockSpec((1, 1, 6 * d), lambda s: (first_half(s) // ns + mb0, 0, 0)),
        pl.BlockSpec((1, 1, 6 * d), lambda s: (second_half(s) // ns + mb0, 0, 0)),
        layer_weights(w_all.shape), const(scw.shape), const(cvw.shape), const(cvb.shape),
        const(cvg.shape), const(cvbeta.shape), const(sgg.shape), const(sgbeta.shape),
        const(sgw.shape), const(sgbt.shape), const(bg.shape), const(ln1g.shape), const(ln1b.shape),
        const(wrt.shape), const(brc.shape),
    ]
    out_specs = [
        pl.BlockSpec((1, ts, d), lambda s: (second_half(s) // ns, second_half(s) % ns, 0)),
        pl.BlockSpec((1, ts, d // 2), lambda s: (second_half(s) // ns, second_half(s) % ns, 0)),
        pl.BlockSpec((SUBLANES, n_tok), lambda s: (0, 0)),
        pl.BlockSpec((ts, SUBLANES), lambda s: (second_half(s), 0)),
        pl.BlockSpec((SUBLANES, LANES), lambda s: (0, 0)),
    ]
    out_shape = [
        jax.ShapeDtypeStruct((bsz, seq, d), F32),
        jax.ShapeDtypeStruct((bsz, seq, d // 2), jnp.uint32),
        jax.ShapeDtypeStruct((SUBLANES, n_tok), jnp.int32),
        jax.ShapeDtypeStruct((n_tok, SUBLANES), F32),
        jax.ShapeDtypeStruct((SUBLANES, LANES), jnp.int32),
    ]
    return pl.pallas_call(
        functools.partial(_mixer_kernel, ns, len(x_args)),
        grid=(n_tiles + 1,),
        in_specs=in_specs,
        out_specs=out_specs,
        out_shape=out_shape,
        scratch_shapes=[
            pltpu.VMEM((SC_HALO + ts, W_BRANCH), F32),
            pltpu.VMEM((CV_HALO + ts, PROJ_COLS), F32),
            pltpu.VMEM((CV_HALO + ts, PROJ_COLS), F32),
            pltpu.VMEM((ts, W_BRANCH), F32),
            pltpu.VMEM((3, ts, W_BRANCH), BF16),
            pltpu.VMEM((ts, 3 * W_BRANCH), F32),
            pltpu.VMEM((ts, 2 * W_BRANCH), F32),
            pltpu.VMEM((ts, 3 * D_MODEL), F32),
            pltpu.VMEM((ts, d), BF16),
            pltpu.VMEM((ts, d), F32),
            pltpu.VMEM((n_tiles, SUBLANES, ts), jnp.int32),
            pltpu.VMEM((N_EXPERTS, LANES), F32),
        ],
        compiler_params=pltpu.CompilerParams(
            dimension_semantics=("arbitrary",),
            vmem_limit_bytes=MIXER_VMEM_LIMIT),
    )(*x_args, mod3, mod3, w_all, scw, cvw, cvb, cvg, cvbeta, sgg, sgbeta, sgw, sgbt, bg,
      ln1g, ln1b, wrt, brc)


def _sc_workers():
    info = plsc.get_sparse_core_info()
    return info.num_cores, info.num_cores * info.num_subcores


def _sc_scatter_rows(rows, dest_a, dest_b, n_out):
    n, d = rows.shape
    nc, nw = _sc_workers()
    per_w = n // nw
    n_win = per_w // SC_WINDOW
    ia = dest_a.reshape(nw, n_win, SC_WINDOW)
    ib = dest_b.reshape(nw, n_win, SC_WINDOW)
    mesh = plsc.VectorSubcoreMesh(core_axis_name="c", subcore_axis_name="s")

    @functools.partial(
        pl.kernel, mesh=mesh,
        out_type=jax.ShapeDtypeStruct((n_out, d), rows.dtype),
        scratch_types=[
            pltpu.VMEM((n_win, SC_WINDOW), jnp.int32),
            pltpu.VMEM((n_win, SC_WINDOW), jnp.int32),
            pltpu.VMEM((SC_WINDOW, d), rows.dtype),
        ],
    )
    def scatter(rows_hbm, ia_hbm, ib_hbm, out_hbm, ia_v, ib_v, rows_v):
        wid = lax.axis_index("s") * nc + lax.axis_index("c")
        pltpu.sync_copy(ia_hbm.at[wid], ia_v)
        pltpu.sync_copy(ib_hbm.at[wid], ib_v)
        base = wid * per_w

        @pl.loop(0, n_win)
        def _(j):
            pltpu.sync_copy(rows_hbm.at[pl.ds(base + j * SC_WINDOW, SC_WINDOW)], rows_v)
            pltpu.sync_copy(rows_v, out_hbm.at[ia_v.at[j]])
            pltpu.sync_copy(rows_v, out_hbm.at[ib_v.at[j]])

    return scatter(rows, ia, ib)


def _sc_gather_rows(table, idx):
    n = idx.shape[0]
    d = table.shape[1]
    nc, nw = _sc_workers()
    per_w = n // nw
    n_win = per_w // SC_WINDOW
    idx3 = idx.reshape(nw, n_win, SC_WINDOW)
    mesh = plsc.VectorSubcoreMesh(core_axis_name="c", subcore_axis_name="s")

    @functools.partial(
        pl.kernel, mesh=mesh,
        out_type=jax.ShapeDtypeStruct((n, d), table.dtype),
        scratch_types=[
            pltpu.VMEM((n_win, SC_WINDOW), jnp.int32),
            pltpu.VMEM((SC_WINDOW, d), table.dtype),
        ],
    )
    def gather(table_hbm, idx_hbm, out_hbm, idx_v, rows_v):
        wid = lax.axis_index("s") * nc + lax.axis_index("c")
        pltpu.sync_copy(idx_hbm.at[wid], idx_v)
        base = wid * per_w

        @pl.loop(0, n_win)
        def _(j):
            pltpu.sync_copy(table_hbm.at[idx_v.at[j]], rows_v)
            pltpu.sync_copy(rows_v, out_hbm.at[pl.ds(base + j * SC_WINDOW, SC_WINDOW)])

    return gather(table, idx3)


def _expert_kernel(layer, be_ref, nused_ref, x_ref, w1_hbm, w3_hbm, w2_hbm, o_ref,
                   w1f, w3f, w2f, w1s, w3s, w2s, sems, group_ref):
    i = pl.program_id(0)
    n_used = nused_ref[0]
    e_now = be_ref[i]

    def fetch(e, slot):
        return [pltpu.make_async_copy(w_hbm.at[layer, e], w_f.at[slot], sems.at[k, slot])
                for k, (w_hbm, w_f) in enumerate(((w1_hbm, w1f), (w3_hbm, w3f), (w2_hbm, w2f)))]

    @pl.when(i == 0)
    def _():
        group_ref[0] = 0
        for copy in fetch(e_now, 0):
            copy.start()

    first_of_group = jnp.logical_and(
        i < n_used, jnp.logical_or(i == 0, e_now != be_ref[jnp.maximum(i - 1, 0)]))

    @pl.when(first_of_group)
    def _():
        slot = group_ref[0] % 2
        for copy in fetch(e_now, slot):
            copy.wait()
        w1s[...] = w1f[slot].astype(BF16)
        w3s[...] = w3f[slot].astype(BF16)
        w2s[...] = w2f[slot].astype(BF16)
        last = be_ref.shape[0] - 1
        j = lax.while_loop(
            lambda j: jnp.logical_and(j < n_used, be_ref[jnp.minimum(j, last)] == e_now),
            lambda j: j + 1, i + 1)
        next_e = be_ref[jnp.minimum(j, last)]

        @pl.when(j < n_used)
        def _():
            for copy in fetch(next_e, 1 - slot):
                copy.start()

        group_ref[0] = group_ref[0] + 1

    @pl.when(i < n_used)
    def _():
        lo, hi = _unpack_bf16_pairs(x_ref[...])
        xb = jnp.concatenate([lo, hi], axis=1).astype(BF16)
        a = _dot(xb, w1s[...])
        b = _dot(xb, w3s[...])
        for r0 in range(0, MOE_ROWS, MOE_SLAB):
            rs = slice(r0, r0 + MOE_SLAB)
            h = a[rs] * jax.nn.sigmoid(a[rs]) * b[rs]
            o_ref[rs, :] = _pack_bf16_pairs(_dot(h.astype(BF16), w2s[...]))


def _experts(layer, buf, block_e, n_used, w1, w3, w2):
    n_rows, dp = buf.shape
    nb = n_rows // MOE_ROWS
    d, fe = w1.shape[-2:]

    def row_map(i, be, nu):
        return (jnp.minimum(i, nu[0] - 1), 0)

    grid_spec = pltpu.PrefetchScalarGridSpec(
        num_scalar_prefetch=2,
        grid=(nb,),
        in_specs=[
            pl.BlockSpec((MOE_ROWS, dp), row_map),
            pl.BlockSpec(memory_space=pl.ANY),
            pl.BlockSpec(memory_space=pl.ANY),
            pl.BlockSpec(memory_space=pl.ANY),
        ],
        out_specs=pl.BlockSpec((MOE_ROWS, dp), row_map),
        scratch_shapes=[
            pltpu.VMEM((2, d, fe), F32), pltpu.VMEM((2, d, fe), F32), pltpu.VMEM((2, fe, d), F32),
            pltpu.VMEM((d, fe), BF16), pltpu.VMEM((d, fe), BF16), pltpu.VMEM((fe, d), BF16),
            pltpu.SemaphoreType.DMA((3, 2)),
            pltpu.SMEM((1,), jnp.int32),
        ],
    )
    return pl.pallas_call(
        functools.partial(_expert_kernel, layer),
        grid_spec=grid_spec,
        out_shape=jax.ShapeDtypeStruct((n_rows, dp), jnp.uint32),
        compiler_params=pltpu.CompilerParams(dimension_semantics=("arbitrary",),
                                             vmem_limit_bytes=EXPERT_VMEM_LIMIT),
    )(block_e, n_used, buf, w1, w3, w2)


def _combine_kernel(x1_ref, ya_ref, yb_ref, gates_ref, mod_ref, g_ref, b_ref, *rest):
    o_ref = rest[-1]
    d = D_MODEL
    o_ref[0] = _moe_output(x1_ref[0], ya_ref[0], yb_ref[0], gates_ref[...],
                           mod_ref[0][:, 5 * d:6 * d], g_ref[...], b_ref[...])


def _combine(x1, pairs, gates, mod3, mb0, g, b, out_bsz, ob0, prev):
    bsz, seq, d = x1.shape
    ts = min(COMB_ROWS, seq)
    ns = seq // ts
    in_specs = [
        pl.BlockSpec((1, ts, d), lambda i, j: (i, j, 0)),
        pl.BlockSpec((1, ts, d // 2), lambda i, j: (0, i * ns + j, 0)),
        pl.BlockSpec((1, ts, d // 2), lambda i, j: (1, i * ns + j, 0)),
        pl.BlockSpec((ts, SUBLANES), lambda i, j: (i * ns + j, 0)),
        pl.BlockSpec((1, 1, 6 * d), lambda i, j: (i + mb0, 0, 0)),
        pl.BlockSpec((1, d), lambda i, j: (0, 0)),
        pl.BlockSpec((1, d), lambda i, j: (0, 0)),
    ]
    args = [x1, pairs, pairs, gates, mod3, g, b]
    aliases = {}
    if prev is not None:
        in_specs.append(pl.BlockSpec(memory_space=pl.ANY))
        aliases = {len(args): 0}
        args.append(prev)
    return pl.pallas_call(
        _combine_kernel,
        grid=(bsz, ns),
        in_specs=in_specs,
        out_specs=pl.BlockSpec((1, ts, d), lambda i, j: (i + ob0, j, 0)),
        out_shape=jax.ShapeDtypeStruct((out_bsz, seq, d), F32),
        input_output_aliases=aliases,
        compiler_params=pltpu.CompilerParams(dimension_semantics=("arbitrary", "arbitrary"),
                                             vmem_limit_bytes=MIXER_VMEM_LIMIT),
    )(*args)


def _expert_layout(counts):
    shift = MOE_ROWS.bit_length() - 1
    padded = lax.shift_left(
        lax.shift_right_logical(counts.astype(jnp.int32) + (MOE_ROWS - 1), shift), shift)
    e_out = lax.broadcasted_iota(jnp.int32, (N_EXPERTS, N_EXPERTS), 0)
    e_in = lax.broadcasted_iota(jnp.int32, (N_EXPERTS, N_EXPERTS), 1)
    upto = (e_in <= e_out).astype(BF16)
    padded_f = jnp.broadcast_to(padded.astype(F32), (N_EXPERTS, LANES))
    pad_end = _dot(upto, padded_f.astype(BF16))
    pad_start = pad_end[:, 0:1] - padded.astype(F32)
    block_start = (lax.broadcasted_iota(jnp.int32, (N_EXPERTS, LANES), 1) * MOE_ROWS).astype(F32)
    block_e = jnp.sum((block_start >= pad_end).astype(jnp.int32), axis=0, keepdims=True)
    n_used = lax.shift_right_logical(pad_end[N_EXPERTS - 1:N_EXPERTS, :].astype(jnp.int32), shift)
    blk = jnp.concatenate([jnp.minimum(block_e, N_EXPERTS - 1), n_used,
                           jnp.zeros((SUBLANES - 2, LANES), jnp.int32)], axis=0)
    return pad_start, blk


def _slots(ri, pad_start):
    n = ri.shape[1]
    erow = lax.broadcasted_iota(jnp.int32, (N_EXPERTS, n), 0)
    rows = []
    for k in range(TOP_K):
        start = jnp.sum(jnp.where(erow == ri[k:k + 1, :], pad_start, 0.0), axis=0, keepdims=True)
        rows.append(start.astype(jnp.int32) + ri[TOP_K + k:TOP_K + k + 1, :])
    rows.append(jnp.zeros((SUBLANES - TOP_K, n), jnp.int32))
    return jnp.concatenate(rows, axis=0)


def kernel(x, c, w_ada, b_ada, w_in, sc_conv, cv_conv, cv_conv_b, cv_ln_g, cv_ln_b, sg_ln_g, sg_ln_b, sg_w, sg_b, w_branch, w_gate, b_gate, w_o, ln1_g, ln1_b, w_router, b_router, w1, w3, w2, ln2_g, ln2_b):
    bsz, seq, d = x.shape
    mod3 = _ada(c, w_ada, b_ada).reshape(DEPTH, bsz, 1, 6 * d)
    wrt = w_router.T
    brc = b_router.reshape(N_EXPERTS, 1)
    assert W_BRANCH == d
    w_all = _pack_weights(w_in, jnp.transpose(w_gate, (0, 2, 1, 3)), w_branch, w_o)
    cb = bsz // N_CHAINS
    n_tok = cb * seq
    n_blocks = (n_tok * TOP_K + N_EXPERTS * (MOE_ROWS - 1) + MOE_ROWS - 1) // MOE_ROWS
    assert n_blocks <= LANES and MOE_ROWS & (MOE_ROWS - 1) == 0
    chains = [(x, h * cb) for h in range(N_CHAINS)]
    for l in range(DEPTH):
        last = l == DEPTH - 1
        result = None
        for h in range(N_CHAINS):
            x1, u2, dest, rf, blk = _mixer(
                l, chains[h], mod3[l], h * cb, cb, w_all, sc_conv[l], cv_conv[l],
                cv_conv_b[l].reshape(1, -1), cv_ln_g[l].reshape(1, -1), cv_ln_b[l].reshape(1, -1),
                sg_ln_g[l].reshape(1, -1), sg_ln_b[l].reshape(1, -1), sg_w[l], sg_b[l].T,
                b_gate[l], ln1_g[l].reshape(1, -1), ln1_b[l].reshape(1, -1), wrt, brc)
            buf = _sc_scatter_rows(u2.reshape(n_tok, d // 2), dest[0], dest[1], n_blocks * MOE_ROWS)
            obuf = _experts(l, buf, blk[0, 0:n_blocks], blk[1, 0:1], w1, w3, w2)
            pair_idx = dest[0:TOP_K].reshape(TOP_K * n_tok)
            pairs = _sc_gather_rows(obuf, pair_idx).reshape(TOP_K, n_tok, d // 2)
            gates = rf
            g2, b2 = ln2_g[l].reshape(1, -1), ln2_b[l].reshape(1, -1)
            if last:
                result = _combine(x1, pairs, gates, mod3[l], h * cb, g2, b2, bsz, h * cb, result)
            else:
                chains[h] = (x1, pairs, gates, mod3[l], g2, b2)
    return result
```

```python
import functools

import jax
import jax.numpy as jnp
from jax import lax
from jax.experimental import pallas as pl
from jax.experimental.pallas import tpu as pltpu
from jax.experimental.pallas import tpu_sc as plsc

D_MODEL = 1024
DEPTH = 2
W_BRANCH = 1024
SC_KERNEL = 3
CV_KERNEL = 31
CHUNK = 128
SG_HEADS = 8
N_EXPERTS = 16
N_GROUPS = 4
EXPERTS_PER_GROUP = N_EXPERTS // N_GROUPS
TOP_K = 2
D_EXPERT = 512
ALPHA = (2.0 * DEPTH) ** 0.25
LN_EPS = 1e-5

F32 = jnp.float32
BF16 = jnp.bfloat16

V7X_VMEM_BYTES = 64 * 1024 * 1024
MIXER_VMEM_LIMIT = V7X_VMEM_BYTES - 6 * 1024 * 1024
EXPERT_VMEM_LIMIT = V7X_VMEM_BYTES // 2
SUBLANES = 8
LANES = 128

MIX_ROWS = 256
SC_HALO = SUBLANES
CV_HALO = 32
CONV_ROWS = 128
TIE_LAG = 2
TIE_FREE_JOBS = 4
CONV_COLS = 128
PROJ_COLS = 512
PACK_ROWS = 128
WCOL_GATE = 7 * W_BRANCH
WCOL_BRANCH = WCOL_GATE + 3 * D_MODEL
WCOL_OUT = WCOL_BRANCH + 3 * D_MODEL
MOE_ROWS = 512
MOE_SLAB = 256
COMB_ROWS = 1024
SC_WINDOW = 32
N_CHAINS = 2


def _dot(a, b):
    return jnp.dot(a, b, preferred_element_type=F32)


def _pack_bf16_pairs(v):
    m = v.shape[1] // 2
    lo = lax.bitcast_convert_type(v[:, 0:m].astype(BF16).astype(F32), jnp.uint32)
    hi = lax.bitcast_convert_type(v[:, m:2 * m].astype(BF16).astype(F32), jnp.uint32)
    return jnp.bitwise_or(jnp.bitwise_and(hi, jnp.uint32(0xFFFF0000)),
                          lax.shift_right_logical(lo, jnp.uint32(16)))


def _unpack_bf16_pairs(w):
    lo = lax.bitcast_convert_type(lax.shift_left(w, jnp.uint32(16)), F32)
    hi = lax.bitcast_convert_type(jnp.bitwise_and(w, jnp.uint32(0xFFFF0000)), F32)
    return lo, hi


def _layer_norm(v, g, b):
    mu = jnp.mean(v, axis=-1, keepdims=True)
    vc = v - mu
    var = jnp.mean(vc * vc, axis=-1, keepdims=True)
    return vc * lax.rsqrt(var + LN_EPS) * g + b


def _ada_kernel(c_ref, w_ref, b_ref, o_ref):
    c = c_ref[...]
    c_act = c * jax.nn.sigmoid(c)
    o_ref[0] = jnp.dot(c_act, w_ref[0], preferred_element_type=F32,
                       precision=lax.Precision.HIGHEST) + b_ref[0]


def _ada(c, w_ada, b_ada):
    bsz, d = c.shape
    n = w_ada.shape[-1]
    tn = 1536
    return pl.pallas_call(
        _ada_kernel,
        grid=(DEPTH, n // tn),
        in_specs=[
            pl.BlockSpec((bsz, d), lambda l, j: (0, 0)),
            pl.BlockSpec((1, d, tn), lambda l, j: (l, 0, j)),
            pl.BlockSpec((1, 1, tn), lambda l, j: (l, 0, j)),
        ],
        out_specs=pl.BlockSpec((1, bsz, tn), lambda l, j: (l, 0, j)),
        out_shape=jax.ShapeDtypeStruct((DEPTH, bsz, n), F32),
    )(c, w_ada, b_ada.reshape(DEPTH, 1, n))


def _top2_of4(rows):
    m1 = rows[0]
    i1 = jnp.zeros(rows[0].shape, jnp.int32)
    for k in range(1, 4):
        gt = rows[k] > m1
        m1 = jnp.where(gt, rows[k], m1)
        i1 = jnp.where(gt, k, i1)
    m2 = jnp.full(rows[0].shape, -jnp.inf, F32)
    i2 = jnp.zeros(rows[0].shape, jnp.int32)
    for k in range(4):
        cand = jnp.where(i1 == k, -jnp.inf, rows[k])
        gt = cand > m2
        m2 = jnp.where(gt, cand, m2)
        i2 = jnp.where(gt, k, i2)
    return m1, i1, m2, i2


def _zero_after(v):
    u = lax.bitcast_convert_type(v, jnp.uint32)
    u = lax.shift_right_logical(lax.shift_right_logical(u, jnp.uint32(16)), jnp.uint32(16))
    return lax.bitcast_convert_type(u, F32)


def _conv31_chunk(cvw_ref, cvbufs, cvout, r0, c0, tie):
    cs = slice(c0, c0 + CONV_COLS)
    cvbuf = cvbufs[c0 // PROJ_COLS]
    bs = slice(c0 % PROJ_COLS, c0 % PROJ_COLS + CONV_COLS)
    acc = None
    for r in range(SUBLANES):
        lead = SUBLANES if r else 0
        part = None
        for m in range((CV_KERNEL - 1 - r) // SUBLANES + 1):
            k = CV_KERNEL - 1 - (SUBLANES * m + r)
            start = CV_HALO + r0 - lead - SUBLANES * m
            w_row = cvw_ref[k:k + 1, cs]
            if tie is not None and acc is None and part is None:
                w_row = w_row + tie
            term = w_row * cvbuf[start:start + lead + CONV_ROWS, bs]
            part = term if part is None else part + term
        part = part[lead - r:lead - r + CONV_ROWS]
        acc = part if acc is None else acc + part
    cvout[r0:r0 + CONV_ROWS, cs] = acc


def _bf16_weights(packed):
    return pltpu.bitcast(packed, BF16)


def _pack_weights_kernel(*refs):
    o_ref = refs[-1]
    c0 = 0
    for w_ref in refs[:-1]:
        w = w_ref[0] if len(w_ref.shape) == 3 else w_ref[0, 0]
        o_ref[0, :, c0:c0 + w.shape[1]] = pltpu.bitcast(w.astype(BF16), jnp.uint32)
        c0 += w.shape[1]


def _pack_weights(w_in, w_gate_t, w_branch, w_o):
    depth, k, n_in = w_in.shape
    d = w_o.shape[-1]
    assert w_branch.shape[2] == k and w_gate_t.shape[2] == k and w_o.shape[1] == k
    kb = PACK_ROWS
    per_branch = [pl.BlockSpec((1, 1, kb, d), functools.partial(lambda n, l, j: (l, n, j, 0), n))
                  for n in range(3)]
    return pl.pallas_call(
        _pack_weights_kernel,
        grid=(depth, k // kb),
        in_specs=([pl.BlockSpec((1, kb, n_in), lambda l, j: (l, j, 0))] + per_branch + per_branch
                  + [pl.BlockSpec((1, kb, d), lambda l, j: (l, j, 0))]),
        out_specs=pl.BlockSpec((1, kb // 2, n_in + 7 * d), lambda l, j: (l, j, 0)),
        out_shape=jax.ShapeDtypeStruct((depth, k // 2, n_in + 7 * d), jnp.uint32),
        compiler_params=pltpu.CompilerParams(vmem_limit_bytes=MIXER_VMEM_LIMIT),
    )(w_in, w_gate_t, w_gate_t, w_gate_t, w_branch, w_branch, w_branch, w_o)


def _moe_output(x1, ya_packed, yb_packed, gates, gate2, g, b):
    ya = jnp.concatenate(_unpack_bf16_pairs(ya_packed), axis=1)
    yb = jnp.concatenate(_unpack_bf16_pairs(yb_packed), axis=1)
    h = gates[:, 0:1] * ya + gates[:, 1:2] * yb
    return _layer_norm(ALPHA * x1 + gate2 * h, g, b)


def _mixer_kernel(tiles_per_seq, n_x_refs, *refs):
    last_step = pl.num_programs(0) - 1

    @pl.when(pl.program_id(0) < last_step)
    def _():
        _mixer_step(tiles_per_seq, n_x_refs, True, *refs)

    @pl.when(pl.program_id(0) == last_step)
    def _():
        _mixer_step(tiles_per_seq, n_x_refs, False, *refs)


def _mixer_step(tiles_per_seq, n_x_refs, run_first_half, *refs):
    x_refs = refs[:n_x_refs]
    (mod_ref, modt_ref, w_ref, scw_ref, cvw_ref, cvb_ref, cvg_ref, cvbeta_ref, sgg_ref,
     sgbeta_ref, sgw_ref, sgbt_ref, bg_ref, ln1g_ref, ln1b_ref, wrt_ref, brc_ref,
     x1_ref, u2_ref, dest_ref, rf_ref, blk_ref,
     qbuf, cvbuf0, cvbuf1, cvout, ybuf, pa_buf, pc_buf, gl_buf, mg_buf, xs_buf, ri_all,
     base_ref) = refs[n_x_refs:]
    cvbufs = (cvbuf0, cvbuf1)
    ts = MIX_ROWS
    d = D_MODEL
    wb = W_BRANCH
    step = pl.program_id(0)
    tile = jnp.minimum(step, pl.num_programs(0) - 2)
    first_tile = tile % tiles_per_seq == 0

    @pl.when(step == 0)
    def _():
        base_ref[...] = jnp.zeros_like(base_ref)
        mg_buf[...] = jnp.zeros_like(mg_buf)
        xs_buf[...] = jnp.zeros_like(xs_buf)

    def w_in(c0):
        return _bf16_weights(w_ref[0, :, c0:c0 + PROJ_COLS])

    def w_gate(c0):
        return w_in(WCOL_GATE + c0)

    def w_square(c0):
        return _bf16_weights(w_ref[0, :, c0:c0 + d])

    if not run_first_half:
        for _ in _mixer_tail(step, mg_buf, xs_buf, modt_ref, w_square(WCOL_OUT), ln1g_ref, ln1b_ref, wrt_ref,
                             brc_ref, x1_ref, u2_ref, ri_all, rf_ref, base_ref):
            pass
        pad_start, blk_ref[...] = _expert_layout(base_ref[:, 0:1])
        for t in range(ri_all.shape[0]):
            dest_ref[:, t * ts:(t + 1) * ts] = _slots(ri_all[t], pad_start)
        return

    @pl.when(first_tile)
    def _():
        qbuf[0:SC_HALO, :] = jnp.zeros((SC_HALO, wb), F32)
        for cvbuf in cvbufs:
            cvbuf[0:CV_HALO, :] = jnp.zeros((CV_HALO, PROJ_COLS), F32)

    if n_x_refs == 1:
        x = x_refs[0][0]
    else:
        x1p_ref, ya_ref, yb_ref, gates_ref, modp_ref, g2_ref, b2_ref = x_refs
        x = _moe_output(x1p_ref[0], ya_ref[0], yb_ref[0], gates_ref[...],
                        modp_ref[0][:, 5 * d:6 * d], g2_ref[...], b2_ref[...])
    mod = mod_ref[0]
    shift1, scale1 = mod[:, 0:d], mod[:, d:2 * d]
    ub = (x * (1.0 + scale1) + shift1).astype(BF16)

    def glu_block(c0):
        a = _dot(ub, w_in(3 * wb + c0))
        g = _dot(ub, w_in(4 * wb + c0))
        cvbufs[c0 // PROJ_COLS][CV_HALO:CV_HALO + ts, :] = a * jax.nn.sigmoid(g)

    tail = _mixer_tail(step, mg_buf, xs_buf, modt_ref, w_square(WCOL_OUT), ln1g_ref, ln1b_ref, wrt_ref,
                       brc_ref, x1_ref, u2_ref, ri_all, rf_ref, base_ref)
    next(tail)
    glu_block(0)
    ties = {}

    def add_tie(chunk, tie):
        ties[chunk] = tie + ties[chunk] if chunk in ties else tie

    branch_z = {}

    def mixer_a():
        qbuf[SC_HALO:SC_HALO + ts, :] = pa_buf[:, wb:2 * wb] * pa_buf[:, 2 * wb:3 * wb]
        conv = scw_ref[SC_KERNEL - 1:SC_KERNEL, :] * qbuf[SC_HALO:SC_HALO + ts, :]
        for k in range(SC_KERNEL - 1):
            off = SC_HALO - (SC_KERNEL - 1) + k
            conv = conv + scw_ref[k:k + 1, :] * qbuf[off:off + ts, :]
        ybuf[0] = (pa_buf[:, 0:wb] * conv).astype(BF16)
        qbuf[0:SC_HALO, :] = qbuf[ts:ts + SC_HALO, :]
        branch_z[0] = _dot(ybuf[0], w_square(WCOL_BRANCH))

    def mixer_c():
        gu = jax.nn.gelu(pc_buf[:, 0:wb])
        gv = _layer_norm(jax.nn.gelu(pc_buf[:, wb:2 * wb]), sgg_ref[...],
                         sgbeta_ref[...]).astype(BF16)
        row = lax.broadcasted_iota(jnp.int32, (CHUNK, CHUNK), 0)
        col = lax.broadcasted_iota(jnp.int32, (CHUNK, CHUNK), 1)
        hd = wb // SG_HEADS
        for h in range(SG_HEADS):
            wm = jnp.where(row >= col, sgw_ref[h], 0.0).astype(BF16)
            bias = sgbt_ref[:, h:h + 1]
            for n in range(ts // CHUNK):
                rs = slice(n * CHUNK, (n + 1) * CHUNK)
                cs = slice(h * hd, (h + 1) * hd)
                mixed = _dot(wm, gv[rs, cs]) + bias
                ybuf[2, rs, cs] = (gu[rs, cs] * mixed).astype(BF16)
        branch_z[2] = _dot(ybuf[2], w_square(WCOL_BRANCH + 2 * d))

    def glu_rest():
        for c0 in range(PROJ_COLS, wb, PROJ_COLS):
            glu_block(c0)

    def tail_stage():
        next(tail, None)

    mxu_jobs = ([(pa_buf, w_in, c0, c0) for c0 in range(0, 3 * wb, PROJ_COLS)]
                + [(pc_buf, w_in, c0, 5 * wb + c0) for c0 in range(0, 2 * wb, PROJ_COLS)]
                + [(gl_buf, w_gate, c0, c0) for c0 in range(0, 3 * d, PROJ_COLS)])
    glu_rest()
    after_job = {7: tail_stage, 12: tail_stage}
    conv_jobs = [(r0, c0) for c0 in range(0, wb, CONV_COLS) for r0 in range(0, ts, CONV_ROWS)]
    chunks_per_job = len(conv_jobs) // len(mxu_jobs)
    for c, conv_job in enumerate(conv_jobs):
        _conv31_chunk(cvw_ref, cvbufs, cvout, *conv_job, ties.get(c))
        if (c + 1) % chunks_per_job:
            continue
        i = c // chunks_per_job
        dst, weights, dc, wc = mxu_jobs[i]
        res = _dot(ub, weights(wc))
        dst[:, dc:dc + PROJ_COLS] = res
        if i < len(mxu_jobs) - TIE_FREE_JOBS:
            add_tie(c + TIE_LAG, _zero_after(res[ts - 1:ts, PROJ_COLS - CONV_COLS:PROJ_COLS]))
        if i in after_job:
            after_job[i]()
    for cvbuf in cvbufs:
        cvbuf[0:CV_HALO, :] = cvbuf[ts:ts + CV_HALO, :]

    mixer_a()
    cv = _layer_norm(cvout[...] + cvb_ref[...], cvg_ref[...], cvbeta_ref[...])
    ybuf[1] = (cv * jax.nn.sigmoid(cv)).astype(BF16)
    branch_z[1] = _dot(ybuf[1], w_square(WCOL_BRANCH + d))
    mixer_c()

    merged = None
    for n in range(3):
        gated = jax.nn.sigmoid(gl_buf[:, n * d:(n + 1) * d] + bg_ref[n:n + 1, :]) * branch_z[n]
        merged = gated if merged is None else merged + gated
    mg_buf[...] = merged.astype(BF16)
    xs_buf[...] = x


def _mixer_tail(step, mg_buf, xs_buf, modt_ref, w_out, ln1g_ref, ln1b_ref, wrt_ref, brc_ref,
                x1_ref, u2_ref, ri_all, rf_ref, base_ref):
    ts = MIX_ROWS
    d = D_MODEL
    modt = modt_ref[0]
    gate1, shift2, scale2 = modt[:, 2 * d:3 * d], modt[:, 3 * d:4 * d], modt[:, 4 * d:5 * d]
    hmix = _dot(mg_buf[...], w_out)
    x1 = _layer_norm(ALPHA * xs_buf[...] + gate1 * hmix, ln1g_ref[...], ln1b_ref[...])
    x1_ref[0] = x1
    u2 = x1 * (1.0 + scale2) + shift2
    u2_ref[0] = _pack_bf16_pairs(u2)
    yield

    u_hi = u2.astype(BF16)
    u_lo = (u2 - u_hi.astype(F32)).astype(BF16)
    w_r = wrt_ref[...]
    w_hi = w_r.astype(BF16)
    w_lo = (w_r - w_hi.astype(F32)).astype(BF16)
    nt = (((1,), (1,)), ((), ()))
    both = lax.dot_general(jnp.concatenate([w_hi, w_lo], axis=0), u_hi, nt,
                           preferred_element_type=F32)
    logits = (both[0:N_EXPERTS] + both[N_EXPERTS:2 * N_EXPERTS]
              + lax.dot_general(w_hi, u_lo, nt, preferred_element_type=F32))
    mx = jnp.max(logits, axis=0, keepdims=True)
    ex = jnp.exp(logits - mx)
    scores = ex / jnp.sum(ex, axis=0, keepdims=True)
    sel = scores + brc_ref[...]
    tops = []
    for g in range(N_GROUPS):
        rows = [sel[g * EXPERTS_PER_GROUP + k:g * EXPERTS_PER_GROUP + k + 1, :]
                for k in range(EXPERTS_PER_GROUP)]
        tops.append(_top2_of4(rows))
    best = tops[0][0] + tops[0][2]
    g_idx = jnp.zeros(best.shape, jnp.int32)
    loc1, loc2 = tops[0][1], tops[0][3]
    for g in range(1, N_GROUPS):
        gs = tops[g][0] + tops[g][2]
        gt = gs > best
        best = jnp.where(gt, gs, best)
        g_idx = jnp.where(gt, g, g_idx)
        loc1 = jnp.where(gt, tops[g][1], loc1)
        loc2 = jnp.where(gt, tops[g][3], loc2)
    e0 = g_idx * EXPERTS_PER_GROUP + loc1
    e1 = g_idx * EXPERTS_PER_GROUP + loc2
    erow = lax.broadcasted_iota(jnp.int32, (N_EXPERTS, ts), 0)
    is0 = erow == e0
    is1 = erow == e1
    s0 = jnp.sum(jnp.where(is0, scores, 0.0), axis=0, keepdims=True)
    s1 = jnp.sum(jnp.where(is1, scores, 0.0), axis=0, keepdims=True)
    ssum = s0 + s1
    yield

    onehot = jnp.logical_or(is0, is1).astype(BF16)
    src = lax.broadcasted_iota(jnp.int32, (ts, ts), 0)
    dst = lax.broadcasted_iota(jnp.int32, (ts, ts), 1)
    earlier = (src < dst).astype(BF16)
    prior = _dot(onehot, earlier) + base_ref[:, 0:1]
    r0 = jnp.sum(jnp.where(is0, prior, 0.0), axis=0, keepdims=True)
    r1 = jnp.sum(jnp.where(is1, prior, 0.0), axis=0, keepdims=True)
    counts = jnp.sum(onehot.astype(F32), axis=1, keepdims=True)
    base_ref[...] = base_ref[...] + jnp.where(step > 0, counts, 0.0)

    zi = jnp.zeros((SUBLANES - 4, ts), jnp.int32)
    ri_all[jnp.maximum(step - 1, 0)] = jnp.concatenate(
        [e0, e1, r0.astype(jnp.int32), r1.astype(jnp.int32), zi], axis=0)
    zf = jnp.zeros((SUBLANES - 2, ts), F32)
    rf_ref[...] = jnp.concatenate([s0 / ssum, s1 / ssum, zf], axis=0).T


def _mixer(layer, xsrc, mod3, mb0, bsz, w_all, scw, cvw, cvb, cvg, cvbeta, sgg, sgbeta, sgw, sgbt,
           bg, ln1g, ln1b, wrt, brc):
    _, seq, d = xsrc[0].shape
    ts = MIX_ROWS
    ns = seq // ts
    n_tok = bsz * seq

    n_tiles = bsz * ns

    def const(shape):
        zeros = (0,) * len(shape)
        return pl.BlockSpec(shape, lambda s: zeros, pipeline_mode=pl.Buffered(1))

    def first_half(s):
        return jnp.minimum(s, n_tiles - 1)

    def second_half(s):
        return jnp.maximum(s - 1, 0)

    def layer_weights(shape):
        block = (1,) + tuple(shape[1:])
        index = (layer,) + (0,) * (len(shape) - 1)
        return pl.BlockSpec(block, lambda s: index, pipeline_mode=pl.Buffered(1))

    if len(xsrc) == 2:
        x, xb0 = xsrc
        x_args = [x]
        x_specs = [pl.BlockSpec((1, ts, d),
                                lambda s: (first_half(s) // ns + xb0, first_half(s) % ns, 0))]
    else:
        x1p, pairs, gates, mod3p, g2, b2 = xsrc
        x_args = [x1p, pairs, pairs, gates, mod3p, g2, b2]
        x_specs = [
            pl.BlockSpec((1, ts, d), lambda s: (first_half(s) // ns, first_half(s) % ns, 0)),
            pl.BlockSpec((1, ts, d // 2), lambda s: (0, first_half(s), 0)),
            pl.BlockSpec((1, ts, d // 2), lambda s: (1, first_half(s), 0)),
            pl.BlockSpec((ts, SUBLANES), lambda s: (first_half(s), 0)),
            pl.BlockSpec((1, 1, 6 * d), lambda s: (first_half(s) // ns + mb0, 0, 0)),
            const(g2.shape), const(b2.shape),
        ]
    in_specs = x_specs + [
        pl.BlockSpec((1, 1, 6 * d), lambda s: (first_half(s) // ns + mb0, 0, 0)),
        pl.BlockSpec((1, 1, 6 * d), lambda s: (second_half(s) // ns + mb0, 0, 0)),
        layer_weights(w_all.shape), const(scw.shape), const(cvw.shape), const(cvb.shape),
        const(cvg.shape), const(cvbeta.shape), const(sgg.shape), const(sgbeta.shape),
        const(sgw.shape), const(sgbt.shape), const(bg.shape), const(ln1g.shape), const(ln1b.shape),
        const(wrt.shape), const(brc.shape),
    ]
    out_specs = [
        pl.BlockSpec((1, ts, d), lambda s: (second_half(s) // ns, second_half(s) % ns, 0)),
        pl.BlockSpec((1, ts, d // 2), lambda s: (second_half(s) // ns, second_half(s) % ns, 0)),
        pl.BlockSpec((SUBLANES, n_tok), lambda s: (0, 0)),
        pl.BlockSpec((ts, SUBLANES), lambda s: (second_half(s), 0)),
        pl.BlockSpec((SUBLANES, LANES), lambda s: (0, 0)),
    ]
    out_shape = [
        jax.ShapeDtypeStruct((bsz, seq, d), F32),
        jax.ShapeDtypeStruct((bsz, seq, d // 2), jnp.uint32),
        jax.ShapeDtypeStruct((SUBLANES, n_tok), jnp.int32),
        jax.ShapeDtypeStruct((n_tok, SUBLANES), F32),
        jax.ShapeDtypeStruct((SUBLANES, LANES), jnp.int32),
    ]
    return pl.pallas_call(
        functools.partial(_mixer_kernel, ns, len(x_args)),
        grid=(n_tiles + 1,),
        in_specs=in_specs,
        out_specs=out_specs,
        out_shape=out_shape,
        scratch_shapes=[
            pltpu.VMEM((SC_HALO + ts, W_BRANCH), F32),
            pltpu.VMEM((CV_HALO + ts, PROJ_COLS), F32),
            pltpu.VMEM((CV_HALO + ts, PROJ_COLS), F32),
            pltpu.VMEM((ts, W_BRANCH), F32),
            pltpu.VMEM((3, ts, W_BRANCH), BF16),
            pltpu.VMEM((ts, 3 * W_BRANCH), F32),
            pltpu.VMEM((ts, 2 * W_BRANCH), F32),
            pltpu.VMEM((ts, 3 * D_MODEL), F32),
            pltpu.VMEM((ts, d), BF16),
            pltpu.VMEM((ts, d), F32),
            pltpu.VMEM((n_tiles, SUBLANES, ts), jnp.int32),
            pltpu.VMEM((N_EXPERTS, LANES), F32),
        ],
        compiler_params=pltpu.CompilerParams(
            dimension_semantics=("arbitrary",),
            vmem_limit_bytes=MIXER_VMEM_LIMIT),
    )(*x_args, mod3, mod3, w_all, scw, cvw, cvb, cvg, cvbeta, sgg, sgbeta, sgw, sgbt, bg,
      ln1g, ln1b, wrt, brc)


def _sc_workers():
    info = plsc.get_sparse_core_info()
    return info.num_cores, info.num_cores * info.num_subcores


def _sc_scatter_rows(rows, dest_a, dest_b, n_out):
    n, d = rows.shape
    nc, nw = _sc_workers()
    per_w = n // nw
    n_win = per_w // SC_WINDOW
    ia = dest_a.reshape(nw, n_win, SC_WINDOW)
    ib = dest_b.reshape(nw, n_win, SC_WINDOW)
    mesh = plsc.VectorSubcoreMesh(core_axis_name="c", subcore_axis_name="s")

    @functools.partial(
        pl.kernel, mesh=mesh,
        out_type=jax.ShapeDtypeStruct((n_out, d), rows.dtype),
        scratch_types=[
            pltpu.VMEM((n_win, SC_WINDOW), jnp.int32),
            pltpu.VMEM((n_win, SC_WINDOW), jnp.int32),
            pltpu.VMEM((SC_WINDOW, d), rows.dtype),
        ],
    )
    def scatter(rows_hbm, ia_hbm, ib_hbm, out_hbm, ia_v, ib_v, rows_v):
        wid = lax.axis_index("s") * nc + lax.axis_index("c")
        pltpu.sync_copy(ia_hbm.at[wid], ia_v)
        pltpu.sync_copy(ib_hbm.at[wid], ib_v)
        base = wid * per_w

        @pl.loop(0, n_win)
        def _(j):
            pltpu.sync_copy(rows_hbm.at[pl.ds(base + j * SC_WINDOW, SC_WINDOW)], rows_v)
            pltpu.sync_copy(rows_v, out_hbm.at[ia_v.at[j]])
            pltpu.sync_copy(rows_v, out_hbm.at[ib_v.at[j]])

    return scatter(rows, ia, ib)


def _sc_gather_rows(table, idx):
    n = idx.shape[0]
    d = table.shape[1]
    nc, nw = _sc_workers()
    per_w = n // nw
    n_win = per_w // SC_WINDOW
    idx3 = idx.reshape(nw, n_win, SC_WINDOW)
    mesh = plsc.VectorSubcoreMesh(core_axis_name="c", subcore_axis_name="s")

    @functools.partial(
        pl.kernel, mesh=mesh,
        out_type=jax.ShapeDtypeStruct((n, d), table.dtype),
        scratch_types=[
            pltpu.VMEM((n_win, SC_WINDOW), jnp.int32),
            pltpu.VMEM((SC_WINDOW, d), table.dtype),
        ],
    )
    def gather(table_hbm, idx_hbm, out_hbm, idx_v, rows_v):
        wid = lax.axis_index("s") * nc + lax.axis_index("c")
        pltpu.sync_copy(idx_hbm.at[wid], idx_v)
        base = wid * per_w

        @pl.loop(0, n_win)
        def _(j):
            pltpu.sync_copy(table_hbm.at[idx_v.at[j]], rows_v)
            pltpu.sync_copy(rows_v, out_hbm.at[pl.ds(base + j * SC_WINDOW, SC_WINDOW)])

    return gather(table, idx3)


def _expert_kernel(layer, be_ref, nused_ref, x_ref, w1_hbm, w3_hbm, w2_hbm, o_ref,
                   w1f, w3f, w2f, w1s, w3s, w2s, sems, group_ref):
    i = pl.program_id(0)
    n_used = nused_ref[0]
    e_now = be_ref[i]

    def fetch(e, slot):
        return [pltpu.make_async_copy(w_hbm.at[layer, e], w_f.at[slot], sems.at[k, slot])
                for k, (w_hbm, w_f) in enumerate(((w1_hbm, w1f), (w3_hbm, w3f), (w2_hbm, w2f)))]

    @pl.when(i == 0)
    def _():
        group_ref[0] = 0
        for copy in fetch(e_now, 0):
            copy.start()

    first_of_group = jnp.logical_and(
        i < n_used, jnp.logical_or(i == 0, e_now != be_ref[jnp.maximum(i - 1, 0)]))

    @pl.when(first_of_group)
    def _():
        slot = group_ref[0] % 2
        for copy in fetch(e_now, slot):
            copy.wait()
        w1s[...] = w1f[slot].astype(BF16)
        w3s[...] = w3f[slot].astype(BF16)
        w2s[...] = w2f[slot].astype(BF16)
        last = be_ref.shape[0] - 1
        j = lax.while_loop(
            lambda j: jnp.logical_and(j < n_used, be_ref[jnp.minimum(j, last)] == e_now),
            lambda j: j + 1, i + 1)
        next_e = be_ref[jnp.minimum(j, last)]

        @pl.when(j < n_used)
        def _():
            for copy in fetch(next_e, 1 - slot):
                copy.start()

        group_ref[0] = group_ref[0] + 1

    @pl.when(i < n_used)
    def _():
        lo, hi = _unpack_bf16_pairs(x_ref[...])
        xb = jnp.concatenate([lo, hi], axis=1).astype(BF16)
        a = _dot(xb, w1s[...])
        b = _dot(xb, w3s[...])
        for r0 in range(0, MOE_ROWS, MOE_SLAB):
            rs = slice(r0, r0 + MOE_SLAB)
            h = a[rs] * jax.nn.sigmoid(a[rs]) * b[rs]
            o_ref[rs, :] = _pack_bf16_pairs(_dot(h.astype(BF16), w2s[...]))


def _experts(layer, buf, block_e, n_used, w1, w3, w2):
    n_rows, dp = buf.shape
    nb = n_rows // MOE_ROWS
    d, fe = w1.shape[-2:]

    def row_map(i, be, nu):
        return (jnp.minimum(i, nu[0] - 1), 0)

    grid_spec = pltpu.PrefetchScalarGridSpec(
        num_scalar_prefetch=2,
        grid=(nb,),
        in_specs=[
            pl.BlockSpec((MOE_ROWS, dp), row_map),
            pl.BlockSpec(memory_space=pl.ANY),
            pl.BlockSpec(memory_space=pl.ANY),
            pl.BlockSpec(memory_space=pl.ANY),
        ],
        out_specs=pl.BlockSpec((MOE_ROWS, dp), row_map),
        scratch_shapes=[
            pltpu.VMEM((2, d, fe), F32), pltpu.VMEM((2, d, fe), F32), pltpu.VMEM((2, fe, d), F32),
            pltpu.VMEM((d, fe), BF16), pltpu.VMEM((d, fe), BF16), pltpu.VMEM((fe, d), BF16),
            pltpu.SemaphoreType.DMA((3, 2)),
            pltpu.SMEM((1,), jnp.int32),
        ],
    )
    return pl.pallas_call(
        functools.partial(_expert_kernel, layer),
        grid_spec=grid_spec,
        out_shape=jax.ShapeDtypeStruct((n_rows, dp), jnp.uint32),
        compiler_params=pltpu.CompilerParams(dimension_semantics=("arbitrary",),
                                             vmem_limit_bytes=EXPERT_VMEM_LIMIT),
    )(block_e, n_used, buf, w1, w3, w2)


def _combine_kernel(x1_ref, ya_ref, yb_ref, gates_ref, mod_ref, g_ref, b_ref, *rest):
    o_ref = rest[-1]
    d = D_MODEL
    o_ref[0] = _moe_output(x1_ref[0], ya_ref[0], yb_ref[0], gates_ref[...],
                           mod_ref[0][:, 5 * d:6 * d], g_ref[...], b_ref[...])


def _combine(x1, pairs, gates, mod3, mb0, g, b, out_bsz, ob0, prev):
    bsz, seq, d = x1.shape
    ts = min(COMB_ROWS, seq)
    ns = seq // ts
    in_specs = [
        pl.BlockSpec((1, ts, d), lambda i, j: (i, j, 0)),
        pl.BlockSpec((1, ts, d // 2), lambda i, j: (0, i * ns + j, 0)),
        pl.BlockSpec((1, ts, d // 2), lambda i, j: (1, i * ns + j, 0)),
        pl.BlockSpec((ts, SUBLANES), lambda i, j: (i * ns + j, 0)),
        pl.BlockSpec((1, 1, 6 * d), lambda i, j: (i + mb0, 0, 0)),
        pl.BlockSpec((1, d), lambda i, j: (0, 0)),
        pl.BlockSpec((1, d), lambda i, j: (0, 0)),
    ]
    args = [x1, pairs, pairs, gates, mod3, g, b]
    aliases = {}
    if prev is not None:
        in_specs.append(pl.BlockSpec(memory_space=pl.ANY))
        aliases = {len(args): 0}
        args.append(prev)
    return pl.pallas_call(
        _combine_kernel,
        grid=(bsz, ns),
        in_specs=in_specs,
        out_specs=pl.BlockSpec((1, ts, d), lambda i, j: (i + ob0, j, 0)),
        out_shape=jax.ShapeDtypeStruct((out_bsz, seq, d), F32),
        input_output_aliases=aliases,
        compiler_params=pltpu.CompilerParams(dimension_semantics=("arbitrary", "arbitrary"),
                                             vmem_limit_bytes=MIXER_VMEM_LIMIT),
    )(*args)


def _expert_layout(counts):
    shift = MOE_ROWS.bit_length() - 1
    padded = lax.shift_left(
        lax.shift_right_logical(counts.astype(jnp.int32) + (MOE_ROWS - 1), shift), shift)
    e_out = lax.broadcasted_iota(jnp.int32, (N_EXPERTS, N_EXPERTS), 0)
    e_in = lax.broadcasted_iota(jnp.int32, (N_EXPERTS, N_EXPERTS), 1)
    upto = (e_in <= e_out).astype(BF16)
    padded_f = jnp.broadcast_to(padded.astype(F32), (N_EXPERTS, LANES))
    pad_end = _dot(upto, padded_f.astype(BF16))
    pad_start = pad_end[:, 0:1] - padded.astype(F32)
    block_start = (lax.broadcasted_iota(jnp.int32, (N_EXPERTS, LANES), 1) * MOE_ROWS).astype(F32)
    block_e = jnp.sum((block_start >= pad_end).astype(jnp.int32), axis=0, keepdims=True)
    n_used = lax.shift_right_logical(pad_end[N_EXPERTS - 1:N_EXPERTS, :].astype(jnp.int32), shift)
    blk = jnp.concatenate([jnp.minimum(block_e, N_EXPERTS - 1), n_used,
                           jnp.zeros((SUBLANES - 2, LANES), jnp.int32)], axis=0)
    return pad_start, blk


def _slots(ri, pad_start):
    n = ri.shape[1]
    erow = lax.broadcasted_iota(jnp.int32, (N_EXPERTS, n), 0)
    rows = []
    for k in range(TOP_K):
        start = jnp.sum(jnp.where(erow == ri[k:k + 1, :], pad_start, 0.0), axis=0, keepdims=True)
        rows.append(start.astype(jnp.int32) + ri[TOP_K + k:TOP_K + k + 1, :])
    rows.append(jnp.zeros((SUBLANES - TOP_K, n), jnp.int32))
    return jnp.concatenate(rows, axis=0)


def kernel(x, c, w_ada, b_ada, w_in, sc_conv, cv_conv, cv_conv_b, cv_ln_g, cv_ln_b, sg_ln_g, sg_ln_b, sg_w, sg_b, w_branch, w_gate, b_gate, w_o, ln1_g, ln1_b, w_router, b_router, w1, w3, w2, ln2_g, ln2_b):
    bsz, seq, d = x.shape
    mod3 = _ada(c, w_ada, b_ada).reshape(DEPTH, bsz, 1, 6 * d)
    wrt = w_router.T
    brc = b_router.reshape(N_EXPERTS, 1)
    assert W_BRANCH == d
    w_all = _pack_weights(w_in, jnp.transpose(w_gate, (0, 2, 1, 3)), w_branch, w_o)
    cb = bsz // N_CHAINS
    n_tok = cb * seq
    n_blocks = (n_tok * TOP_K + N_EXPERTS * (MOE_ROWS - 1) + MOE_ROWS - 1) // MOE_ROWS
    assert n_blocks <= LANES and MOE_ROWS & (MOE_ROWS - 1) == 0
    chains = [(x, h * cb) for h in range(N_CHAINS)]
    for l in range(DEPTH):
        last = l == DEPTH - 1
        result = None
        for h in range(N_CHAINS):
            x1, u2, dest, rf, blk = _mixer(
                l, chains[h], mod3[l], h * cb, cb, w_all, sc_conv[l], cv_conv[l],
                cv_conv_b[l].reshape(1, -1), cv_ln_g[l].reshape(1, -1), cv_ln_b[l].reshape(1, -1),
                sg_ln_g[l].reshape(1, -1), sg_ln_b[l].reshape(1, -1), sg_w[l], sg_b[l].T,
                b_gate[l], ln1_g[l].reshape(1, -1), ln1_b[l].reshape(1, -1), wrt, brc)
            buf = _sc_scatter_rows(u2.reshape(n_tok, d // 2), dest[0], dest[1], n_blocks * MOE_ROWS)
            obuf = _experts(l, buf, blk[0, 0:n_blocks], blk[1, 0:1], w1, w3, w2)
            pair_idx = dest[0:TOP_K].reshape(TOP_K * n_tok)
            pairs = _sc_gather_rows(obuf, pair_idx).reshape(TOP_K, n_tok, d // 2)
            gates = rf
            g2, b2 = ln2_g[l].reshape(1, -1), ln2_b[l].reshape(1, -1)
            if last:
                result = _combine(x1, pairs, gates, mod3[l], h * cb, g2, b2, bsz, h * cb, result)
            else:
                chains[h] = (x1, pairs, gates, mod3[l], g2, b2)
    return result
```

```python
import functools

import jax
import jax.numpy as jnp
from jax import lax
from jax.experimental import pallas as pl
from jax.experimental.pallas import tpu as pltpu
from jax.experimental.pallas import tpu_sc as plsc

D_MODEL = 1024
DEPTH = 2
W_BRANCH = 1024
SC_KERNEL = 3
CV_KERNEL = 31
CHUNK = 128
SG_HEADS = 8
N_EXPERTS = 16
N_GROUPS = 4
EXPERTS_PER_GROUP = N_EXPERTS // N_GROUPS
TOP_K = 2
D_EXPERT = 512
ALPHA = (2.0 * DEPTH) ** 0.25
LN_EPS = 1e-5

F32 = jnp.float32
BF16 = jnp.bfloat16

V7X_VMEM_BYTES = 64 * 1024 * 1024
MIXER_VMEM_LIMIT = V7X_VMEM_BYTES - 6 * 1024 * 1024
EXPERT_VMEM_LIMIT = V7X_VMEM_BYTES // 2
SUBLANES = 8
LANES = 128

MIX_ROWS = 256
SC_HALO = SUBLANES
CV_HALO = 32
CONV_ROWS = 128
TIE_LAG = 2
TIE_FREE_JOBS = 4
CONV_COLS = 128
PROJ_COLS = 512
PACK_ROWS = 128
WCOL_GATE = 7 * W_BRANCH
WCOL_BRANCH = WCOL_GATE + 3 * D_MODEL
WCOL_OUT = WCOL_BRANCH + 3 * D_MODEL
MOE_ROWS = 512
MOE_SLAB = 256
COMB_ROWS = 1024
SC_WINDOW = 32
N_CHAINS = 2


def _dot(a, b):
    return jnp.dot(a, b, preferred_element_type=F32)


def _pack_bf16_pairs(v):
    m = v.shape[1] // 2
    lo = lax.bitcast_convert_type(v[:, 0:m].astype(BF16).astype(F32), jnp.uint32)
    hi = lax.bitcast_convert_type(v[:, m:2 * m].astype(BF16).astype(F32), jnp.uint32)
    return jnp.bitwise_or(jnp.bitwise_and(hi, jnp.uint32(0xFFFF0000)),
                          lax.shift_right_logical(lo, jnp.uint32(16)))


def _unpack_bf16_pairs(w):
    lo = lax.bitcast_convert_type(lax.shift_left(w, jnp.uint32(16)), F32)
    hi = lax.bitcast_convert_type(jnp.bitwise_and(w, jnp.uint32(0xFFFF0000)), F32)
    return lo, hi


def _layer_norm(v, g, b):
    mu = jnp.mean(v, axis=-1, keepdims=True)
    vc = v - mu
    var = jnp.mean(vc * vc, axis=-1, keepdims=True)
    return vc * lax.rsqrt(var + LN_EPS) * g + b


def _ada_kernel(c_ref, w_ref, b_ref, o_ref):
    c = c_ref[...]
    c_act = c * jax.nn.sigmoid(c)
    n = c.shape[0]
    w = w_ref[0]
    c_hi = c_act.astype(BF16)
    c_lo = (c_act - c_hi.astype(F32)).astype(BF16)
    w_hi = w.astype(BF16)
    w_lo = (w - w_hi.astype(F32)).astype(BF16)
    both = _dot(jnp.concatenate([c_hi, c_lo], axis=0), w_hi)
    o_ref[0] = both[0:n] + both[n:2 * n] + _dot(c_hi, w_lo) + b_ref[0]


def _ada(c, w_ada, b_ada):
    bsz, d = c.shape
    n = w_ada.shape[-1]
    tn = 1536
    return pl.pallas_call(
        _ada_kernel,
        grid=(DEPTH, n // tn),
        in_specs=[
            pl.BlockSpec((bsz, d), lambda l, j: (0, 0)),
            pl.BlockSpec((1, d, tn), lambda l, j: (l, 0, j)),
            pl.BlockSpec((1, 1, tn), lambda l, j: (l, 0, j)),
        ],
        out_specs=pl.BlockSpec((1, bsz, tn), lambda l, j: (l, 0, j)),
        out_shape=jax.ShapeDtypeStruct((DEPTH, bsz, n), F32),
    )(c, w_ada, b_ada.reshape(DEPTH, 1, n))


def _top2_of4(rows):
    m1 = rows[0]
    i1 = jnp.zeros(rows[0].shape, jnp.int32)
    for k in range(1, 4):
        gt = rows[k] > m1
        m1 = jnp.where(gt, rows[k], m1)
        i1 = jnp.where(gt, k, i1)
    m2 = jnp.full(rows[0].shape, -jnp.inf, F32)
    i2 = jnp.zeros(rows[0].shape, jnp.int32)
    for k in range(4):
        cand = jnp.where(i1 == k, -jnp.inf, rows[k])
        gt = cand > m2
        m2 = jnp.where(gt, cand, m2)
        i2 = jnp.where(gt, k, i2)
    return m1, i1, m2, i2


def _zero_after(v):
    u = lax.bitcast_convert_type(v, jnp.uint32)
    u = lax.shift_right_logical(lax.shift_right_logical(u, jnp.uint32(16)), jnp.uint32(16))
    return lax.bitcast_convert_type(u, F32)


def _conv31_chunk(cvw_ref, cvbufs, cvout, r0, c0, tie):
    cs = slice(c0, c0 + CONV_COLS)
    cvbuf = cvbufs[c0 // PROJ_COLS]
    bs = slice(c0 % PROJ_COLS, c0 % PROJ_COLS + CONV_COLS)
    acc = None
    for r in range(SUBLANES):
        lead = SUBLANES if r else 0
        part = None
        for m in range((CV_KERNEL - 1 - r) // SUBLANES + 1):
            k = CV_KERNEL - 1 - (SUBLANES * m + r)
            start = CV_HALO + r0 - lead - SUBLANES * m
            w_row = cvw_ref[k:k + 1, cs]
            if tie is not None and acc is None and part is None:
                w_row = w_row + tie
            term = w_row * cvbuf[start:start + lead + CONV_ROWS, bs]
            part = term if part is None else part + term
        part = part[lead - r:lead - r + CONV_ROWS]
        acc = part if acc is None else acc + part
    cvout[r0:r0 + CONV_ROWS, cs] = acc


def _bf16_weights(packed):
    return pltpu.bitcast(packed, BF16)


def _pack_weights_kernel(*refs):
    o_ref = refs[-1]
    c0 = 0
    for w_ref in refs[:-1]:
        w = w_ref[0] if len(w_ref.shape) == 3 else w_ref[0, 0]
        o_ref[0, :, c0:c0 + w.shape[1]] = pltpu.bitcast(w.astype(BF16), jnp.uint32)
        c0 += w.shape[1]


def _pack_weights(w_in, w_gate_t, w_branch, w_o):
    depth, k, n_in = w_in.shape
    d = w_o.shape[-1]
    assert w_branch.shape[2] == k and w_gate_t.shape[2] == k and w_o.shape[1] == k
    kb = PACK_ROWS
    per_branch = [pl.BlockSpec((1, 1, kb, d), functools.partial(lambda n, l, j: (l, n, j, 0), n))
                  for n in range(3)]
    return pl.pallas_call(
        _pack_weights_kernel,
        grid=(depth, k // kb),
        in_specs=([pl.BlockSpec((1, kb, n_in), lambda l, j: (l, j, 0))] + per_branch + per_branch
                  + [pl.BlockSpec((1, kb, d), lambda l, j: (l, j, 0))]),
        out_specs=pl.BlockSpec((1, kb // 2, n_in + 7 * d), lambda l, j: (l, j, 0)),
        out_shape=jax.ShapeDtypeStruct((depth, k // 2, n_in + 7 * d), jnp.uint32),
        compiler_params=pltpu.CompilerParams(vmem_limit_bytes=MIXER_VMEM_LIMIT),
    )(w_in, w_gate_t, w_gate_t, w_gate_t, w_branch, w_branch, w_branch, w_o)


def _moe_output(x1, ya_packed, yb_packed, gates, gate2, g, b):
    ya = jnp.concatenate(_unpack_bf16_pairs(ya_packed), axis=1)
    yb = jnp.concatenate(_unpack_bf16_pairs(yb_packed), axis=1)
    h = gates[:, 0:1] * ya + gates[:, 1:2] * yb
    return _layer_norm(ALPHA * x1 + gate2 * h, g, b)


def _mixer_kernel(tiles_per_seq, n_x_refs, *refs):
    last_step = pl.num_programs(0) - 1

    @pl.when(pl.program_id(0) < last_step)
    def _():
        _mixer_step(tiles_per_seq, n_x_refs, True, *refs)

    @pl.when(pl.program_id(0) == last_step)
    def _():
        _mixer_step(tiles_per_seq, n_x_refs, False, *refs)


def _mixer_step(tiles_per_seq, n_x_refs, run_first_half, *refs):
    x_refs = refs[:n_x_refs]
    (mod_ref, modt_ref, w_ref, scw_ref, cvw_ref, cvb_ref, cvg_ref, cvbeta_ref, sgg_ref,
     sgbeta_ref, sgw_ref, sgbt_ref, bg_ref, ln1g_ref, ln1b_ref, wrt_ref, brc_ref,
     x1_ref, u2_ref, dest_ref, rf_ref, blk_ref,
     qbuf, cvbuf0, cvbuf1, cvout, ybuf, pa_buf, pc_buf, gl_buf, mg_buf, xs_buf, ri_all,
     base_ref) = refs[n_x_refs:]
    cvbufs = (cvbuf0, cvbuf1)
    ts = MIX_ROWS
    d = D_MODEL
    wb = W_BRANCH
    step = pl.program_id(0)
    tile = jnp.minimum(step, pl.num_programs(0) - 2)
    first_tile = tile % tiles_per_seq == 0

    @pl.when(step == 0)
    def _():
        base_ref[...] = jnp.zeros_like(base_ref)
        mg_buf[...] = jnp.zeros_like(mg_buf)
        xs_buf[...] = jnp.zeros_like(xs_buf)

    def w_in(c0):
        return _bf16_weights(w_ref[0, :, c0:c0 + PROJ_COLS])

    def w_gate(c0):
        return w_in(WCOL_GATE + c0)

    def w_square(c0):
        return _bf16_weights(w_ref[0, :, c0:c0 + d])

    if not run_first_half:
        for _ in _mixer_tail(step, mg_buf, xs_buf, modt_ref, w_square(WCOL_OUT), ln1g_ref, ln1b_ref, wrt_ref,
                             brc_ref, x1_ref, u2_ref, ri_all, rf_ref, base_ref):
            pass
        pad_start, blk_ref[...] = _expert_layout(base_ref[:, 0:1])
        for t in range(ri_all.shape[0]):
            dest_ref[:, t * ts:(t + 1) * ts] = _slots(ri_all[t], pad_start)
        return

    @pl.when(first_tile)
    def _():
        qbuf[0:SC_HALO, :] = jnp.zeros((SC_HALO, wb), F32)
        for cvbuf in cvbufs:
            cvbuf[0:CV_HALO, :] = jnp.zeros((CV_HALO, PROJ_COLS), F32)

    if n_x_refs == 1:
        x = x_refs[0][0]
    else:
        x1p_ref, ya_ref, yb_ref, gates_ref, modp_ref, g2_ref, b2_ref = x_refs
        x = _moe_output(x1p_ref[0], ya_ref[0], yb_ref[0], gates_ref[...],
                        modp_ref[0][:, 5 * d:6 * d], g2_ref[...], b2_ref[...])
    mod = mod_ref[0]
    shift1, scale1 = mod[:, 0:d], mod[:, d:2 * d]
    ub = (x * (1.0 + scale1) + shift1).astype(BF16)

    def glu_block(c0):
        a = _dot(ub, w_in(3 * wb + c0))
        g = _dot(ub, w_in(4 * wb + c0))
        cvbufs[c0 // PROJ_COLS][CV_HALO:CV_HALO + ts, :] = a * jax.nn.sigmoid(g)

    tail = _mixer_tail(step, mg_buf, xs_buf, modt_ref, w_square(WCOL_OUT), ln1g_ref, ln1b_ref, wrt_ref,
                       brc_ref, x1_ref, u2_ref, ri_all, rf_ref, base_ref)
    next(tail)
    glu_block(0)
    ties = {}

    def add_tie(chunk, tie):
        ties[chunk] = tie + ties[chunk] if chunk in ties else tie

    branch_z = {}

    def mixer_a():
        qbuf[SC_HALO:SC_HALO + ts, :] = pa_buf[:, wb:2 * wb] * pa_buf[:, 2 * wb:3 * wb]
        conv = scw_ref[SC_KERNEL - 1:SC_KERNEL, :] * qbuf[SC_HALO:SC_HALO + ts, :]
        for k in range(SC_KERNEL - 1):
            off = SC_HALO - (SC_KERNEL - 1) + k
            conv = conv + scw_ref[k:k + 1, :] * qbuf[off:off + ts, :]
        ybuf[0] = (pa_buf[:, 0:wb] * conv).astype(BF16)
        qbuf[0:SC_HALO, :] = qbuf[ts:ts + SC_HALO, :]
        branch_z[0] = _dot(ybuf[0], w_square(WCOL_BRANCH))

    def mixer_c():
        gu = jax.nn.gelu(pc_buf[:, 0:wb])
        gv = _layer_norm(jax.nn.gelu(pc_buf[:, wb:2 * wb]), sgg_ref[...],
                         sgbeta_ref[...]).astype(BF16)
        row = lax.broadcasted_iota(jnp.int32, (CHUNK, CHUNK), 0)
        col = lax.broadcasted_iota(jnp.int32, (CHUNK, CHUNK), 1)
        hd = wb // SG_HEADS
        for h in range(SG_HEADS):
            wm = jnp.where(row >= col, sgw_ref[h], 0.0).astype(BF16)
            bias = sgbt_ref[:, h:h + 1]
            for n in range(ts // CHUNK):
                rs = slice(n * CHUNK, (n + 1) * CHUNK)
                cs = slice(h * hd, (h + 1) * hd)
                mixed = _dot(wm, gv[rs, cs]) + bias
                ybuf[2, rs, cs] = (gu[rs, cs] * mixed).astype(BF16)
        branch_z[2] = _dot(ybuf[2], w_square(WCOL_BRANCH + 2 * d))

    def glu_rest():
        for c0 in range(PROJ_COLS, wb, PROJ_COLS):
            glu_block(c0)

    def tail_stage():
        next(tail, None)

    mxu_jobs = ([(pa_buf, w_in, c0, c0) for c0 in range(0, 3 * wb, PROJ_COLS)]
                + [(pc_buf, w_in, c0, 5 * wb + c0) for c0 in range(0, 2 * wb, PROJ_COLS)]
                + [(gl_buf, w_gate, c0, c0) for c0 in range(0, 3 * d, PROJ_COLS)])
    glu_rest()
    after_job = {7: tail_stage, 12: tail_stage}
    conv_jobs = [(r0, c0) for c0 in range(0, wb, CONV_COLS) for r0 in range(0, ts, CONV_ROWS)]
    chunks_per_job = len(conv_jobs) // len(mxu_jobs)
    for c, conv_job in enumerate(conv_jobs):
        _conv31_chunk(cvw_ref, cvbufs, cvout, *conv_job, ties.get(c))
        if (c + 1) % chunks_per_job:
            continue
        i = c // chunks_per_job
        dst, weights, dc, wc = mxu_jobs[i]
        res = _dot(ub, weights(wc))
        dst[:, dc:dc + PROJ_COLS] = res
        if i < len(mxu_jobs) - TIE_FREE_JOBS:
            add_tie(c + TIE_LAG, _zero_after(res[ts - 1:ts, PROJ_COLS - CONV_COLS:PROJ_COLS]))
        if i in after_job:
            after_job[i]()
    for cvbuf in cvbufs:
        cvbuf[0:CV_HALO, :] = cvbuf[ts:ts + CV_HALO, :]

    mixer_a()
    cv = _layer_norm(cvout[...] + cvb_ref[...], cvg_ref[...], cvbeta_ref[...])
    ybuf[1] = (cv * jax.nn.sigmoid(cv)).astype(BF16)
    branch_z[1] = _dot(ybuf[1], w_square(WCOL_BRANCH + d))
    mixer_c()

    merged = None
    for n in range(3):
        gated = jax.nn.sigmoid(gl_buf[:, n * d:(n + 1) * d] + bg_ref[n:n + 1, :]) * branch_z[n]
        merged = gated if merged is None else merged + gated
    mg_buf[...] = merged.astype(BF16)
    xs_buf[...] = x


def _mixer_tail(step, mg_buf, xs_buf, modt_ref, w_out, ln1g_ref, ln1b_ref, wrt_ref, brc_ref,
                x1_ref, u2_ref, ri_all, rf_ref, base_ref):
    ts = MIX_ROWS
    d = D_MODEL
    modt = modt_ref[0]
    gate1, shift2, scale2 = modt[:, 2 * d:3 * d], modt[:, 3 * d:4 * d], modt[:, 4 * d:5 * d]
    hmix = _dot(mg_buf[...], w_out)
    x1 = _layer_norm(ALPHA * xs_buf[...] + gate1 * hmix, ln1g_ref[...], ln1b_ref[...])
    x1_ref[0] = x1
    u2 = x1 * (1.0 + scale2) + shift2
    u2_ref[0] = _pack_bf16_pairs(u2)
    yield

    u_hi = u2.astype(BF16)
    u_lo = (u2 - u_hi.astype(F32)).astype(BF16)
    w_r = wrt_ref[...]
    w_hi = w_r.astype(BF16)
    w_lo = (w_r - w_hi.astype(F32)).astype(BF16)
    nt = (((1,), (1,)), ((), ()))
    both = lax.dot_general(jnp.concatenate([w_hi, w_lo], axis=0), u_hi, nt,
                           preferred_element_type=F32)
    logits = (both[0:N_EXPERTS] + both[N_EXPERTS:2 * N_EXPERTS]
              + lax.dot_general(w_hi, u_lo, nt, preferred_element_type=F32))
    mx = jnp.max(logits, axis=0, keepdims=True)
    ex = jnp.exp(logits - mx)
    scores = ex / jnp.sum(ex, axis=0, keepdims=True)
    sel = scores + brc_ref[...]
    tops = []
    for g in range(N_GROUPS):
        rows = [sel[g * EXPERTS_PER_GROUP + k:g * EXPERTS_PER_GROUP + k + 1, :]
                for k in range(EXPERTS_PER_GROUP)]
        tops.append(_top2_of4(rows))
    best = tops[0][0] + tops[0][2]
    g_idx = jnp.zeros(best.shape, jnp.int32)
    loc1, loc2 = tops[0][1], tops[0][3]
    for g in range(1, N_GROUPS):
        gs = tops[g][0] + tops[g][2]
        gt = gs > best
        best = jnp.where(gt, gs, best)
        g_idx = jnp.where(gt, g, g_idx)
        loc1 = jnp.where(gt, tops[g][1], loc1)
        loc2 = jnp.where(gt, tops[g][3], loc2)
    e0 = g_idx * EXPERTS_PER_GROUP + loc1
    e1 = g_idx * EXPERTS_PER_GROUP + loc2
    erow = lax.broadcasted_iota(jnp.int32, (N_EXPERTS, ts), 0)
    is0 = erow == e0
    is1 = erow == e1
    s0 = jnp.sum(jnp.where(is0, scores, 0.0), axis=0, keepdims=True)
    s1 = jnp.sum(jnp.where(is1, scores, 0.0), axis=0, keepdims=True)
    ssum = s0 + s1
    yield

    onehot = jnp.logical_or(is0, is1).astype(BF16)
    src = lax.broadcasted_iota(jnp.int32, (ts, ts), 0)
    dst = lax.broadcasted_iota(jnp.int32, (ts, ts), 1)
    earlier = (src < dst).astype(BF16)
    prior = _dot(onehot, earlier) + base_ref[:, 0:1]
    r0 = jnp.sum(jnp.where(is0, prior, 0.0), axis=0, keepdims=True)
    r1 = jnp.sum(jnp.where(is1, prior, 0.0), axis=0, keepdims=True)
    counts = jnp.sum(onehot.astype(F32), axis=1, keepdims=True)
    base_ref[...] = base_ref[...] + jnp.where(step > 0, counts, 0.0)

    zi = jnp.zeros((SUBLANES - 4, ts), jnp.int32)
    ri_all[jnp.maximum(step - 1, 0)] = jnp.concatenate(
        [e0, e1, r0.astype(jnp.int32), r1.astype(jnp.int32), zi], axis=0)
    zf = jnp.zeros((SUBLANES - 2, ts), F32)
    rf_ref[...] = jnp.concatenate([s0 / ssum, s1 / ssum, zf], axis=0).T


def _mixer(layer, xsrc, mod3, mb0, bsz, w_all, scw, cvw, cvb, cvg, cvbeta, sgg, sgbeta, sgw, sgbt,
           bg, ln1g, ln1b, wrt, brc):
    _, seq, d = xsrc[0].shape
    ts = MIX_ROWS
    ns = seq // ts
    n_tok = bsz * seq

    n_tiles = bsz * ns

    def const(shape):
        zeros = (0,) * len(shape)
        return pl.BlockSpec(shape, lambda s: zeros, pipeline_mode=pl.Buffered(1))

    def first_half(s):
        return jnp.minimum(s, n_tiles - 1)

    def second_half(s):
        return jnp.maximum(s - 1, 0)

    def layer_weights(shape):
        block = (1,) + tuple(shape[1:])
        index = (layer,) + (0,) * (len(shape) - 1)
        return pl.BlockSpec(block, lambda s: index, pipeline_mode=pl.Buffered(1))

    if len(xsrc) == 2:
        x, xb0 = xsrc
        x_args = [x]
        x_specs = [pl.BlockSpec((1, ts, d),
                                lambda s: (first_half(s) // ns + xb0, first_half(s) % ns, 0))]
    else:
        x1p, pairs, gates, mod3p, g2, b2 = xsrc
        x_args = [x1p, pairs, pairs, gates, mod3p, g2, b2]
        x_specs = [
            pl.BlockSpec((1, ts, d), lambda s: (first_half(s) // ns, first_half(s) % ns, 0)),
            pl.BlockSpec((1, ts, d // 2), lambda s: (0, first_half(s), 0)),
            pl.BlockSpec((1, ts, d // 2), lambda s: (1, first_half(s), 0)),
            pl.BlockSpec((ts, SUBLANES), lambda s: (first_half(s), 0)),
            pl.BlockSpec((1, 1, 6 * d), lambda s: (first_half(s) // ns + mb0, 0, 0)),
            const(g2.shape), const(b2.shape),
        ]
    in_specs = x_specs + [
        pl.BlockSpec((1, 1, 6 * d), lambda s: (first_half(s) // ns + mb0, 0, 0)),
        pl.BlockSpec((1, 1, 6 * d), lambda s: (second_half(s) // ns + mb0, 0, 0)),
        layer_weights(w_all.shape), const(scw.shape), const(cvw.shape), const(cvb.shape),
        const(cvg.shape), const(cvbeta.shape), const(sgg.shape), const(sgbeta.shape),
        const(sgw.shape), const(sgbt.shape), const(bg.shape), const(ln1g.shape), const(ln1b.shape),
        const(wrt.shape), const(brc.shape),
    ]
    out_specs = [
        pl.BlockSpec((1, ts, d), lambda s: (second_half(s) // ns, second_half(s) % ns, 0)),
        pl.BlockSpec((1, ts, d // 2), lambda s: (second_half(s) // ns, second_half(s) % ns, 0)),
        pl.BlockSpec((SUBLANES, n_tok), lambda s: (0, 0)),
        pl.BlockSpec((ts, SUBLANES), lambda s: (second_half(s), 0)),
        pl.BlockSpec((SUBLANES, LANES), lambda s: (0, 0)),
    ]
    out_shape = [
        jax.ShapeDtypeStruct((bsz, seq, d), F32),
        jax.ShapeDtypeStruct((bsz, seq, d // 2), jnp.uint32),
        jax.ShapeDtypeStruct((SUBLANES, n_tok), jnp.int32),
        jax.ShapeDtypeStruct((n_tok, SUBLANES), F32),
        jax.ShapeDtypeStruct((SUBLANES, LANES), jnp.int32),
    ]
    return pl.pallas_call(
        functools.partial(_mixer_kernel, ns, len(x_args)),
        grid=(n_tiles + 1,),
        in_specs=in_specs,
        out_specs=out_specs,
        out_shape=out_shape,
        scratch_shapes=[
            pltpu.VMEM((SC_HALO + ts, W_BRANCH), F32),
            pltpu.VMEM((CV_HALO + ts, PROJ_COLS), F32),
            pltpu.VMEM((CV_HALO + ts, PROJ_COLS), F32),
            pltpu.VMEM((ts, W_BRANCH), F32),
            pltpu.VMEM((3, ts, W_BRANCH), BF16),
            pltpu.VMEM((ts, 3 * W_BRANCH), F32),
            pltpu.VMEM((ts, 2 * W_BRANCH), F32),
            pltpu.VMEM((ts, 3 * D_MODEL), F32),
            pltpu.VMEM((ts, d), BF16),
            pltpu.VMEM((ts, d), F32),
            pltpu.VMEM((n_tiles, SUBLANES, ts), jnp.int32),
            pltpu.VMEM((N_EXPERTS, LANES), F32),
        ],
        compiler_params=pltpu.CompilerParams(
            dimension_semantics=("arbitrary",),
            vmem_limit_bytes=MIXER_VMEM_LIMIT),
    )(*x_args, mod3, mod3, w_all, scw, cvw, cvb, cvg, cvbeta, sgg, sgbeta, sgw, sgbt, bg,
      ln1g, ln1b, wrt, brc)


def _sc_workers():
    info = plsc.get_sparse_core_info()
    return info.num_cores, info.num_cores * info.num_subcores


def _sc_scatter_rows(rows, dest_a, dest_b, n_out):
    n, d = rows.shape
    nc, nw = _sc_workers()
    per_w = n // nw
    n_win = per_w // SC_WINDOW
    ia = dest_a.reshape(nw, n_win, SC_WINDOW)
    ib = dest_b.reshape(nw, n_win, SC_WINDOW)
    mesh = plsc.VectorSubcoreMesh(core_axis_name="c", subcore_axis_name="s")

    @functools.partial(
        pl.kernel, mesh=mesh,
        out_type=jax.ShapeDtypeStruct((n_out, d), rows.dtype),
        scratch_types=[
            pltpu.VMEM((n_win, SC_WINDOW), jnp.int32),
            pltpu.VMEM((n_win, SC_WINDOW), jnp.int32),
            pltpu.VMEM((SC_WINDOW, d), rows.dtype),
        ],
    )
    def scatter(rows_hbm, ia_hbm, ib_hbm, out_hbm, ia_v, ib_v, rows_v):
        wid = lax.axis_index("s") * nc + lax.axis_index("c")
        pltpu.sync_copy(ia_hbm.at[wid], ia_v)
        pltpu.sync_copy(ib_hbm.at[wid], ib_v)
        base = wid * per_w

        @pl.loop(0, n_win)
        def _(j):
            pltpu.sync_copy(rows_hbm.at[pl.ds(base + j * SC_WINDOW, SC_WINDOW)], rows_v)
            pltpu.sync_copy(rows_v, out_hbm.at[ia_v.at[j]])
            pltpu.sync_copy(rows_v, out_hbm.at[ib_v.at[j]])

    return scatter(rows, ia, ib)


def _sc_gather_rows(table, idx):
    n = idx.shape[0]
    d = table.shape[1]
    nc, nw = _sc_workers()
    per_w = n // nw
    n_win = per_w // SC_WINDOW
    idx3 = idx.reshape(nw, n_win, SC_WINDOW)
    mesh = plsc.VectorSubcoreMesh(core_axis_name="c", subcore_axis_name="s")

    @functools.partial(
        pl.kernel, mesh=mesh,
        out_type=jax.ShapeDtypeStruct((n, d), table.dtype),
        scratch_types=[
            pltpu.VMEM((n_win, SC_WINDOW), jnp.int32),
            pltpu.VMEM((SC_WINDOW, d), table.dtype),
        ],
    )
    def gather(table_hbm, idx_hbm, out_hbm, idx_v, rows_v):
        wid = lax.axis_index("s") * nc + lax.axis_index("c")
        pltpu.sync_copy(idx_hbm.at[wid], idx_v)
        base = wid * per_w

        @pl.loop(0, n_win)
        def _(j):
            pltpu.sync_copy(table_hbm.at[idx_v.at[j]], rows_v)
            pltpu.sync_copy(rows_v, out_hbm.at[pl.ds(base + j * SC_WINDOW, SC_WINDOW)])

    return gather(table, idx3)


def _expert_kernel(layer, be_ref, nused_ref, x_ref, w1_hbm, w3_hbm, w2_hbm, o_ref,
                   w1f, w3f, w2f, w1s, w3s, w2s, sems, group_ref):
    i = pl.program_id(0)
    n_used = nused_ref[0]
    e_now = be_ref[i]

    def fetch(e, slot):
        return [pltpu.make_async_copy(w_hbm.at[layer, e], w_f.at[slot], sems.at[k, slot])
                for k, (w_hbm, w_f) in enumerate(((w1_hbm, w1f), (w3_hbm, w3f), (w2_hbm, w2f)))]

    @pl.when(i == 0)
    def _():
        group_ref[0] = 0
        for copy in fetch(e_now, 0):
            copy.start()

    first_of_group = jnp.logical_and(
        i < n_used, jnp.logical_or(i == 0, e_now != be_ref[jnp.maximum(i - 1, 0)]))

    @pl.when(first_of_group)
    def _():
        slot = group_ref[0] % 2
        for copy in fetch(e_now, slot):
            copy.wait()
        w1s[...] = w1f[slot].astype(BF16)
        w3s[...] = w3f[slot].astype(BF16)
        w2s[...] = w2f[slot].astype(BF16)
        last = be_ref.shape[0] - 1
        j = lax.while_loop(
            lambda j: jnp.logical_and(j < n_used, be_ref[jnp.minimum(j, last)] == e_now),
            lambda j: j + 1, i + 1)
        next_e = be_ref[jnp.minimum(j, last)]

        @pl.when(j < n_used)
        def _():
            for copy in fetch(next_e, 1 - slot):
                copy.start()

        group_ref[0] = group_ref[0] + 1

    @pl.when(i < n_used)
    def _():
        lo, hi = _unpack_bf16_pairs(x_ref[...])
        xb = jnp.concatenate([lo, hi], axis=1).astype(BF16)
        a = _dot(xb, w1s[...])
        b = _dot(xb, w3s[...])
        for r0 in range(0, MOE_ROWS, MOE_SLAB):
            rs = slice(r0, r0 + MOE_SLAB)
            h = a[rs] * jax.nn.sigmoid(a[rs]) * b[rs]
            o_ref[rs, :] = _pack_bf16_pairs(_dot(h.astype(BF16), w2s[...]))


def _experts(layer, buf, block_e, n_used, w1, w3, w2):
    n_rows, dp = buf.shape
    nb = n_rows // MOE_ROWS
    d, fe = w1.shape[-2:]

    def row_map(i, be, nu):
        return (jnp.minimum(i, nu[0] - 1), 0)

    grid_spec = pltpu.PrefetchScalarGridSpec(
        num_scalar_prefetch=2,
        grid=(nb,),
        in_specs=[
            pl.BlockSpec((MOE_ROWS, dp), row_map),
            pl.BlockSpec(memory_space=pl.ANY),
            pl.BlockSpec(memory_space=pl.ANY),
            pl.BlockSpec(memory_space=pl.ANY),
        ],
        out_specs=pl.BlockSpec((MOE_ROWS, dp), row_map),
        scratch_shapes=[
            pltpu.VMEM((2, d, fe), F32), pltpu.VMEM((2, d, fe), F32), pltpu.VMEM((2, fe, d), F32),
            pltpu.VMEM((d, fe), BF16), pltpu.VMEM((d, fe), BF16), pltpu.VMEM((fe, d), BF16),
            pltpu.SemaphoreType.DMA((3, 2)),
            pltpu.SMEM((1,), jnp.int32),
        ],
    )
    return pl.pallas_call(
        functools.partial(_expert_kernel, layer),
        grid_spec=grid_spec,
        out_shape=jax.ShapeDtypeStruct((n_rows, dp), jnp.uint32),
        compiler_params=pltpu.CompilerParams(dimension_semantics=("arbitrary",),
                                             vmem_limit_bytes=EXPERT_VMEM_LIMIT),
    )(block_e, n_used, buf, w1, w3, w2)


def _combine_kernel(x1_ref, ya_ref, yb_ref, gates_ref, mod_ref, g_ref, b_ref, *rest):
    o_ref = rest[-1]
    d = D_MODEL
    o_ref[0] = _moe_output(x1_ref[0], ya_ref[0], yb_ref[0], gates_ref[...],
                           mod_ref[0][:, 5 * d:6 * d], g_ref[...], b_ref[...])


def _combine(x1, pairs, gates, mod3, mb0, g, b, out_bsz, ob0, prev):
    bsz, seq, d = x1.shape
    ts = min(COMB_ROWS, seq)
    ns = seq // ts
    in_specs = [
        pl.BlockSpec((1, ts, d), lambda i, j: (i, j, 0)),
        pl.BlockSpec((1, ts, d // 2), lambda i, j: (0, i * ns + j, 0)),
        pl.BlockSpec((1, ts, d // 2), lambda i, j: (1, i * ns + j, 0)),
        pl.BlockSpec((ts, SUBLANES), lambda i, j: (i * ns + j, 0)),
        pl.BlockSpec((1, 1, 6 * d), lambda i, j: (i + mb0, 0, 0)),
        pl.BlockSpec((1, d), lambda i, j: (0, 0)),
        pl.BlockSpec((1, d), lambda i, j: (0, 0)),
    ]
    args = [x1, pairs, pairs, gates, mod3, g, b]
    aliases = {}
    if prev is not None:
        in_specs.append(pl.BlockSpec(memory_space=pl.ANY))
        aliases = {len(args): 0}
        args.append(prev)
    return pl.pallas_call(
        _combine_kernel,
        grid=(bsz, ns),
        in_specs=in_specs,
        out_specs=pl.BlockSpec((1, ts, d), lambda i, j: (i + ob0, j, 0)),
        out_shape=jax.ShapeDtypeStruct((out_bsz, seq, d), F32),
        input_output_aliases=aliases,
        compiler_params=pltpu.CompilerParams(dimension_semantics=("arbitrary", "arbitrary"),
                                             vmem_limit_bytes=MIXER_VMEM_LIMIT),
    )(*args)


def _expert_layout(counts):
    shift = MOE_ROWS.bit_length() - 1
    padded = lax.shift_left(
        lax.shift_right_logical(counts.astype(jnp.int32) + (MOE_ROWS - 1), shift), shift)
    e_out = lax.broadcasted_iota(jnp.int32, (N_EXPERTS, N_EXPERTS), 0)
    e_in = lax.broadcasted_iota(jnp.int32, (N_EXPERTS, N_EXPERTS), 1)
    upto = (e_in <= e_out).astype(BF16)
    padded_f = jnp.broadcast_to(padded.astype(F32), (N_EXPERTS, LANES))
    pad_end = _dot(upto, padded_f.astype(BF16))
    pad_start = pad_end[:, 0:1] - padded.astype(F32)
    block_start = (lax.broadcasted_iota(jnp.int32, (N_EXPERTS, LANES), 1) * MOE_ROWS).astype(F32)
    block_e = jnp.sum((block_start >= pad_end).astype(jnp.int32), axis=0, keepdims=True)
    n_used = lax.shift_right_logical(pad_end[N_EXPERTS - 1:N_EXPERTS, :].astype(jnp.int32), shift)
    blk = jnp.concatenate([jnp.minimum(block_e, N_EXPERTS - 1), n_used,
                           jnp.zeros((SUBLANES - 2, LANES), jnp.int32)], axis=0)
    return pad_start, blk


def _slots(ri, pad_start):
    n = ri.shape[1]
    erow = lax.broadcasted_iota(jnp.int32, (N_EXPERTS, n), 0)
    rows = []
    for k in range(TOP_K):
        start = jnp.sum(jnp.where(erow == ri[k:k + 1, :], pad_start, 0.0), axis=0, keepdims=True)
        rows.append(start.astype(jnp.int32) + ri[TOP_K + k:TOP_K + k + 1, :])
    rows.append(jnp.zeros((SUBLANES - TOP_K, n), jnp.int32))
    return jnp.concatenate(rows, axis=0)


def kernel(x, c, w_ada, b_ada, w_in, sc_conv, cv_conv, cv_conv_b, cv_ln_g, cv_ln_b, sg_ln_g, sg_ln_b, sg_w, sg_b, w_branch, w_gate, b_gate, w_o, ln1_g, ln1_b, w_router, b_router, w1, w3, w2, ln2_g, ln2_b):
    bsz, seq, d = x.shape
    mod3 = _ada(c, w_ada, b_ada).reshape(DEPTH, bsz, 1, 6 * d)
    wrt = w_router.T
    brc = b_router.reshape(N_EXPERTS, 1)
    assert W_BRANCH == d
    w_all = _pack_weights(w_in, jnp.transpose(w_gate, (0, 2, 1, 3)), w_branch, w_o)
    cb = bsz // N_CHAINS
    n_tok = cb * seq
    n_blocks = (n_tok * TOP_K + N_EXPERTS * (MOE_ROWS - 1) + MOE_ROWS - 1) // MOE_ROWS
    assert n_blocks <= LANES and MOE_ROWS & (MOE_ROWS - 1) == 0
    chains = [(x, h * cb) for h in range(N_CHAINS)]
    for l in range(DEPTH):
        last = l == DEPTH - 1
        result = None
        for h in range(N_CHAINS):
            x1, u2, dest, rf, blk = _mixer(
                l, chains[h], mod3[l], h * cb, cb, w_all, sc_conv[l], cv_conv[l],
                cv_conv_b[l].reshape(1, -1), cv_ln_g[l].reshape(1, -1), cv_ln_b[l].reshape(1, -1),
                sg_ln_g[l].reshape(1, -1), sg_ln_b[l].reshape(1, -1), sg_w[l], sg_b[l].T,
                b_gate[l], ln1_g[l].reshape(1, -1), ln1_b[l].reshape(1, -1), wrt, brc)
            buf = _sc_scatter_rows(u2.reshape(n_tok, d // 2), dest[0], dest[1], n_blocks * MOE_ROWS)
            obuf = _experts(l, buf, blk[0, 0:n_blocks], blk[1, 0:1], w1, w3, w2)
            pair_idx = dest[0:TOP_K].reshape(TOP_K * n_tok)
            pairs = _sc_gather_rows(obuf, pair_idx).reshape(TOP_K, n_tok, d // 2)
            gates = rf
            g2, b2 = ln2_g[l].reshape(1, -1), ln2_b[l].reshape(1, -1)
            if last:
                result = _combine(x1, pairs, gates, mod3[l], h * cb, g2, b2, bsz, h * cb, result)
            else:
                chains[h] = (x1, pairs, gates, mod3[l], g2, b2)
    return result
```

```python
import functools

import jax
import jax.numpy as jnp
from jax import lax
from jax.experimental import pallas as pl
from jax.experimental.pallas import tpu as pltpu
from jax.experimental.pallas import tpu_sc as plsc

D_MODEL = 1024
DEPTH = 2
W_BRANCH = 1024
SC_KERNEL = 3
CV_KERNEL = 31
CHUNK = 128
SG_HEADS = 8
N_EXPERTS = 16
N_GROUPS = 4
EXPERTS_PER_GROUP = N_EXPERTS // N_GROUPS
TOP_K = 2
D_EXPERT = 512
ALPHA = (2.0 * DEPTH) ** 0.25
LN_EPS = 1e-5

F32 = jnp.float32
BF16 = jnp.bfloat16

V7X_VMEM_BYTES = 64 * 1024 * 1024
MIXER_VMEM_LIMIT = V7X_VMEM_BYTES - 6 * 1024 * 1024
EXPERT_VMEM_LIMIT = V7X_VMEM_BYTES // 2
SUBLANES = 8
LANES = 128

MIX_ROWS = 256
SC_HALO = SUBLANES
CV_HALO = 32
CONV_ROWS = 128
TIE_LAG = 2
TIE_FREE_JOBS = 4
CONV_COLS = 128
PROJ_COLS = 512
PACK_ROWS = 128
WCOL_GATE = 7 * W_BRANCH
WCOL_BRANCH = WCOL_GATE + 3 * D_MODEL
WCOL_OUT = WCOL_BRANCH + 3 * D_MODEL
MOE_ROWS = 512
MOE_SLAB = 256
COMB_ROWS = 1024
SC_WINDOW = 32
N_CHAINS = 2


def _dot(a, b):
    return jnp.dot(a, b, preferred_element_type=F32)


def _pack_bf16_pairs(v):
    m = v.shape[1] // 2
    lo = lax.bitcast_convert_type(v[:, 0:m].astype(BF16).astype(F32), jnp.uint32)
    hi = lax.bitcast_convert_type(v[:, m:2 * m].astype(BF16).astype(F32), jnp.uint32)
    return jnp.bitwise_or(jnp.bitwise_and(hi, jnp.uint32(0xFFFF0000)),
                          lax.shift_right_logical(lo, jnp.uint32(16)))


def _unpack_bf16_pairs(w):
    lo = lax.bitcast_convert_type(lax.shift_left(w, jnp.uint32(16)), F32)
    hi = lax.bitcast_convert_type(jnp.bitwise_and(w, jnp.uint32(0xFFFF0000)), F32)
    return lo, hi


def _layer_norm(v, g, b):
    mu = jnp.mean(v, axis=-1, keepdims=True)
    vc = v - mu
    var = jnp.mean(vc * vc, axis=-1, keepdims=True)
    return vc * lax.rsqrt(var + LN_EPS) * g + b


def _ada_kernel(c_ref, w_ref, b_ref, o_ref):
    c = c_ref[...]
    c_act = c * jax.nn.sigmoid(c)
    n = c.shape[0]
    w = w_ref[0]
    c_hi = c_act.astype(BF16)
    c_lo = (c_act - c_hi.astype(F32)).astype(BF16)
    w_hi = w.astype(BF16)
    w_lo = (w - w_hi.astype(F32)).astype(BF16)
    both = _dot(jnp.concatenate([c_hi, c_lo], axis=0), w_hi)
    o_ref[0] = both[0:n] + both[n:2 * n] + _dot(c_hi, w_lo) + b_ref[0]


def _ada(c, w_ada, b_ada):
    bsz, d = c.shape
    n = w_ada.shape[-1]
    tn = 1536
    return pl.pallas_call(
        _ada_kernel,
        grid=(DEPTH, n // tn),
        in_specs=[
            pl.BlockSpec((bsz, d), lambda l, j: (0, 0)),
            pl.BlockSpec((1, d, tn), lambda l, j: (l, 0, j)),
            pl.BlockSpec((1, 1, tn), lambda l, j: (l, 0, j)),
        ],
        out_specs=pl.BlockSpec((1, bsz, tn), lambda l, j: (l, 0, j)),
        out_shape=jax.ShapeDtypeStruct((DEPTH, bsz, n), F32),
    )(c, w_ada, b_ada.reshape(DEPTH, 1, n))


def _top2_of4(rows):
    m1 = rows[0]
    i1 = jnp.zeros(rows[0].shape, jnp.int32)
    for k in range(1, 4):
        gt = rows[k] > m1
        m1 = jnp.where(gt, rows[k], m1)
        i1 = jnp.where(gt, k, i1)
    m2 = jnp.full(rows[0].shape, -jnp.inf, F32)
    i2 = jnp.zeros(rows[0].shape, jnp.int32)
    for k in range(4):
        cand = jnp.where(i1 == k, -jnp.inf, rows[k])
        gt = cand > m2
        m2 = jnp.where(gt, cand, m2)
        i2 = jnp.where(gt, k, i2)
    return m1, i1, m2, i2


def _zero_after(v):
    u = lax.bitcast_convert_type(v, jnp.uint32)
    u = lax.shift_right_logical(lax.shift_right_logical(u, jnp.uint32(16)), jnp.uint32(16))
    return lax.bitcast_convert_type(u, F32)


def _conv31_chunk(cvw_ref, cvbufs, cvout, r0, c0, tie):
    cs = slice(c0, c0 + CONV_COLS)
    cvbuf = cvbufs[c0 // PROJ_COLS]
    bs = slice(c0 % PROJ_COLS, c0 % PROJ_COLS + CONV_COLS)
    acc = None
    for r in range(SUBLANES):
        lead = SUBLANES if r else 0
        part = None
        for m in range((CV_KERNEL - 1 - r) // SUBLANES + 1):
            k = CV_KERNEL - 1 - (SUBLANES * m + r)
            start = CV_HALO + r0 - lead - SUBLANES * m
            w_row = cvw_ref[k:k + 1, cs]
            if tie is not None and acc is None and part is None:
                w_row = w_row + tie
            term = w_row * cvbuf[start:start + lead + CONV_ROWS, bs]
            part = term if part is None else part + term
        part = part[lead - r:lead - r + CONV_ROWS]
        acc = part if acc is None else acc + part
    cvout[r0:r0 + CONV_ROWS, cs] = acc


def _bf16_weights(packed):
    return pltpu.bitcast(packed, BF16)


def _pack_weights_kernel(*refs):
    o_ref = refs[-1]
    c0 = 0
    for w_ref in refs[:-1]:
        w = w_ref[0] if len(w_ref.shape) == 3 else w_ref[0, 0]
        o_ref[0, :, c0:c0 + w.shape[1]] = pltpu.bitcast(w.astype(BF16), jnp.uint32)
        c0 += w.shape[1]


def _pack_weights(w_in, w_gate_t, w_branch, w_o):
    depth, k, n_in = w_in.shape
    d = w_o.shape[-1]
    assert w_branch.shape[2] == k and w_gate_t.shape[2] == k and w_o.shape[1] == k
    kb = PACK_ROWS
    per_branch = [pl.BlockSpec((1, 1, kb, d), functools.partial(lambda n, l, j: (l, n, j, 0), n))
                  for n in range(3)]
    return pl.pallas_call(
        _pack_weights_kernel,
        grid=(depth, k // kb),
        in_specs=([pl.BlockSpec((1, kb, n_in), lambda l, j: (l, j, 0))] + per_branch + per_branch
                  + [pl.BlockSpec((1, kb, d), lambda l, j: (l, j, 0))]),
        out_specs=pl.BlockSpec((1, kb // 2, n_in + 7 * d), lambda l, j: (l, j, 0)),
        out_shape=jax.ShapeDtypeStruct((depth, k // 2, n_in + 7 * d), jnp.uint32),
        compiler_params=pltpu.CompilerParams(vmem_limit_bytes=MIXER_VMEM_LIMIT),
    )(w_in, w_gate_t, w_gate_t, w_gate_t, w_branch, w_branch, w_branch, w_o)


def _moe_output(x1, ya_packed, yb_packed, gates, gate2, g, b):
    ya = jnp.concatenate(_unpack_bf16_pairs(ya_packed), axis=1)
    yb = jnp.concatenate(_unpack_bf16_pairs(yb_packed), axis=1)
    h = gates[:, 0:1] * ya + gates[:, 1:2] * yb
    return _layer_norm(ALPHA * x1 + gate2 * h, g, b)


def _mixer_kernel(tiles_per_seq, n_x_refs, *refs):
    last_step = pl.num_programs(0) - 1

    @pl.when(pl.program_id(0) < last_step)
    def _():
        _mixer_step(tiles_per_seq, n_x_refs, True, *refs)

    @pl.when(pl.program_id(0) == last_step)
    def _():
        _mixer_step(tiles_per_seq, n_x_refs, False, *refs)


def _mixer_step(tiles_per_seq, n_x_refs, run_first_half, *refs):
    x_refs = refs[:n_x_refs]
    (mod_ref, modt_ref, w_ref, scw_ref, cvw_ref, cvb_ref, cvg_ref, cvbeta_ref, sgg_ref,
     sgbeta_ref, sgw_ref, sgbt_ref, bg_ref, ln1g_ref, ln1b_ref, wrt_ref, brc_ref,
     x1_ref, u2_ref, dest_ref, rf_ref, blk_ref,
     qbuf, cvbuf0, cvbuf1, cvout, ybuf, pa_buf, pc_buf, gl_buf, mg_buf, xs_buf, ri_all,
     base_ref) = refs[n_x_refs:]
    cvbufs = (cvbuf0, cvbuf1)
    ts = MIX_ROWS
    d = D_MODEL
    wb = W_BRANCH
    step = pl.program_id(0)
    tile = jnp.minimum(step, pl.num_programs(0) - 2)
    first_tile = tile % tiles_per_seq == 0

    @pl.when(step == 0)
    def _():
        base_ref[...] = jnp.zeros_like(base_ref)
        mg_buf[...] = jnp.zeros_like(mg_buf)
        xs_buf[...] = jnp.zeros_like(xs_buf)

    def w_in(c0):
        return _bf16_weights(w_ref[0, :, c0:c0 + PROJ_COLS])

    def w_gate(c0):
        return w_in(WCOL_GATE + c0)

    def w_square(c0):
        return _bf16_weights(w_ref[0, :, c0:c0 + d])

    if not run_first_half:
        for _ in _mixer_tail(step, mg_buf, xs_buf, modt_ref, w_square(WCOL_OUT), ln1g_ref, ln1b_ref, wrt_ref,
                             brc_ref, x1_ref, u2_ref, ri_all, rf_ref, base_ref):
            pass
        pad_start, blk_ref[...] = _expert_layout(base_ref[:, 0:1])
        for t in range(ri_all.shape[0]):
            dest_ref[:, t * ts:(t + 1) * ts] = _slots(ri_all[t], pad_start)
        return

    @pl.when(first_tile)
    def _():
        qbuf[0:SC_HALO, :] = jnp.zeros((SC_HALO, wb), F32)
        for cvbuf in cvbufs:
            cvbuf[0:CV_HALO, :] = jnp.zeros((CV_HALO, PROJ_COLS), F32)

    if n_x_refs == 1:
        x = x_refs[0][0]
    else:
        x1p_ref, ya_ref, yb_ref, gates_ref, modp_ref, g2_ref, b2_ref = x_refs
        x = _moe_output(x1p_ref[0], ya_ref[0], yb_ref[0], gates_ref[...],
                        modp_ref[0][:, 5 * d:6 * d], g2_ref[...], b2_ref[...])
    mod = mod_ref[0]
    shift1, scale1 = mod[:, 0:d], mod[:, d:2 * d]
    ub = (x * (1.0 + scale1) + shift1).astype(BF16)

    def glu_block(c0):
        a = _dot(ub, w_in(3 * wb + c0))
        g = _dot(ub, w_in(4 * wb + c0))
        cvbufs[c0 // PROJ_COLS][CV_HALO:CV_HALO + ts, :] = a * jax.nn.sigmoid(g)

    tail = _mixer_tail(step, mg_buf, xs_buf, modt_ref, w_square(WCOL_OUT), ln1g_ref, ln1b_ref, wrt_ref,
                       brc_ref, x1_ref, u2_ref, ri_all, rf_ref, base_ref)
    next(tail)
    glu_block(0)
    ties = {}

    def add_tie(chunk, tie):
        ties[chunk] = tie + ties[chunk] if chunk in ties else tie

    branch_z = {}

    def mixer_a():
        qbuf[SC_HALO:SC_HALO + ts, :] = pa_buf[:, wb:2 * wb] * pa_buf[:, 2 * wb:3 * wb]
        conv = scw_ref[SC_KERNEL - 1:SC_KERNEL, :] * qbuf[SC_HALO:SC_HALO + ts, :]
        for k in range(SC_KERNEL - 1):
            off = SC_HALO - (SC_KERNEL - 1) + k
            conv = conv + scw_ref[k:k + 1, :] * qbuf[off:off + ts, :]
        ybuf[0] = (pa_buf[:, 0:wb] * conv).astype(BF16)
        qbuf[0:SC_HALO, :] = qbuf[ts:ts + SC_HALO, :]
        branch_z[0] = _dot(ybuf[0], w_square(WCOL_BRANCH))

    def mixer_c():
        gu = jax.nn.gelu(pc_buf[:, 0:wb])
        gv = _layer_norm(jax.nn.gelu(pc_buf[:, wb:2 * wb]), sgg_ref[...],
                         sgbeta_ref[...]).astype(BF16)
        row = lax.broadcasted_iota(jnp.int32, (CHUNK, CHUNK), 0)
        col = lax.broadcasted_iota(jnp.int32, (CHUNK, CHUNK), 1)
        hd = wb // SG_HEADS
        for h in range(SG_HEADS):
            wm = jnp.where(row >= col, sgw_ref[h], 0.0).astype(BF16)
            bias = sgbt_ref[:, h:h + 1]
            for n in range(ts // CHUNK):
                rs = slice(n * CHUNK, (n + 1) * CHUNK)
                cs = slice(h * hd, (h + 1) * hd)
                mixed = _dot(wm, gv[rs, cs]) + bias
                ybuf[2, rs, cs] = (gu[rs, cs] * mixed).astype(BF16)
        branch_z[2] = _dot(ybuf[2], w_square(WCOL_BRANCH + 2 * d))

    def glu_rest():
        for c0 in range(PROJ_COLS, wb, PROJ_COLS):
            glu_block(c0)

    def tail_stage():
        next(tail, None)

    mxu_jobs = ([(pa_buf, w_in, c0, c0) for c0 in range(0, 3 * wb, PROJ_COLS)]
                + [(pc_buf, w_in, c0, 5 * wb + c0) for c0 in range(0, 2 * wb, PROJ_COLS)]
                + [(gl_buf, w_gate, c0, c0) for c0 in range(0, 3 * d, PROJ_COLS)])
    glu_rest()
    after_job = {7: tail_stage, 12: tail_stage}
    conv_jobs = [(r0, c0) for c0 in range(0, wb, CONV_COLS) for r0 in range(0, ts, CONV_ROWS)]
    chunks_per_job = len(conv_jobs) // len(mxu_jobs)
    for c, conv_job in enumerate(conv_jobs):
        _conv31_chunk(cvw_ref, cvbufs, cvout, *conv_job, ties.get(c))
        if (c + 1) % chunks_per_job:
            continue
        i = c // chunks_per_job
        dst, weights, dc, wc = mxu_jobs[i]
        res = _dot(ub, weights(wc))
        dst[:, dc:dc + PROJ_COLS] = res
        if i < len(mxu_jobs) - TIE_FREE_JOBS:
            add_tie(c + TIE_LAG, _zero_after(res[ts - 1:ts, PROJ_COLS - CONV_COLS:PROJ_COLS]))
        if i in after_job:
            after_job[i]()
    for cvbuf in cvbufs:
        cvbuf[0:CV_HALO, :] = cvbuf[ts:ts + CV_HALO, :]

    mixer_a()
    cv = _layer_norm(cvout[...] + cvb_ref[...], cvg_ref[...], cvbeta_ref[...])
    ybuf[1] = (cv * jax.nn.sigmoid(cv)).astype(BF16)
    branch_z[1] = _dot(ybuf[1], w_square(WCOL_BRANCH + d))
    mixer_c()

    merged = None
    for n in range(3):
        gated = jax.nn.sigmoid(gl_buf[:, n * d:(n + 1) * d] + bg_ref[n:n + 1, :]) * branch_z[n]
        merged = gated if merged is None else merged + gated
    mg_buf[...] = merged.astype(BF16)
    xs_buf[...] = x


def _mixer_tail(step, mg_buf, xs_buf, modt_ref, w_out, ln1g_ref, ln1b_ref, wrt_ref, brc_ref,
                x1_ref, u2_ref, ri_all, rf_ref, base_ref):
    ts = MIX_ROWS
    d = D_MODEL
    modt = modt_ref[0]
    gate1, shift2, scale2 = modt[:, 2 * d:3 * d], modt[:, 3 * d:4 * d], modt[:, 4 * d:5 * d]
    hmix = _dot(mg_buf[...], w_out)
    x1 = _layer_norm(ALPHA * xs_buf[...] + gate1 * hmix, ln1g_ref[...], ln1b_ref[...])
    x1_ref[0] = x1
    u2 = x1 * (1.0 + scale2) + shift2
    u2_ref[0] = _pack_bf16_pairs(u2)
    yield

    u_hi = u2.astype(BF16)
    u_lo = (u2 - u_hi.astype(F32)).astype(BF16)
    w_r = wrt_ref[...]
    w_hi = w_r.astype(BF16)
    w_lo = (w_r - w_hi.astype(F32)).astype(BF16)
    nt = (((1,), (1,)), ((), ()))
    both = lax.dot_general(jnp.concatenate([w_hi, w_lo], axis=0), u_hi, nt,
                           preferred_element_type=F32)
    logits = (both[0:N_EXPERTS] + both[N_EXPERTS:2 * N_EXPERTS]
              + lax.dot_general(w_hi, u_lo, nt, preferred_element_type=F32))
    mx = jnp.max(logits, axis=0, keepdims=True)
    ex = jnp.exp(logits - mx)
    scores = ex / jnp.sum(ex, axis=0, keepdims=True)
    sel = scores + brc_ref[...]
    tops = []
    for g in range(N_GROUPS):
        rows = [sel[g * EXPERTS_PER_GROUP + k:g * EXPERTS_PER_GROUP + k + 1, :]
                for k in range(EXPERTS_PER_GROUP)]
        tops.append(_top2_of4(rows))
    best = tops[0][0] + tops[0][2]
    g_idx = jnp.zeros(best.shape, jnp.int32)
    loc1, loc2 = tops[0][1], tops[0][3]
    for g in range(1, N_GROUPS):
        gs = tops[g][0] + tops[g][2]
        gt = gs > best
        best = jnp.where(gt, gs, best)
        g_idx = jnp.where(gt, g, g_idx)
        loc1 = jnp.where(gt, tops[g][1], loc1)
        loc2 = jnp.where(gt, tops[g][3], loc2)
    e0 = g_idx * EXPERTS_PER_GROUP + loc1
    e1 = g_idx * EXPERTS_PER_GROUP + loc2
    erow = lax.broadcasted_iota(jnp.int32, (N_EXPERTS, ts), 0)
    is0 = erow == e0
    is1 = erow == e1
    s0 = jnp.sum(jnp.where(is0, scores, 0.0), axis=0, keepdims=True)
    s1 = jnp.sum(jnp.where(is1, scores, 0.0), axis=0, keepdims=True)
    ssum = s0 + s1
    yield

    onehot = jnp.logical_or(is0, is1).astype(BF16)
    src = lax.broadcasted_iota(jnp.int32, (ts, ts), 0)
    dst = lax.broadcasted_iota(jnp.int32, (ts, ts), 1)
    earlier = (src < dst).astype(BF16)
    prior = _dot(onehot, earlier) + base_ref[:, 0:1]
    r0 = jnp.sum(jnp.where(is0, prior, 0.0), axis=0, keepdims=True)
    r1 = jnp.sum(jnp.where(is1, prior, 0.0), axis=0, keepdims=True)
    counts = jnp.sum(onehot.astype(F32), axis=1, keepdims=True)
    base_ref[...] = base_ref[...] + jnp.where(step > 0, counts, 0.0)

    zi = jnp.zeros((SUBLANES - 4, ts), jnp.int32)
    ri_all[jnp.maximum(step - 1, 0)] = jnp.concatenate(
        [e0, e1, r0.astype(jnp.int32), r1.astype(jnp.int32), zi], axis=0)
    zf = jnp.zeros((SUBLANES - 2, ts), F32)
    rf_ref[...] = jnp.concatenate([s0 / ssum, s1 / ssum, zf], axis=0).T


def _mixer(layer, xsrc, mod3, mb0, bsz, w_all, scw, cvw, cvb, cvg, cvbeta, sgg, sgbeta, sgw, sgbt,
           bg, ln1g, ln1b, wrt, brc):
    _, seq, d = xsrc[0].shape
    ts = MIX_ROWS
    ns = seq // ts
    n_tok = bsz * seq

    n_tiles = bsz * ns

    def const(shape):
        zeros = (0,) * len(shape)
        return pl.BlockSpec(shape, lambda s: zeros, pipeline_mode=pl.Buffered(1))

    def first_half(s):
        return jnp.minimum(s, n_tiles - 1)

    def second_half(s):
        return jnp.maximum(s - 1, 0)

    def layer_weights(shape):
        block = (1,) + tuple(shape[1:])
        index = (layer,) + (0,) * (len(shape) - 1)
        return pl.BlockSpec(block, lambda s: index, pipeline_mode=pl.Buffered(1))

    def layer_const(shape, lyr=layer):
        index = (lyr,) + (0,) * (len(shape) - 1)
        return pl.BlockSpec((None,) + tuple(shape[1:]), lambda s: index,
                            pipeline_mode=pl.Buffered(1))

    def mod_spec(lyr, tile):
        return pl.BlockSpec((None, 1, 1, 6 * d), lambda s: (lyr, tile(s) // ns + mb0, 0, 0))

    if len(xsrc) == 2:
        x, xb0 = xsrc
        x_args = [x]
        x_specs = [pl.BlockSpec((1, ts, d),
                                lambda s: (first_half(s) // ns + xb0, first_half(s) % ns, 0))]
    else:
        x1p, pairs, gates, g2, b2 = xsrc
        x_args = [x1p, pairs, pairs, gates, mod3, g2, b2]
        x_specs = [
            pl.BlockSpec((1, ts, d), lambda s: (first_half(s) // ns, first_half(s) % ns, 0)),
            pl.BlockSpec((1, ts, d // 2), lambda s: (0, first_half(s), 0)),
            pl.BlockSpec((1, ts, d // 2), lambda s: (1, first_half(s), 0)),
            pl.BlockSpec((ts, SUBLANES), lambda s: (first_half(s), 0)),
            mod_spec(layer - 1, first_half),
            layer_const(g2.shape, layer - 1), layer_const(b2.shape, layer - 1),
        ]
    in_specs = x_specs + [
        mod_spec(layer, first_half), mod_spec(layer, second_half),
        layer_weights(w_all.shape), layer_const(scw.shape), layer_const(cvw.shape),
        layer_const(cvb.shape), layer_const(cvg.shape), layer_const(cvbeta.shape),
        layer_const(sgg.shape), layer_const(sgbeta.shape), layer_const(sgw.shape),
        layer_const(sgbt.shape), layer_const(bg.shape), layer_const(ln1g.shape),
        layer_const(ln1b.shape), const(wrt.shape), const(brc.shape),
    ]
    out_specs = [
        pl.BlockSpec((1, ts, d), lambda s: (second_half(s) // ns, second_half(s) % ns, 0)),
        pl.BlockSpec((1, ts, d // 2), lambda s: (second_half(s) // ns, second_half(s) % ns, 0)),
        pl.BlockSpec((SUBLANES, n_tok), lambda s: (0, 0)),
        pl.BlockSpec((ts, SUBLANES), lambda s: (second_half(s), 0)),
        pl.BlockSpec((SUBLANES, LANES), lambda s: (0, 0)),
    ]
    out_shape = [
        jax.ShapeDtypeStruct((bsz, seq, d), F32),
        jax.ShapeDtypeStruct((bsz, seq, d // 2), jnp.uint32),
        jax.ShapeDtypeStruct((SUBLANES, n_tok), jnp.int32),
        jax.ShapeDtypeStruct((n_tok, SUBLANES), F32),
        jax.ShapeDtypeStruct((SUBLANES, LANES), jnp.int32),
    ]
    return pl.pallas_call(
        functools.partial(_mixer_kernel, ns, len(x_args)),
        grid=(n_tiles + 1,),
        in_specs=in_specs,
        out_specs=out_specs,
        out_shape=out_shape,
        scratch_shapes=[
            pltpu.VMEM((SC_HALO + ts, W_BRANCH), F32),
            pltpu.VMEM((CV_HALO + ts, PROJ_COLS), F32),
            pltpu.VMEM((CV_HALO + ts, PROJ_COLS), F32),
            pltpu.VMEM((ts, W_BRANCH), F32),
            pltpu.VMEM((3, ts, W_BRANCH), BF16),
            pltpu.VMEM((ts, 3 * W_BRANCH), F32),
            pltpu.VMEM((ts, 2 * W_BRANCH), F32),
            pltpu.VMEM((ts, 3 * D_MODEL), F32),
            pltpu.VMEM((ts, d), BF16),
            pltpu.VMEM((ts, d), F32),
            pltpu.VMEM((n_tiles, SUBLANES, ts), jnp.int32),
            pltpu.VMEM((N_EXPERTS, LANES), F32),
        ],
        compiler_params=pltpu.CompilerParams(
            dimension_semantics=("arbitrary",),
            vmem_limit_bytes=MIXER_VMEM_LIMIT),
    )(*x_args, mod3, mod3, w_all, scw, cvw, cvb, cvg, cvbeta, sgg, sgbeta, sgw, sgbt, bg,
      ln1g, ln1b, wrt, brc)


def _sc_workers():
    info = plsc.get_sparse_core_info()
    return info.num_cores, info.num_cores * info.num_subcores


def _sc_scatter_rows(rows, dest_a, dest_b, n_out):
    n, d = rows.shape
    nc, nw = _sc_workers()
    per_w = n // nw
    n_win = per_w // SC_WINDOW
    ia = dest_a.reshape(nw, n_win, SC_WINDOW)
    ib = dest_b.reshape(nw, n_win, SC_WINDOW)
    mesh = plsc.VectorSubcoreMesh(core_axis_name="c", subcore_axis_name="s")

    @functools.partial(
        pl.kernel, mesh=mesh,
        out_type=jax.ShapeDtypeStruct((n_out, d), rows.dtype),
        scratch_types=[
            pltpu.VMEM((n_win, SC_WINDOW), jnp.int32),
            pltpu.VMEM((n_win, SC_WINDOW), jnp.int32),
            pltpu.VMEM((SC_WINDOW, d), rows.dtype),
        ],
    )
    def scatter(rows_hbm, ia_hbm, ib_hbm, out_hbm, ia_v, ib_v, rows_v):
        wid = lax.axis_index("s") * nc + lax.axis_index("c")
        pltpu.sync_copy(ia_hbm.at[wid], ia_v)
        pltpu.sync_copy(ib_hbm.at[wid], ib_v)
        base = wid * per_w

        @pl.loop(0, n_win)
        def _(j):
            pltpu.sync_copy(rows_hbm.at[pl.ds(base + j * SC_WINDOW, SC_WINDOW)], rows_v)
            pltpu.sync_copy(rows_v, out_hbm.at[ia_v.at[j]])
            pltpu.sync_copy(rows_v, out_hbm.at[ib_v.at[j]])

    return scatter(rows, ia, ib)


def _sc_gather_rows(table, idx):
    n = idx.shape[0]
    d = table.shape[1]
    nc, nw = _sc_workers()
    per_w = n // nw
    n_win = per_w // SC_WINDOW
    idx3 = idx.reshape(nw, n_win, SC_WINDOW)
    mesh = plsc.VectorSubcoreMesh(core_axis_name="c", subcore_axis_name="s")

    @functools.partial(
        pl.kernel, mesh=mesh,
        out_type=jax.ShapeDtypeStruct((n, d), table.dtype),
        scratch_types=[
            pltpu.VMEM((n_win, SC_WINDOW), jnp.int32),
            pltpu.VMEM((SC_WINDOW, d), table.dtype),
        ],
    )
    def gather(table_hbm, idx_hbm, out_hbm, idx_v, rows_v):
        wid = lax.axis_index("s") * nc + lax.axis_index("c")
        pltpu.sync_copy(idx_hbm.at[wid], idx_v)
        base = wid * per_w

        @pl.loop(0, n_win)
        def _(j):
            pltpu.sync_copy(table_hbm.at[idx_v.at[j]], rows_v)
            pltpu.sync_copy(rows_v, out_hbm.at[pl.ds(base + j * SC_WINDOW, SC_WINDOW)])

    return gather(table, idx3)


def _expert_kernel(layer, be_ref, nused_ref, x_ref, w1_hbm, w3_hbm, w2_hbm, o_ref,
                   w1f, w3f, w2f, w1s, w3s, w2s, sems, group_ref):
    i = pl.program_id(0)
    n_used = nused_ref[0]
    e_now = be_ref[i]

    def fetch(e, slot):
        return [pltpu.make_async_copy(w_hbm.at[layer, e], w_f.at[slot], sems.at[k, slot])
                for k, (w_hbm, w_f) in enumerate(((w1_hbm, w1f), (w3_hbm, w3f), (w2_hbm, w2f)))]

    @pl.when(i == 0)
    def _():
        group_ref[0] = 0
        for copy in fetch(e_now, 0):
            copy.start()

    first_of_group = jnp.logical_and(
        i < n_used, jnp.logical_or(i == 0, e_now != be_ref[jnp.maximum(i - 1, 0)]))

    @pl.when(first_of_group)
    def _():
        slot = group_ref[0] % 2
        for copy in fetch(e_now, slot):
            copy.wait()
        w1s[...] = w1f[slot].astype(BF16)
        w3s[...] = w3f[slot].astype(BF16)
        w2s[...] = w2f[slot].astype(BF16)
        last = be_ref.shape[0] - 1
        j = lax.while_loop(
            lambda j: jnp.logical_and(j < n_used, be_ref[jnp.minimum(j, last)] == e_now),
            lambda j: j + 1, i + 1)
        next_e = be_ref[jnp.minimum(j, last)]

        @pl.when(j < n_used)
        def _():
            for copy in fetch(next_e, 1 - slot):
                copy.start()

        group_ref[0] = group_ref[0] + 1

    @pl.when(i < n_used)
    def _():
        lo, hi = _unpack_bf16_pairs(x_ref[...])
        xb = jnp.concatenate([lo, hi], axis=1).astype(BF16)
        a = _dot(xb, w1s[...])
        b = _dot(xb, w3s[...])
        for r0 in range(0, MOE_ROWS, MOE_SLAB):
            rs = slice(r0, r0 + MOE_SLAB)
            h = a[rs] * jax.nn.sigmoid(a[rs]) * b[rs]
            o_ref[rs, :] = _pack_bf16_pairs(_dot(h.astype(BF16), w2s[...]))


def _experts(layer, buf, block_e, n_used, w1, w3, w2):
    n_rows, dp = buf.shape
    nb = n_rows // MOE_ROWS
    d, fe = w1.shape[-2:]

    def row_map(i, be, nu):
        return (jnp.minimum(i, nu[0] - 1), 0)

    grid_spec = pltpu.PrefetchScalarGridSpec(
        num_scalar_prefetch=2,
        grid=(nb,),
        in_specs=[
            pl.BlockSpec((MOE_ROWS, dp), row_map),
            pl.BlockSpec(memory_space=pl.ANY),
            pl.BlockSpec(memory_space=pl.ANY),
            pl.BlockSpec(memory_space=pl.ANY),
        ],
        out_specs=pl.BlockSpec((MOE_ROWS, dp), row_map),
        scratch_shapes=[
            pltpu.VMEM((2, d, fe), F32), pltpu.VMEM((2, d, fe), F32), pltpu.VMEM((2, fe, d), F32),
            pltpu.VMEM((d, fe), BF16), pltpu.VMEM((d, fe), BF16), pltpu.VMEM((fe, d), BF16),
            pltpu.SemaphoreType.DMA((3, 2)),
            pltpu.SMEM((1,), jnp.int32),
        ],
    )
    return pl.pallas_call(
        functools.partial(_expert_kernel, layer),
        grid_spec=grid_spec,
        out_shape=jax.ShapeDtypeStruct((n_rows, dp), jnp.uint32),
        compiler_params=pltpu.CompilerParams(dimension_semantics=("arbitrary",),
                                             vmem_limit_bytes=EXPERT_VMEM_LIMIT),
    )(block_e, n_used, buf, w1, w3, w2)


def _combine_kernel(x1_ref, ya_ref, yb_ref, gates_ref, mod_ref, g_ref, b_ref, *rest):
    o_ref = rest[-1]
    d = D_MODEL
    o_ref[0] = _moe_output(x1_ref[0], ya_ref[0], yb_ref[0], gates_ref[...],
                           mod_ref[0][:, 5 * d:6 * d], g_ref[...], b_ref[...])


def _combine(x1, pairs, gates, mod3, mb0, g, b, out_bsz, ob0, prev):
    bsz, seq, d = x1.shape
    ts = min(COMB_ROWS, seq)
    ns = seq // ts
    in_specs = [
        pl.BlockSpec((1, ts, d), lambda i, j: (i, j, 0)),
        pl.BlockSpec((1, ts, d // 2), lambda i, j: (0, i * ns + j, 0)),
        pl.BlockSpec((1, ts, d // 2), lambda i, j: (1, i * ns + j, 0)),
        pl.BlockSpec((ts, SUBLANES), lambda i, j: (i * ns + j, 0)),
        pl.BlockSpec((1, 1, 6 * d), lambda i, j: (i + mb0, 0, 0)),
        pl.BlockSpec((1, d), lambda i, j: (0, 0)),
        pl.BlockSpec((1, d), lambda i, j: (0, 0)),
    ]
    args = [x1, pairs, pairs, gates, mod3, g, b]
    aliases = {}
    if prev is not None:
        in_specs.append(pl.BlockSpec(memory_space=pl.ANY))
        aliases = {len(args): 0}
        args.append(prev)
    return pl.pallas_call(
        _combine_kernel,
        grid=(bsz, ns),
        in_specs=in_specs,
        out_specs=pl.BlockSpec((1, ts, d), lambda i, j: (i + ob0, j, 0)),
        out_shape=jax.ShapeDtypeStruct((out_bsz, seq, d), F32),
        input_output_aliases=aliases,
        compiler_params=pltpu.CompilerParams(dimension_semantics=("arbitrary", "arbitrary"),
                                             vmem_limit_bytes=MIXER_VMEM_LIMIT),
    )(*args)


def _expert_layout(counts):
    shift = MOE_ROWS.bit_length() - 1
    padded = lax.shift_left(
        lax.shift_right_logical(counts.astype(jnp.int32) + (MOE_ROWS - 1), shift), shift)
    e_out = lax.broadcasted_iota(jnp.int32, (N_EXPERTS, N_EXPERTS), 0)
    e_in = lax.broadcasted_iota(jnp.int32, (N_EXPERTS, N_EXPERTS), 1)
    upto = (e_in <= e_out).astype(BF16)
    padded_f = jnp.broadcast_to(padded.astype(F32), (N_EXPERTS, LANES))
    pad_end = _dot(upto, padded_f.astype(BF16))
    pad_start = pad_end[:, 0:1] - padded.astype(F32)
    block_start = (lax.broadcasted_iota(jnp.int32, (N_EXPERTS, LANES), 1) * MOE_ROWS).astype(F32)
    block_e = jnp.sum((block_start >= pad_end).astype(jnp.int32), axis=0, keepdims=True)
    n_used = lax.shift_right_logical(pad_end[N_EXPERTS - 1:N_EXPERTS, :].astype(jnp.int32), shift)
    blk = jnp.concatenate([jnp.minimum(block_e, N_EXPERTS - 1), n_used,
                           jnp.zeros((SUBLANES - 2, LANES), jnp.int32)], axis=0)
    return pad_start, blk


def _slots(ri, pad_start):
    n = ri.shape[1]
    erow = lax.broadcasted_iota(jnp.int32, (N_EXPERTS, n), 0)
    rows = []
    for k in range(TOP_K):
        start = jnp.sum(jnp.where(erow == ri[k:k + 1, :], pad_start, 0.0), axis=0, keepdims=True)
        rows.append(start.astype(jnp.int32) + ri[TOP_K + k:TOP_K + k + 1, :])
    rows.append(jnp.zeros((SUBLANES - TOP_K, n), jnp.int32))
    return jnp.concatenate(rows, axis=0)


def kernel(x, c, w_ada, b_ada, w_in, sc_conv, cv_conv, cv_conv_b, cv_ln_g, cv_ln_b, sg_ln_g, sg_ln_b, sg_w, sg_b, w_branch, w_gate, b_gate, w_o, ln1_g, ln1_b, w_router, b_router, w1, w3, w2, ln2_g, ln2_b):
    bsz, seq, d = x.shape
    mod3 = _ada(c, w_ada, b_ada).reshape(DEPTH, bsz, 1, 6 * d)
    wrt = w_router.T
    brc = b_router.reshape(N_EXPERTS, 1)
    assert W_BRANCH == d
    w_all = _pack_weights(w_in, jnp.transpose(w_gate, (0, 2, 1, 3)), w_branch, w_o)
    cb = bsz // N_CHAINS
    n_tok = cb * seq
    n_blocks = (n_tok * TOP_K + N_EXPERTS * (MOE_ROWS - 1) + MOE_ROWS - 1) // MOE_ROWS
    assert n_blocks <= LANES and MOE_ROWS & (MOE_ROWS - 1) == 0
    chains = [(x, h * cb) for h in range(N_CHAINS)]

    def rows(p):
        return p.reshape(p.shape[0], 1, p.shape[1])

    layer_params = (sc_conv, cv_conv, rows(cv_conv_b), rows(cv_ln_g), rows(cv_ln_b), rows(sg_ln_g),
                    rows(sg_ln_b), sg_w, jnp.transpose(sg_b, (0, 2, 1)), b_gate, rows(ln1_g),
                    rows(ln1_b))
    g2_all, b2_all = rows(ln2_g), rows(ln2_b)
    for l in range(DEPTH):
        last = l == DEPTH - 1
        result = None
        for h in range(N_CHAINS):
            x1, u2, dest, rf, blk = _mixer(l, chains[h], mod3, h * cb, cb, w_all, *layer_params,
                                           wrt, brc)
            buf = _sc_scatter_rows(u2.reshape(n_tok, d // 2), dest[0], dest[1], n_blocks * MOE_ROWS)
            obuf = _experts(l, buf, blk[0, 0:n_blocks], blk[1, 0:1], w1, w3, w2)
            pair_idx = dest[0:TOP_K].reshape(TOP_K * n_tok)
            pairs = _sc_gather_rows(obuf, pair_idx).reshape(TOP_K, n_tok, d // 2)
            gates = rf
            g2, b2 = ln2_g[l].reshape(1, -1), ln2_b[l].reshape(1, -1)
            if last:
                result = _combine(x1, pairs, gates, mod3[l], h * cb, g2, b2, bsz, h * cb, result)
            else:
                chains[h] = (x1, pairs, gates, g2_all, b2_all)
    return result
```

```python
import functools

import jax
import jax.numpy as jnp
from jax import lax
from jax.experimental import pallas as pl
from jax.experimental.pallas import tpu as pltpu
from jax.experimental.pallas import tpu_sc as plsc

D_MODEL = 1024
DEPTH = 2
W_BRANCH = 1024
SC_KERNEL = 3
CV_KERNEL = 31
CHUNK = 128
SG_HEADS = 8
N_EXPERTS = 16
N_GROUPS = 4
EXPERTS_PER_GROUP = N_EXPERTS // N_GROUPS
TOP_K = 2
D_EXPERT = 512
ALPHA = (2.0 * DEPTH) ** 0.25
LN_EPS = 1e-5

F32 = jnp.float32
BF16 = jnp.bfloat16

V7X_VMEM_BYTES = 64 * 1024 * 1024
MIXER_VMEM_LIMIT = V7X_VMEM_BYTES - 6 * 1024 * 1024
EXPERT_VMEM_LIMIT = V7X_VMEM_BYTES // 2
SUBLANES = 8
LANES = 128

MIX_ROWS = 256
SC_HALO = SUBLANES
CV_HALO = 32
CONV_ROWS = 128
TIE_LAG = 2
TIE_FREE_JOBS = 4
CONV_COLS = 128
PROJ_COLS = 512
PACK_ROWS = 128
WCOL_GATE = 7 * W_BRANCH
WCOL_BRANCH = WCOL_GATE + 3 * D_MODEL
WCOL_OUT = WCOL_BRANCH + 3 * D_MODEL
MOE_ROWS = 512
MOE_SLAB = 256
COMB_ROWS = 1024
SC_WINDOW = 32
N_CHAINS = 2


def _dot(a, b):
    return jnp.dot(a, b, preferred_element_type=F32)


def _pack_bf16_pairs(v):
    m = v.shape[1] // 2
    lo = lax.bitcast_convert_type(v[:, 0:m].astype(BF16).astype(F32), jnp.uint32)
    hi = lax.bitcast_convert_type(v[:, m:2 * m].astype(BF16).astype(F32), jnp.uint32)
    return jnp.bitwise_or(jnp.bitwise_and(hi, jnp.uint32(0xFFFF0000)),
                          lax.shift_right_logical(lo, jnp.uint32(16)))


def _unpack_bf16_pairs(w):
    lo = lax.bitcast_convert_type(lax.shift_left(w, jnp.uint32(16)), F32)
    hi = lax.bitcast_convert_type(jnp.bitwise_and(w, jnp.uint32(0xFFFF0000)), F32)
    return lo, hi


def _layer_norm(v, g, b):
    mu = jnp.mean(v, axis=-1, keepdims=True)
    vc = v - mu
    var = jnp.mean(vc * vc, axis=-1, keepdims=True)
    return vc * lax.rsqrt(var + LN_EPS) * g + b


def _ada_kernel(c_ref, w_ref, b_ref, o_ref):
    c = c_ref[...]
    c_act = c * jax.nn.sigmoid(c)
    n = c.shape[0]
    w = w_ref[0]
    c_hi = c_act.astype(BF16)
    c_lo = (c_act - c_hi.astype(F32)).astype(BF16)
    w_hi = w.astype(BF16)
    w_lo = (w - w_hi.astype(F32)).astype(BF16)
    both = _dot(jnp.concatenate([c_hi, c_lo], axis=0), w_hi)
    o_ref[0] = both[0:n] + both[n:2 * n] + _dot(c_hi, w_lo) + b_ref[0]


def _ada(c, w_ada, b_ada):
    bsz, d = c.shape
    n = w_ada.shape[-1]
    tn = 1536
    return pl.pallas_call(
        _ada_kernel,
        grid=(DEPTH, n // tn),
        in_specs=[
            pl.BlockSpec((bsz, d), lambda l, j: (0, 0)),
            pl.BlockSpec((1, d, tn), lambda l, j: (l, 0, j)),
            pl.BlockSpec((1, 1, tn), lambda l, j: (l, 0, j)),
        ],
        out_specs=pl.BlockSpec((1, bsz, tn), lambda l, j: (l, 0, j)),
        out_shape=jax.ShapeDtypeStruct((DEPTH, bsz, n), F32),
    )(c, w_ada, b_ada.reshape(DEPTH, 1, n))


def _top2_of4(rows):
    m1 = rows[0]
    i1 = jnp.zeros(rows[0].shape, jnp.int32)
    for k in range(1, 4):
        gt = rows[k] > m1
        m1 = jnp.where(gt, rows[k], m1)
        i1 = jnp.where(gt, k, i1)
    m2 = jnp.full(rows[0].shape, -jnp.inf, F32)
    i2 = jnp.zeros(rows[0].shape, jnp.int32)
    for k in range(4):
        cand = jnp.where(i1 == k, -jnp.inf, rows[k])
        gt = cand > m2
        m2 = jnp.where(gt, cand, m2)
        i2 = jnp.where(gt, k, i2)
    return m1, i1, m2, i2


def _zero_after(v):
    u = lax.bitcast_convert_type(v, jnp.uint32)
    u = lax.shift_right_logical(lax.shift_right_logical(u, jnp.uint32(16)), jnp.uint32(16))
    return lax.bitcast_convert_type(u, F32)


def _conv31_chunk(cvw_ref, cvbufs, cvout, r0, c0, tie):
    cs = slice(c0, c0 + CONV_COLS)
    cvbuf = cvbufs[c0 // PROJ_COLS]
    bs = slice(c0 % PROJ_COLS, c0 % PROJ_COLS + CONV_COLS)
    acc = None
    for r in range(SUBLANES):
        lead = SUBLANES if r else 0
        part = None
        for m in range((CV_KERNEL - 1 - r) // SUBLANES + 1):
            k = CV_KERNEL - 1 - (SUBLANES * m + r)
            start = CV_HALO + r0 - lead - SUBLANES * m
            w_row = cvw_ref[k:k + 1, cs]
            if tie is not None and acc is None and part is None:
                w_row = w_row + tie
            term = w_row * cvbuf[start:start + lead + CONV_ROWS, bs]
            part = term if part is None else part + term
        part = part[lead - r:lead - r + CONV_ROWS]
        acc = part if acc is None else acc + part
    cvout[r0:r0 + CONV_ROWS, cs] = acc


def _bf16_weights(packed):
    return pltpu.bitcast(packed, BF16)


def _pack_weights_kernel(*refs):
    o_ref = refs[-1]
    c0 = 0
    for w_ref in refs[:-1]:
        w = w_ref[0] if len(w_ref.shape) == 3 else w_ref[0, 0]
        o_ref[0, :, c0:c0 + w.shape[1]] = pltpu.bitcast(w.astype(BF16), jnp.uint32)
        c0 += w.shape[1]


def _pack_weights(w_in, w_gate_t, w_branch, w_o):
    depth, k, n_in = w_in.shape
    d = w_o.shape[-1]
    assert w_branch.shape[2] == k and w_gate_t.shape[2] == k and w_o.shape[1] == k
    kb = PACK_ROWS
    per_branch = [pl.BlockSpec((1, 1, kb, d), functools.partial(lambda n, l, j: (l, n, j, 0), n))
                  for n in range(3)]
    return pl.pallas_call(
        _pack_weights_kernel,
        grid=(depth, k // kb),
        in_specs=([pl.BlockSpec((1, kb, n_in), lambda l, j: (l, j, 0))] + per_branch + per_branch
                  + [pl.BlockSpec((1, kb, d), lambda l, j: (l, j, 0))]),
        out_specs=pl.BlockSpec((1, kb // 2, n_in + 7 * d), lambda l, j: (l, j, 0)),
        out_shape=jax.ShapeDtypeStruct((depth, k // 2, n_in + 7 * d), jnp.uint32),
        compiler_params=pltpu.CompilerParams(vmem_limit_bytes=MIXER_VMEM_LIMIT),
    )(w_in, w_gate_t, w_gate_t, w_gate_t, w_branch, w_branch, w_branch, w_o)


def _moe_output(x1, ya_packed, yb_packed, gates, gate2, g, b):
    ya = jnp.concatenate(_unpack_bf16_pairs(ya_packed), axis=1)
    yb = jnp.concatenate(_unpack_bf16_pairs(yb_packed), axis=1)
    h = gates[:, 0:1] * ya + gates[:, 1:2] * yb
    return _layer_norm(ALPHA * x1 + gate2 * h, g, b)


def _mixer_kernel(tiles_per_seq, n_x_refs, *refs):
    last_step = pl.num_programs(0) - 1

    @pl.when(pl.program_id(0) < last_step)
    def _():
        _mixer_step(tiles_per_seq, n_x_refs, True, *refs)

    @pl.when(pl.program_id(0) == last_step)
    def _():
        _mixer_step(tiles_per_seq, n_x_refs, False, *refs)


def _mixer_step(tiles_per_seq, n_x_refs, run_first_half, *refs):
    x_refs = refs[:n_x_refs]
    (mod_ref, modt_ref, w_ref, scw_ref, cvw_ref, cvb_ref, cvg_ref, cvbeta_ref, sgg_ref,
     sgbeta_ref, sgw_ref, sgbt_ref, bg_ref, ln1g_ref, ln1b_ref, wrt_ref, brc_ref,
     x1_ref, u2_ref, dest_ref, rf_ref, blk_ref,
     qbuf, cvbuf0, cvbuf1, cvout, ybuf, pa_buf, pc_buf, gl_buf, mg_buf, xs_buf, ri_all,
     base_ref) = refs[n_x_refs:]
    cvbufs = (cvbuf0, cvbuf1)
    ts = MIX_ROWS
    d = D_MODEL
    wb = W_BRANCH
    step = pl.program_id(0)
    tile = jnp.minimum(step, pl.num_programs(0) - 2)
    first_tile = tile % tiles_per_seq == 0

    @pl.when(step == 0)
    def _():
        base_ref[...] = jnp.zeros_like(base_ref)
        mg_buf[...] = jnp.zeros_like(mg_buf)
        xs_buf[...] = jnp.zeros_like(xs_buf)

    def w_in(c0):
        return _bf16_weights(w_ref[0, :, c0:c0 + PROJ_COLS])

    def w_gate(c0):
        return w_in(WCOL_GATE + c0)

    def w_square(c0):
        return _bf16_weights(w_ref[0, :, c0:c0 + d])

    if not run_first_half:
        for _ in _mixer_tail(step, mg_buf, xs_buf, modt_ref, w_square(WCOL_OUT), ln1g_ref, ln1b_ref, wrt_ref,
                             brc_ref, x1_ref, u2_ref, ri_all, rf_ref, base_ref):
            pass
        pad_start, blk_ref[...] = _expert_layout(base_ref[:, 0:1])
        for t in range(ri_all.shape[0]):
            dest_ref[:, t * ts:(t + 1) * ts] = _slots(ri_all[t], pad_start)
        return

    @pl.when(first_tile)
    def _():
        qbuf[0:SC_HALO, :] = jnp.zeros((SC_HALO, wb), F32)
        for cvbuf in cvbufs:
            cvbuf[0:CV_HALO, :] = jnp.zeros((CV_HALO, PROJ_COLS), F32)

    if n_x_refs == 1:
        x = x_refs[0][0]
    else:
        x1p_ref, ya_ref, yb_ref, gates_ref, modp_ref, g2_ref, b2_ref = x_refs
        x = _moe_output(x1p_ref[0], ya_ref[0], yb_ref[0], gates_ref[...],
                        modp_ref[0][:, 5 * d:6 * d], g2_ref[...], b2_ref[...])
    mod = mod_ref[0]
    shift1, scale1 = mod[:, 0:d], mod[:, d:2 * d]
    ub = (x * (1.0 + scale1) + shift1).astype(BF16)

    def glu_block(c0):
        a = _dot(ub, w_in(3 * wb + c0))
        g = _dot(ub, w_in(4 * wb + c0))
        cvbufs[c0 // PROJ_COLS][CV_HALO:CV_HALO + ts, :] = a * jax.nn.sigmoid(g)

    tail = _mixer_tail(step, mg_buf, xs_buf, modt_ref, w_square(WCOL_OUT), ln1g_ref, ln1b_ref, wrt_ref,
                       brc_ref, x1_ref, u2_ref, ri_all, rf_ref, base_ref)
    next(tail)
    glu_block(0)
    ties = {}

    def add_tie(chunk, tie):
        ties[chunk] = tie + ties[chunk] if chunk in ties else tie

    branch_z = {}

    def mixer_a():
        qbuf[SC_HALO:SC_HALO + ts, :] = pa_buf[:, wb:2 * wb] * pa_buf[:, 2 * wb:3 * wb]
        conv = scw_ref[SC_KERNEL - 1:SC_KERNEL, :] * qbuf[SC_HALO:SC_HALO + ts, :]
        for k in range(SC_KERNEL - 1):
            off = SC_HALO - (SC_KERNEL - 1) + k
            conv = conv + scw_ref[k:k + 1, :] * qbuf[off:off + ts, :]
        ybuf[0] = (pa_buf[:, 0:wb] * conv).astype(BF16)
        qbuf[0:SC_HALO, :] = qbuf[ts:ts + SC_HALO, :]
        branch_z[0] = _dot(ybuf[0], w_square(WCOL_BRANCH))

    def mixer_c():
        gu = jax.nn.gelu(pc_buf[:, 0:wb])
        gv = _layer_norm(jax.nn.gelu(pc_buf[:, wb:2 * wb]), sgg_ref[...],
                         sgbeta_ref[...]).astype(BF16)
        row = lax.broadcasted_iota(jnp.int32, (CHUNK, CHUNK), 0)
        col = lax.broadcasted_iota(jnp.int32, (CHUNK, CHUNK), 1)
        hd = wb // SG_HEADS
        for h in range(SG_HEADS):
            wm = jnp.where(row >= col, sgw_ref[h], 0.0).astype(BF16)
            bias = sgbt_ref[:, h:h + 1]
            for n in range(ts // CHUNK):
                rs = slice(n * CHUNK, (n + 1) * CHUNK)
                cs = slice(h * hd, (h + 1) * hd)
                mixed = _dot(wm, gv[rs, cs]) + bias
                ybuf[2, rs, cs] = (gu[rs, cs] * mixed).astype(BF16)
        branch_z[2] = _dot(ybuf[2], w_square(WCOL_BRANCH + 2 * d))

    def glu_rest():
        for c0 in range(PROJ_COLS, wb, PROJ_COLS):
            glu_block(c0)

    def tail_stage():
        next(tail, None)

    mxu_jobs = ([(pa_buf, w_in, c0, c0) for c0 in range(0, 3 * wb, PROJ_COLS)]
                + [(pc_buf, w_in, c0, 5 * wb + c0) for c0 in range(0, 2 * wb, PROJ_COLS)]
                + [(gl_buf, w_gate, c0, c0) for c0 in range(0, 3 * d, PROJ_COLS)])
    glu_rest()
    after_job = {9: tail_stage, 13: tail_stage}
    conv_jobs = [(r0, c0) for c0 in range(0, wb, CONV_COLS) for r0 in range(0, ts, CONV_ROWS)]
    chunks_per_job = len(conv_jobs) // len(mxu_jobs)
    for c, conv_job in enumerate(conv_jobs):
        _conv31_chunk(cvw_ref, cvbufs, cvout, *conv_job, ties.get(c))
        if (c + 1) % chunks_per_job:
            continue
        i = c // chunks_per_job
        dst, weights, dc, wc = mxu_jobs[i]
        res = _dot(ub, weights(wc))
        dst[:, dc:dc + PROJ_COLS] = res
        if i < len(mxu_jobs) - TIE_FREE_JOBS:
            add_tie(c + TIE_LAG, _zero_after(res[ts - 1:ts, PROJ_COLS - CONV_COLS:PROJ_COLS]))
        if i in after_job:
            after_job[i]()
    for cvbuf in cvbufs:
        cvbuf[0:CV_HALO, :] = cvbuf[ts:ts + CV_HALO, :]

    mixer_a()
    cv = _layer_norm(cvout[...] + cvb_ref[...], cvg_ref[...], cvbeta_ref[...])
    ybuf[1] = (cv * jax.nn.sigmoid(cv)).astype(BF16)
    branch_z[1] = _dot(ybuf[1], w_square(WCOL_BRANCH + d))
    mixer_c()

    merged = None
    for n in range(3):
        gated = jax.nn.sigmoid(gl_buf[:, n * d:(n + 1) * d] + bg_ref[n:n + 1, :]) * branch_z[n]
        merged = gated if merged is None else merged + gated
    mg_buf[...] = merged.astype(BF16)
    xs_buf[...] = x


def _mixer_tail(step, mg_buf, xs_buf, modt_ref, w_out, ln1g_ref, ln1b_ref, wrt_ref, brc_ref,
                x1_ref, u2_ref, ri_all, rf_ref, base_ref):
    ts = MIX_ROWS
    d = D_MODEL
    modt = modt_ref[0]
    gate1, shift2, scale2 = modt[:, 2 * d:3 * d], modt[:, 3 * d:4 * d], modt[:, 4 * d:5 * d]
    hmix = _dot(mg_buf[...], w_out)
    x1 = _layer_norm(ALPHA * xs_buf[...] + gate1 * hmix, ln1g_ref[...], ln1b_ref[...])
    x1_ref[0] = x1
    u2 = x1 * (1.0 + scale2) + shift2
    u2_ref[0] = _pack_bf16_pairs(u2)
    yield

    u_hi = u2.astype(BF16)
    u_lo = (u2 - u_hi.astype(F32)).astype(BF16)
    w_r = wrt_ref[...]
    w_hi = w_r.astype(BF16)
    w_lo = (w_r - w_hi.astype(F32)).astype(BF16)
    nt = (((1,), (1,)), ((), ()))
    both = lax.dot_general(jnp.concatenate([w_hi, w_lo], axis=0), u_hi, nt,
                           preferred_element_type=F32)
    logits = (both[0:N_EXPERTS] + both[N_EXPERTS:2 * N_EXPERTS]
              + lax.dot_general(w_hi, u_lo, nt, preferred_element_type=F32))
    mx = jnp.max(logits, axis=0, keepdims=True)
    ex = jnp.exp(logits - mx)
    scores = ex / jnp.sum(ex, axis=0, keepdims=True)
    sel = scores + brc_ref[...]
    tops = []
    for g in range(N_GROUPS):
        rows = [sel[g * EXPERTS_PER_GROUP + k:g * EXPERTS_PER_GROUP + k + 1, :]
                for k in range(EXPERTS_PER_GROUP)]
        tops.append(_top2_of4(rows))
    best = tops[0][0] + tops[0][2]
    g_idx = jnp.zeros(best.shape, jnp.int32)
    loc1, loc2 = tops[0][1], tops[0][3]
    for g in range(1, N_GROUPS):
        gs = tops[g][0] + tops[g][2]
        gt = gs > best
        best = jnp.where(gt, gs, best)
        g_idx = jnp.where(gt, g, g_idx)
        loc1 = jnp.where(gt, tops[g][1], loc1)
        loc2 = jnp.where(gt, tops[g][3], loc2)
    e0 = g_idx * EXPERTS_PER_GROUP + loc1
    e1 = g_idx * EXPERTS_PER_GROUP + loc2
    erow = lax.broadcasted_iota(jnp.int32, (N_EXPERTS, ts), 0)
    is0 = erow == e0
    is1 = erow == e1
    s0 = jnp.sum(jnp.where(is0, scores, 0.0), axis=0, keepdims=True)
    s1 = jnp.sum(jnp.where(is1, scores, 0.0), axis=0, keepdims=True)
    ssum = s0 + s1
    yield

    onehot = jnp.logical_or(is0, is1).astype(BF16)
    src = lax.broadcasted_iota(jnp.int32, (ts, ts), 0)
    dst = lax.broadcasted_iota(jnp.int32, (ts, ts), 1)
    earlier = (src < dst).astype(BF16)
    prior = _dot(onehot, earlier) + base_ref[:, 0:1]
    r0 = jnp.sum(jnp.where(is0, prior, 0.0), axis=0, keepdims=True)
    r1 = jnp.sum(jnp.where(is1, prior, 0.0), axis=0, keepdims=True)
    counts = jnp.sum(onehot.astype(F32), axis=1, keepdims=True)
    base_ref[...] = base_ref[...] + jnp.where(step > 0, counts, 0.0)

    zi = jnp.zeros((SUBLANES - 4, ts), jnp.int32)
    ri_all[jnp.maximum(step - 1, 0)] = jnp.concatenate(
        [e0, e1, r0.astype(jnp.int32), r1.astype(jnp.int32), zi], axis=0)
    zf = jnp.zeros((SUBLANES - 2, ts), F32)
    rf_ref[...] = jnp.concatenate([s0 / ssum, s1 / ssum, zf], axis=0).T


def _mixer(layer, xsrc, mod3, mb0, bsz, w_all, scw, cvw, cvb, cvg, cvbeta, sgg, sgbeta, sgw, sgbt,
           bg, ln1g, ln1b, wrt, brc):
    _, seq, d = xsrc[0].shape
    ts = MIX_ROWS
    ns = seq // ts
    n_tok = bsz * seq

    n_tiles = bsz * ns

    def const(shape):
        zeros = (0,) * len(shape)
        return pl.BlockSpec(shape, lambda s: zeros, pipeline_mode=pl.Buffered(1))

    def first_half(s):
        return jnp.minimum(s, n_tiles - 1)

    def second_half(s):
        return jnp.maximum(s - 1, 0)

    def layer_weights(shape):
        block = (1,) + tuple(shape[1:])
        index = (layer,) + (0,) * (len(shape) - 1)
        return pl.BlockSpec(block, lambda s: index, pipeline_mode=pl.Buffered(1))

    if len(xsrc) == 2:
        x, xb0 = xsrc
        x_args = [x]
        x_specs = [pl.BlockSpec((1, ts, d),
                                lambda s: (first_half(s) // ns + xb0, first_half(s) % ns, 0))]
    else:
        x1p, pairs, gates, mod3p, g2, b2 = xsrc
        x_args = [x1p, pairs, pairs, gates, mod3p, g2, b2]
        x_specs = [
            pl.BlockSpec((1, ts, d), lambda s: (first_half(s) // ns, first_half(s) % ns, 0)),
            pl.BlockSpec((1, ts, d // 2), lambda s: (0, first_half(s), 0)),
            pl.BlockSpec((1, ts, d // 2), lambda s: (1, first_half(s), 0)),
            pl.BlockSpec((ts, SUBLANES), lambda s: (first_half(s), 0)),
            pl.BlockSpec((1, 1, 6 * d), lambda s: (first_half(s) // ns + mb0, 0, 0)),
            const(g2.shape), const(b2.shape),
        ]
    in_specs = x_specs + [
        pl.BlockSpec((1, 1, 6 * d), lambda s: (first_half(s) // ns + mb0, 0, 0)),
        pl.BlockSpec((1, 1, 6 * d), lambda s: (second_half(s) // ns + mb0, 0, 0)),
        layer_weights(w_all.shape), const(scw.shape), const(cvw.shape), const(cvb.shape),
        const(cvg.shape), const(cvbeta.shape), const(sgg.shape), const(sgbeta.shape),
        const(sgw.shape), const(sgbt.shape), const(bg.shape), const(ln1g.shape), const(ln1b.shape),
        const(wrt.shape), const(brc.shape),
    ]
    out_specs = [
        pl.BlockSpec((1, ts, d), lambda s: (second_half(s) // ns, second_half(s) % ns, 0)),
        pl.BlockSpec((1, ts, d // 2), lambda s: (second_half(s) // ns, second_half(s) % ns, 0)),
        pl.BlockSpec((SUBLANES, n_tok), lambda s: (0, 0)),
        pl.BlockSpec((ts, SUBLANES), lambda s: (second_half(s), 0)),
        pl.BlockSpec((SUBLANES, LANES), lambda s: (0, 0)),
    ]
    out_shape = [
        jax.ShapeDtypeStruct((bsz, seq, d), F32),
        jax.ShapeDtypeStruct((bsz, seq, d // 2), jnp.uint32),
        jax.ShapeDtypeStruct((SUBLANES, n_tok), jnp.int32),
        jax.ShapeDtypeStruct((n_tok, SUBLANES), F32),
        jax.ShapeDtypeStruct((SUBLANES, LANES), jnp.int32),
    ]
    return pl.pallas_call(
        functools.partial(_mixer_kernel, ns, len(x_args)),
        grid=(n_tiles + 1,),
        in_specs=in_specs,
        out_specs=out_specs,
        out_shape=out_shape,
        scratch_shapes=[
            pltpu.VMEM((SC_HALO + ts, W_BRANCH), F32),
            pltpu.VMEM((CV_HALO + ts, PROJ_COLS), F32),
            pltpu.VMEM((CV_HALO + ts, PROJ_COLS), F32),
            pltpu.VMEM((ts, W_BRANCH), F32),
            pltpu.VMEM((3, ts, W_BRANCH), BF16),
            pltpu.VMEM((ts, 3 * W_BRANCH), F32),
            pltpu.VMEM((ts, 2 * W_BRANCH), F32),
            pltpu.VMEM((ts, 3 * D_MODEL), F32),
            pltpu.VMEM((ts, d), BF16),
            pltpu.VMEM((ts, d), F32),
            pltpu.VMEM((n_tiles, SUBLANES, ts), jnp.int32),
            pltpu.VMEM((N_EXPERTS, LANES), F32),
        ],
        compiler_params=pltpu.CompilerParams(
            dimension_semantics=("arbitrary",),
            vmem_limit_bytes=MIXER_VMEM_LIMIT),
    )(*x_args, mod3, mod3, w_all, scw, cvw, cvb, cvg, cvbeta, sgg, sgbeta, sgw, sgbt, bg,
      ln1g, ln1b, wrt, brc)


def _sc_workers():
    info = plsc.get_sparse_core_info()
    return info.num_cores, info.num_cores * info.num_subcores


def _sc_scatter_rows(rows, dest_a, dest_b, n_out):
    n, d = rows.shape
    nc, nw = _sc_workers()
    per_w = n // nw
    n_win = per_w // SC_WINDOW
    ia = dest_a.reshape(nw, n_win, SC_WINDOW)
    ib = dest_b.reshape(nw, n_win, SC_WINDOW)
    mesh = plsc.VectorSubcoreMesh(core_axis_name="c", subcore_axis_name="s")

    @functools.partial(
        pl.kernel, mesh=mesh,
        out_type=jax.ShapeDtypeStruct((n_out, d), rows.dtype),
        scratch_types=[
            pltpu.VMEM((n_win, SC_WINDOW), jnp.int32),
            pltpu.VMEM((n_win, SC_WINDOW), jnp.int32),
            pltpu.VMEM((SC_WINDOW, d), rows.dtype),
        ],
    )
    def scatter(rows_hbm, ia_hbm, ib_hbm, out_hbm, ia_v, ib_v, rows_v):
        wid = lax.axis_index("s") * nc + lax.axis_index("c")
        pltpu.sync_copy(ia_hbm.at[wid], ia_v)
        pltpu.sync_copy(ib_hbm.at[wid], ib_v)
        base = wid * per_w

        @pl.loop(0, n_win)
        def _(j):
            pltpu.sync_copy(rows_hbm.at[pl.ds(base + j * SC_WINDOW, SC_WINDOW)], rows_v)
            pltpu.sync_copy(rows_v, out_hbm.at[ia_v.at[j]])
            pltpu.sync_copy(rows_v, out_hbm.at[ib_v.at[j]])

    return scatter(rows, ia, ib)


def _sc_gather_rows(table, idx):
    n = idx.shape[0]
    d = table.shape[1]
    nc, nw = _sc_workers()
    per_w = n // nw
    n_win = per_w // SC_WINDOW
    idx3 = idx.reshape(nw, n_win, SC_WINDOW)
    mesh = plsc.VectorSubcoreMesh(core_axis_name="c", subcore_axis_name="s")

    @functools.partial(
        pl.kernel, mesh=mesh,
        out_type=jax.ShapeDtypeStruct((n, d), table.dtype),
        scratch_types=[
            pltpu.VMEM((n_win, SC_WINDOW), jnp.int32),
            pltpu.VMEM((SC_WINDOW, d), table.dtype),
        ],
    )
    def gather(table_hbm, idx_hbm, out_hbm, idx_v, rows_v):
        wid = lax.axis_index("s") * nc + lax.axis_index("c")
        pltpu.sync_copy(idx_hbm.at[wid], idx_v)
        base = wid * per_w

        @pl.loop(0, n_win)
        def _(j):
            pltpu.sync_copy(table_hbm.at[idx_v.at[j]], rows_v)
            pltpu.sync_copy(rows_v, out_hbm.at[pl.ds(base + j * SC_WINDOW, SC_WINDOW)])

    return gather(table, idx3)


def _expert_kernel(layer, be_ref, nused_ref, x_ref, w1_hbm, w3_hbm, w2_hbm, o_ref,
                   w1f, w3f, w2f, w1s, w3s, w2s, sems, group_ref):
    i = pl.program_id(0)
    n_used = nused_ref[0]
    e_now = be_ref[i]

    def fetch(e, slot):
        return [pltpu.make_async_copy(w_hbm.at[layer, e], w_f.at[slot], sems.at[k, slot])
                for k, (w_hbm, w_f) in enumerate(((w1_hbm, w1f), (w3_hbm, w3f), (w2_hbm, w2f)))]

    @pl.when(i == 0)
    def _():
        group_ref[0] = 0
        for copy in fetch(e_now, 0):
            copy.start()

    first_of_group = jnp.logical_and(
        i < n_used, jnp.logical_or(i == 0, e_now != be_ref[jnp.maximum(i - 1, 0)]))

    @pl.when(first_of_group)
    def _():
        slot = group_ref[0] % 2
        for copy in fetch(e_now, slot):
            copy.wait()
        w1s[...] = w1f[slot].astype(BF16)
        w3s[...] = w3f[slot].astype(BF16)
        w2s[...] = w2f[slot].astype(BF16)
        last = be_ref.shape[0] - 1
        j = lax.while_loop(
            lambda j: jnp.logical_and(j < n_used, be_ref[jnp.minimum(j, last)] == e_now),
            lambda j: j + 1, i + 1)
        next_e = be_ref[jnp.minimum(j, last)]

        @pl.when(j < n_used)
        def _():
            for copy in fetch(next_e, 1 - slot):
                copy.start()

        group_ref[0] = group_ref[0] + 1

    @pl.when(i < n_used)
    def _():
        lo, hi = _unpack_bf16_pairs(x_ref[...])
        xb = jnp.concatenate([lo, hi], axis=1).astype(BF16)
        a = _dot(xb, w1s[...])
        b = _dot(xb, w3s[...])
        for r0 in range(0, MOE_ROWS, MOE_SLAB):
            rs = slice(r0, r0 + MOE_SLAB)
            h = a[rs] * jax.nn.sigmoid(a[rs]) * b[rs]
            o_ref[rs, :] = _pack_bf16_pairs(_dot(h.astype(BF16), w2s[...]))


def _experts(layer, buf, block_e, n_used, w1, w3, w2):
    n_rows, dp = buf.shape
    nb = n_rows // MOE_ROWS
    d, fe = w1.shape[-2:]

    def row_map(i, be, nu):
        return (jnp.minimum(i, nu[0] - 1), 0)

    grid_spec = pltpu.PrefetchScalarGridSpec(
        num_scalar_prefetch=2,
        grid=(nb,),
        in_specs=[
            pl.BlockSpec((MOE_ROWS, dp), row_map),
            pl.BlockSpec(memory_space=pl.ANY),
            pl.BlockSpec(memory_space=pl.ANY),
            pl.BlockSpec(memory_space=pl.ANY),
        ],
        out_specs=pl.BlockSpec((MOE_ROWS, dp), row_map),
        scratch_shapes=[
            pltpu.VMEM((2, d, fe), F32), pltpu.VMEM((2, d, fe), F32), pltpu.VMEM((2, fe, d), F32),
            pltpu.VMEM((d, fe), BF16), pltpu.VMEM((d, fe), BF16), pltpu.VMEM((fe, d), BF16),
            pltpu.SemaphoreType.DMA((3, 2)),
            pltpu.SMEM((1,), jnp.int32),
        ],
    )
    return pl.pallas_call(
        functools.partial(_expert_kernel, layer),
        grid_spec=grid_spec,
        out_shape=jax.ShapeDtypeStruct((n_rows, dp), jnp.uint32),
        compiler_params=pltpu.CompilerParams(dimension_semantics=("arbitrary",),
                                             vmem_limit_bytes=EXPERT_VMEM_LIMIT),
    )(block_e, n_used, buf, w1, w3, w2)


def _combine_kernel(x1_ref, ya_ref, yb_ref, gates_ref, mod_ref, g_ref, b_ref, *rest):
    o_ref = rest[-1]
    d = D_MODEL
    o_ref[0] = _moe_output(x1_ref[0], ya_ref[0], yb_ref[0], gates_ref[...],
                           mod_ref[0][:, 5 * d:6 * d], g_ref[...], b_ref[...])


def _combine(x1, pairs, gates, mod3, mb0, g, b, out_bsz, ob0, prev):
    bsz, seq, d = x1.shape
    ts = min(COMB_ROWS, seq)
    ns = seq // ts
    in_specs = [
        pl.BlockSpec((1, ts, d), lambda i, j: (i, j, 0)),
        pl.BlockSpec((1, ts, d // 2), lambda i, j: (0, i * ns + j, 0)),
        pl.BlockSpec((1, ts, d // 2), lambda i, j: (1, i * ns + j, 0)),
        pl.BlockSpec((ts, SUBLANES), lambda i, j: (i * ns + j, 0)),
        pl.BlockSpec((1, 1, 6 * d), lambda i, j: (i + mb0, 0, 0)),
        pl.BlockSpec((1, d), lambda i, j: (0, 0)),
        pl.BlockSpec((1, d), lambda i, j: (0, 0)),
    ]
    args = [x1, pairs, pairs, gates, mod3, g, b]
    aliases = {}
    if prev is not None:
        in_specs.append(pl.BlockSpec(memory_space=pl.ANY))
        aliases = {len(args): 0}
        args.append(prev)
    return pl.pallas_call(
        _combine_kernel,
        grid=(bsz, ns),
        in_specs=in_specs,
        out_specs=pl.BlockSpec((1, ts, d), lambda i, j: (i + ob0, j, 0)),
        out_shape=jax.ShapeDtypeStruct((out_bsz, seq, d), F32),
        input_output_aliases=aliases,
        compiler_params=pltpu.CompilerParams(dimension_semantics=("arbitrary", "arbitrary"),
                                             vmem_limit_bytes=MIXER_VMEM_LIMIT),
    )(*args)


def _expert_layout(counts):
    shift = MOE_ROWS.bit_length() - 1
    padded = lax.shift_left(
        lax.shift_right_logical(counts.astype(jnp.int32) + (MOE_ROWS - 1), shift), shift)
    e_out = lax.broadcasted_iota(jnp.int32, (N_EXPERTS, N_EXPERTS), 0)
    e_in = lax.broadcasted_iota(jnp.int32, (N_EXPERTS, N_EXPERTS), 1)
    upto = (e_in <= e_out).astype(BF16)
    padded_f = jnp.broadcast_to(padded.astype(F32), (N_EXPERTS, LANES))
    pad_end = _dot(upto, padded_f.astype(BF16))
    pad_start = pad_end[:, 0:1] - padded.astype(F32)
    block_start = (lax.broadcasted_iota(jnp.int32, (N_EXPERTS, LANES), 1) * MOE_ROWS).astype(F32)
    block_e = jnp.sum((block_start >= pad_end).astype(jnp.int32), axis=0, keepdims=True)
    n_used = lax.shift_right_logical(pad_end[N_EXPERTS - 1:N_EXPERTS, :].astype(jnp.int32), shift)
    blk = jnp.concatenate([jnp.minimum(block_e, N_EXPERTS - 1), n_used,
                           jnp.zeros((SUBLANES - 2, LANES), jnp.int32)], axis=0)
    return pad_start, blk


def _slots(ri, pad_start):
    n = ri.shape[1]
    erow = lax.broadcasted_iota(jnp.int32, (N_EXPERTS, n), 0)
    rows = []
    for k in range(TOP_K):
        start = jnp.sum(jnp.where(erow == ri[k:k + 1, :], pad_start, 0.0), axis=0, keepdims=True)
        rows.append(start.astype(jnp.int32) + ri[TOP_K + k:TOP_K + k + 1, :])
    rows.append(jnp.zeros((SUBLANES - TOP_K, n), jnp.int32))
    return jnp.concatenate(rows, axis=0)


def kernel(x, c, w_ada, b_ada, w_in, sc_conv, cv_conv, cv_conv_b, cv_ln_g, cv_ln_b, sg_ln_g, sg_ln_b, sg_w, sg_b, w_branch, w_gate, b_gate, w_o, ln1_g, ln1_b, w_router, b_router, w1, w3, w2, ln2_g, ln2_b):
    bsz, seq, d = x.shape
    mod3 = _ada(c, w_ada, b_ada).reshape(DEPTH, bsz, 1, 6 * d)
    wrt = w_router.T
    brc = b_router.reshape(N_EXPERTS, 1)
    assert W_BRANCH == d
    w_all = _pack_weights(w_in, jnp.transpose(w_gate, (0, 2, 1, 3)), w_branch, w_o)
    cb = bsz // N_CHAINS
    n_tok = cb * seq
    n_blocks = (n_tok * TOP_K + N_EXPERTS * (MOE_ROWS - 1) + MOE_ROWS - 1) // MOE_ROWS
    assert n_blocks <= LANES and MOE_ROWS & (MOE_ROWS - 1) == 0
    chains = [(x, h * cb) for h in range(N_CHAINS)]
    for l in range(DEPTH):
        last = l == DEPTH - 1
        result = None
        for h in range(N_CHAINS):
            x1, u2, dest, rf, blk = _mixer(
                l, chains[h], mod3[l], h * cb, cb, w_all, sc_conv[l], cv_conv[l],
                cv_conv_b[l].reshape(1, -1), cv_ln_g[l].reshape(1, -1), cv_ln_b[l].reshape(1, -1),
                sg_ln_g[l].reshape(1, -1), sg_ln_b[l].reshape(1, -1), sg_w[l], sg_b[l].T,
                b_gate[l], ln1_g[l].reshape(1, -1), ln1_b[l].reshape(1, -1), wrt, brc)
            buf = _sc_scatter_rows(u2.reshape(n_tok, d // 2), dest[0], dest[1], n_blocks * MOE_ROWS)
            obuf = _experts(l, buf, blk[0, 0:n_blocks], blk[1, 0:1], w1, w3, w2)
            pair_idx = dest[0:TOP_K].reshape(TOP_K * n_tok)
            pairs = _sc_gather_rows(obuf, pair_idx).reshape(TOP_K, n_tok, d // 2)
            gates = rf
            g2, b2 = ln2_g[l].reshape(1, -1), ln2_b[l].reshape(1, -1)
            if last:
                result = _combine(x1, pairs, gates, mod3[l], h * cb, g2, b2, bsz, h * cb, result)
            else:
                chains[h] = (x1, pairs, gates, mod3[l], g2, b2)
    return result
```
